```python
import jax, jax.numpy as jnp
from jax import lax
import numpy as np

D_MODEL = 1024
BATCH = 32
SEQ = 2048
DEPTH = 1

CHUNK = 64
Q_BLOCK = 128
GLA_HEADS = 4
GLA_DK = 128
GLA_DV = 256
GLA_GATE_RANK = 16
GLA_GATE_TAU = 16.0
MLA_HEADS = 8
MLA_Q_RANK = 768
MLA_KV_RANK = 256
MLA_NOPE = 128
MLA_ROPE = 64
MLA_V = 128
ROPE_THETA = 10000.0
N_EXPERTS = 256
TOP_K = 8
N_GROUPS = 8
TOPK_GROUPS = 4
D_EXPERT = 256
ROUTED_SCALE = 2.5
MOE_BLOCK = 256
LN_EPS = 1e-5
RMS_EPS = 1e-6
DEEPNORM_ALPHA = (2.0 * DEPTH) ** 0.25
DEEPNORM_BETA = (8.0 * DEPTH) ** -0.25
IN_SPLITS = (GLA_HEADS * GLA_DK, GLA_HEADS * GLA_DK, GLA_HEADS * GLA_DV, GLA_HEADS * GLA_DV, GLA_GATE_RANK, MLA_Q_RANK, MLA_KV_RANK, MLA_ROPE, D_MODEL, D_MODEL)
IN_TOTAL = sum(IN_SPLITS)

kernel_name = "hybrid_gla_mla_moe_deepnorm_adaln"


def layer_norm(x, g, b):
    xf = x.astype(jnp.float32)
    mu = jnp.mean(xf, -1, keepdims=True)
    var = jnp.mean(jnp.square(xf - mu), -1, keepdims=True)
    return ((xf - mu) * lax.rsqrt(var + LN_EPS)).astype(x.dtype) * g + b


def rms_norm(x, g):
    xf = x.astype(jnp.float32)
    return (xf * lax.rsqrt(jnp.mean(xf * xf, -1, keepdims=True) + RMS_EPS)).astype(x.dtype) * g


def rope_tables(positions):
    half = MLA_ROPE // 2
    inv_freq = ROPE_THETA ** (-jnp.arange(half, dtype=jnp.float32) * (2.0 / MLA_ROPE))
    ang = positions.astype(jnp.float32)[..., None] * inv_freq
    return jnp.cos(ang), jnp.sin(ang)


def apply_rope(x, cos, sin):
    x1, x2 = jnp.split(x, 2, axis=-1)
    return jnp.concatenate([x1 * cos - x2 * sin, x2 * cos + x1 * sin], axis=-1).astype(x.dtype)


def gla_chunked(q, k, v, log_a):
    B, S = q.shape[:2]
    nc = S // CHUNK

    def chunks(t):
        return t.reshape(B, nc, CHUNK, *t.shape[2:]).swapaxes(0, 1).astype(jnp.float32)

    G = jnp.cumsum(chunks(log_a), axis=2)
    G_end = G[:, :, -1]
    k_dec = chunks(k) * jnp.exp(G_end[:, :, None] - G)

    def step(state, inp):
        q_c, k_c, v_c, g_end = inp
        state = state * jnp.exp(g_end)[..., None] + jnp.einsum('bchk,bchv->bhkv', k_c, v_c)
        return state, jnp.einsum('bchk,bhkv->bchv', q_c, state)

    state0 = jnp.zeros((B, GLA_HEADS, GLA_DK, GLA_DV), jnp.float32)
    _, o = lax.scan(step, state0, (chunks(q), k_dec, chunks(v), G_end))
    return o.swapaxes(0, 1).reshape(B, S, GLA_HEADS, GLA_DV)


def mla_attention(q_nope, q_rope, k_nope, k_rope, v):
    S = q_nope.shape[1]
    scale = (MLA_NOPE + MLA_ROPE) ** -0.5
    chunk_id = jnp.arange(S) // CHUNK
    outs = []
    for start in range(0, S, Q_BLOCK):
        end = start + Q_BLOCK
        s = (jnp.einsum('bqhd,bkhd->bhqk', q_nope[:, start:end], k_nope[:, :end])
             + jnp.einsum('bqhr,bkr->bhqk', q_rope[:, start:end], k_rope[:, :end]))
        s = s.astype(jnp.float32) * scale
        mask = chunk_id[None, :end] <= chunk_id[start:end, None]
        p = jax.nn.softmax(jnp.where(mask, s, -jnp.inf), axis=-1).astype(v.dtype)
        outs.append(jnp.einsum('bhqk,bkhv->bqhv', p, v[:, :end]))
    return jnp.concatenate(outs, axis=1)


def hybrid_mixer(h, cos, sin, w_in, w_gla_a2, b_gla_a, g_gla_norm, w_gla_o, g_cq, w_uq, g_ckv, w_ukv, w_mla_o, w_out):
    B, S, _ = h.shape
    bounds, acc = [], 0
    for n in IN_SPLITS[:-1]:
        acc += n
        bounds.append(acc)
    gq, gk, gv, gr, ga, cq, ckv, kr, gate_a, gate_b = jnp.split(h @ w_in, bounds, axis=-1)

    q = gq.reshape(B, S, GLA_HEADS, GLA_DK) * (GLA_DK ** -0.5)
    k = gk.reshape(B, S, GLA_HEADS, GLA_DK)
    v = gv.reshape(B, S, GLA_HEADS, GLA_DV)
    log_a = (jax.nn.log_sigmoid((ga @ w_gla_a2 + b_gla_a).astype(jnp.float32)) / GLA_GATE_TAU).reshape(B, S, GLA_HEADS, GLA_DK)
    o_a = gla_chunked(q, k, v, log_a).astype(h.dtype)
    o_a = rms_norm(o_a, g_gla_norm).reshape(B, S, GLA_HEADS * GLA_DV) * jax.nn.silu(gr)
    y_a = o_a @ w_gla_o

    qf = (rms_norm(cq, g_cq) @ w_uq).reshape(B, S, MLA_HEADS, MLA_NOPE + MLA_ROPE)
    q_nope = qf[..., :MLA_NOPE]
    q_rope = apply_rope(qf[..., MLA_NOPE:], cos[:, :, None], sin[:, :, None])
    kv = (rms_norm(ckv, g_ckv) @ w_ukv).reshape(B, S, MLA_HEADS, MLA_NOPE + MLA_V)
    k_nope, v_b = kv[..., :MLA_NOPE], kv[..., MLA_NOPE:]
    k_rope = apply_rope(kr, cos, sin)
    o_b = mla_attention(q_nope, q_rope, k_nope, k_rope, v_b).reshape(B, S, MLA_HEADS * MLA_V)
    y_b = o_b @ w_mla_o

    y = jax.nn.sigmoid(gate_a) * y_a + jax.nn.sigmoid(gate_b) * y_b
    return y @ w_out


def moe_ffn(h, w_router, b_router, w_e1, w_e3, w_e2, w_s1, w_s3, w_s2):
    B, S, D = h.shape
    N = B * S
    t = h.reshape(N, D)
    scores = jax.nn.sigmoid((t @ w_router).astype(jnp.float32))
    biased = scores + b_router.astype(jnp.float32)
    grp_score = lax.top_k(biased.reshape(N, N_GROUPS, N_EXPERTS // N_GROUPS), 2)[0].sum(-1)
    _, grp_idx = lax.top_k(grp_score, TOPK_GROUPS)
    grp_mask = jnp.sum(jax.nn.one_hot(grp_idx, N_GROUPS, dtype=jnp.float32), axis=1) > 0
    expert_mask = jnp.repeat(grp_mask, N_EXPERTS // N_GROUPS, axis=1)
    _, idx = lax.top_k(jnp.where(expert_mask, biased, -jnp.inf), TOP_K)
    w = jnp.take_along_axis(scores, idx, axis=1)
    w = w / jnp.sum(w, -1, keepdims=True) * ROUTED_SCALE

    A = N * TOP_K
    flat_e = idx.reshape(A).astype(jnp.int32)
    order = jnp.argsort(flat_e).astype(jnp.int32)
    sorted_e = flat_e[order]
    counts = jnp.bincount(flat_e, length=N_EXPERTS).astype(jnp.int32)
    padded = (counts + MOE_BLOCK - 1) // MOE_BLOCK * MOE_BLOCK
    pad_end = jnp.cumsum(padded)
    pad_start = pad_end - padded
    raw_start = jnp.cumsum(counts) - counts
    dest = pad_start[sorted_e] + jnp.arange(A, dtype=jnp.int32) - raw_start[sorted_e]
    n_slots = -(-A // MOE_BLOCK) * MOE_BLOCK + N_EXPERTS * MOE_BLOCK
    n_blocks = n_slots // MOE_BLOCK
    slot_tok = jnp.full((n_slots,), N, jnp.int32).at[dest].set(order // TOP_K)
    slot_w = jnp.zeros((n_slots,), h.dtype).at[dest].set(w.reshape(A)[order].astype(h.dtype))
    blk_start = jnp.arange(n_blocks, dtype=jnp.int32) * MOE_BLOCK
    blk_e = jnp.minimum(jnp.searchsorted(pad_end, blk_start, side='right'), N_EXPERTS - 1)
    t_pad = jnp.concatenate([t, jnp.zeros((1, D), t.dtype)], axis=0)

    def block_step(acc, inp):
        tok, wt, e = inp
        xb = t_pad[tok]
        hid = jax.nn.silu(xb @ w_e1[e]) * (xb @ w_e3[e])
        return acc.at[tok].add((hid @ w_e2[e]) * wt[:, None]), None

    acc0 = jnp.zeros((N + 1, D), h.dtype)
    acc, _ = lax.scan(block_step, acc0, (slot_tok.reshape(n_blocks, MOE_BLOCK), slot_w.reshape(n_blocks, MOE_BLOCK), blk_e))
    shared = (jax.nn.silu(t @ w_s1) * (t @ w_s3)) @ w_s2
    return (acc[:N] + shared).reshape(B, S, D)


def setup_inputs(seed: int = 0) -> dict:
    key = jax.random.key(seed)
    ks = iter(jax.random.split(key, 40))
    L, D, F = DEPTH, D_MODEL, D_EXPERT

    def nrm(shape, fan_in, scale=1.0):
        return jax.random.normal(next(ks), shape, jnp.float32) * (scale * fan_in ** -0.5)

    def gain(shape):
        return 1.0 + 0.02 * jax.random.normal(next(ks), shape, jnp.float32)

    def small(shape, s):
        return s * jax.random.normal(next(ks), shape, jnp.float32)

    x = jax.random.normal(next(ks), (BATCH, SEQ, D), jnp.float32)
    c = jax.random.normal(next(ks), (BATCH, D), jnp.float32)
    offset = jax.random.randint(next(ks), (BATCH, 1), 0, 4096, dtype=jnp.int32)
    positions = offset + jnp.arange(SEQ, dtype=jnp.int32)[None, :]
    return {
        "x": x, "c": c, "positions": positions,
        "w_ada": nrm((L, D, 6 * D), D, 0.1), "b_ada": small((L, 6 * D), 0.02),
        "w_in": nrm((L, D, IN_TOTAL), D),
        "w_gla_a2": nrm((L, GLA_GATE_RANK, GLA_HEADS * GLA_DK), GLA_GATE_RANK),
        "b_gla_a": small((L, GLA_HEADS * GLA_DK), 0.1),
        "g_gla_norm": gain((L, GLA_DV)),
        "w_gla_o": nrm((L, GLA_HEADS * GLA_DV, D), GLA_HEADS * GLA_DV),
        "g_cq": gain((L, MLA_Q_RANK)),
        "w_uq": nrm((L, MLA_Q_RANK, MLA_HEADS * (MLA_NOPE + MLA_ROPE)), MLA_Q_RANK),
        "g_ckv": gain((L, MLA_KV_RANK)),
        "w_ukv": nrm((L, MLA_KV_RANK, MLA_HEADS * (MLA_NOPE + MLA_V)), MLA_KV_RANK),
        "w_mla_o": nrm((L, MLA_HEADS * MLA_V, D), MLA_HEADS * MLA_V),
        "w_out": nrm((L, D, D), D, DEEPNORM_BETA),
        "ln1_g": gain((L, D)), "ln1_b": small((L, D), 0.02),
        "w_router": nrm((L, D, N_EXPERTS), D), "b_router": small((L, N_EXPERTS), 0.01),
        "w_e1": nrm((L, N_EXPERTS, D, F), D), "w_e3": nrm((L, N_EXPERTS, D, F), D),
        "w_e2": nrm((L, N_EXPERTS, F, D), F, DEEPNORM_BETA),
        "w_s1": nrm((L, D, F), D), "w_s3": nrm((L, D, F), D),
        "w_s2": nrm((L, F, D), F, DEEPNORM_BETA),
        "ln2_g": gain((L, D)), "ln2_b": small((L, D), 0.02),
    }


def reference(x, c, positions, w_ada, b_ada, w_in, w_gla_a2, b_gla_a, g_gla_norm, w_gla_o, g_cq, w_uq, g_ckv, w_ukv, w_mla_o, w_out, ln1_g, ln1_b, w_router, b_router, w_e1, w_e3, w_e2, w_s1, w_s3, w_s2, ln2_g, ln2_b):
    cos, sin = rope_tables(positions)
    cond = jax.nn.silu(c)
    for l in range(DEPTH):
        mod = (cond @ w_ada[l] + b_ada[l])[:, None, :]
        sh1, sc1, gt1, sh2, sc2, gt2 = jnp.split(mod, 6, axis=-1)
        h = x * (1 + sc1) + sh1
        mix = hybrid_mixer(h, cos, sin, w_in[l], w_gla_a2[l], b_gla_a[l], g_gla_norm[l], w_gla_o[l], g_cq[l], w_uq[l], g_ckv[l], w_ukv[l], w_mla_o[l], w_out[l])
        x = layer_norm(DEEPNORM_ALPHA * x + (1 + gt1) * mix, ln1_g[l], ln1_b[l])
        h = x * (1 + sc2) + sh2
        ffn = moe_ffn(h, w_router[l], b_router[l], w_e1[l], w_e3[l], w_e2[l], w_s1[l], w_s3[l], w_s2[l])
        x = layer_norm(DEEPNORM_ALPHA * x + (1 + gt2) * ffn, ln2_g[l], ln2_b[l])
    return x
```

```python
import functools

import jax
import jax.numpy as jnp
from jax import lax
from jax.experimental import pallas as pl
from jax.experimental.pallas import tpu as pltpu
from jax.experimental.pallas import tpu_sc as plsc

CHUNK = 64
GLA_HEADS = 4
GLA_DK = 128
GLA_DV = 256
GLA_GATE_RANK = 16
GLA_GATE_TAU = 16.0
MLA_HEADS = 8
MLA_Q_RANK = 768
MLA_KV_RANK = 256
MLA_NOPE = 128
MLA_ROPE = 64
MLA_V = 128
ROPE_THETA = 10000.0
N_EXPERTS = 256
TOP_K = 8
N_GROUPS = 8
TOPK_GROUPS = 4
D_EXPERT = 256
ROUTED_SCALE = 2.5
LN_EPS = 1e-5
RMS_EPS = 1e-6

LANES = 128
VMEM_LIMIT = 56 * 1024 * 1024
EXPERT_BLOCK = 256

F32 = jnp.float32
BF16 = jnp.bfloat16
I32 = jnp.int32


def _cparams(*sem):
    return pltpu.CompilerParams(dimension_semantics=sem, vmem_limit_bytes=VMEM_LIMIT)


def _sigmoid(x):
    return 1.0 / (1.0 + jnp.exp(-x))


def _silu(x):
    return x * _sigmoid(x)


def _dot(a, b):
    return jnp.dot(a, b, preferred_element_type=F32)


def _dot_nt(a, b):
    return lax.dot_general(a, b, (((1,), (1,)), ((), ())), preferred_element_type=F32)


def _dot_tn(a, b):
    return lax.dot_general(a, b, (((0,), (0,)), ((), ())), preferred_element_type=F32)


def _pack_bf16_pair(x):
    w = x.shape[1] // 2
    u = lax.bitcast_convert_type(x.astype(BF16).astype(F32), I32)
    lo = lax.shift_right_logical(u[:, :w], jnp.int32(16))
    hi = jnp.bitwise_and(u[:, w:], jnp.int32(-65536))
    return jnp.bitwise_or(lo, hi)


def _unpack_bf16_pair(p):
    lo = lax.bitcast_convert_type(lax.shift_left(p, jnp.int32(16)), F32)
    hi = lax.bitcast_convert_type(jnp.bitwise_and(p, jnp.int32(-65536)), F32)
    return lo, hi


def _mod_kernel(c_ref, w_ref, b_ref, o_ref):
    cond = _silu(c_ref[...]).astype(BF16)
    o_ref[...] = _dot(cond, w_ref[...].astype(BF16)) + b_ref[...]


def _modulation(c, w_ada, b_ada):
    B, D = c.shape
    W = w_ada.shape[1]
    tn = D
    return pl.pallas_call(
        _mod_kernel,
        out_shape=jax.ShapeDtypeStruct((B, W), F32),
        grid=(W // tn,),
        in_specs=[
            pl.BlockSpec((B, D), lambda j: (0, 0)),
            pl.BlockSpec((D, tn), lambda j: (0, j)),
            pl.BlockSpec((1, tn), lambda j: (0, j)),
        ],
        out_specs=pl.BlockSpec((B, tn), lambda j: (0, j)),
        compiler_params=_cparams("arbitrary"),
        name="adaln_mod",
    )(c, w_ada, b_ada.reshape(1, W))


def _rope_kernel(pos_ref, f_ref, ph_ref, cc_ref, ss_ref):
    ang = pos_ref[...] * f_ref[...]
    cc_ref[...] = jnp.cos(ang)
    ss_ref[...] = jnp.sin(ang) * ph_ref[...]


def _rope_tables(positions):
    B, S = positions.shape
    N = B * S
    half = MLA_ROPE // 2
    inv_freq = ROPE_THETA ** (-jnp.arange(half, dtype=F32) * (2.0 / MLA_ROPE))
    f4 = jnp.tile(inv_freq, LANES // half).reshape(1, LANES)
    sign = jnp.tile(jnp.concatenate([-jnp.ones((half,), F32), jnp.ones((half,), F32)]), LANES // MLA_ROPE)
    pos = jnp.broadcast_to(positions.reshape(N, 1).astype(F32), (N, LANES))
    tm = min(N, 1024)
    spec = pl.BlockSpec((tm, LANES), lambda i: (i, 0))
    vec = pl.BlockSpec((1, LANES), lambda i: (0, 0))
    return pl.pallas_call(
        _rope_kernel,
        out_shape=(jax.ShapeDtypeStruct((N, LANES), F32), jax.ShapeDtypeStruct((N, LANES), F32)),
        grid=(N // tm,),
        in_specs=[spec, vec, vec],
        out_specs=(spec, spec),
        compiler_params=_cparams("arbitrary"),
        name="rope_tables",
    )(pos, f4, sign.reshape(1, LANES))


QK_W = 2 * GLA_HEADS * GLA_DK
GV_W = GLA_HEADS * GLA_DV
TAIL_W = 3 * LANES
IN_SEG = (QK_W, GV_W, GV_W, MLA_Q_RANK, MLA_KV_RANK, 1024, 1024, TAIL_W)


def _prep_w_in(w_in, D):
    s = [GLA_HEADS * GLA_DK, GLA_HEADS * GLA_DK, GV_W, GV_W, GLA_GATE_RANK, MLA_Q_RANK, MLA_KV_RANK, MLA_ROPE, D, D]
    offs = [0]
    for n in s:
        offs.append(offs[-1] + n)
    gq, gk, gv, gr, ga, cq, ckv, kr, gate_a, gate_b = [w_in[:, offs[i]:offs[i + 1]] for i in range(10)]
    half = MLA_ROPE // 2
    kr_sw = jnp.concatenate([kr[:, half:], kr[:, :half]], axis=1)
    pad = jnp.zeros((w_in.shape[0], LANES - GLA_GATE_RANK), w_in.dtype)
    return jnp.concatenate([gq, gk, gv, gr, cq, ckv, gate_a, gate_b, kr, kr, kr_sw, kr_sw, ga, pad], axis=1).astype(BF16)


def _inproj_kernel(x_ref, mod_ref, w_ref, *out_refs):
    sh1 = mod_ref[0, 0:1, :]
    sc1 = mod_ref[0, 1:2, :]
    h = (x_ref[...] * (1.0 + sc1) + sh1).astype(BF16)
    off = 0
    for ref in out_refs:
        n = ref.shape[-1]
        ref[...] = _dot(h, w_ref[:, off:off + n]).astype(ref.dtype)
        off += n


def _in_projection(x2, mod6, w_in_p, S):
    N, D = x2.shape
    W = w_in_p.shape[1]
    tm = min(S, 512)
    per_b = S // tm
    return pl.pallas_call(
        _inproj_kernel,
        out_shape=tuple(jax.ShapeDtypeStruct((N, n), BF16) for n in IN_SEG),
        grid=(N // tm,),
        in_specs=[
            pl.BlockSpec((tm, D), lambda i: (i, 0)),
            pl.BlockSpec((1, 6, D), lambda i: (i // per_b, 0, 0)),
            pl.BlockSpec((D, W), lambda i: (0, 0), pipeline_mode=pl.Buffered(1)),
        ],
        out_specs=tuple(pl.BlockSpec((tm, n), lambda i: (i, 0)) for n in IN_SEG),
        compiler_params=_cparams("arbitrary"),
        name="in_proj",
    )(x2, mod6, w_in_p)


def _gla_kernel(qk_ref, v_ref, gr_ref, tail_ref, wa2_ref, ba_ref, gn_ref, wo_ref, y_ref, st_ref, o_ref):
    t = pl.program_id(1)

    @pl.when(t == 0)
    def _():
        st_ref[...] = jnp.zeros_like(st_ref)

    ts = qk_ref.shape[0]
    HK = GLA_HEADS * GLA_DK
    r = lax.broadcasted_iota(I32, (CHUNK, CHUNK), 0)
    c = lax.broadcasted_iota(I32, (CHUNK, CHUNK), 1)
    tri = (r >= c).astype(BF16)
    qscale = GLA_DK ** -0.5

    def chunk_body(ci, carry):
        rows = pl.ds(pl.multiple_of(ci * CHUNK, CHUNK), CHUNK)
        ga = tail_ref[rows, 2 * LANES:3 * LANES]
        z = _dot(ga, wa2_ref[...]) + ba_ref[...]
        log_a = (jnp.minimum(z, 0.0) - jnp.log(1.0 + jnp.exp(-jnp.abs(z)))) * (1.0 / GLA_GATE_TAU)
        la_hi = log_a.astype(BF16)
        la_lo = (log_a - la_hi.astype(F32)).astype(BF16)
        G = _dot(tri, la_hi) + _dot(tri, la_lo)
        g_end = G[CHUNK - 1:CHUNK, :]
        kdec_all = qk_ref[rows, HK:2 * HK].astype(F32) * jnp.exp(g_end - G)
        dec_all = jnp.exp(g_end)
        for h in range(GLA_HEADS):
            ks = slice(h * GLA_DK, (h + 1) * GLA_DK)
            vs = slice(h * GLA_DV, (h + 1) * GLA_DV)
            kd = kdec_all[:, ks].astype(BF16)
            vh = v_ref[rows, vs]
            st = st_ref[h] * dec_all[:, ks] + _dot_tn(vh, kd)
            st_ref[h] = st
            qh = (qk_ref[rows, ks].astype(F32) * qscale).astype(BF16)
            o = _dot_nt(qh, st.astype(BF16))
            o = o * lax.rsqrt(jnp.mean(o * o, axis=-1, keepdims=True) + RMS_EPS) * gn_ref[...]
            o_ref[rows, vs] = (o * _silu(gr_ref[rows, vs].astype(F32))).astype(BF16)
        return carry

    lax.fori_loop(0, ts // CHUNK, chunk_body, 0)
    y_ref[...] = _dot(o_ref[...], wo_ref[...]).astype(y_ref.dtype)


def _gla(qk, gv, gr, tail, wa2_p, b_a, g_norm, w_o, B, S):
    N = B * S
    D = w_o.shape[1]
    ts = min(S, 512)
    per_b = S // ts
    HK = GLA_HEADS * GLA_DK
    tok = lambda n: pl.BlockSpec((ts, n), lambda b, t: (b * per_b + t, 0))
    full = lambda a: pl.BlockSpec(a.shape, lambda b, t: (0,) * a.ndim)
    return pl.pallas_call(
        _gla_kernel,
        out_shape=jax.ShapeDtypeStruct((N, D), BF16),
        grid=(B, per_b),
        in_specs=[tok(QK_W), tok(GV_W), tok(GV_W), tok(TAIL_W), full(wa2_p), full(b_a), full(g_norm), full(w_o)],
        out_specs=tok(D),
        scratch_shapes=[pltpu.VMEM((GLA_HEADS, GLA_DV, GLA_DK), F32), pltpu.VMEM((ts, GV_W), BF16)],
        compiler_params=_cparams("arbitrary", "arbitrary"),
        name="gla",
    )(qk, gv, gr, tail, wa2_p, b_a, g_norm, w_o)


HQ = MLA_HEADS * 2 * LANES


def _prep_w_uq(w_uq):
    dh = MLA_NOPE + MLA_ROPE
    half = MLA_ROPE // 2
    nope = [w_uq[:, h * dh:h * dh + MLA_NOPE] for h in range(MLA_HEADS)]
    rope = [w_uq[:, h * dh + MLA_NOPE:(h + 1) * dh] for h in range(MLA_HEADS)]
    rope_sw = [jnp.concatenate([r[:, half:], r[:, :half]], axis=1) for r in rope]
    return jnp.concatenate(nope + rope + rope_sw, axis=1).astype(BF16)


def _prep_w_ukv(w_ukv):
    dh = MLA_NOPE + MLA_V
    kn = [w_ukv[:, h * dh:h * dh + MLA_NOPE] for h in range(MLA_HEADS)]
    vv = [w_ukv[:, h * dh + MLA_NOPE:(h + 1) * dh] for h in range(MLA_HEADS)]
    return jnp.concatenate(kn + vv, axis=1).astype(BF16)


def _rms(x, g):
    return x * lax.rsqrt(jnp.mean(x * x, axis=-1, keepdims=True) + RMS_EPS) * g


def _mla_prep_kernel(cq_ref, ckv_ref, tail_ref, cc_ref, ss_ref, gq_ref, wq_ref, gkv_ref, wkv_ref, q_ref, k_ref, v_ref):
    tm = cq_ref.shape[0]
    NP = MLA_HEADS * MLA_NOPE
    RP = MLA_HEADS * MLA_ROPE
    scale = (MLA_NOPE + MLA_ROPE) ** -0.5
    cc = cc_ref[...]
    ss = ss_ref[...]
    cqn = _rms(cq_ref[...].astype(F32), gq_ref[...]).astype(BF16)
    qf = _dot(cqn, wq_ref[...]) * scale
    ckvn = _rms(ckv_ref[...].astype(F32), gkv_ref[...]).astype(BF16)
    kv = _dot(ckvn, wkv_ref[...])
    krr = (tail_ref[:, 0:LANES].astype(F32) * cc + tail_ref[:, LANES:2 * LANES].astype(F32) * ss).astype(BF16)
    lane = lax.broadcasted_iota(I32, (tm, LANES), 1)
    first = lane < MLA_ROPE
    for j in range(MLA_HEADS // 2):
        a = NP + j * LANES
        rot = qf[:, a:a + LANES] * cc + qf[:, a + RP:a + RP + LANES] * ss
        for h, keep in ((2 * j, first), (2 * j + 1, jnp.logical_not(first))):
            base = h * 2 * LANES
            q_ref[:, base:base + LANES] = qf[:, h * MLA_NOPE:(h + 1) * MLA_NOPE].astype(BF16)
            q_ref[:, base + LANES:base + 2 * LANES] = jnp.where(keep, rot, 0.0).astype(BF16)
            k_ref[:, base:base + LANES] = kv[:, h * MLA_NOPE:(h + 1) * MLA_NOPE].astype(BF16)
            k_ref[:, base + LANES:base + 2 * LANES] = krr
    v_ref[...] = kv[:, NP:].astype(BF16)


def _mla_prep(cq, ckv, tail, cc, ss, g_cq, w_uq_p, g_ckv, w_ukv_p):
    N = cq.shape[0]
    tm = min(N, 512)
    tok = lambda n: pl.BlockSpec((tm, n), lambda i: (i, 0))
    full = lambda a: pl.BlockSpec(a.shape, lambda i: (0,) * a.ndim)
    HV = MLA_HEADS * MLA_V
    return pl.pallas_call(
        _mla_prep_kernel,
        out_shape=(jax.ShapeDtypeStruct((N, HQ), BF16), jax.ShapeDtypeStruct((N, HQ), BF16),
                   jax.ShapeDtypeStruct((N, HV), BF16)),
        grid=(N // tm,),
        in_specs=[tok(MLA_Q_RANK), tok(MLA_KV_RANK), tok(TAIL_W), tok(LANES), tok(LANES),
                  full(g_cq), full(w_uq_p), full(g_ckv), full(w_ukv_p)],
        out_specs=(tok(HQ), tok(HQ), tok(HV)),
        compiler_params=_cparams("arbitrary"),
        name="mla_prep",
    )(cq, ckv, tail, cc, ss, g_cq, w_uq_p, g_ckv, w_ukv_p)


def _mla_attn_kernel(q_ref, k_ref, v_ref, wo_ref, y_ref, o_ref):
    i = pl.program_id(1)
    tq = q_ref.shape[0]
    qchunk = (i * tq + lax.broadcasted_iota(I32, (tq, tq), 0)) // CHUNK
    kio = lax.broadcasted_iota(I32, (tq, tq), 1)
    for h in range(MLA_HEADS):
        qh = q_ref[:, h * 2 * LANES:(h + 1) * 2 * LANES]

        def kv_body(j, carry):
            m, l, acc = carry
            rows = pl.ds(pl.multiple_of(j * tq, tq), tq)
            s = _dot_nt(qh, k_ref[rows, h * 2 * LANES:(h + 1) * 2 * LANES])
            s = jnp.where((j * tq + kio) // CHUNK <= qchunk, s, -jnp.inf)
            m_new = jnp.maximum(m, jnp.max(s, axis=-1, keepdims=True))
            p = jnp.exp(s - m_new)
            a = jnp.exp(m - m_new)
            l = a * l + jnp.sum(p, axis=-1, keepdims=True)
            acc = a * acc + _dot(p.astype(BF16), v_ref[rows, h * MLA_V:(h + 1) * MLA_V])
            return m_new, l, acc

        init = (jnp.full((tq, 1), -jnp.inf, F32), jnp.zeros((tq, 1), F32), jnp.zeros((tq, MLA_V), F32))
        m, l, acc = lax.fori_loop(0, i + 1, kv_body, init)
        o_ref[:, h * MLA_V:(h + 1) * MLA_V] = (acc / l).astype(BF16)
    y_ref[...] = _dot(o_ref[...], wo_ref[...]).astype(y_ref.dtype)


def _mla_attention(qc, kc, vv, w_o, B, S):
    N = B * S
    D = w_o.shape[1]
    HV = MLA_HEADS * MLA_V
    tq = min(S, 256)
    per_b = S // tq
    return pl.pallas_call(
        _mla_attn_kernel,
        out_shape=jax.ShapeDtypeStruct((N, D), BF16),
        grid=(B, per_b),
        in_specs=[
            pl.BlockSpec((tq, HQ), lambda b, i: (b * per_b + i, 0)),
            pl.BlockSpec((S, HQ), lambda b, i: (b, 0)),
            pl.BlockSpec((S, HV), lambda b, i: (b, 0)),
            pl.BlockSpec(w_o.shape, lambda b, i: (0, 0)),
        ],
        out_specs=pl.BlockSpec((tq, D), lambda b, i: (b * per_b + i, 0)),
        scratch_shapes=[pltpu.VMEM((tq, HV), BF16)],
        compiler_params=_cparams("arbitrary", "arbitrary"),
        name="mla_attn",
    )(qc, kc, vv, w_o)


def _layer_norm(u, g, b):
    mu = jnp.mean(u, axis=-1, keepdims=True)
    d = u - mu
    var = jnp.mean(d * d, axis=-1, keepdims=True)
    return d * lax.rsqrt(var + LN_EPS) * g + b


def _merge_kernel(alpha, x_ref, ya_ref, yb_ref, ga_ref, gb_ref, mod_ref, wout_ref, g_ref, b_ref, wr_ref,
                  x1_ref, h2_ref, lg_ref):
    gt1 = mod_ref[0, 2:3, :]
    sh2 = mod_ref[0, 3:4, :]
    sc2 = mod_ref[0, 4:5, :]
    y = (_sigmoid(ga_ref[...].astype(F32)) * ya_ref[...].astype(F32)
         + _sigmoid(gb_ref[...].astype(F32)) * yb_ref[...].astype(F32))
    mix = _dot(y.astype(BF16), wout_ref[...])
    x1 = _layer_norm(alpha * x_ref[...] + (1.0 + gt1) * mix, g_ref[...], b_ref[...])
    x1_ref[...] = x1
    h2 = x1 * (1.0 + sc2) + sh2
    h2_ref[...] = _pack_bf16_pair(h2)
    lg_ref[...] = _dot_nt(wr_ref[...], h2.astype(BF16))


def _merge(alpha, x2, y_a, y_b, gate_a, gate_b, mod6, w_out, ln_g, ln_b, wr_t, S):
    N, D = x2.shape
    E = wr_t.shape[0]
    tm = min(S, 512)
    per_b = S // tm
    tok = lambda n: pl.BlockSpec((tm, n), lambda i: (i, 0))
    full = lambda a: pl.BlockSpec(a.shape, lambda i: (0,) * a.ndim)
    return pl.pallas_call(
        functools.partial(_merge_kernel, alpha),
        out_shape=(jax.ShapeDtypeStruct((N, D), F32), jax.ShapeDtypeStruct((N, D // 2), I32),
                   jax.ShapeDtypeStruct((E, N), F32)),
        grid=(N // tm,),
        in_specs=[tok(D), tok(D), tok(D), tok(D), tok(D),
                  pl.BlockSpec((1, 6, D), lambda i: (i // per_b, 0, 0)),
                  full(w_out), full(ln_g), full(ln_b), full(wr_t)],
        out_specs=(tok(D), tok(D // 2), pl.BlockSpec((E, tm), lambda i: (0, i))),
        compiler_params=_cparams("arbitrary"),
        name="merge_ln1",
    )(x2, y_a, y_b, gate_a, gate_b, mod6, w_out, ln_g, ln_b, wr_t)


def _first_argmax(v, io, n):
    m = jnp.max(v, axis=0, keepdims=True)
    idx = jnp.min(jnp.where(v == m, io, n), axis=0, keepdims=True)
    return m, idx


def _route_kernel(lg_ref, br_ref, idx_ref, wt_ref, rank_ref, cnt_ref, carry_ref):
    step = pl.program_id(0)

    @pl.when(step == 0)
    def _():
        carry_ref[...] = jnp.zeros_like(carry_ref)

    E, T = lg_ref.shape
    gsz = E // N_GROUPS
    neg = -jnp.inf
    s = _sigmoid(lg_ref[...])
    biased = s + br_ref[...]
    eio = lax.broadcasted_iota(I32, (E, T), 0)
    gio = lax.broadcasted_iota(I32, (gsz, T), 0)

    gs = []
    for g in range(N_GROUPS):
        blk = biased[g * gsz:(g + 1) * gsz]
        m1, i1 = _first_argmax(blk, gio, gsz)
        m2 = jnp.max(jnp.where(gio == i1, neg, blk), axis=0, keepdims=True)
        gs.append(m1 + m2)
    cur = jnp.concatenate(gs, axis=0)
    nio = lax.broadcasted_iota(I32, (N_GROUPS, T), 0)
    gsel = jnp.zeros((N_GROUPS, T), F32)
    for _ in range(TOPK_GROUPS):
        _, gi = _first_argmax(cur, nio, N_GROUPS)
        hit = nio == gi
        gsel = jnp.where(hit, 1.0, gsel)
        cur = jnp.where(hit, neg, cur)
    emask = jnp.concatenate([jnp.broadcast_to(gsel[g:g + 1], (gsz, T)) for g in range(N_GROUPS)], axis=0) > 0.0

    cur = jnp.where(emask, biased, neg)
    idxs, ws = [], []
    sel = jnp.zeros((E, T), F32)
    for _ in range(TOP_K):
        _, ei = _first_argmax(cur, eio, E)
        hit = eio == ei
        idxs.append(ei)
        ws.append(jnp.sum(jnp.where(hit, s, 0.0), axis=0, keepdims=True))
        sel = jnp.where(hit, 1.0, sel)
        cur = jnp.where(hit, neg, cur)
    w = jnp.concatenate(ws, axis=0)
    w = w / jnp.sum(w, axis=0, keepdims=True) * ROUTED_SCALE
    idx_ref[...] = jnp.concatenate(idxs, axis=0)

    wpad = jnp.concatenate([w, jnp.zeros((LANES - TOP_K, T), F32)], axis=0)
    wt_ref[...] = wpad.T

    r = lax.broadcasted_iota(I32, (T, T), 0)
    c = lax.broadcasted_iota(I32, (T, T), 1)
    before = (r < c).astype(BF16)
    cnt = _dot(sel.astype(BF16), before) + carry_ref[...]
    ranks = [jnp.sum(jnp.where(eio == idxs[k], cnt, 0.0), axis=0, keepdims=True) for k in range(TOP_K)]
    rank_ref[...] = jnp.concatenate(ranks, axis=0).astype(I32)
    total = cnt[:, T - 1:T] + sel[:, T - 1:T]
    carry_ref[...] = total
    cnt_ref[...] = jnp.broadcast_to(total, cnt_ref.shape).astype(I32)


def _route(logits_t, b_router):
    E, N = logits_t.shape
    T = min(N, 512)
    return pl.pallas_call(
        _route_kernel,
        out_shape=(jax.ShapeDtypeStruct((TOP_K, N), I32), jax.ShapeDtypeStruct((N, LANES), F32),
                   jax.ShapeDtypeStruct((TOP_K, N), I32), jax.ShapeDtypeStruct((E, LANES), I32)),
        grid=(N // T,),
        in_specs=[pl.BlockSpec((E, T), lambda i: (0, i)), pl.BlockSpec((E, 1), lambda i: (0, 0))],
        out_specs=(pl.BlockSpec((TOP_K, T), lambda i: (0, i)), pl.BlockSpec((T, LANES), lambda i: (i, 0)),
                   pl.BlockSpec((TOP_K, T), lambda i: (0, i)), pl.BlockSpec((E, LANES), lambda i: (0, 0))),
        scratch_shapes=[pltpu.VMEM((E, 1), F32)],
        compiler_params=_cparams("arbitrary"),
        name="route",
    )(logits_t, b_router.reshape(E, 1).astype(F32))


def _dest_kernel(idx_ref, rank_ref, ps_ref, dest_ref):
    K, T = idx_ref.shape
    E = ps_ref.shape[0]
    eio = lax.broadcasted_iota(I32, (E, T), 0)
    ps = ps_ref[...]
    rows = [jnp.sum(jnp.where(eio == idx_ref[k:k + 1, :], ps, 0), axis=0, keepdims=True) for k in range(K)]
    dest_ref[...] = jnp.concatenate(rows, axis=0) + rank_ref[...]


def _dest_slots(idx, rank, pad_start):
    K, N = idx.shape
    E = pad_start.shape[0]
    T = min(N, 512)
    spec = pl.BlockSpec((K, T), lambda i: (0, i))
    return pl.pallas_call(
        _dest_kernel,
        out_shape=jax.ShapeDtypeStruct((K, N), I32),
        grid=(N // T,),
        in_specs=[spec, spec, pl.BlockSpec((E, 1), lambda i: (0, 0))],
        out_specs=spec,
        compiler_params=_cparams("arbitrary"),
        name="dest_slots",
    )(idx, rank, pad_start.reshape(E, 1))


SC_ROWS = 64


def _sc_workers():
    info = plsc.get_sparse_core_info()
    return info.num_cores, info.num_subcores


def _dispatch_rows(h2p, dest3, n_slots):
    N, W = h2p.shape
    n_chunks, K, R = dest3.shape
    nc, ns = _sc_workers()
    per_w = n_chunks // (nc * ns)
    mesh = plsc.VectorSubcoreMesh(core_axis_name="c", subcore_axis_name="s")

    @functools.partial(
        pl.kernel, mesh=mesh,
        out_type=jax.ShapeDtypeStruct((n_slots, W), I32),
        scratch_types=[pltpu.VMEM((K, R), I32), pltpu.VMEM((R, W), I32), pltpu.SemaphoreType.DMA],
    )
    def k(h_hbm, d_hbm, xs_hbm, idx_v, rows_v, sem):
        wid = lax.axis_index("s") * nc + lax.axis_index("c")

        @pl.loop(0, per_w)
        def _(j):
            ch = wid * per_w + j
            pltpu.sync_copy(d_hbm.at[ch], idx_v)
            pltpu.sync_copy(h_hbm.at[pl.ds(ch * R, R)], rows_v)
            copies = [pltpu.async_copy(rows_v, xs_hbm.at[idx_v.at[kk]], sem) for kk in range(K)]
            for cp in copies:
                cp.wait()

    return k(h2p, dest3)


def _combine_rows(ys, dest3, N):
    n_slots, W = ys.shape
    n_chunks, K, R = dest3.shape
    nc, ns = _sc_workers()
    per_w = n_chunks // (nc * ns)
    mesh = plsc.VectorSubcoreMesh(core_axis_name="c", subcore_axis_name="s")

    @functools.partial(
        pl.kernel, mesh=mesh,
        out_type=jax.ShapeDtypeStruct((K, N, W), I32),
        scratch_types=[pltpu.VMEM((K, R), I32), pltpu.VMEM((R, W), I32), pltpu.SemaphoreType.DMA],
    )
    def k(ys_hbm, d_hbm, yk_hbm, idx_v, rows_v, sem):
        wid = lax.axis_index("s") * nc + lax.axis_index("c")

        @pl.loop(0, per_w)
        def _(j):
            ch = wid * per_w + j
            pltpu.sync_copy(d_hbm.at[ch], idx_v)
            for kk in range(K):
                pltpu.async_copy(ys_hbm.at[idx_v.at[kk]], rows_v, sem).wait()
                pltpu.sync_copy(rows_v, yk_hbm.at[kk, pl.ds(ch * R, R)])

    return k(ys, dest3)


def _expert_kernel(be_ref, nu_ref, xs_ref, w1_ref, w3_ref, w2_ref, ys_ref, w13_s, w2_s):
    i = pl.program_id(0)
    F = w1_ref.shape[2]
    prev = be_ref[jnp.maximum(i - 1, 0)]
    fresh = jnp.logical_or(i == 0, be_ref[i] != prev)

    @pl.when(jnp.logical_and(fresh, i < nu_ref[0]))
    def _():
        w13_s[:, :F] = w1_ref[0].astype(BF16)
        w13_s[:, F:] = w3_ref[0].astype(BF16)
        w2_s[...] = w2_ref[0].astype(BF16)

    @pl.when(i < nu_ref[0])
    def _():
        half = xs_ref.shape[1]
        lo, hi = _unpack_bf16_pair(xs_ref[...])
        ab = _dot(lo.astype(BF16), w13_s[:half, :]) + _dot(hi.astype(BF16), w13_s[half:, :])
        hid = (_silu(ab[:, :F]) * ab[:, F:]).astype(BF16)
        ys_ref[...] = _pack_bf16_pair(_dot(hid, w2_s[...]))


def _experts(xs, blk_e, n_used, w_e1, w_e3, w_e2):
    n_slots, W = xs.shape
    E, D, F = w_e1.shape
    bm = EXPERT_BLOCK
    n_blocks = n_slots // bm
    row = lambda i, be, nu: (jnp.minimum(i, nu[0] - 1), 0)
    grid_spec = pltpu.PrefetchScalarGridSpec(
        num_scalar_prefetch=2,
        grid=(n_blocks,),
        in_specs=[
            pl.BlockSpec((bm, W), row),
            pl.BlockSpec((1, D, F), lambda i, be, nu: (be[i], 0, 0)),
            pl.BlockSpec((1, D, F), lambda i, be, nu: (be[i], 0, 0)),
            pl.BlockSpec((1, F, D), lambda i, be, nu: (be[i], 0, 0)),
        ],
        out_specs=pl.BlockSpec((bm, W), row),
        scratch_shapes=[pltpu.VMEM((D, 2 * F), BF16), pltpu.VMEM((F, D), BF16)],
    )
    return pl.pallas_call(
        _expert_kernel,
        out_shape=jax.ShapeDtypeStruct((n_slots, W), I32),
        grid_spec=grid_spec,
        compiler_params=_cparams("arbitrary"),
        name="experts",
    )(blk_e, n_used, xs, w_e1, w_e3, w_e2)


def _final_kernel(alpha, yk_ref, wt_ref, h2_ref, x1_ref, mod_ref, ws13_ref, ws2_ref, g_ref, b_ref, o_ref):
    gt2 = mod_ref[0, 5:6, :]
    F = ws2_ref.shape[0]
    half = h2_ref.shape[1]
    wt = wt_ref[...]
    mlo = jnp.zeros(h2_ref.shape, F32)
    mhi = jnp.zeros(h2_ref.shape, F32)
    for k in range(TOP_K):
        lo, hi = _unpack_bf16_pair(yk_ref[k])
        wk = wt[:, k:k + 1]
        mlo = mlo + wk * lo
        mhi = mhi + wk * hi
    lo, hi = _unpack_bf16_pair(h2_ref[...])
    ab = _dot(lo.astype(BF16), ws13_ref[:half, :]) + _dot(hi.astype(BF16), ws13_ref[half:, :])
    hid = (_silu(ab[:, :F]) * ab[:, F:]).astype(BF16)
    ffn = jnp.concatenate([mlo, mhi], axis=1) + _dot(hid, ws2_ref[...])
    o_ref[...] = _layer_norm(alpha * x1_ref[...] + (1.0 + gt2) * ffn, g_ref[...], b_ref[...])


def _final(alpha, yk, wt, h2p, x1, mod6, ws13, ws2, ln_g, ln_b, S):
    N, D = x1.shape
    K = yk.shape[0]
    tm = min(S, 256)
    per_b = S // tm
    tok = lambda n: pl.BlockSpec((tm, n), lambda i: (i, 0))
    full = lambda a: pl.BlockSpec(a.shape, lambda i: (0,) * a.ndim)
    return pl.pallas_call(
        functools.partial(_final_kernel, alpha),
        out_shape=jax.ShapeDtypeStruct((N, D), F32),
        grid=(N // tm,),
        in_specs=[pl.BlockSpec((K, tm, D // 2), lambda i: (0, i, 0)), tok(LANES), tok(D // 2), tok(D),
                  pl.BlockSpec((1, 6, D), lambda i: (i // per_b, 0, 0)),
                  full(ws13), full(ws2), full(ln_g), full(ln_b)],
        out_specs=tok(D),
        compiler_params=_cparams("arbitrary"),
        name="combine_ln2",
    )(yk, wt, h2p, x1, mod6, ws13, ws2, ln_g, ln_b)


def _moe(alpha, h2p, logits_t, x1, mod6, b_router, w_e1, w_e3, w_e2, w_s1, w_s3, w_s2, ln_g, ln_b, S):
    N = x1.shape[0]
    E = w_e1.shape[0]
    bm = EXPERT_BLOCK
    idx, wt, rank, cnt = _route(logits_t, b_router)
    counts = cnt[:, 0]
    padded = (counts + bm - 1) // bm * bm
    pad_end = jnp.cumsum(padded)
    pad_start = pad_end - padded
    n_slots = N * TOP_K + E * bm
    n_blocks = n_slots // bm
    blk_start = jnp.arange(n_blocks, dtype=I32) * bm
    blk_e = jnp.minimum(jnp.searchsorted(pad_end, blk_start, side="right"), E - 1).astype(I32)
    n_used = (pad_end[-1:] // bm).astype(I32)
    dest = _dest_slots(idx, rank, pad_start.astype(I32))
    dest3 = dest.reshape(TOP_K, N // SC_ROWS, SC_ROWS).transpose(1, 0, 2)
    xs = _dispatch_rows(h2p, dest3, n_slots)
    ys = _experts(xs, blk_e, n_used, w_e1, w_e3, w_e2)
    yk = _combine_rows(ys, dest3, N)
    ws13 = jnp.concatenate([w_s1, w_s3], axis=1).astype(BF16)
    return _final(alpha, yk, wt, h2p, x1, mod6, ws13, w_s2.astype(BF16), ln_g, ln_b, S)


def kernel(x, c, positions, w_ada, b_ada, w_in, w_gla_a2, b_gla_a, g_gla_norm, w_gla_o, g_cq, w_uq, g_ckv, w_ukv, w_mla_o, w_out, ln1_g, ln1_b, w_router, b_router, w_e1, w_e3, w_e2, w_s1, w_s3, w_s2, ln2_g, ln2_b):
    B, S, D = x.shape
    N = B * S
    depth = w_ada.shape[0]
    alpha = (2.0 * depth) ** 0.25
    row = lambda a: a.reshape(1, -1)
    cc, ss = _rope_tables(positions)
    x2 = x.reshape(N, D)
    for l in range(depth):
        mod6 = _modulation(c, w_ada[l], b_ada[l]).reshape(B, 6, D)
        qk, gv, gr, cq, ckv, gate_a, gate_b, tail = _in_projection(x2, mod6, _prep_w_in(w_in[l], D), S)
        wa2_p = jnp.concatenate(
            [w_gla_a2[l], jnp.zeros((LANES - GLA_GATE_RANK, w_gla_a2.shape[2]), F32)], axis=0).astype(BF16)
        y_a = _gla(qk, gv, gr, tail, wa2_p, row(b_gla_a[l]), row(g_gla_norm[l]), w_gla_o[l].astype(BF16), B, S)
        qc, kc, vv = _mla_prep(cq, ckv, tail, cc, ss, row(g_cq[l]), _prep_w_uq(w_uq[l]),
                               row(g_ckv[l]), _prep_w_ukv(w_ukv[l]))
        y_b = _mla_attention(qc, kc, vv, w_mla_o[l].astype(BF16), B, S)
        x1, h2p, logits_t = _merge(alpha, x2, y_a, y_b, gate_a, gate_b, mod6, w_out[l].astype(BF16),
                                   row(ln1_g[l]), row(ln1_b[l]), w_router[l].T.astype(BF16), S)
        x2 = _moe(alpha, h2p, logits_t, x1, mod6, b_router[l], w_e1[l], w_e3[l], w_e2[l],
                  w_s1[l], w_s3[l], w_s2[l], row(ln2_g[l]), row(ln2_b[l]), S)
    return x2.reshape(B, S, D)
```

```python
import functools

import jax
import jax.numpy as jnp
from jax import lax
from jax.experimental import pallas as pl
from jax.experimental.pallas import tpu as pltpu
from jax.experimental.pallas import tpu_sc as plsc

CHUNK = 64
GLA_HEADS = 4
GLA_DK = 128
GLA_DV = 256
GLA_GATE_RANK = 16
GLA_GATE_TAU = 16.0
MLA_HEADS = 8
MLA_Q_RANK = 768
MLA_KV_RANK = 256
MLA_NOPE = 128
MLA_ROPE = 64
MLA_V = 128
ROPE_THETA = 10000.0
N_EXPERTS = 256
TOP_K = 8
N_GROUPS = 8
TOPK_GROUPS = 4
D_EXPERT = 256
ROUTED_SCALE = 2.5
LN_EPS = 1e-5
RMS_EPS = 1e-6
LOG2E = 1.4426950408889634

LANES = 128
VMEM_LIMIT = 56 * 1024 * 1024
EXPERT_BLOCK = 256

F32 = jnp.float32
BF16 = jnp.bfloat16
I32 = jnp.int32


def _cparams(*sem):
    return pltpu.CompilerParams(dimension_semantics=sem, vmem_limit_bytes=VMEM_LIMIT)


def _sigmoid(x):
    return 1.0 / (1.0 + jnp.exp(-x))


def _silu(x):
    return x * _sigmoid(x)


def _dot(a, b):
    return jnp.dot(a, b, preferred_element_type=F32)


def _dot_nt(a, b):
    return lax.dot_general(a, b, (((1,), (1,)), ((), ())), preferred_element_type=F32)


def _dot_tn(a, b):
    return lax.dot_general(a, b, (((0,), (0,)), ((), ())), preferred_element_type=F32)


def _pack_bf16_pair(x):
    w = x.shape[1] // 2
    u = lax.bitcast_convert_type(x.astype(BF16).astype(F32), I32)
    lo = lax.shift_right_logical(u[:, :w], jnp.int32(16))
    hi = jnp.bitwise_and(u[:, w:], jnp.int32(-65536))
    return jnp.bitwise_or(lo, hi)


def _unpack_bf16_pair(p):
    lo = lax.bitcast_convert_type(lax.shift_left(p, jnp.int32(16)), F32)
    hi = lax.bitcast_convert_type(jnp.bitwise_and(p, jnp.int32(-65536)), F32)
    return lo, hi


def _mod_kernel(c_ref, w_ref, b_ref, o_ref):
    cond = _silu(c_ref[...]).astype(BF16)
    o_ref[...] = _dot(cond, w_ref[...].astype(BF16)) + b_ref[...]


def _modulation(c, w_ada, b_ada):
    B, D = c.shape
    W = w_ada.shape[1]
    tn = D
    return pl.pallas_call(
        _mod_kernel,
        out_shape=jax.ShapeDtypeStruct((B, W), F32),
        grid=(W // tn,),
        in_specs=[
            pl.BlockSpec((B, D), lambda j: (0, 0)),
            pl.BlockSpec((D, tn), lambda j: (0, j)),
            pl.BlockSpec((1, tn), lambda j: (0, j)),
        ],
        out_specs=pl.BlockSpec((B, tn), lambda j: (0, j)),
        compiler_params=_cparams("arbitrary"),
        name="adaln_mod",
    )(c, w_ada, b_ada.reshape(1, W))


def _rope_kernel(pos_ref, f_ref, ph_ref, cc_ref, ss_ref):
    ang = pos_ref[...] * f_ref[...]
    cc_ref[...] = jnp.cos(ang)
    ss_ref[...] = jnp.sin(ang) * ph_ref[...]


def _rope_tables(positions):
    B, S = positions.shape
    N = B * S
    half = MLA_ROPE // 2
    inv_freq = ROPE_THETA ** (-jnp.arange(half, dtype=F32) * (2.0 / MLA_ROPE))
    f4 = jnp.tile(inv_freq, LANES // half).reshape(1, LANES)
    sign = jnp.tile(jnp.concatenate([-jnp.ones((half,), F32), jnp.ones((half,), F32)]), LANES // MLA_ROPE)
    pos = jnp.broadcast_to(positions.reshape(N, 1).astype(F32), (N, LANES))
    tm = min(N, 1024)
    spec = pl.BlockSpec((tm, LANES), lambda i: (i, 0))
    vec = pl.BlockSpec((1, LANES), lambda i: (0, 0))
    return pl.pallas_call(
        _rope_kernel,
        out_shape=(jax.ShapeDtypeStruct((N, LANES), F32), jax.ShapeDtypeStruct((N, LANES), F32)),
        grid=(N // tm,),
        in_specs=[spec, vec, vec],
        out_specs=(spec, spec),
        compiler_params=_cparams("arbitrary"),
        name="rope_tables",
    )(pos, f4, sign.reshape(1, LANES))


QK_W = 2 * GLA_HEADS * GLA_DK
GV_W = GLA_HEADS * GLA_DV
TAIL_W = 3 * LANES
IN_SEG = (QK_W, GV_W, GV_W, MLA_Q_RANK, MLA_KV_RANK, 1024, 1024, TAIL_W)


def _prep_w_in(w_in, D):
    s = [GLA_HEADS * GLA_DK, GLA_HEADS * GLA_DK, GV_W, GV_W, GLA_GATE_RANK, MLA_Q_RANK, MLA_KV_RANK, MLA_ROPE, D, D]
    offs = [0]
    for n in s:
        offs.append(offs[-1] + n)
    gq, gk, gv, gr, ga, cq, ckv, kr, gate_a, gate_b = [w_in[:, offs[i]:offs[i + 1]] for i in range(10)]
    half = MLA_ROPE // 2
    kr_sw = jnp.concatenate([kr[:, half:], kr[:, :half]], axis=1)
    pad = jnp.zeros((w_in.shape[0], LANES - GLA_GATE_RANK), w_in.dtype)
    return jnp.concatenate([gq, gk, gv, gr, cq, ckv, gate_a, gate_b, kr, kr, kr_sw, kr_sw, ga, pad], axis=1).astype(BF16)


def _inproj_kernel(x_ref, mod_ref, w_ref, *out_refs):
    sh1 = mod_ref[0, 0:1, :]
    sc1 = mod_ref[0, 1:2, :]
    h = (x_ref[...] * (1.0 + sc1) + sh1).astype(BF16)
    off = 0
    for ref in out_refs:
        n = ref.shape[-1]
        ref[...] = _dot(h, w_ref[:, off:off + n]).astype(ref.dtype)
        off += n


def _in_projection(x2, mod6, w_in_p, S):
    N, D = x2.shape
    W = w_in_p.shape[1]
    tm = min(S, 512)
    per_b = S // tm
    return pl.pallas_call(
        _inproj_kernel,
        out_shape=tuple(jax.ShapeDtypeStruct((N, n), BF16) for n in IN_SEG),
        grid=(N // tm,),
        in_specs=[
            pl.BlockSpec((tm, D), lambda i: (i, 0)),
            pl.BlockSpec((1, 6, D), lambda i: (i // per_b, 0, 0)),
            pl.BlockSpec((D, W), lambda i: (0, 0), pipeline_mode=pl.Buffered(1)),
        ],
        out_specs=tuple(pl.BlockSpec((tm, n), lambda i: (i, 0)) for n in IN_SEG),
        compiler_params=_cparams("arbitrary"),
        name="in_proj",
    )(x2, mod6, w_in_p)


def _gla_kernel(qk_ref, v_ref, gr_ref, tail_ref, wa2_ref, ba_ref, gn_ref, wo_ref, y_ref, st_ref, o_ref):
    t = pl.program_id(1)

    @pl.when(t == 0)
    def _():
        st_ref[...] = jnp.zeros_like(st_ref)

    ts = qk_ref.shape[0]
    HK = GLA_HEADS * GLA_DK
    r = lax.broadcasted_iota(I32, (CHUNK, CHUNK), 0)
    c = lax.broadcasted_iota(I32, (CHUNK, CHUNK), 1)
    tri = (r >= c).astype(BF16)
    qscale = GLA_DK ** -0.5

    def chunk_body(ci, carry):
        rows = pl.ds(pl.multiple_of(ci * CHUNK, CHUNK), CHUNK)
        ga = tail_ref[rows, 2 * LANES:3 * LANES]
        z = _dot(ga, wa2_ref[...]) + ba_ref[...]
        log_a = (jnp.minimum(z, 0.0) - jnp.log(1.0 + jnp.exp(-jnp.abs(z)))) * (1.0 / GLA_GATE_TAU)
        la_hi = log_a.astype(BF16)
        la_lo = (log_a - la_hi.astype(F32)).astype(BF16)
        G = _dot(tri, la_hi) + _dot(tri, la_lo)
        g_end = G[CHUNK - 1:CHUNK, :]
        kdec_all = qk_ref[rows, HK:2 * HK].astype(F32) * jnp.exp(g_end - G)
        dec_all = jnp.exp(g_end)
        for h in range(GLA_HEADS):
            ks = slice(h * GLA_DK, (h + 1) * GLA_DK)
            vs = slice(h * GLA_DV, (h + 1) * GLA_DV)
            kd = kdec_all[:, ks].astype(BF16)
            vh = v_ref[rows, vs]
            st = st_ref[h] * dec_all[:, ks] + _dot_tn(vh, kd)
            st_ref[h] = st
            qh = (qk_ref[rows, ks].astype(F32) * qscale).astype(BF16)
            o = _dot_nt(qh, st.astype(BF16))
            o = o * lax.rsqrt(jnp.mean(o * o, axis=-1, keepdims=True) + RMS_EPS) * gn_ref[...]
            o_ref[rows, vs] = (o * _silu(gr_ref[rows, vs].astype(F32))).astype(BF16)
        return carry

    lax.fori_loop(0, ts // CHUNK, chunk_body, 0)
    y_ref[...] = _dot(o_ref[...], wo_ref[...]).astype(y_ref.dtype)


def _gla(qk, gv, gr, tail, wa2_p, b_a, g_norm, w_o, B, S):
    N = B * S
    D = w_o.shape[1]
    ts = min(S, 512)
    per_b = S // ts
    HK = GLA_HEADS * GLA_DK
    tok = lambda n: pl.BlockSpec((ts, n), lambda b, t: (b * per_b + t, 0))
    full = lambda a: pl.BlockSpec(a.shape, lambda b, t: (0,) * a.ndim)
    return pl.pallas_call(
        _gla_kernel,
        out_shape=jax.ShapeDtypeStruct((N, D), BF16),
        grid=(B, per_b),
        in_specs=[tok(QK_W), tok(GV_W), tok(GV_W), tok(TAIL_W), full(wa2_p), full(b_a), full(g_norm), full(w_o)],
        out_specs=tok(D),
        scratch_shapes=[pltpu.VMEM((GLA_HEADS, GLA_DV, GLA_DK), F32), pltpu.VMEM((ts, GV_W), BF16)],
        compiler_params=_cparams("arbitrary", "arbitrary"),
        name="gla",
    )(qk, gv, gr, tail, wa2_p, b_a, g_norm, w_o)


HQ = MLA_HEADS * 2 * LANES


def _prep_w_uq(w_uq):
    dh = MLA_NOPE + MLA_ROPE
    half = MLA_ROPE // 2
    nope = [w_uq[:, h * dh:h * dh + MLA_NOPE] for h in range(MLA_HEADS)]
    rope = [w_uq[:, h * dh + MLA_NOPE:(h + 1) * dh] for h in range(MLA_HEADS)]
    rope_sw = [jnp.concatenate([r[:, half:], r[:, :half]], axis=1) for r in rope]
    return jnp.concatenate(nope + rope + rope_sw, axis=1).astype(BF16)


def _prep_w_ukv(w_ukv):
    dh = MLA_NOPE + MLA_V
    kn = [w_ukv[:, h * dh:h * dh + MLA_NOPE] for h in range(MLA_HEADS)]
    vv = [w_ukv[:, h * dh + MLA_NOPE:(h + 1) * dh] for h in range(MLA_HEADS)]
    return jnp.concatenate(kn + vv, axis=1).astype(BF16)


def _rms(x, g):
    return x * lax.rsqrt(jnp.mean(x * x, axis=-1, keepdims=True) + RMS_EPS) * g


def _mla_prep_kernel(cq_ref, ckv_ref, tail_ref, cc_ref, ss_ref, gq_ref, wq_ref, gkv_ref, wkv_ref, q_ref, k_ref, v_ref):
    tm = cq_ref.shape[0]
    NP = MLA_HEADS * MLA_NOPE
    RP = MLA_HEADS * MLA_ROPE
    scale = (MLA_NOPE + MLA_ROPE) ** -0.5 * LOG2E
    cc = cc_ref[...]
    ss = ss_ref[...]
    cqn = _rms(cq_ref[...].astype(F32), gq_ref[...]).astype(BF16)
    qf = _dot(cqn, wq_ref[...]) * scale
    ckvn = _rms(ckv_ref[...].astype(F32), gkv_ref[...]).astype(BF16)
    kv = _dot(ckvn, wkv_ref[...])
    krr = (tail_ref[:, 0:LANES].astype(F32) * cc + tail_ref[:, LANES:2 * LANES].astype(F32) * ss).astype(BF16)
    lane = lax.broadcasted_iota(I32, (tm, LANES), 1)
    first = lane < MLA_ROPE
    for j in range(MLA_HEADS // 2):
        a = NP + j * LANES
        rot = qf[:, a:a + LANES] * cc + qf[:, a + RP:a + RP + LANES] * ss
        for h, keep in ((2 * j, first), (2 * j + 1, jnp.logical_not(first))):
            base = h * 2 * LANES
            q_ref[:, base:base + LANES] = qf[:, h * MLA_NOPE:(h + 1) * MLA_NOPE].astype(BF16)
            q_ref[:, base + LANES:base + 2 * LANES] = jnp.where(keep, rot, 0.0).astype(BF16)
            k_ref[:, base:base + LANES] = kv[:, h * MLA_NOPE:(h + 1) * MLA_NOPE].astype(BF16)
            k_ref[:, base + LANES:base + 2 * LANES] = krr
    v_ref[...] = kv[:, NP:].astype(BF16)


def _mla_prep(cq, ckv, tail, cc, ss, g_cq, w_uq_p, g_ckv, w_ukv_p):
    N = cq.shape[0]
    tm = min(N, 512)
    tok = lambda n: pl.BlockSpec((tm, n), lambda i: (i, 0))
    full = lambda a: pl.BlockSpec(a.shape, lambda i: (0,) * a.ndim)
    HV = MLA_HEADS * MLA_V
    return pl.pallas_call(
        _mla_prep_kernel,
        out_shape=(jax.ShapeDtypeStruct((N, HQ), BF16), jax.ShapeDtypeStruct((N, HQ), BF16),
                   jax.ShapeDtypeStruct((N, HV), BF16)),
        grid=(N // tm,),
        in_specs=[tok(MLA_Q_RANK), tok(MLA_KV_RANK), tok(TAIL_W), tok(LANES), tok(LANES),
                  full(g_cq), full(w_uq_p), full(g_ckv), full(w_ukv_p)],
        out_specs=(tok(HQ), tok(HQ), tok(HV)),
        compiler_params=_cparams("arbitrary"),
        name="mla_prep",
    )(cq, ckv, tail, cc, ss, g_cq, w_uq_p, g_ckv, w_ukv_p)


ATTN_TQ = 256


def _mla_attn_kernel(q_ref, k_ref, v_ref, o_ref):
    S = q_ref.shape[0]
    tq = min(S, ATTN_TQ)
    r = lax.broadcasted_iota(I32, (tq, tq), 0) // CHUNK
    c = lax.broadcasted_iota(I32, (tq, tq), 1) // CHUNK
    diag_mask = c <= r
    for ii in range(S // tq):
        l0 = ii * tq
        q = q_ref[l0:l0 + tq, :]
        sd = jnp.where(diag_mask, _dot_nt(q, k_ref[l0:l0 + tq, :]), -jnp.inf)
        m = jnp.max(sd, axis=-1, keepdims=True)
        if ii > 0:
            so = _dot_nt(q, k_ref[0:l0, :])
            m = jnp.maximum(m, jnp.max(so, axis=-1, keepdims=True))
            po = jnp.exp2(so - m)
            l = jnp.sum(po, axis=-1, keepdims=True)
            acc = _dot(po.astype(BF16), v_ref[0:l0, :])
        pd = jnp.exp2(sd - m)
        if ii > 0:
            l = l + jnp.sum(pd, axis=-1, keepdims=True)
            acc = acc + _dot(pd.astype(BF16), v_ref[l0:l0 + tq, :])
        else:
            l = jnp.sum(pd, axis=-1, keepdims=True)
            acc = _dot(pd.astype(BF16), v_ref[l0:l0 + tq, :])
        o_ref[l0:l0 + tq, :] = (acc / l).astype(BF16)


def _mla_attention(qc, kc, vv, B, S):
    N = B * S
    HV = MLA_HEADS * MLA_V
    return pl.pallas_call(
        _mla_attn_kernel,
        out_shape=jax.ShapeDtypeStruct((N, HV), BF16),
        grid=(B, MLA_HEADS),
        in_specs=[
            pl.BlockSpec((S, 2 * LANES), lambda b, h: (b, h)),
            pl.BlockSpec((S, 2 * LANES), lambda b, h: (b, h)),
            pl.BlockSpec((S, MLA_V), lambda b, h: (b, h)),
        ],
        out_specs=pl.BlockSpec((S, MLA_V), lambda b, h: (b, h)),
        compiler_params=_cparams("arbitrary", "arbitrary"),
        name="mla_attn",
    )(qc, kc, vv)


def _layer_norm(u, g, b):
    mu = jnp.mean(u, axis=-1, keepdims=True)
    d = u - mu
    var = jnp.mean(d * d, axis=-1, keepdims=True)
    return d * lax.rsqrt(var + LN_EPS) * g + b


def _merge_kernel(alpha, x_ref, ya_ref, ob_ref, ga_ref, gb_ref, mod_ref, wmo_ref, wout_ref, g_ref, b_ref, wr_ref,
                  x1_ref, h2_ref, lg_ref):
    gt1 = mod_ref[0, 2:3, :]
    sh2 = mod_ref[0, 3:4, :]
    sc2 = mod_ref[0, 4:5, :]
    y = (_sigmoid(ga_ref[...].astype(F32)) * ya_ref[...].astype(F32)
         + _sigmoid(gb_ref[...].astype(F32)) * _dot(ob_ref[...], wmo_ref[...]))
    mix = _dot(y.astype(BF16), wout_ref[...])
    x1 = _layer_norm(alpha * x_ref[...] + (1.0 + gt1) * mix, g_ref[...], b_ref[...])
    x1_ref[...] = x1
    h2 = x1 * (1.0 + sc2) + sh2
    h2_ref[...] = _pack_bf16_pair(h2)
    lg_ref[...] = _dot_nt(wr_ref[...], h2.astype(BF16))


def _merge(alpha, x2, y_a, o_b, gate_a, gate_b, mod6, w_mla_o, w_out, ln_g, ln_b, wr_t, S):
    N, D = x2.shape
    E = wr_t.shape[0]
    tm = min(S, 512)
    per_b = S // tm
    tok = lambda n: pl.BlockSpec((tm, n), lambda i: (i, 0))
    full = lambda a: pl.BlockSpec(a.shape, lambda i: (0,) * a.ndim)
    return pl.pallas_call(
        functools.partial(_merge_kernel, alpha),
        out_shape=(jax.ShapeDtypeStruct((N, D), F32), jax.ShapeDtypeStruct((N, D // 2), I32),
                   jax.ShapeDtypeStruct((E, N), F32)),
        grid=(N // tm,),
        in_specs=[tok(D), tok(D), tok(D), tok(D), tok(D),
                  pl.BlockSpec((1, 6, D), lambda i: (i // per_b, 0, 0)),
                  full(w_mla_o), full(w_out), full(ln_g), full(ln_b), full(wr_t)],
        out_specs=(tok(D), tok(D // 2), pl.BlockSpec((E, tm), lambda i: (0, i))),
        compiler_params=_cparams("arbitrary"),
        name="merge_ln1",
    )(x2, y_a, o_b, gate_a, gate_b, mod6, w_mla_o, w_out, ln_g, ln_b, wr_t)


def _first_argmax(v, io, n):
    m = jnp.max(v, axis=0, keepdims=True)
    idx = jnp.min(jnp.where(v == m, io, n), axis=0, keepdims=True)
    return m, idx


def _route_kernel(lg_ref, br_ref, idx_ref, wt_ref, rank_ref, cnt_ref, carry_ref):
    step = pl.program_id(0)

    @pl.when(step == 0)
    def _():
        carry_ref[...] = jnp.zeros_like(carry_ref)

    E, T = lg_ref.shape
    gsz = E // N_GROUPS
    neg = -jnp.inf
    s = _sigmoid(lg_ref[...])
    biased = s + br_ref[...]
    eio = lax.broadcasted_iota(I32, (E, T), 0)
    gio = lax.broadcasted_iota(I32, (gsz, T), 0)

    gs = []
    for g in range(N_GROUPS):
        blk = biased[g * gsz:(g + 1) * gsz]
        m1, i1 = _first_argmax(blk, gio, gsz)
        m2 = jnp.max(jnp.where(gio == i1, neg, blk), axis=0, keepdims=True)
        gs.append(m1 + m2)
    cur = jnp.concatenate(gs, axis=0)
    nio = lax.broadcasted_iota(I32, (N_GROUPS, T), 0)
    gsel = jnp.zeros((N_GROUPS, T), F32)
    for _ in range(TOPK_GROUPS):
        _, gi = _first_argmax(cur, nio, N_GROUPS)
        hit = nio == gi
        gsel = jnp.where(hit, 1.0, gsel)
        cur = jnp.where(hit, neg, cur)
    emask = jnp.concatenate([jnp.broadcast_to(gsel[g:g + 1], (gsz, T)) for g in range(N_GROUPS)], axis=0) > 0.0

    cur = jnp.where(emask, biased, neg)
    idxs, ws = [], []
    sel = jnp.zeros((E, T), F32)
    for _ in range(TOP_K):
        _, ei = _first_argmax(cur, eio, E)
        hit = eio == ei
        idxs.append(ei)
        ws.append(jnp.sum(jnp.where(hit, s, 0.0), axis=0, keepdims=True))
        sel = jnp.where(hit, 1.0, sel)
        cur = jnp.where(hit, neg, cur)
    w = jnp.concatenate(ws, axis=0)
    w = w / jnp.sum(w, axis=0, keepdims=True) * ROUTED_SCALE
    idx_ref[...] = jnp.concatenate(idxs, axis=0)

    wpad = jnp.concatenate([w, jnp.zeros((LANES - TOP_K, T), F32)], axis=0)
    wt_ref[...] = wpad.T

    r = lax.broadcasted_iota(I32, (T, T), 0)
    c = lax.broadcasted_iota(I32, (T, T), 1)
    before = (r < c).astype(BF16)
    cnt = _dot(sel.astype(BF16), before) + carry_ref[...]
    ranks = [jnp.sum(jnp.where(eio == idxs[k], cnt, 0.0), axis=0, keepdims=True) for k in range(TOP_K)]
    rank_ref[...] = jnp.concatenate(ranks, axis=0).astype(I32)
    total = cnt[:, T - 1:T] + sel[:, T - 1:T]
    carry_ref[...] = total
    cnt_ref[...] = jnp.broadcast_to(total, cnt_ref.shape).astype(I32)


def _route(logits_t, b_router):
    E, N = logits_t.shape
    T = min(N, 512)
    return pl.pallas_call(
        _route_kernel,
        out_shape=(jax.ShapeDtypeStruct((TOP_K, N), I32), jax.ShapeDtypeStruct((N, LANES), F32),
                   jax.ShapeDtypeStruct((TOP_K, N), I32), jax.ShapeDtypeStruct((E, LANES), I32)),
        grid=(N // T,),
        in_specs=[pl.BlockSpec((E, T), lambda i: (0, i)), pl.BlockSpec((E, 1), lambda i: (0, 0))],
        out_specs=(pl.BlockSpec((TOP_K, T), lambda i: (0, i)), pl.BlockSpec((T, LANES), lambda i: (i, 0)),
                   pl.BlockSpec((TOP_K, T), lambda i: (0, i)), pl.BlockSpec((E, LANES), lambda i: (0, 0))),
        scratch_shapes=[pltpu.VMEM((E, 1), F32)],
        compiler_params=_cparams("arbitrary"),
        name="route",
    )(logits_t, b_router.reshape(E, 1).astype(F32))


def _dest_kernel(idx_ref, rank_ref, ps_ref, dest_ref):
    K, T = idx_ref.shape
    E = ps_ref.shape[0]
    eio = lax.broadcasted_iota(I32, (E, T), 0)
    ps = ps_ref[...]
    rows = [jnp.sum(jnp.where(eio == idx_ref[k:k + 1, :], ps, 0), axis=0, keepdims=True) for k in range(K)]
    dest_ref[...] = jnp.concatenate(rows, axis=0) + rank_ref[...]


def _dest_slots(idx, rank, pad_start):
    K, N = idx.shape
    E = pad_start.shape[0]
    T = min(N, 512)
    spec = pl.BlockSpec((K, T), lambda i: (0, i))
    return pl.pallas_call(
        _dest_kernel,
        out_shape=jax.ShapeDtypeStruct((K, N), I32),
        grid=(N // T,),
        in_specs=[spec, spec, pl.BlockSpec((E, 1), lambda i: (0, 0))],
        out_specs=spec,
        compiler_params=_cparams("arbitrary"),
        name="dest_slots",
    )(idx, rank, pad_start.reshape(E, 1))


SC_ROWS = 64


def _sc_workers():
    info = plsc.get_sparse_core_info()
    return info.num_cores, info.num_subcores


def _dispatch_rows(h2p, dest3, n_slots):
    N, W = h2p.shape
    n_chunks, K, R = dest3.shape
    nc, ns = _sc_workers()
    per_w = n_chunks // (nc * ns)
    mesh = plsc.VectorSubcoreMesh(core_axis_name="c", subcore_axis_name="s")

    @functools.partial(
        pl.kernel, mesh=mesh,
        out_type=jax.ShapeDtypeStruct((n_slots, W), I32),
        scratch_types=[pltpu.VMEM((K, R), I32), pltpu.VMEM((R, W), I32), pltpu.SemaphoreType.DMA],
    )
    def k(h_hbm, d_hbm, xs_hbm, idx_v, rows_v, sem):
        wid = lax.axis_index("s") * nc + lax.axis_index("c")

        @pl.loop(0, per_w)
        def _(j):
            ch = wid * per_w + j
            pltpu.sync_copy(d_hbm.at[ch], idx_v)
            pltpu.sync_copy(h_hbm.at[pl.ds(ch * R, R)], rows_v)
            copies = [pltpu.async_copy(rows_v, xs_hbm.at[idx_v.at[kk]], sem) for kk in range(K)]
            for cp in copies:
                cp.wait()

    return k(h2p, dest3)


def _combine_rows(ys, dest3, N):
    n_slots, W = ys.shape
    n_chunks, K, R = dest3.shape
    nc, ns = _sc_workers()
    per_w = n_chunks // (nc * ns)
    mesh = plsc.VectorSubcoreMesh(core_axis_name="c", subcore_axis_name="s")

    @functools.partial(
        pl.kernel, mesh=mesh,
        out_type=jax.ShapeDtypeStruct((K, N, W), I32),
        scratch_types=[pltpu.VMEM((K, R), I32), pltpu.VMEM((R, W), I32), pltpu.SemaphoreType.DMA],
    )
    def k(ys_hbm, d_hbm, yk_hbm, idx_v, rows_v, sem):
        wid = lax.axis_index("s") * nc + lax.axis_index("c")

        @pl.loop(0, per_w)
        def _(j):
            ch = wid * per_w + j
            pltpu.sync_copy(d_hbm.at[ch], idx_v)
            for kk in range(K):
                pltpu.async_copy(ys_hbm.at[idx_v.at[kk]], rows_v, sem).wait()
                pltpu.sync_copy(rows_v, yk_hbm.at[kk, pl.ds(ch * R, R)])

    return k(ys, dest3)


def _expert_kernel(b0_ref, nb_ref, xs_hbm, w1_ref, w3_ref, w2_ref, ys_hbm, w13_s, w2_s, xbuf, ybuf, sem_in, sem_out):
    e = pl.program_id(0)
    n_exp = pl.num_programs(0)
    F = w1_ref.shape[2]
    bm = xbuf.shape[1]
    half = xbuf.shape[2]
    nb = nb_ref[e]
    b0 = b0_ref[e]
    total = b0_ref[n_exp - 1] + nb_ref[n_exp - 1]

    def in_copy(g):
        slot = g % 2
        return pltpu.make_async_copy(xs_hbm.at[pl.ds(g * bm, bm)], xbuf.at[slot], sem_in.at[slot])

    def out_copy(g):
        slot = g % 2
        return pltpu.make_async_copy(ybuf.at[slot], ys_hbm.at[pl.ds(g * bm, bm)], sem_out.at[slot])

    @pl.when(e == 0)
    def _():
        in_copy(0).start()

    @pl.when(nb > 0)
    def _():
        w13_s[:, :F] = w1_ref[0].astype(BF16)
        w13_s[:, F:] = w3_ref[0].astype(BF16)
        w2_s[...] = w2_ref[0].astype(BF16)

    def block(j, carry):
        g = b0 + j
        slot = g % 2
        in_copy(g).wait()

        @pl.when(g + 1 < total)
        def _():
            in_copy(g + 1).start()

        @pl.when(g >= 2)
        def _():
            out_copy(g - 2).wait()

        lo, hi = _unpack_bf16_pair(xbuf[slot])
        ab = _dot(lo.astype(BF16), w13_s[:half, :]) + _dot(hi.astype(BF16), w13_s[half:, :])
        hid = (_silu(ab[:, :F]) * ab[:, F:]).astype(BF16)
        ybuf[slot] = _pack_bf16_pair(_dot(hid, w2_s[...]))
        out_copy(g).start()
        return carry

    lax.fori_loop(0, nb, block, 0)

    @pl.when(e == n_exp - 1)
    def _():
        @pl.when(total >= 2)
        def _():
            out_copy(total - 2).wait()

        out_copy(total - 1).wait()


def _experts(xs, blk0, nblk, w_e1, w_e3, w_e2):
    n_slots, W = xs.shape
    E, D, F = w_e1.shape
    bm = EXPERT_BLOCK
    grid_spec = pltpu.PrefetchScalarGridSpec(
        num_scalar_prefetch=2,
        grid=(E,),
        in_specs=[
            pl.BlockSpec(memory_space=pl.ANY),
            pl.BlockSpec((1, D, F), lambda e, b0, nb: (e, 0, 0)),
            pl.BlockSpec((1, D, F), lambda e, b0, nb: (e, 0, 0)),
            pl.BlockSpec((1, F, D), lambda e, b0, nb: (e, 0, 0)),
        ],
        out_specs=pl.BlockSpec(memory_space=pl.ANY),
        scratch_shapes=[pltpu.VMEM((D, 2 * F), BF16), pltpu.VMEM((F, D), BF16),
                        pltpu.VMEM((2, bm, W), I32), pltpu.VMEM((2, bm, W), I32),
                        pltpu.SemaphoreType.DMA((2,)), pltpu.SemaphoreType.DMA((2,))],
    )
    return pl.pallas_call(
        _expert_kernel,
        out_shape=jax.ShapeDtypeStruct((n_slots, W), I32),
        grid_spec=grid_spec,
        compiler_params=_cparams("arbitrary"),
        name="experts",
    )(blk0, nblk, xs, w_e1, w_e3, w_e2)


def _final_kernel(alpha, yk_ref, wt_ref, h2_ref, x1_ref, mod_ref, ws13_ref, ws2_ref, g_ref, b_ref, o_ref):
    gt2 = mod_ref[0, 5:6, :]
    F = ws2_ref.shape[0]
    half = h2_ref.shape[1]
    wt = wt_ref[...]
    mlo = jnp.zeros(h2_ref.shape, F32)
    mhi = jnp.zeros(h2_ref.shape, F32)
    for k in range(TOP_K):
        lo, hi = _unpack_bf16_pair(yk_ref[k])
        wk = wt[:, k:k + 1]
        mlo = mlo + wk * lo
        mhi = mhi + wk * hi
    lo, hi = _unpack_bf16_pair(h2_ref[...])
    ab = _dot(lo.astype(BF16), ws13_ref[:half, :]) + _dot(hi.astype(BF16), ws13_ref[half:, :])
    hid = (_silu(ab[:, :F]) * ab[:, F:]).astype(BF16)
    ffn = jnp.concatenate([mlo, mhi], axis=1) + _dot(hid, ws2_ref[...])
    o_ref[...] = _layer_norm(alpha * x1_ref[...] + (1.0 + gt2) * ffn, g_ref[...], b_ref[...])


def _final(alpha, yk, wt, h2p, x1, mod6, ws13, ws2, ln_g, ln_b, S):
    N, D = x1.shape
    K = yk.shape[0]
    tm = min(S, 256)
    per_b = S // tm
    tok = lambda n: pl.BlockSpec((tm, n), lambda i: (i, 0))
    full = lambda a: pl.BlockSpec(a.shape, lambda i: (0,) * a.ndim)
    return pl.pallas_call(
        functools.partial(_final_kernel, alpha),
        out_shape=jax.ShapeDtypeStruct((N, D), F32),
        grid=(N // tm,),
        in_specs=[pl.BlockSpec((K, tm, D // 2), lambda i: (0, i, 0)), tok(LANES), tok(D // 2), tok(D),
                  pl.BlockSpec((1, 6, D), lambda i: (i // per_b, 0, 0)),
                  full(ws13), full(ws2), full(ln_g), full(ln_b)],
        out_specs=tok(D),
        compiler_params=_cparams("arbitrary"),
        name="combine_ln2",
    )(yk, wt, h2p, x1, mod6, ws13, ws2, ln_g, ln_b)


def _moe(alpha, h2p, logits_t, x1, mod6, b_router, w_e1, w_e3, w_e2, w_s1, w_s3, w_s2, ln_g, ln_b, S):
    N = x1.shape[0]
    E = w_e1.shape[0]
    bm = EXPERT_BLOCK
    idx, wt, rank, cnt = _route(logits_t, b_router)
    counts = cnt[:, 0]
    padded = (counts + bm - 1) // bm * bm
    pad_end = jnp.cumsum(padded)
    pad_start = pad_end - padded
    n_slots = N * TOP_K + E * bm
    dest = _dest_slots(idx, rank, pad_start.astype(I32))
    dest3 = dest.reshape(TOP_K, N // SC_ROWS, SC_ROWS).transpose(1, 0, 2)
    xs = _dispatch_rows(h2p, dest3, n_slots)
    ys = _experts(xs, (pad_start // bm).astype(I32), (padded // bm).astype(I32), w_e1, w_e3, w_e2)
    yk = _combine_rows(ys, dest3, N)
    ws13 = jnp.concatenate([w_s1, w_s3], axis=1).astype(BF16)
    return _final(alpha, yk, wt, h2p, x1, mod6, ws13, w_s2.astype(BF16), ln_g, ln_b, S)


def kernel(x, c, positions, w_ada, b_ada, w_in, w_gla_a2, b_gla_a, g_gla_norm, w_gla_o, g_cq, w_uq, g_ckv, w_ukv, w_mla_o, w_out, ln1_g, ln1_b, w_router, b_router, w_e1, w_e3, w_e2, w_s1, w_s3, w_s2, ln2_g, ln2_b):
    B, S, D = x.shape
    N = B * S
    depth = w_ada.shape[0]
    alpha = (2.0 * depth) ** 0.25
    row = lambda a: a.reshape(1, -1)
    cc, ss = _rope_tables(positions)
    x2 = x.reshape(N, D)
    for l in range(depth):
        mod6 = _modulation(c, w_ada[l], b_ada[l]).reshape(B, 6, D)
        qk, gv, gr, cq, ckv, gate_a, gate_b, tail = _in_projection(x2, mod6, _prep_w_in(w_in[l], D), S)
        wa2_p = jnp.concatenate(
            [w_gla_a2[l], jnp.zeros((LANES - GLA_GATE_RANK, w_gla_a2.shape[2]), F32)], axis=0).astype(BF16)
        y_a = _gla(qk, gv, gr, tail, wa2_p, row(b_gla_a[l]), row(g_gla_norm[l]), w_gla_o[l].astype(BF16), B, S)
        qc, kc, vv = _mla_prep(cq, ckv, tail, cc, ss, row(g_cq[l]), _prep_w_uq(w_uq[l]),
                               row(g_ckv[l]), _prep_w_ukv(w_ukv[l]))
        o_b = _mla_attention(qc, kc, vv, B, S)
        x1, h2p, logits_t = _merge(alpha, x2, y_a, o_b, gate_a, gate_b, mod6,
                                   w_mla_o[l].astype(BF16), w_out[l].astype(BF16),
                                   row(ln1_g[l]), row(ln1_b[l]), w_router[l].T.astype(BF16), S)
        x2 = _moe(alpha, h2p, logits_t, x1, mod6, b_router[l], w_e1[l], w_e3[l], w_e2[l],
                  w_s1[l], w_s3[l], w_s2[l], row(ln2_g[l]), row(ln2_b[l]), S)
    return x2.reshape(B, S, D)
```

```python
import functools

import jax
import jax.numpy as jnp
from jax import lax
from jax.experimental import pallas as pl
from jax.experimental.pallas import tpu as pltpu
from jax.experimental.pallas import tpu_sc as plsc

CHUNK = 64
GLA_HEADS = 4
GLA_DK = 128
GLA_DV = 256
GLA_GATE_RANK = 16
GLA_GATE_TAU = 16.0
MLA_HEADS = 8
MLA_Q_RANK = 768
MLA_KV_RANK = 256
MLA_NOPE = 128
MLA_ROPE = 64
MLA_V = 128
ROPE_THETA = 10000.0
N_EXPERTS = 256
TOP_K = 8
N_GROUPS = 8
TOPK_GROUPS = 4
D_EXPERT = 256
ROUTED_SCALE = 2.5
LN_EPS = 1e-5
RMS_EPS = 1e-6
LOG2E = 1.4426950408889634

LANES = 128
VMEM_LIMIT = 56 * 1024 * 1024
EXPERT_BLOCK = 256
EXPERT_RING = 4

F32 = jnp.float32
BF16 = jnp.bfloat16
I32 = jnp.int32


def _cparams(*sem):
    return pltpu.CompilerParams(dimension_semantics=sem, vmem_limit_bytes=VMEM_LIMIT)


def _sigmoid(x):
    return 1.0 / (1.0 + jnp.exp(-x))


def _silu(x):
    return x * _sigmoid(x)


def _dot(a, b):
    return jnp.dot(a, b, preferred_element_type=F32)


def _dot_nt(a, b):
    return lax.dot_general(a, b, (((1,), (1,)), ((), ())), preferred_element_type=F32)


def _dot_tn(a, b):
    return lax.dot_general(a, b, (((0,), (0,)), ((), ())), preferred_element_type=F32)


def _pack_bf16_pair(x):
    w = x.shape[1] // 2
    u = lax.bitcast_convert_type(x.astype(BF16).astype(F32), I32)
    lo = lax.shift_right_logical(u[:, :w], jnp.int32(16))
    hi = jnp.bitwise_and(u[:, w:], jnp.int32(-65536))
    return jnp.bitwise_or(lo, hi)


def _unpack_bf16_pair(p):
    lo = lax.bitcast_convert_type(lax.shift_left(p, jnp.int32(16)), F32)
    hi = lax.bitcast_convert_type(jnp.bitwise_and(p, jnp.int32(-65536)), F32)
    return lo, hi


def _mod_kernel(c_ref, w_ref, b_ref, o_ref):
    cond = _silu(c_ref[...]).astype(BF16)
    o_ref[...] = _dot(cond, w_ref[...].astype(BF16)) + b_ref[...]


def _modulation(c, w_ada, b_ada):
    B, D = c.shape
    W = w_ada.shape[1]
    tn = D
    return pl.pallas_call(
        _mod_kernel,
        out_shape=jax.ShapeDtypeStruct((B, W), F32),
        grid=(W // tn,),
        in_specs=[
            pl.BlockSpec((B, D), lambda j: (0, 0)),
            pl.BlockSpec((D, tn), lambda j: (0, j)),
            pl.BlockSpec((1, tn), lambda j: (0, j)),
        ],
        out_specs=pl.BlockSpec((B, tn), lambda j: (0, j)),
        compiler_params=_cparams("arbitrary"),
        name="adaln_mod",
    )(c, w_ada, b_ada.reshape(1, W))


def _rope_kernel(pos_ref, f_ref, ph_ref, cc_ref, ss_ref):
    ang = pos_ref[...] * f_ref[...]
    cc_ref[...] = jnp.cos(ang)
    ss_ref[...] = jnp.sin(ang) * ph_ref[...]


def _rope_tables(positions):
    B, S = positions.shape
    N = B * S
    half = MLA_ROPE // 2
    inv_freq = ROPE_THETA ** (-jnp.arange(half, dtype=F32) * (2.0 / MLA_ROPE))
    f4 = jnp.tile(inv_freq, LANES // half).reshape(1, LANES)
    sign = jnp.tile(jnp.concatenate([-jnp.ones((half,), F32), jnp.ones((half,), F32)]), LANES // MLA_ROPE)
    pos = jnp.broadcast_to(positions.reshape(N, 1).astype(F32), (N, LANES))
    tm = min(N, 1024)
    spec = pl.BlockSpec((tm, LANES), lambda i: (i, 0))
    vec = pl.BlockSpec((1, LANES), lambda i: (0, 0))
    return pl.pallas_call(
        _rope_kernel,
        out_shape=(jax.ShapeDtypeStruct((N, LANES), F32), jax.ShapeDtypeStruct((N, LANES), F32)),
        grid=(N // tm,),
        in_specs=[spec, vec, vec],
        out_specs=(spec, spec),
        compiler_params=_cparams("arbitrary"),
        name="rope_tables",
    )(pos, f4, sign.reshape(1, LANES))


QK_W = 2 * GLA_HEADS * GLA_DK
GV_W = GLA_HEADS * GLA_DV
TAIL_W = 3 * LANES
IN_SEG = (QK_W, GV_W, GV_W, MLA_Q_RANK, MLA_KV_RANK, 1024, 1024, TAIL_W)


def _prep_w_in(w_in, D):
    s = [GLA_HEADS * GLA_DK, GLA_HEADS * GLA_DK, GV_W, GV_W, GLA_GATE_RANK, MLA_Q_RANK, MLA_KV_RANK, MLA_ROPE, D, D]
    offs = [0]
    for n in s:
        offs.append(offs[-1] + n)
    gq, gk, gv, gr, ga, cq, ckv, kr, gate_a, gate_b = [w_in[:, offs[i]:offs[i + 1]] for i in range(10)]
    half = MLA_ROPE // 2
    kr_sw = jnp.concatenate([kr[:, half:], kr[:, :half]], axis=1)
    pad = jnp.zeros((w_in.shape[0], LANES - GLA_GATE_RANK), w_in.dtype)
    return jnp.concatenate([gq, gk, gv, gr, cq, ckv, gate_a, gate_b, kr, kr, kr_sw, kr_sw, ga, pad], axis=1).astype(BF16)


def _inproj_kernel(x_ref, mod_ref, w_ref, *out_refs):
    sh1 = mod_ref[0, 0:1, :]
    sc1 = mod_ref[0, 1:2, :]
    h = (x_ref[...] * (1.0 + sc1) + sh1).astype(BF16)
    off = 0
    for ref in out_refs:
        n = ref.shape[-1]
        ref[...] = _dot(h, w_ref[:, off:off + n]).astype(ref.dtype)
        off += n


def _in_projection(x2, mod6, w_in_p, S):
    N, D = x2.shape
    W = w_in_p.shape[1]
    tm = min(S, 512)
    per_b = S // tm
    return pl.pallas_call(
        _inproj_kernel,
        out_shape=tuple(jax.ShapeDtypeStruct((N, n), BF16) for n in IN_SEG),
        grid=(N // tm,),
        in_specs=[
            pl.BlockSpec((tm, D), lambda i: (i, 0)),
            pl.BlockSpec((1, 6, D), lambda i: (i // per_b, 0, 0)),
            pl.BlockSpec((D, W), lambda i: (0, 0), pipeline_mode=pl.Buffered(1)),
        ],
        out_specs=tuple(pl.BlockSpec((tm, n), lambda i: (i, 0)) for n in IN_SEG),
        compiler_params=_cparams("arbitrary"),
        name="in_proj",
    )(x2, mod6, w_in_p)


def _gla_kernel(qk_ref, v_ref, gr_ref, tail_ref, wa2_ref, ba_ref, gn_ref, wo_ref, y_ref,
                st_ref, kd_ref, dec_ref, sall_ref, o_ref):
    t = pl.program_id(1)

    @pl.when(t == 0)
    def _():
        st_ref[...] = jnp.zeros_like(st_ref)

    ts = qk_ref.shape[0]
    nch = ts // CHUNK
    HK = GLA_HEADS * GLA_DK
    r = lax.broadcasted_iota(I32, (CHUNK, CHUNK), 0)
    c = lax.broadcasted_iota(I32, (CHUNK, CHUNK), 1)
    tri = (r >= c).astype(BF16)
    qscale = GLA_DK ** -0.5

    z = _dot(tail_ref[:, 2 * LANES:3 * LANES], wa2_ref[...]) + ba_ref[...]
    log_a = (jnp.minimum(z, 0.0) - jnp.log(1.0 + jnp.exp(-jnp.abs(z)))) * (1.0 / GLA_GATE_TAU)
    la_hi = log_a.astype(BF16)
    la_lo = (log_a - la_hi.astype(F32)).astype(BF16)
    for n in range(nch):
        rows = slice(n * CHUNK, (n + 1) * CHUNK)
        G = _dot(tri, la_hi[rows]) + _dot(tri, la_lo[rows])
        g_end = G[CHUNK - 1:CHUNK, :]
        kd_ref[rows, :] = (qk_ref[rows, HK:2 * HK].astype(F32) * jnp.exp(g_end - G)).astype(BF16)
        dec_ref[n:n + 1, :] = jnp.exp(g_end)

    for h in range(GLA_HEADS):
        ks = slice(h * GLA_DK, (h + 1) * GLA_DK)
        vs = slice(h * GLA_DV, (h + 1) * GLA_DV)
        st = st_ref[h]
        for n in range(nch):
            rows = slice(n * CHUNK, (n + 1) * CHUNK)
            st = st * dec_ref[n:n + 1, ks] + _dot_tn(v_ref[rows, vs], kd_ref[rows, ks])
            sall_ref[n * GLA_HEADS + h] = st.astype(BF16)
        st_ref[h] = st

    for n in range(nch):
        rows = slice(n * CHUNK, (n + 1) * CHUNK)
        for h in range(GLA_HEADS):
            ks = slice(h * GLA_DK, (h + 1) * GLA_DK)
            vs = slice(h * GLA_DV, (h + 1) * GLA_DV)
            qh = (qk_ref[rows, ks].astype(F32) * qscale).astype(BF16)
            o = _dot_nt(qh, sall_ref[n * GLA_HEADS + h])
            o = o * lax.rsqrt(jnp.mean(o * o, axis=-1, keepdims=True) + RMS_EPS) * gn_ref[...]
            o_ref[rows, vs] = (o * _silu(gr_ref[rows, vs].astype(F32))).astype(BF16)
    y_ref[...] = _dot(o_ref[...], wo_ref[...]).astype(y_ref.dtype)


def _gla(qk, gv, gr, tail, wa2_p, b_a, g_norm, w_o, B, S):
    N = B * S
    D = w_o.shape[1]
    ts = min(S, 512)
    per_b = S // ts
    HK = GLA_HEADS * GLA_DK
    tok = lambda n: pl.BlockSpec((ts, n), lambda b, t: (b * per_b + t, 0))
    full = lambda a: pl.BlockSpec(a.shape, lambda b, t: (0,) * a.ndim)
    return pl.pallas_call(
        _gla_kernel,
        out_shape=jax.ShapeDtypeStruct((N, D), BF16),
        grid=(B, per_b),
        in_specs=[tok(QK_W), tok(GV_W), tok(GV_W), tok(TAIL_W), full(wa2_p), full(b_a), full(g_norm), full(w_o)],
        out_specs=tok(D),
        scratch_shapes=[pltpu.VMEM((GLA_HEADS, GLA_DV, GLA_DK), F32),
                        pltpu.VMEM((ts, GLA_HEADS * GLA_DK), BF16),
                        pltpu.VMEM((ts // CHUNK, GLA_HEADS * GLA_DK), F32),
                        pltpu.VMEM((ts // CHUNK * GLA_HEADS, GLA_DV, GLA_DK), BF16),
                        pltpu.VMEM((ts, GV_W), BF16)],
        compiler_params=_cparams("arbitrary", "arbitrary"),
        name="gla",
    )(qk, gv, gr, tail, wa2_p, b_a, g_norm, w_o)


HQ = MLA_HEADS * 2 * LANES


def _prep_w_uq(w_uq):
    dh = MLA_NOPE + MLA_ROPE
    half = MLA_ROPE // 2
    nope = [w_uq[:, h * dh:h * dh + MLA_NOPE] for h in range(MLA_HEADS)]
    rope = [w_uq[:, h * dh + MLA_NOPE:(h + 1) * dh] for h in range(MLA_HEADS)]
    rope_sw = [jnp.concatenate([r[:, half:], r[:, :half]], axis=1) for r in rope]
    return jnp.concatenate(nope + rope + rope_sw, axis=1).astype(BF16)


def _prep_w_ukv(w_ukv):
    dh = MLA_NOPE + MLA_V
    kn = [w_ukv[:, h * dh:h * dh + MLA_NOPE] for h in range(MLA_HEADS)]
    vv = [w_ukv[:, h * dh + MLA_NOPE:(h + 1) * dh] for h in range(MLA_HEADS)]
    return jnp.concatenate(kn + vv, axis=1).astype(BF16)


def _rms(x, g):
    return x * lax.rsqrt(jnp.mean(x * x, axis=-1, keepdims=True) + RMS_EPS) * g


def _mla_prep_kernel(cq_ref, ckv_ref, tail_ref, cc_ref, ss_ref, gq_ref, wq_ref, gkv_ref, wkv_ref, q_ref, k_ref, v_ref):
    tm = cq_ref.shape[0]
    NP = MLA_HEADS * MLA_NOPE
    RP = MLA_HEADS * MLA_ROPE
    scale = (MLA_NOPE + MLA_ROPE) ** -0.5 * LOG2E
    cc = cc_ref[...]
    ss = ss_ref[...]
    cqn = _rms(cq_ref[...].astype(F32), gq_ref[...]).astype(BF16)
    qf = _dot(cqn, wq_ref[...]) * scale
    ckvn = _rms(ckv_ref[...].astype(F32), gkv_ref[...]).astype(BF16)
    kv = _dot(ckvn, wkv_ref[...])
    krr = (tail_ref[:, 0:LANES].astype(F32) * cc + tail_ref[:, LANES:2 * LANES].astype(F32) * ss).astype(BF16)
    lane = lax.broadcasted_iota(I32, (tm, LANES), 1)
    first = lane < MLA_ROPE
    for j in range(MLA_HEADS // 2):
        a = NP + j * LANES
        rot = qf[:, a:a + LANES] * cc + qf[:, a + RP:a + RP + LANES] * ss
        for h, keep in ((2 * j, first), (2 * j + 1, jnp.logical_not(first))):
            base = h * 2 * LANES
            q_ref[:, base:base + LANES] = qf[:, h * MLA_NOPE:(h + 1) * MLA_NOPE].astype(BF16)
            q_ref[:, base + LANES:base + 2 * LANES] = jnp.where(keep, rot, 0.0).astype(BF16)
            k_ref[:, base:base + LANES] = kv[:, h * MLA_NOPE:(h + 1) * MLA_NOPE].astype(BF16)
            k_ref[:, base + LANES:base + 2 * LANES] = krr
    v_ref[...] = kv[:, NP:].astype(BF16)


def _mla_prep(cq, ckv, tail, cc, ss, g_cq, w_uq_p, g_ckv, w_ukv_p):
    N = cq.shape[0]
    tm = min(N, 512)
    tok = lambda n: pl.BlockSpec((tm, n), lambda i: (i, 0))
    full = lambda a: pl.BlockSpec(a.shape, lambda i: (0,) * a.ndim)
    HV = MLA_HEADS * MLA_V
    return pl.pallas_call(
        _mla_prep_kernel,
        out_shape=(jax.ShapeDtypeStruct((N, HQ), BF16), jax.ShapeDtypeStruct((N, HQ), BF16),
                   jax.ShapeDtypeStruct((N, HV), BF16)),
        grid=(N // tm,),
        in_specs=[tok(MLA_Q_RANK), tok(MLA_KV_RANK), tok(TAIL_W), tok(LANES), tok(LANES),
                  full(g_cq), full(w_uq_p), full(g_ckv), full(w_ukv_p)],
        out_specs=(tok(HQ), tok(HQ), tok(HV)),
        compiler_params=_cparams("arbitrary"),
        name="mla_prep",
    )(cq, ckv, tail, cc, ss, g_cq, w_uq_p, g_ckv, w_ukv_p)


ATTN_TQ = 256


def _mla_attn_kernel(q_ref, k_ref, v_ref, o_ref):
    S = q_ref.shape[0]
    tq = min(S, ATTN_TQ)
    r = lax.broadcasted_iota(I32, (tq, tq), 0) // CHUNK
    c = lax.broadcasted_iota(I32, (tq, tq), 1) // CHUNK
    diag_mask = c <= r
    for ii in range(S // tq):
        l0 = ii * tq
        q = q_ref[l0:l0 + tq, :]
        sd = jnp.where(diag_mask, _dot_nt(q, k_ref[l0:l0 + tq, :]), -jnp.inf)
        m = jnp.max(sd, axis=-1, keepdims=True)
        if ii > 0:
            so = _dot_nt(q, k_ref[0:l0, :])
            m = jnp.maximum(m, jnp.max(so, axis=-1, keepdims=True))
            po = jnp.exp2(so - m)
            l = jnp.sum(po, axis=-1, keepdims=True)
            acc = _dot(po.astype(BF16), v_ref[0:l0, :])
        pd = jnp.exp2(sd - m)
        if ii > 0:
            l = l + jnp.sum(pd, axis=-1, keepdims=True)
            acc = acc + _dot(pd.astype(BF16), v_ref[l0:l0 + tq, :])
        else:
            l = jnp.sum(pd, axis=-1, keepdims=True)
            acc = _dot(pd.astype(BF16), v_ref[l0:l0 + tq, :])
        o_ref[l0:l0 + tq, :] = (acc / l).astype(BF16)


def _mla_attention(qc, kc, vv, B, S):
    N = B * S
    HV = MLA_HEADS * MLA_V
    return pl.pallas_call(
        _mla_attn_kernel,
        out_shape=jax.ShapeDtypeStruct((N, HV), BF16),
        grid=(B, MLA_HEADS),
        in_specs=[
            pl.BlockSpec((S, 2 * LANES), lambda b, h: (b, h)),
            pl.BlockSpec((S, 2 * LANES), lambda b, h: (b, h)),
            pl.BlockSpec((S, MLA_V), lambda b, h: (b, h)),
        ],
        out_specs=pl.BlockSpec((S, MLA_V), lambda b, h: (b, h)),
        compiler_params=_cparams("arbitrary", "arbitrary"),
        name="mla_attn",
    )(qc, kc, vv)


def _layer_norm(u, g, b):
    mu = jnp.mean(u, axis=-1, keepdims=True)
    d = u - mu
    var = jnp.mean(d * d, axis=-1, keepdims=True)
    return d * lax.rsqrt(var + LN_EPS) * g + b


def _merge_kernel(alpha, x_ref, ya_ref, ob_ref, ga_ref, gb_ref, mod_ref, wmo_ref, wout_ref, g_ref, b_ref, wr_ref,
                  x1_ref, h2_ref, lg_ref):
    gt1 = mod_ref[0, 2:3, :]
    sh2 = mod_ref[0, 3:4, :]
    sc2 = mod_ref[0, 4:5, :]
    y = (_sigmoid(ga_ref[...].astype(F32)) * ya_ref[...].astype(F32)
         + _sigmoid(gb_ref[...].astype(F32)) * _dot(ob_ref[...], wmo_ref[...]))
    mix = _dot(y.astype(BF16), wout_ref[...])
    x1 = _layer_norm(alpha * x_ref[...] + (1.0 + gt1) * mix, g_ref[...], b_ref[...])
    x1_ref[...] = x1
    h2 = x1 * (1.0 + sc2) + sh2
    h2_ref[...] = _pack_bf16_pair(h2)
    lg_ref[...] = _dot_nt(wr_ref[...], h2.astype(BF16))


def _merge(alpha, x2, y_a, o_b, gate_a, gate_b, mod6, w_mla_o, w_out, ln_g, ln_b, wr_t, S):
    N, D = x2.shape
    E = wr_t.shape[0]
    tm = min(S, 512)
    per_b = S // tm
    tok = lambda n: pl.BlockSpec((tm, n), lambda i: (i, 0))
    full = lambda a: pl.BlockSpec(a.shape, lambda i: (0,) * a.ndim)
    return pl.pallas_call(
        functools.partial(_merge_kernel, alpha),
        out_shape=(jax.ShapeDtypeStruct((N, D), F32), jax.ShapeDtypeStruct((N, D // 2), I32),
                   jax.ShapeDtypeStruct((E, N), F32)),
        grid=(N // tm,),
        in_specs=[tok(D), tok(D), tok(D), tok(D), tok(D),
                  pl.BlockSpec((1, 6, D), lambda i: (i // per_b, 0, 0)),
                  full(w_mla_o), full(w_out), full(ln_g), full(ln_b), full(wr_t)],
        out_specs=(tok(D), tok(D // 2), pl.BlockSpec((E, tm), lambda i: (0, i))),
        compiler_params=_cparams("arbitrary"),
        name="merge_ln1",
    )(x2, y_a, o_b, gate_a, gate_b, mod6, w_mla_o, w_out, ln_g, ln_b, wr_t)


def _first_argmax(v, io, n):
    m = jnp.max(v, axis=0, keepdims=True)
    idx = jnp.min(jnp.where(v == m, io, n), axis=0, keepdims=True)
    return m, idx


def _route_kernel(lg_ref, br_ref, idx_ref, wt_ref, rank_ref, cnt_ref, carry_ref):
    step = pl.program_id(0)

    @pl.when(step == 0)
    def _():
        carry_ref[...] = jnp.zeros_like(carry_ref)

    E, T = lg_ref.shape
    gsz = E // N_GROUPS
    neg = -jnp.inf
    s = _sigmoid(lg_ref[...])
    biased = s + br_ref[...]
    eio = lax.broadcasted_iota(I32, (E, T), 0)
    gio = lax.broadcasted_iota(I32, (gsz, T), 0)

    gs = []
    for g in range(N_GROUPS):
        blk = biased[g * gsz:(g + 1) * gsz]
        m1, i1 = _first_argmax(blk, gio, gsz)
        m2 = jnp.max(jnp.where(gio == i1, neg, blk), axis=0, keepdims=True)
        gs.append(m1 + m2)
    cur = jnp.concatenate(gs, axis=0)
    nio = lax.broadcasted_iota(I32, (N_GROUPS, T), 0)
    gsel = jnp.zeros((N_GROUPS, T), F32)
    for _ in range(TOPK_GROUPS):
        _, gi = _first_argmax(cur, nio, N_GROUPS)
        hit = nio == gi
        gsel = jnp.where(hit, 1.0, gsel)
        cur = jnp.where(hit, neg, cur)
    emask = jnp.concatenate([jnp.broadcast_to(gsel[g:g + 1], (gsz, T)) for g in range(N_GROUPS)], axis=0) > 0.0

    cur = jnp.where(emask, biased, neg)
    idxs, ws = [], []
    sel = jnp.zeros((E, T), F32)
    for _ in range(TOP_K):
        _, ei = _first_argmax(cur, eio, E)
        hit = eio == ei
        idxs.append(ei)
        ws.append(jnp.sum(jnp.where(hit, s, 0.0), axis=0, keepdims=True))
        sel = jnp.where(hit, 1.0, sel)
        cur = jnp.where(hit, neg, cur)
    w = jnp.concatenate(ws, axis=0)
    w = w / jnp.sum(w, axis=0, keepdims=True) * ROUTED_SCALE
    idx_ref[...] = jnp.concatenate(idxs, axis=0)

    wpad = jnp.concatenate([w, jnp.zeros((LANES - TOP_K, T), F32)], axis=0)
    wt_ref[...] = wpad.T

    r = lax.broadcasted_iota(I32, (T, T), 0)
    c = lax.broadcasted_iota(I32, (T, T), 1)
    before = (r < c).astype(BF16)
    cnt = _dot(sel.astype(BF16), before) + carry_ref[...]
    ranks = [jnp.sum(jnp.where(eio == idxs[k], cnt, 0.0), axis=0, keepdims=True) for k in range(TOP_K)]
    rank_ref[...] = jnp.concatenate(ranks, axis=0).astype(I32)
    total = cnt[:, T - 1:T] + sel[:, T - 1:T]
    carry_ref[...] = total
    cnt_ref[...] = jnp.broadcast_to(total, cnt_ref.shape).astype(I32)


def _route(logits_t, b_router):
    E, N = logits_t.shape
    T = min(N, 512)
    return pl.pallas_call(
        _route_kernel,
        out_shape=(jax.ShapeDtypeStruct((TOP_K, N), I32), jax.ShapeDtypeStruct((N, LANES), F32),
                   jax.ShapeDtypeStruct((TOP_K, N), I32), jax.ShapeDtypeStruct((E, LANES), I32)),
        grid=(N // T,),
        in_specs=[pl.BlockSpec((E, T), lambda i: (0, i)), pl.BlockSpec((E, 1), lambda i: (0, 0))],
        out_specs=(pl.BlockSpec((TOP_K, T), lambda i: (0, i)), pl.BlockSpec((T, LANES), lambda i: (i, 0)),
                   pl.BlockSpec((TOP_K, T), lambda i: (0, i)), pl.BlockSpec((E, LANES), lambda i: (0, 0))),
        scratch_shapes=[pltpu.VMEM((E, 1), F32)],
        compiler_params=_cparams("arbitrary"),
        name="route",
    )(logits_t, b_router.reshape(E, 1).astype(F32))


def _dest_kernel(idx_ref, rank_ref, ps_ref, dest_ref):
    K, T = idx_ref.shape
    E = ps_ref.shape[0]
    eio = lax.broadcasted_iota(I32, (E, T), 0)
    ps = ps_ref[...]
    rows = [jnp.sum(jnp.where(eio == idx_ref[k:k + 1, :], ps, 0), axis=0, keepdims=True) for k in range(K)]
    dest_ref[...] = jnp.concatenate(rows, axis=0) + rank_ref[...]


def _dest_slots(idx, rank, pad_start):
    K, N = idx.shape
    E = pad_start.shape[0]
    T = min(N, 512)
    spec = pl.BlockSpec((K, T), lambda i: (0, i))
    return pl.pallas_call(
        _dest_kernel,
        out_shape=jax.ShapeDtypeStruct((K, N), I32),
        grid=(N // T,),
        in_specs=[spec, spec, pl.BlockSpec((E, 1), lambda i: (0, 0))],
        out_specs=spec,
        compiler_params=_cparams("arbitrary"),
        name="dest_slots",
    )(idx, rank, pad_start.reshape(E, 1))


SC_ROWS = 64


def _sc_workers():
    info = plsc.get_sparse_core_info()
    return info.num_cores, info.num_subcores


def _dispatch_rows(h2p, dest3, n_slots):
    N, W = h2p.shape
    n_chunks, K, R = dest3.shape
    nc, ns = _sc_workers()
    per_w = n_chunks // (nc * ns)
    mesh = plsc.VectorSubcoreMesh(core_axis_name="c", subcore_axis_name="s")

    @functools.partial(
        pl.kernel, mesh=mesh,
        out_type=jax.ShapeDtypeStruct((n_slots, W), I32),
        scratch_types=[pltpu.VMEM((K, R), I32), pltpu.VMEM((K, R), I32),
                       pltpu.VMEM((R, W), I32), pltpu.VMEM((R, W), I32),
                       pltpu.SemaphoreType.DMA, pltpu.SemaphoreType.DMA, pltpu.SemaphoreType.DMA],
    )
    def k(h_hbm, d_hbm, xs_hbm, idx0, idx1, rows0, rows1, lsem0, lsem1, ssem):
        idx, rows, lsem = (idx0, idx1), (rows0, rows1), (lsem0, lsem1)
        base = (lax.axis_index("s") * nc + lax.axis_index("c")) * per_w

        def loads(ch, b):
            return (pltpu.make_async_copy(d_hbm.at[ch], idx[b], lsem[b]),
                    pltpu.make_async_copy(h_hbm.at[pl.ds(ch * R, R)], rows[b], lsem[b]))

        for cp in loads(base, 0):
            cp.start()
        for cp in loads(base, 0):
            cp.wait()

        @pl.loop(0, per_w, step=2)
        def _(j):
            for b in range(2):
                ch = base + j + b
                more = j + b + 1 < per_w

                @pl.when(more)
                def _():
                    for cp in loads(ch + 1, 1 - b):
                        cp.start()

                scatters = [pltpu.make_async_copy(rows[b], xs_hbm.at[idx[b].at[kk]], ssem) for kk in range(K)]
                for cp in scatters:
                    cp.start()
                for cp in scatters:
                    cp.wait()

                @pl.when(more)
                def _():
                    for cp in loads(ch + 1, 1 - b):
                        cp.wait()

    return k(h2p, dest3)


def _combine_rows(ys, dest3, N):
    n_slots, W = ys.shape
    n_chunks, K, R = dest3.shape
    nc, ns = _sc_workers()
    per_w = n_chunks // (nc * ns)
    mesh = plsc.VectorSubcoreMesh(core_axis_name="c", subcore_axis_name="s")

    @functools.partial(
        pl.kernel, mesh=mesh,
        out_type=jax.ShapeDtypeStruct((K, N, W), I32),
        scratch_types=[pltpu.VMEM((K, R), I32), pltpu.VMEM((R, W), I32), pltpu.VMEM((R, W), I32),
                       pltpu.SemaphoreType.DMA, pltpu.SemaphoreType.DMA,
                       pltpu.SemaphoreType.DMA, pltpu.SemaphoreType.DMA],
    )
    def k(ys_hbm, d_hbm, yk_hbm, idx_v, rows0, rows1, gsem0, gsem1, wsem0, wsem1):
        rows, gsem, wsem = (rows0, rows1), (gsem0, gsem1), (wsem0, wsem1)
        base = (lax.axis_index("s") * nc + lax.axis_index("c")) * per_w

        @pl.loop(0, per_w)
        def _(j):
            ch = base + j
            pltpu.sync_copy(d_hbm.at[ch], idx_v)

            def gather(kk):
                return pltpu.make_async_copy(ys_hbm.at[idx_v.at[kk]], rows[kk % 2], gsem[kk % 2])

            def write(kk):
                return pltpu.make_async_copy(rows[kk % 2], yk_hbm.at[kk, pl.ds(ch * R, R)], wsem[kk % 2])

            gather(0).start()
            for kk in range(K):
                gather(kk).wait()
                if kk + 1 < K:
                    if kk >= 1:
                        write(kk - 1).wait()
                    gather(kk + 1).start()
                write(kk).start()
            write(K - 2).wait()
            write(K - 1).wait()

    return k(ys, dest3)


def _expert_kernel(b0_ref, nb_ref, xs_hbm, w1_ref, w3_ref, w2_ref, ys_hbm, w13_s, w2_s, xbuf, ybuf, sem_in, sem_out):
    e = pl.program_id(0)
    n_exp = pl.num_programs(0)
    F = w1_ref.shape[2]
    bm = xbuf.shape[1]
    half = xbuf.shape[2]
    nb = nb_ref[e]
    b0 = b0_ref[e]
    total = b0_ref[n_exp - 1] + nb_ref[n_exp - 1]

    nbuf = xbuf.shape[0]

    def in_copy(g):
        slot = g % nbuf
        return pltpu.make_async_copy(xs_hbm.at[pl.ds(g * bm, bm)], xbuf.at[slot], sem_in.at[slot])

    def out_copy(g):
        slot = g % nbuf
        return pltpu.make_async_copy(ybuf.at[slot], ys_hbm.at[pl.ds(g * bm, bm)], sem_out.at[slot])

    @pl.when(e == 0)
    def _():
        for g0 in range(nbuf - 1):
            @pl.when(g0 < total)
            def _():
                in_copy(g0).start()

    @pl.when(nb > 0)
    def _():
        w13_s[:, :F] = w1_ref[0].astype(BF16)
        w13_s[:, F:] = w3_ref[0].astype(BF16)
        w2_s[...] = w2_ref[0].astype(BF16)

    def block(j, carry):
        g = b0 + j
        slot = g % nbuf
        in_copy(g).wait()

        @pl.when(g + nbuf - 1 < total)
        def _():
            in_copy(g + nbuf - 1).start()

        @pl.when(g >= nbuf)
        def _():
            out_copy(g - nbuf).wait()

        lo, hi = _unpack_bf16_pair(xbuf[slot])
        ab = _dot(lo.astype(BF16), w13_s[:half, :]) + _dot(hi.astype(BF16), w13_s[half:, :])
        hid = (_silu(ab[:, :F]) * ab[:, F:]).astype(BF16)
        ybuf[slot] = _pack_bf16_pair(_dot(hid, w2_s[...]))
        out_copy(g).start()
        return carry

    lax.fori_loop(0, nb, block, 0)

    @pl.when(e == n_exp - 1)
    def _():
        for back in range(nbuf, 0, -1):
            @pl.when(total >= back)
            def _():
                out_copy(total - back).wait()


def _experts(xs, blk0, nblk, w_e1, w_e3, w_e2):
    n_slots, W = xs.shape
    E, D, F = w_e1.shape
    bm = EXPERT_BLOCK
    grid_spec = pltpu.PrefetchScalarGridSpec(
        num_scalar_prefetch=2,
        grid=(E,),
        in_specs=[
            pl.BlockSpec(memory_space=pl.ANY),
            pl.BlockSpec((1, D, F), lambda e, b0, nb: (e, 0, 0)),
            pl.BlockSpec((1, D, F), lambda e, b0, nb: (e, 0, 0)),
            pl.BlockSpec((1, F, D), lambda e, b0, nb: (e, 0, 0)),
        ],
        out_specs=pl.BlockSpec(memory_space=pl.ANY),
        scratch_shapes=[pltpu.VMEM((D, 2 * F), BF16), pltpu.VMEM((F, D), BF16),
                        pltpu.VMEM((EXPERT_RING, bm, W), I32), pltpu.VMEM((EXPERT_RING, bm, W), I32),
                        pltpu.SemaphoreType.DMA((EXPERT_RING,)), pltpu.SemaphoreType.DMA((EXPERT_RING,))],
    )
    return pl.pallas_call(
        _expert_kernel,
        out_shape=jax.ShapeDtypeStruct((n_slots, W), I32),
        grid_spec=grid_spec,
        compiler_params=_cparams("arbitrary"),
        name="experts",
    )(blk0, nblk, xs, w_e1, w_e3, w_e2)


def _final_kernel(alpha, yk_ref, wt_ref, h2_ref, x1_ref, mod_ref, ws13_ref, ws2_ref, g_ref, b_ref, o_ref):
    gt2 = mod_ref[0, 5:6, :]
    F = ws2_ref.shape[0]
    half = h2_ref.shape[1]
    wt = wt_ref[...]
    mlo = jnp.zeros(h2_ref.shape, F32)
    mhi = jnp.zeros(h2_ref.shape, F32)
    for k in range(TOP_K):
        lo, hi = _unpack_bf16_pair(yk_ref[k])
        wk = wt[:, k:k + 1]
        mlo = mlo + wk * lo
        mhi = mhi + wk * hi
    lo, hi = _unpack_bf16_pair(h2_ref[...])
    ab = _dot(lo.astype(BF16), ws13_ref[:half, :]) + _dot(hi.astype(BF16), ws13_ref[half:, :])
    hid = (_silu(ab[:, :F]) * ab[:, F:]).astype(BF16)
    ffn = jnp.concatenate([mlo, mhi], axis=1) + _dot(hid, ws2_ref[...])
    o_ref[...] = _layer_norm(alpha * x1_ref[...] + (1.0 + gt2) * ffn, g_ref[...], b_ref[...])


def _final(alpha, yk, wt, h2p, x1, mod6, ws13, ws2, ln_g, ln_b, S):
    N, D = x1.shape
    K = yk.shape[0]
    tm = min(S, 256)
    per_b = S // tm
    tok = lambda n: pl.BlockSpec((tm, n), lambda i: (i, 0))
    full = lambda a: pl.BlockSpec(a.shape, lambda i: (0,) * a.ndim)
    return pl.pallas_call(
        functools.partial(_final_kernel, alpha),
        out_shape=jax.ShapeDtypeStruct((N, D), F32),
        grid=(N // tm,),
        in_specs=[pl.BlockSpec((K, tm, D // 2), lambda i: (0, i, 0)), tok(LANES), tok(D // 2), tok(D),
                  pl.BlockSpec((1, 6, D), lambda i: (i // per_b, 0, 0)),
                  full(ws13), full(ws2), full(ln_g), full(ln_b)],
        out_specs=tok(D),
        compiler_params=_cparams("arbitrary"),
        name="combine_ln2",
    )(yk, wt, h2p, x1, mod6, ws13, ws2, ln_g, ln_b)


def _moe(alpha, h2p, logits_t, x1, mod6, b_router, w_e1, w_e3, w_e2, w_s1, w_s3, w_s2, ln_g, ln_b, S):
    N = x1.shape[0]
    E = w_e1.shape[0]
    bm = EXPERT_BLOCK
    idx, wt, rank, cnt = _route(logits_t, b_router)
    counts = cnt[:, 0]
    padded = (counts + bm - 1) // bm * bm
    pad_end = jnp.cumsum(padded)
    pad_start = pad_end - padded
    n_slots = N * TOP_K + E * bm
    dest = _dest_slots(idx, rank, pad_start.astype(I32))
    dest3 = dest.reshape(TOP_K, N // SC_ROWS, SC_ROWS).transpose(1, 0, 2)
    xs = _dispatch_rows(h2p, dest3, n_slots)
    ys = _experts(xs, (pad_start // bm).astype(I32), (padded // bm).astype(I32), w_e1, w_e3, w_e2)
    yk = _combine_rows(ys, dest3, N)
    ws13 = jnp.concatenate([w_s1, w_s3], axis=1).astype(BF16)
    return _final(alpha, yk, wt, h2p, x1, mod6, ws13, w_s2.astype(BF16), ln_g, ln_b, S)


def kernel(x, c, positions, w_ada, b_ada, w_in, w_gla_a2, b_gla_a, g_gla_norm, w_gla_o, g_cq, w_uq, g_ckv, w_ukv, w_mla_o, w_out, ln1_g, ln1_b, w_router, b_router, w_e1, w_e3, w_e2, w_s1, w_s3, w_s2, ln2_g, ln2_b):
    B, S, D = x.shape
    N = B * S
    depth = w_ada.shape[0]
    alpha = (2.0 * depth) ** 0.25
    row = lambda a: a.reshape(1, -1)
    cc, ss = _rope_tables(positions)
    x2 = x.reshape(N, D)
    for l in range(depth):
        mod6 = _modulation(c, w_ada[l], b_ada[l]).reshape(B, 6, D)
        qk, gv, gr, cq, ckv, gate_a, gate_b, tail = _in_projection(x2, mod6, _prep_w_in(w_in[l], D), S)
        wa2_p = jnp.concatenate(
            [w_gla_a2[l], jnp.zeros((LANES - GLA_GATE_RANK, w_gla_a2.shape[2]), F32)], axis=0).astype(BF16)
        y_a = _gla(qk, gv, gr, tail, wa2_p, row(b_gla_a[l]), row(g_gla_norm[l]), w_gla_o[l].astype(BF16), B, S)
        qc, kc, vv = _mla_prep(cq, ckv, tail, cc, ss, row(g_cq[l]), _prep_w_uq(w_uq[l]),
                               row(g_ckv[l]), _prep_w_ukv(w_ukv[l]))
        o_b = _mla_attention(qc, kc, vv, B, S)
        x1, h2p, logits_t = _merge(alpha, x2, y_a, o_b, gate_a, gate_b, mod6,
                                   w_mla_o[l].astype(BF16), w_out[l].astype(BF16),
                                   row(ln1_g[l]), row(ln1_b[l]), w_router[l].T.astype(BF16), S)
        x2 = _moe(alpha, h2p, logits_t, x1, mod6, b_router[l], w_e1[l], w_e3[l], w_e2[l],
                  w_s1[l], w_s3[l], w_s2[l], row(ln2_g[l]), row(ln2_b[l]), S)
    return x2.reshape(B, S, D)
```

```python
import functools

import jax
import jax.numpy as jnp
from jax import lax
from jax.experimental import pallas as pl
from jax.experimental.pallas import tpu as pltpu
from jax.experimental.pallas import tpu_sc as plsc

CHUNK = 64
GLA_HEADS = 4
GLA_DK = 128
GLA_DV = 256
GLA_GATE_RANK = 16
GLA_GATE_TAU = 16.0
MLA_HEADS = 8
MLA_Q_RANK = 768
MLA_KV_RANK = 256
MLA_NOPE = 128
MLA_ROPE = 64
MLA_V = 128
ROPE_THETA = 10000.0
N_EXPERTS = 256
TOP_K = 8
N_GROUPS = 8
TOPK_GROUPS = 4
D_EXPERT = 256
ROUTED_SCALE = 2.5
LN_EPS = 1e-5
RMS_EPS = 1e-6
LOG2E = 1.4426950408889634

LANES = 128
VMEM_LIMIT = 56 * 1024 * 1024
EXPERT_BLOCK = 512
EXPERT_RING = 3
COMBINE_PARTS = 4

F32 = jnp.float32
BF16 = jnp.bfloat16
I32 = jnp.int32


def _cparams(*sem):
    return pltpu.CompilerParams(dimension_semantics=sem, vmem_limit_bytes=VMEM_LIMIT)


def _sigmoid(x):
    return 1.0 / (1.0 + jnp.exp(-x))


def _silu(x):
    return x * _sigmoid(x)


def _dot(a, b):
    return jnp.dot(a, b, preferred_element_type=F32)


def _dot_nt(a, b):
    return lax.dot_general(a, b, (((1,), (1,)), ((), ())), preferred_element_type=F32)


def _dot_tn(a, b):
    return lax.dot_general(a, b, (((0,), (0,)), ((), ())), preferred_element_type=F32)


def _pack_bf16_pair(x):
    w = x.shape[1] // 2
    u = lax.bitcast_convert_type(x.astype(BF16).astype(F32), I32)
    lo = lax.shift_right_logical(u[:, :w], jnp.int32(16))
    hi = jnp.bitwise_and(u[:, w:], jnp.int32(-65536))
    return jnp.bitwise_or(lo, hi)


def _unpack_bf16_pair(p):
    lo = lax.bitcast_convert_type(lax.shift_left(p, jnp.int32(16)), F32)
    hi = lax.bitcast_convert_type(jnp.bitwise_and(p, jnp.int32(-65536)), F32)
    return lo, hi


def _mod_kernel(c_ref, w_ref, b_ref, o_ref):
    cond = _silu(c_ref[...]).astype(BF16)
    o_ref[...] = _dot(cond, w_ref[...].astype(BF16)) + b_ref[...]


def _modulation(c, w_ada, b_ada):
    B, D = c.shape
    W = w_ada.shape[1]
    tn = D
    return pl.pallas_call(
        _mod_kernel,
        out_shape=jax.ShapeDtypeStruct((B, W), F32),
        grid=(W // tn,),
        in_specs=[
            pl.BlockSpec((B, D), lambda j: (0, 0)),
            pl.BlockSpec((D, tn), lambda j: (0, j)),
            pl.BlockSpec((1, tn), lambda j: (0, j)),
        ],
        out_specs=pl.BlockSpec((B, tn), lambda j: (0, j)),
        compiler_params=_cparams("arbitrary"),
        name="adaln_mod",
    )(c, w_ada, b_ada.reshape(1, W))


def _rope_kernel(pos_ref, f_ref, ph_ref, cc_ref, ss_ref):
    ang = pos_ref[...] * f_ref[...]
    cc_ref[...] = jnp.cos(ang)
    ss_ref[...] = jnp.sin(ang) * ph_ref[...]


def _rope_tables(positions):
    B, S = positions.shape
    N = B * S
    half = MLA_ROPE // 2
    inv_freq = ROPE_THETA ** (-jnp.arange(half, dtype=F32) * (2.0 / MLA_ROPE))
    f4 = jnp.tile(inv_freq, LANES // half).reshape(1, LANES)
    sign = jnp.tile(jnp.concatenate([-jnp.ones((half,), F32), jnp.ones((half,), F32)]), LANES // MLA_ROPE)
    pos = jnp.broadcast_to(positions.reshape(N, 1).astype(F32), (N, LANES))
    tm = min(N, 1024)
    spec = pl.BlockSpec((tm, LANES), lambda i: (i, 0))
    vec = pl.BlockSpec((1, LANES), lambda i: (0, 0))
    return pl.pallas_call(
        _rope_kernel,
        out_shape=(jax.ShapeDtypeStruct((N, LANES), F32), jax.ShapeDtypeStruct((N, LANES), F32)),
        grid=(N // tm,),
        in_specs=[spec, vec, vec],
        out_specs=(spec, spec),
        compiler_params=_cparams("arbitrary"),
        name="rope_tables",
    )(pos, f4, sign.reshape(1, LANES))


QK_W = 2 * GLA_HEADS * GLA_DK
GV_W = GLA_HEADS * GLA_DV
TAIL_W = 3 * LANES
IN_SEG = (QK_W, GV_W, GV_W, MLA_Q_RANK, MLA_KV_RANK, 1024, 1024, TAIL_W)


def _prep_w_in(w_in, D):
    s = [GLA_HEADS * GLA_DK, GLA_HEADS * GLA_DK, GV_W, GV_W, GLA_GATE_RANK, MLA_Q_RANK, MLA_KV_RANK, MLA_ROPE, D, D]
    offs = [0]
    for n in s:
        offs.append(offs[-1] + n)
    gq, gk, gv, gr, ga, cq, ckv, kr, gate_a, gate_b = [w_in[:, offs[i]:offs[i + 1]] for i in range(10)]
    half = MLA_ROPE // 2
    kr_sw = jnp.concatenate([kr[:, half:], kr[:, :half]], axis=1)
    pad = jnp.zeros((w_in.shape[0], LANES - GLA_GATE_RANK), w_in.dtype)
    return jnp.concatenate([gq, gk, gv, gr, cq, ckv, gate_a, gate_b, kr, kr, kr_sw, kr_sw, ga, pad], axis=1).astype(BF16)


def _inproj_kernel(x_ref, mod_ref, w_ref, *out_refs):
    sh1 = mod_ref[0, 0:1, :]
    sc1 = mod_ref[0, 1:2, :]
    h = (x_ref[...] * (1.0 + sc1) + sh1).astype(BF16)
    off = 0
    for ref in out_refs:
        n = ref.shape[-1]
        ref[...] = _dot(h, w_ref[:, off:off + n]).astype(ref.dtype)
        off += n


def _in_projection(x2, mod6, w_in_p, S):
    N, D = x2.shape
    W = w_in_p.shape[1]
    tm = min(S, 512)
    per_b = S // tm
    return pl.pallas_call(
        _inproj_kernel,
        out_shape=tuple(jax.ShapeDtypeStruct((N, n), BF16) for n in IN_SEG),
        grid=(N // tm,),
        in_specs=[
            pl.BlockSpec((tm, D), lambda i: (i, 0)),
            pl.BlockSpec((1, 6, D), lambda i: (i // per_b, 0, 0)),
            pl.BlockSpec((D, W), lambda i: (0, 0), pipeline_mode=pl.Buffered(1)),
        ],
        out_specs=tuple(pl.BlockSpec((tm, n), lambda i: (i, 0)) for n in IN_SEG),
        compiler_params=_cparams("arbitrary"),
        name="in_proj",
    )(x2, mod6, w_in_p)


def _gla_kernel(qk_ref, v_ref, gr_ref, tail_ref, wa2_ref, ba_ref, gn_ref, wo_ref, y_ref,
                st_ref, kd_ref, dec_ref, sall_ref, o_ref):
    t = pl.program_id(1)

    @pl.when(t == 0)
    def _():
        st_ref[...] = jnp.zeros_like(st_ref)

    ts = qk_ref.shape[0]
    nch = ts // CHUNK
    HK = GLA_HEADS * GLA_DK
    r = lax.broadcasted_iota(I32, (CHUNK, CHUNK), 0)
    c = lax.broadcasted_iota(I32, (CHUNK, CHUNK), 1)
    tri = (r >= c).astype(BF16)
    qscale = GLA_DK ** -0.5

    z = _dot(tail_ref[:, 2 * LANES:3 * LANES], wa2_ref[...]) + ba_ref[...]
    log_a = (jnp.minimum(z, 0.0) - jnp.log(1.0 + jnp.exp(-jnp.abs(z)))) * (1.0 / GLA_GATE_TAU)
    la_hi = log_a.astype(BF16)
    la_lo = (log_a - la_hi.astype(F32)).astype(BF16)
    for n in range(nch):
        rows = slice(n * CHUNK, (n + 1) * CHUNK)
        G = _dot(tri, la_hi[rows]) + _dot(tri, la_lo[rows])
        g_end = G[CHUNK - 1:CHUNK, :]
        kd_ref[rows, :] = (qk_ref[rows, HK:2 * HK].astype(F32) * jnp.exp(g_end - G)).astype(BF16)
        dec_ref[n:n + 1, :] = jnp.exp(g_end)

    for h in range(GLA_HEADS):
        ks = slice(h * GLA_DK, (h + 1) * GLA_DK)
        vs = slice(h * GLA_DV, (h + 1) * GLA_DV)
        st = st_ref[h]
        for n in range(nch):
            rows = slice(n * CHUNK, (n + 1) * CHUNK)
            st = st * dec_ref[n:n + 1, ks] + _dot_tn(v_ref[rows, vs], kd_ref[rows, ks])
            sall_ref[n * GLA_HEADS + h] = st.astype(BF16)
        st_ref[h] = st

    for n in range(nch):
        rows = slice(n * CHUNK, (n + 1) * CHUNK)
        for h in range(GLA_HEADS):
            ks = slice(h * GLA_DK, (h + 1) * GLA_DK)
            vs = slice(h * GLA_DV, (h + 1) * GLA_DV)
            qh = (qk_ref[rows, ks].astype(F32) * qscale).astype(BF16)
            o = _dot_nt(qh, sall_ref[n * GLA_HEADS + h])
            o = o * lax.rsqrt(jnp.mean(o * o, axis=-1, keepdims=True) + RMS_EPS) * gn_ref[...]
            o_ref[rows, vs] = (o * _silu(gr_ref[rows, vs].astype(F32))).astype(BF16)
    y_ref[...] = _dot(o_ref[...], wo_ref[...]).astype(y_ref.dtype)


def _gla(qk, gv, gr, tail, wa2_p, b_a, g_norm, w_o, B, S):
    N = B * S
    D = w_o.shape[1]
    ts = min(S, 512)
    per_b = S // ts
    HK = GLA_HEADS * GLA_DK
    tok = lambda n: pl.BlockSpec((ts, n), lambda b, t: (b * per_b + t, 0))
    full = lambda a: pl.BlockSpec(a.shape, lambda b, t: (0,) * a.ndim)
    return pl.pallas_call(
        _gla_kernel,
        out_shape=jax.ShapeDtypeStruct((N, D), BF16),
        grid=(B, per_b),
        in_specs=[tok(QK_W), tok(GV_W), tok(GV_W), tok(TAIL_W), full(wa2_p), full(b_a), full(g_norm), full(w_o)],
        out_specs=tok(D),
        scratch_shapes=[pltpu.VMEM((GLA_HEADS, GLA_DV, GLA_DK), F32),
                        pltpu.VMEM((ts, GLA_HEADS * GLA_DK), BF16),
                        pltpu.VMEM((ts // CHUNK, GLA_HEADS * GLA_DK), F32),
                        pltpu.VMEM((ts // CHUNK * GLA_HEADS, GLA_DV, GLA_DK), BF16),
                        pltpu.VMEM((ts, GV_W), BF16)],
        compiler_params=_cparams("arbitrary", "arbitrary"),
        name="gla",
    )(qk, gv, gr, tail, wa2_p, b_a, g_norm, w_o)


HQ = MLA_HEADS * 2 * LANES


def _prep_w_uq(w_uq):
    dh = MLA_NOPE + MLA_ROPE
    half = MLA_ROPE // 2
    nope = [w_uq[:, h * dh:h * dh + MLA_NOPE] for h in range(MLA_HEADS)]
    rope = [w_uq[:, h * dh + MLA_NOPE:(h + 1) * dh] for h in range(MLA_HEADS)]
    rope_sw = [jnp.concatenate([r[:, half:], r[:, :half]], axis=1) for r in rope]
    return jnp.concatenate(nope + rope + rope_sw, axis=1).astype(BF16)


def _prep_w_ukv(w_ukv):
    dh = MLA_NOPE + MLA_V
    kn = [w_ukv[:, h * dh:h * dh + MLA_NOPE] for h in range(MLA_HEADS)]
    vv = [w_ukv[:, h * dh + MLA_NOPE:(h + 1) * dh] for h in range(MLA_HEADS)]
    return jnp.concatenate(kn + vv, axis=1).astype(BF16)


def _rms(x, g):
    return x * lax.rsqrt(jnp.mean(x * x, axis=-1, keepdims=True) + RMS_EPS) * g


def _mla_prep_kernel(cq_ref, ckv_ref, tail_ref, cc_ref, ss_ref, gq_ref, wq_ref, gkv_ref, wkv_ref, q_ref, k_ref, v_ref):
    tm = cq_ref.shape[0]
    NP = MLA_HEADS * MLA_NOPE
    RP = MLA_HEADS * MLA_ROPE
    scale = (MLA_NOPE + MLA_ROPE) ** -0.5 * LOG2E
    cc = cc_ref[...]
    ss = ss_ref[...]
    cqn = _rms(cq_ref[...].astype(F32), gq_ref[...]).astype(BF16)
    qf = _dot(cqn, wq_ref[...]) * scale
    ckvn = _rms(ckv_ref[...].astype(F32), gkv_ref[...]).astype(BF16)
    kv = _dot(ckvn, wkv_ref[...])
    krr = (tail_ref[:, 0:LANES].astype(F32) * cc + tail_ref[:, LANES:2 * LANES].astype(F32) * ss).astype(BF16)
    lane = lax.broadcasted_iota(I32, (tm, LANES), 1)
    first = lane < MLA_ROPE
    for j in range(MLA_HEADS // 2):
        a = NP + j * LANES
        rot = qf[:, a:a + LANES] * cc + qf[:, a + RP:a + RP + LANES] * ss
        for h, keep in ((2 * j, first), (2 * j + 1, jnp.logical_not(first))):
            base = h * 2 * LANES
            q_ref[:, base:base + LANES] = qf[:, h * MLA_NOPE:(h + 1) * MLA_NOPE].astype(BF16)
            q_ref[:, base + LANES:base + 2 * LANES] = jnp.where(keep, rot, 0.0).astype(BF16)
            k_ref[:, base:base + LANES] = kv[:, h * MLA_NOPE:(h + 1) * MLA_NOPE].astype(BF16)
            k_ref[:, base + LANES:base + 2 * LANES] = krr
    v_ref[...] = kv[:, NP:].astype(BF16)


def _mla_prep(cq, ckv, tail, cc, ss, g_cq, w_uq_p, g_ckv, w_ukv_p):
    N = cq.shape[0]
    tm = min(N, 512)
    tok = lambda n: pl.BlockSpec((tm, n), lambda i: (i, 0))
    full = lambda a: pl.BlockSpec(a.shape, lambda i: (0,) * a.ndim)
    HV = MLA_HEADS * MLA_V
    return pl.pallas_call(
        _mla_prep_kernel,
        out_shape=(jax.ShapeDtypeStruct((N, HQ), BF16), jax.ShapeDtypeStruct((N, HQ), BF16),
                   jax.ShapeDtypeStruct((N, HV), BF16)),
        grid=(N // tm,),
        in_specs=[tok(MLA_Q_RANK), tok(MLA_KV_RANK), tok(TAIL_W), tok(LANES), tok(LANES),
                  full(g_cq), full(w_uq_p), full(g_ckv), full(w_ukv_p)],
        out_specs=(tok(HQ), tok(HQ), tok(HV)),
        compiler_params=_cparams("arbitrary"),
        name="mla_prep",
    )(cq, ckv, tail, cc, ss, g_cq, w_uq_p, g_ckv, w_ukv_p)


ATTN_TQ = 256


def _mla_attn_kernel(q_ref, k_ref, v_ref, o_ref):
    S = q_ref.shape[0]
    tq = min(S, ATTN_TQ)
    r = lax.broadcasted_iota(I32, (tq, tq), 0) // CHUNK
    c = lax.broadcasted_iota(I32, (tq, tq), 1) // CHUNK
    diag_mask = c <= r
    for ii in range(S // tq):
        l0 = ii * tq
        q = q_ref[l0:l0 + tq, :]
        sd = jnp.where(diag_mask, _dot_nt(q, k_ref[l0:l0 + tq, :]), -jnp.inf)
        m = jnp.max(sd, axis=-1, keepdims=True)
        if ii > 0:
            so = _dot_nt(q, k_ref[0:l0, :])
            m = jnp.maximum(m, jnp.max(so, axis=-1, keepdims=True))
            po = jnp.exp2(so - m)
            l = jnp.sum(po, axis=-1, keepdims=True)
            acc = _dot(po.astype(BF16), v_ref[0:l0, :])
        pd = jnp.exp2(sd - m)
        if ii > 0:
            l = l + jnp.sum(pd, axis=-1, keepdims=True)
            acc = acc + _dot(pd.astype(BF16), v_ref[l0:l0 + tq, :])
        else:
            l = jnp.sum(pd, axis=-1, keepdims=True)
            acc = _dot(pd.astype(BF16), v_ref[l0:l0 + tq, :])
        o_ref[l0:l0 + tq, :] = (acc / l).astype(BF16)


def _mla_attention(qc, kc, vv, B, S):
    N = B * S
    HV = MLA_HEADS * MLA_V
    return pl.pallas_call(
        _mla_attn_kernel,
        out_shape=jax.ShapeDtypeStruct((N, HV), BF16),
        grid=(B, MLA_HEADS),
        in_specs=[
            pl.BlockSpec((S, 2 * LANES), lambda b, h: (b, h)),
            pl.BlockSpec((S, 2 * LANES), lambda b, h: (b, h)),
            pl.BlockSpec((S, MLA_V), lambda b, h: (b, h)),
        ],
        out_specs=pl.BlockSpec((S, MLA_V), lambda b, h: (b, h)),
        compiler_params=_cparams("arbitrary", "arbitrary"),
        name="mla_attn",
    )(qc, kc, vv)


def _layer_norm(u, g, b):
    mu = jnp.mean(u, axis=-1, keepdims=True)
    d = u - mu
    var = jnp.mean(d * d, axis=-1, keepdims=True)
    return d * lax.rsqrt(var + LN_EPS) * g + b


def _merge_kernel(alpha, x_ref, ya_ref, ob_ref, ga_ref, gb_ref, mod_ref, wmo_ref, wout_ref, g_ref, b_ref, wr_ref,
                  x1_ref, h2_ref, lg_ref):
    gt1 = mod_ref[0, 2:3, :]
    sh2 = mod_ref[0, 3:4, :]
    sc2 = mod_ref[0, 4:5, :]
    y = (_sigmoid(ga_ref[...].astype(F32)) * ya_ref[...].astype(F32)
         + _sigmoid(gb_ref[...].astype(F32)) * _dot(ob_ref[...], wmo_ref[...]))
    mix = _dot(y.astype(BF16), wout_ref[...])
    x1 = _layer_norm(alpha * x_ref[...] + (1.0 + gt1) * mix, g_ref[...], b_ref[...])
    x1_ref[...] = x1
    h2 = x1 * (1.0 + sc2) + sh2
    h2_ref[...] = _pack_bf16_pair(h2)
    lg_ref[...] = _dot_nt(wr_ref[...], h2.astype(BF16))


def _merge(alpha, x2, y_a, o_b, gate_a, gate_b, mod6, w_mla_o, w_out, ln_g, ln_b, wr_t, S):
    N, D = x2.shape
    E = wr_t.shape[0]
    tm = min(S, 512)
    per_b = S // tm
    tok = lambda n: pl.BlockSpec((tm, n), lambda i: (i, 0))
    full = lambda a: pl.BlockSpec(a.shape, lambda i: (0,) * a.ndim)
    return pl.pallas_call(
        functools.partial(_merge_kernel, alpha),
        out_shape=(jax.ShapeDtypeStruct((N, D), F32), jax.ShapeDtypeStruct((N, D // 2), I32),
                   jax.ShapeDtypeStruct((E, N), F32)),
        grid=(N // tm,),
        in_specs=[tok(D), tok(D), tok(D), tok(D), tok(D),
                  pl.BlockSpec((1, 6, D), lambda i: (i // per_b, 0, 0)),
                  full(w_mla_o), full(w_out), full(ln_g), full(ln_b), full(wr_t)],
        out_specs=(tok(D), tok(D // 2), pl.BlockSpec((E, tm), lambda i: (0, i))),
        compiler_params=_cparams("arbitrary"),
        name="merge_ln1",
    )(x2, y_a, o_b, gate_a, gate_b, mod6, w_mla_o, w_out, ln_g, ln_b, wr_t)


def _first_argmax(v, io, n):
    m = jnp.max(v, axis=0, keepdims=True)
    idx = jnp.min(jnp.where(v == m, io, n), axis=0, keepdims=True)
    return m, idx


def _route_kernel(lg_ref, br_ref, idx_ref, wt_ref, rank_ref, cnt_ref, carry_ref):
    step = pl.program_id(0)

    @pl.when(step == 0)
    def _():
        carry_ref[...] = jnp.zeros_like(carry_ref)

    E, T = lg_ref.shape
    gsz = E // N_GROUPS
    neg = -jnp.inf
    s = _sigmoid(lg_ref[...])
    biased = s + br_ref[...]
    eio = lax.broadcasted_iota(I32, (E, T), 0)
    gio = lax.broadcasted_iota(I32, (gsz, T), 0)

    gs = []
    for g in range(N_GROUPS):
        blk = biased[g * gsz:(g + 1) * gsz]
        m1, i1 = _first_argmax(blk, gio, gsz)
        m2 = jnp.max(jnp.where(gio == i1, neg, blk), axis=0, keepdims=True)
        gs.append(m1 + m2)
    cur = jnp.concatenate(gs, axis=0)
    nio = lax.broadcasted_iota(I32, (N_GROUPS, T), 0)
    gsel = jnp.zeros((N_GROUPS, T), F32)
    for _ in range(TOPK_GROUPS):
        _, gi = _first_argmax(cur, nio, N_GROUPS)
        hit = nio == gi
        gsel = jnp.where(hit, 1.0, gsel)
        cur = jnp.where(hit, neg, cur)
    emask = jnp.concatenate([jnp.broadcast_to(gsel[g:g + 1], (gsz, T)) for g in range(N_GROUPS)], axis=0) > 0.0

    cur = jnp.where(emask, biased, neg)
    idxs, ws = [], []
    sel = jnp.zeros((E, T), F32)
    for _ in range(TOP_K):
        _, ei = _first_argmax(cur, eio, E)
        hit = eio == ei
        idxs.append(ei)
        ws.append(jnp.sum(jnp.where(hit, s, 0.0), axis=0, keepdims=True))
        sel = jnp.where(hit, 1.0, sel)
        cur = jnp.where(hit, neg, cur)
    w = jnp.concatenate(ws, axis=0)
    w = w / jnp.sum(w, axis=0, keepdims=True) * ROUTED_SCALE
    idx_ref[...] = jnp.concatenate(idxs, axis=0)

    wpad = jnp.concatenate([w, jnp.zeros((LANES - TOP_K, T), F32)], axis=0)
    wt_ref[...] = wpad.T

    r = lax.broadcasted_iota(I32, (T, T), 0)
    c = lax.broadcasted_iota(I32, (T, T), 1)
    before = (r < c).astype(BF16)
    cnt = _dot(sel.astype(BF16), before) + carry_ref[...]
    ranks = [jnp.sum(jnp.where(eio == idxs[k], cnt, 0.0), axis=0, keepdims=True) for k in range(TOP_K)]
    rank_ref[...] = jnp.concatenate(ranks, axis=0).astype(I32)
    total = cnt[:, T - 1:T] + sel[:, T - 1:T]
    carry_ref[...] = total
    cnt_ref[...] = jnp.broadcast_to(total, cnt_ref.shape).astype(I32)


def _route(logits_t, b_router):
    E, N = logits_t.shape
    T = min(N, 512)
    return pl.pallas_call(
        _route_kernel,
        out_shape=(jax.ShapeDtypeStruct((TOP_K, N), I32), jax.ShapeDtypeStruct((N, LANES), F32),
                   jax.ShapeDtypeStruct((TOP_K, N), I32), jax.ShapeDtypeStruct((E, LANES), I32)),
        grid=(N // T,),
        in_specs=[pl.BlockSpec((E, T), lambda i: (0, i)), pl.BlockSpec((E, 1), lambda i: (0, 0))],
        out_specs=(pl.BlockSpec((TOP_K, T), lambda i: (0, i)), pl.BlockSpec((T, LANES), lambda i: (i, 0)),
                   pl.BlockSpec((TOP_K, T), lambda i: (0, i)), pl.BlockSpec((E, LANES), lambda i: (0, 0))),
        scratch_shapes=[pltpu.VMEM((E, 1), F32)],
        compiler_params=_cparams("arbitrary"),
        name="route",
    )(logits_t, b_router.reshape(E, 1).astype(F32))


def _dest_kernel(idx_ref, rank_ref, ps_ref, dest_ref):
    K, T = idx_ref.shape
    E = ps_ref.shape[0]
    eio = lax.broadcasted_iota(I32, (E, T), 0)
    ps = ps_ref[...]
    rows = [jnp.sum(jnp.where(eio == idx_ref[k:k + 1, :], ps, 0), axis=0, keepdims=True) for k in range(K)]
    dest_ref[...] = jnp.concatenate(rows, axis=0) + rank_ref[...]


def _dest_slots(idx, rank, pad_start):
    K, N = idx.shape
    E = pad_start.shape[0]
    T = min(N, 512)
    spec = pl.BlockSpec((K, T), lambda i: (0, i))
    return pl.pallas_call(
        _dest_kernel,
        out_shape=jax.ShapeDtypeStruct((K, N), I32),
        grid=(N // T,),
        in_specs=[spec, spec, pl.BlockSpec((E, 1), lambda i: (0, 0))],
        out_specs=spec,
        compiler_params=_cparams("arbitrary"),
        name="dest_slots",
    )(idx, rank, pad_start.reshape(E, 1))


SC_ROWS = 64


def _sc_workers():
    info = plsc.get_sparse_core_info()
    return info.num_cores, info.num_subcores


def _dispatch_rows(h2p, dest3, n_slots):
    N, W = h2p.shape
    n_chunks, K, R = dest3.shape
    nc, ns = _sc_workers()
    per_w = n_chunks // (nc * ns)
    mesh = plsc.VectorSubcoreMesh(core_axis_name="c", subcore_axis_name="s")

    @functools.partial(
        pl.kernel, mesh=mesh,
        out_type=jax.ShapeDtypeStruct((n_slots, W), I32),
        scratch_types=[pltpu.VMEM((K, R), I32), pltpu.VMEM((K, R), I32),
                       pltpu.VMEM((R, W), I32), pltpu.VMEM((R, W), I32),
                       pltpu.SemaphoreType.DMA, pltpu.SemaphoreType.DMA, pltpu.SemaphoreType.DMA],
    )
    def k(h_hbm, d_hbm, xs_hbm, idx0, idx1, rows0, rows1, lsem0, lsem1, ssem):
        idx, rows, lsem = (idx0, idx1), (rows0, rows1), (lsem0, lsem1)
        base = (lax.axis_index("s") * nc + lax.axis_index("c")) * per_w

        def loads(ch, b):
            return (pltpu.make_async_copy(d_hbm.at[ch], idx[b], lsem[b]),
                    pltpu.make_async_copy(h_hbm.at[pl.ds(ch * R, R)], rows[b], lsem[b]))

        for cp in loads(base, 0):
            cp.start()
        for cp in loads(base, 0):
            cp.wait()

        @pl.loop(0, per_w, step=2)
        def _(j):
            for b in range(2):
                ch = base + j + b
                more = j + b + 1 < per_w

                @pl.when(more)
                def _():
                    for cp in loads(ch + 1, 1 - b):
                        cp.start()

                scatters = [pltpu.make_async_copy(rows[b], xs_hbm.at[idx[b].at[kk]], ssem) for kk in range(K)]
                for cp in scatters:
                    cp.start()
                for cp in scatters:
                    cp.wait()

                @pl.when(more)
                def _():
                    for cp in loads(ch + 1, 1 - b):
                        cp.wait()

    return k(h2p, dest3)


def _combine_rows(ys, dest3, N):
    n_slots, W = ys.shape
    n_chunks, K, R = dest3.shape
    nc, ns = _sc_workers()
    per_w = n_chunks // (nc * ns)
    mesh = plsc.VectorSubcoreMesh(core_axis_name="c", subcore_axis_name="s")

    @functools.partial(
        pl.kernel, mesh=mesh,
        out_type=jax.ShapeDtypeStruct((K, N, W), I32),
        scratch_types=[pltpu.VMEM((K, R), I32), pltpu.VMEM((R, W), I32), pltpu.VMEM((R, W), I32),
                       pltpu.SemaphoreType.DMA, pltpu.SemaphoreType.DMA,
                       pltpu.SemaphoreType.DMA, pltpu.SemaphoreType.DMA],
    )
    def k(ys_hbm, d_hbm, yk_hbm, idx_v, rows0, rows1, gsem0, gsem1, wsem0, wsem1):
        rows, gsem, wsem = (rows0, rows1), (gsem0, gsem1), (wsem0, wsem1)
        base = (lax.axis_index("s") * nc + lax.axis_index("c")) * per_w

        @pl.loop(0, per_w)
        def _(j):
            ch = base + j
            pltpu.sync_copy(d_hbm.at[ch], idx_v)

            def gather(kk):
                return pltpu.make_async_copy(ys_hbm.at[idx_v.at[kk]], rows[kk % 2], gsem[kk % 2])

            def write(kk):
                return pltpu.make_async_copy(rows[kk % 2], yk_hbm.at[kk, pl.ds(ch * R, R)], wsem[kk % 2])

            gather(0).start()
            for kk in range(K):
                gather(kk).wait()
                if kk + 1 < K:
                    if kk >= 1:
                        write(kk - 1).wait()
                    gather(kk + 1).start()
                write(kk).start()
            write(K - 2).wait()
            write(K - 1).wait()

    return k(ys, dest3)


def _expert_kernel(b0_ref, nb_ref, xs_hbm, w1_ref, w3_ref, w2_ref, ys_hbm, w13_s, w2_s, xbuf, ybuf, sem_in, sem_out):
    e = pl.program_id(0)
    n_exp = pl.num_programs(0)
    F = w1_ref.shape[2]
    bm = xbuf.shape[1]
    half = xbuf.shape[2]
    nb = nb_ref[e]
    b0 = b0_ref[e]
    total = b0_ref[n_exp - 1] + nb_ref[n_exp - 1]

    nbuf = xbuf.shape[0]

    def in_copy(g):
        slot = g % nbuf
        return pltpu.make_async_copy(xs_hbm.at[pl.ds(g * bm, bm)], xbuf.at[slot], sem_in.at[slot])

    def out_copy(g):
        slot = g % nbuf
        return pltpu.make_async_copy(ybuf.at[slot], ys_hbm.at[pl.ds(g * bm, bm)], sem_out.at[slot])

    @pl.when(e == 0)
    def _():
        for g0 in range(nbuf - 1):
            @pl.when(g0 < total)
            def _():
                in_copy(g0).start()

    @pl.when(nb > 0)
    def _():
        w13_s[:, :F] = w1_ref[0].astype(BF16)
        w13_s[:, F:] = w3_ref[0].astype(BF16)
        w2_s[...] = w2_ref[0].astype(BF16)

    def block(j, carry):
        g = b0 + j
        slot = g % nbuf
        in_copy(g).wait()

        @pl.when(g + nbuf - 1 < total)
        def _():
            in_copy(g + nbuf - 1).start()

        @pl.when(g >= nbuf)
        def _():
            out_copy(g - nbuf).wait()

        lo, hi = _unpack_bf16_pair(xbuf[slot])
        ab = _dot(lo.astype(BF16), w13_s[:half, :]) + _dot(hi.astype(BF16), w13_s[half:, :])
        hid = (_silu(ab[:, :F]) * ab[:, F:]).astype(BF16)
        ybuf[slot] = _pack_bf16_pair(_dot(hid, w2_s[...]))
        out_copy(g).start()
        return carry

    lax.fori_loop(0, nb, block, 0)

    @pl.when(e == n_exp - 1)
    def _():
        for back in range(nbuf, 0, -1):
            @pl.when(total >= back)
            def _():
                out_copy(total - back).wait()


def _experts(xs, blk0, nblk, w_e1, w_e3, w_e2):
    n_slots, W = xs.shape
    E, D, F = w_e1.shape
    bm = EXPERT_BLOCK
    grid_spec = pltpu.PrefetchScalarGridSpec(
        num_scalar_prefetch=2,
        grid=(E,),
        in_specs=[
            pl.BlockSpec(memory_space=pl.ANY),
            pl.BlockSpec((1, D, F), lambda e, b0, nb: (e, 0, 0)),
            pl.BlockSpec((1, D, F), lambda e, b0, nb: (e, 0, 0)),
            pl.BlockSpec((1, F, D), lambda e, b0, nb: (e, 0, 0)),
        ],
        out_specs=pl.BlockSpec(memory_space=pl.ANY),
        scratch_shapes=[pltpu.VMEM((D, 2 * F), BF16), pltpu.VMEM((F, D), BF16),
                        pltpu.VMEM((EXPERT_RING, bm, W), I32), pltpu.VMEM((EXPERT_RING, bm, W), I32),
                        pltpu.SemaphoreType.DMA((EXPERT_RING,)), pltpu.SemaphoreType.DMA((EXPERT_RING,))],
    )
    return pl.pallas_call(
        _expert_kernel,
        out_shape=jax.ShapeDtypeStruct((n_slots, W), I32),
        grid_spec=grid_spec,
        compiler_params=_cparams("arbitrary"),
        name="experts",
    )(blk0, nblk, xs, w_e1, w_e3, w_e2)


def _shared_kernel(h2_ref, ws13_ref, ws2_ref, o_ref):
    F = ws2_ref.shape[0]
    half = h2_ref.shape[1]
    lo, hi = _unpack_bf16_pair(h2_ref[...])
    ab = _dot(lo.astype(BF16), ws13_ref[:half, :]) + _dot(hi.astype(BF16), ws13_ref[half:, :])
    hid = (_silu(ab[:, :F]) * ab[:, F:]).astype(BF16)
    o_ref[...] = _dot(hid, ws2_ref[...]).astype(o_ref.dtype)


def _shared_expert(h2p, ws13, ws2):
    N, W = h2p.shape
    D = ws2.shape[1]
    tm = min(N, 512)
    full = lambda a: pl.BlockSpec(a.shape, lambda i: (0,) * a.ndim)
    return pl.pallas_call(
        _shared_kernel,
        out_shape=jax.ShapeDtypeStruct((N, D), BF16),
        grid=(N // tm,),
        in_specs=[pl.BlockSpec((tm, W), lambda i: (i, 0)), full(ws13), full(ws2)],
        out_specs=pl.BlockSpec((tm, D), lambda i: (i, 0)),
        compiler_params=_cparams("arbitrary"),
        name="shared_expert",
    )(h2p, ws13, ws2)


def _final_kernel(alpha, yk_ref, wt_ref, sh_ref, x1_ref, mod_ref, g_ref, b_ref, *rest):
    o_ref = rest[-1]
    gt2 = mod_ref[0, 5:6, :]
    wt = wt_ref[...]
    mlo = jnp.zeros(yk_ref.shape[1:], F32)
    mhi = jnp.zeros(yk_ref.shape[1:], F32)
    for k in range(TOP_K):
        lo, hi = _unpack_bf16_pair(yk_ref[k])
        wk = wt[:, k:k + 1]
        mlo = mlo + wk * lo
        mhi = mhi + wk * hi
    ffn = jnp.concatenate([mlo, mhi], axis=1) + sh_ref[...].astype(F32)
    o_ref[...] = _layer_norm(alpha * x1_ref[...] + (1.0 + gt2) * ffn, g_ref[...], b_ref[...])


def _final_part(alpha, yk, wt, sh, x1, mod6, ln_g, ln_b, S, part, n_parts, prev):
    N, D = x1.shape
    K, n_part, W = yk.shape
    tm = min(S, 256)
    per_b = S // tm
    steps = n_part // tm
    off = part * steps
    tok = lambda n: pl.BlockSpec((tm, n), lambda i: (off + i, 0))
    full = lambda a: pl.BlockSpec(a.shape, lambda i: (0,) * a.ndim)
    in_specs = [pl.BlockSpec((K, tm, W), lambda i: (0, i, 0)), tok(LANES), tok(D), tok(D),
                pl.BlockSpec((1, 6, D), lambda i: ((off + i) // per_b, 0, 0)), full(ln_g), full(ln_b)]
    args = [yk, wt, sh, x1, mod6, ln_g, ln_b]
    aliases = {}
    if prev is not None:
        in_specs.append(pl.BlockSpec(memory_space=pl.ANY))
        args.append(prev)
        aliases = {len(args) - 1: 0}
    return pl.pallas_call(
        functools.partial(_final_kernel, alpha),
        out_shape=jax.ShapeDtypeStruct((N, D), F32),
        grid=(steps,),
        in_specs=in_specs,
        out_specs=tok(D),
        input_output_aliases=aliases,
        compiler_params=_cparams("arbitrary"),
        name="combine_ln2",
    )(*args)


def _moe(alpha, h2p, logits_t, x1, mod6, b_router, w_e1, w_e3, w_e2, w_s1, w_s3, w_s2, ln_g, ln_b, S):
    N = x1.shape[0]
    E = w_e1.shape[0]
    bm = EXPERT_BLOCK
    idx, wt, rank, cnt = _route(logits_t, b_router)
    counts = cnt[:, 0]
    padded = (counts + bm - 1) // bm * bm
    pad_end = jnp.cumsum(padded)
    pad_start = pad_end - padded
    n_slots = N * TOP_K + E * bm
    dest = _dest_slots(idx, rank, pad_start.astype(I32))
    dest3 = dest.reshape(TOP_K, N // SC_ROWS, SC_ROWS).transpose(1, 0, 2)
    xs = _dispatch_rows(h2p, dest3, n_slots)
    ws13 = jnp.concatenate([w_s1, w_s3], axis=1).astype(BF16)
    sh = _shared_expert(h2p, ws13, w_s2.astype(BF16))
    ys = _experts(xs, (pad_start // bm).astype(I32), (padded // bm).astype(I32), w_e1, w_e3, w_e2)
    n_chunks = dest3.shape[0]
    cpp = n_chunks // COMBINE_PARTS
    out = None
    for p in range(COMBINE_PARTS):
        yk = _combine_rows(ys, dest3[p * cpp:(p + 1) * cpp], cpp * SC_ROWS)
        out = _final_part(alpha, yk, wt, sh, x1, mod6, ln_g, ln_b, S, p, COMBINE_PARTS, out)
    return out


def kernel(x, c, positions, w_ada, b_ada, w_in, w_gla_a2, b_gla_a, g_gla_norm, w_gla_o, g_cq, w_uq, g_ckv, w_ukv, w_mla_o, w_out, ln1_g, ln1_b, w_router, b_router, w_e1, w_e3, w_e2, w_s1, w_s3, w_s2, ln2_g, ln2_b):
    B, S, D = x.shape
    N = B * S
    depth = w_ada.shape[0]
    alpha = (2.0 * depth) ** 0.25
    row = lambda a: a.reshape(1, -1)
    cc, ss = _rope_tables(positions)
    x2 = x.reshape(N, D)
    for l in range(depth):
        mod6 = _modulation(c, w_ada[l], b_ada[l]).reshape(B, 6, D)
        qk, gv, gr, cq, ckv, gate_a, gate_b, tail = _in_projection(x2, mod6, _prep_w_in(w_in[l], D), S)
        wa2_p = jnp.concatenate(
            [w_gla_a2[l], jnp.zeros((LANES - GLA_GATE_RANK, w_gla_a2.shape[2]), F32)], axis=0).astype(BF16)
        y_a = _gla(qk, gv, gr, tail, wa2_p, row(b_gla_a[l]), row(g_gla_norm[l]), w_gla_o[l].astype(BF16), B, S)
        qc, kc, vv = _mla_prep(cq, ckv, tail, cc, ss, row(g_cq[l]), _prep_w_uq(w_uq[l]),
                               row(g_ckv[l]), _prep_w_ukv(w_ukv[l]))
        o_b = _mla_attention(qc, kc, vv, B, S)
        x1, h2p, logits_t = _merge(alpha, x2, y_a, o_b, gate_a, gate_b, mod6,
                                   w_mla_o[l].astype(BF16), w_out[l].astype(BF16),
                                   row(ln1_g[l]), row(ln1_b[l]), w_router[l].T.astype(BF16), S)
        x2 = _moe(alpha, h2p, logits_t, x1, mod6, b_router[l], w_e1[l], w_e3[l], w_e2[l],
                  w_s1[l], w_s3[l], w_s2[l], row(ln2_g[l]), row(ln2_b[l]), S)
    return x2.reshape(B, S, D)
```

```python
import functools

import jax
import jax.numpy as jnp
from jax import lax
from jax.experimental import pallas as pl
from jax.experimental.pallas import tpu as pltpu
from jax.experimental.pallas import tpu_sc as plsc

CHUNK = 64
GLA_HEADS = 4
GLA_DK = 128
GLA_DV = 256
GLA_GATE_RANK = 16
GLA_GATE_TAU = 16.0
MLA_HEADS = 8
MLA_Q_RANK = 768
MLA_KV_RANK = 256
MLA_NOPE = 128
MLA_ROPE = 64
MLA_V = 128
ROPE_THETA = 10000.0
N_EXPERTS = 256
TOP_K = 8
N_GROUPS = 8
TOPK_GROUPS = 4
D_EXPERT = 256
ROUTED_SCALE = 2.5
LN_EPS = 1e-5
RMS_EPS = 1e-6
LOG2E = 1.4426950408889634

LANES = 128
VMEM_LIMIT = 56 * 1024 * 1024
EXPERT_BLOCK = 512
EXPERT_RING = 3
COMBINE_PARTS = 8

F32 = jnp.float32
BF16 = jnp.bfloat16
I32 = jnp.int32


def _cparams(*sem):
    return pltpu.CompilerParams(dimension_semantics=sem, vmem_limit_bytes=VMEM_LIMIT)


def _sigmoid(x):
    return 1.0 / (1.0 + jnp.exp(-x))


def _silu(x):
    return x * _sigmoid(x)


def _dot(a, b):
    return jnp.dot(a, b, preferred_element_type=F32)


def _dot_nt(a, b):
    return lax.dot_general(a, b, (((1,), (1,)), ((), ())), preferred_element_type=F32)


def _dot_tn(a, b):
    return lax.dot_general(a, b, (((0,), (0,)), ((), ())), preferred_element_type=F32)


def _pack_bf16_pair(x):
    w = x.shape[1] // 2
    u = lax.bitcast_convert_type(x.astype(BF16).astype(F32), I32)
    lo = lax.shift_right_logical(u[:, :w], jnp.int32(16))
    hi = jnp.bitwise_and(u[:, w:], jnp.int32(-65536))
    return jnp.bitwise_or(lo, hi)


def _unpack_bf16_pair(p):
    lo = lax.bitcast_convert_type(lax.shift_left(p, jnp.int32(16)), F32)
    hi = lax.bitcast_convert_type(jnp.bitwise_and(p, jnp.int32(-65536)), F32)
    return lo, hi


def _mod_kernel(c_ref, w_ref, b_ref, o_ref):
    cond = _silu(c_ref[...]).astype(BF16)
    o_ref[...] = _dot(cond, w_ref[...].astype(BF16)) + b_ref[...]


def _modulation(c, w_ada, b_ada):
    B, D = c.shape
    W = w_ada.shape[1]
    tn = D
    return pl.pallas_call(
        _mod_kernel,
        out_shape=jax.ShapeDtypeStruct((B, W), F32),
        grid=(W // tn,),
        in_specs=[
            pl.BlockSpec((B, D), lambda j: (0, 0)),
            pl.BlockSpec((D, tn), lambda j: (0, j)),
            pl.BlockSpec((1, tn), lambda j: (0, j)),
        ],
        out_specs=pl.BlockSpec((B, tn), lambda j: (0, j)),
        compiler_params=_cparams("arbitrary"),
        name="adaln_mod",
    )(c, w_ada, b_ada.reshape(1, W))


def _rope_kernel(pos_ref, f_ref, ph_ref, cc_ref, ss_ref):
    ang = pos_ref[...] * f_ref[...]
    cc_ref[...] = jnp.cos(ang)
    ss_ref[...] = jnp.sin(ang) * ph_ref[...]


def _rope_tables(positions):
    B, S = positions.shape
    N = B * S
    half = MLA_ROPE // 2
    inv_freq = ROPE_THETA ** (-jnp.arange(half, dtype=F32) * (2.0 / MLA_ROPE))
    f4 = jnp.tile(inv_freq, LANES // half).reshape(1, LANES)
    sign = jnp.tile(jnp.concatenate([-jnp.ones((half,), F32), jnp.ones((half,), F32)]), LANES // MLA_ROPE)
    pos = jnp.broadcast_to(positions.reshape(N, 1).astype(F32), (N, LANES))
    tm = min(N, 1024)
    spec = pl.BlockSpec((tm, LANES), lambda i: (i, 0))
    vec = pl.BlockSpec((1, LANES), lambda i: (0, 0))
    return pl.pallas_call(
        _rope_kernel,
        out_shape=(jax.ShapeDtypeStruct((N, LANES), F32), jax.ShapeDtypeStruct((N, LANES), F32)),
        grid=(N // tm,),
        in_specs=[spec, vec, vec],
        out_specs=(spec, spec),
        compiler_params=_cparams("arbitrary"),
        name="rope_tables",
    )(pos, f4, sign.reshape(1, LANES))


QK_W = 2 * GLA_HEADS * GLA_DK
GV_W = GLA_HEADS * GLA_DV
TAIL_W = 3 * LANES
IN_SEG = (QK_W, GV_W, GV_W, MLA_Q_RANK, MLA_KV_RANK, 1024, 1024, TAIL_W)


def _prep_w_in(w_in, D):
    s = [GLA_HEADS * GLA_DK, GLA_HEADS * GLA_DK, GV_W, GV_W, GLA_GATE_RANK, MLA_Q_RANK, MLA_KV_RANK, MLA_ROPE, D, D]
    offs = [0]
    for n in s:
        offs.append(offs[-1] + n)
    gq, gk, gv, gr, ga, cq, ckv, kr, gate_a, gate_b = [w_in[:, offs[i]:offs[i + 1]] for i in range(10)]
    half = MLA_ROPE // 2
    kr_sw = jnp.concatenate([kr[:, half:], kr[:, :half]], axis=1)
    pad = jnp.zeros((w_in.shape[0], LANES - GLA_GATE_RANK), w_in.dtype)
    return jnp.concatenate([gq, gk, gv, gr, cq, ckv, gate_a, gate_b, kr, kr, kr_sw, kr_sw, ga, pad], axis=1).astype(BF16)


def _inproj_kernel(x_ref, mod_ref, w_ref, *out_refs):
    sh1 = mod_ref[0, 0:1, :]
    sc1 = mod_ref[0, 1:2, :]
    h = (x_ref[...] * (1.0 + sc1) + sh1).astype(BF16)
    off = 0
    for ref in out_refs:
        n = ref.shape[-1]
        ref[...] = _dot(h, w_ref[:, off:off + n]).astype(ref.dtype)
        off += n


def _in_projection(x2, mod6, w_in_p, S):
    N, D = x2.shape
    W = w_in_p.shape[1]
    tm = min(S, 512)
    per_b = S // tm
    return pl.pallas_call(
        _inproj_kernel,
        out_shape=tuple(jax.ShapeDtypeStruct((N, n), BF16) for n in IN_SEG),
        grid=(N // tm,),
        in_specs=[
            pl.BlockSpec((tm, D), lambda i: (i, 0)),
            pl.BlockSpec((1, 6, D), lambda i: (i // per_b, 0, 0)),
            pl.BlockSpec((D, W), lambda i: (0, 0), pipeline_mode=pl.Buffered(1)),
        ],
        out_specs=tuple(pl.BlockSpec((tm, n), lambda i: (i, 0)) for n in IN_SEG),
        compiler_params=_cparams("arbitrary"),
        name="in_proj",
    )(x2, mod6, w_in_p)


def _gla_kernel(qk_ref, v_ref, gr_ref, tail_ref, wa2_ref, ba_ref, gn_ref, wo_ref, y_ref,
                st_ref, kd_ref, dec_ref, sall_ref, o_ref):
    t = pl.program_id(1)

    @pl.when(t == 0)
    def _():
        st_ref[...] = jnp.zeros_like(st_ref)

    ts = qk_ref.shape[0]
    nch = ts // CHUNK
    HK = GLA_HEADS * GLA_DK
    r = lax.broadcasted_iota(I32, (CHUNK, CHUNK), 0)
    c = lax.broadcasted_iota(I32, (CHUNK, CHUNK), 1)
    tri = (r >= c).astype(BF16)
    qscale = GLA_DK ** -0.5

    z = _dot(tail_ref[:, 2 * LANES:3 * LANES], wa2_ref[...]) + ba_ref[...]
    log_a = (jnp.minimum(z, 0.0) - jnp.log(1.0 + jnp.exp(-jnp.abs(z)))) * (1.0 / GLA_GATE_TAU)
    la_hi = log_a.astype(BF16)
    la_lo = (log_a - la_hi.astype(F32)).astype(BF16)
    for n in range(nch):
        rows = slice(n * CHUNK, (n + 1) * CHUNK)
        G = _dot(tri, la_hi[rows]) + _dot(tri, la_lo[rows])
        g_end = G[CHUNK - 1:CHUNK, :]
        kd_ref[rows, :] = (qk_ref[rows, HK:2 * HK].astype(F32) * jnp.exp(g_end - G)).astype(BF16)
        dec_ref[n:n + 1, :] = jnp.exp(g_end)

    for h in range(GLA_HEADS):
        ks = slice(h * GLA_DK, (h + 1) * GLA_DK)
        vs = slice(h * GLA_DV, (h + 1) * GLA_DV)
        st = st_ref[h]
        for n in range(nch):
            rows = slice(n * CHUNK, (n + 1) * CHUNK)
            st = st * dec_ref[n:n + 1, ks] + _dot_tn(v_ref[rows, vs], kd_ref[rows, ks])
            sall_ref[n * GLA_HEADS + h] = st.astype(BF16)
        st_ref[h] = st

    for n in range(nch):
        rows = slice(n * CHUNK, (n + 1) * CHUNK)
        for h in range(GLA_HEADS):
            ks = slice(h * GLA_DK, (h + 1) * GLA_DK)
            vs = slice(h * GLA_DV, (h + 1) * GLA_DV)
            qh = (qk_ref[rows, ks].astype(F32) * qscale).astype(BF16)
            o = _dot_nt(qh, sall_ref[n * GLA_HEADS + h])
            o = o * lax.rsqrt(jnp.mean(o * o, axis=-1, keepdims=True) + RMS_EPS) * gn_ref[...]
            o_ref[rows, vs] = (o * _silu(gr_ref[rows, vs].astype(F32))).astype(BF16)
    y_ref[...] = _dot(o_ref[...], wo_ref[...]).astype(y_ref.dtype)


def _gla(qk, gv, gr, tail, wa2_p, b_a, g_norm, w_o, B, S):
    N = B * S
    D = w_o.shape[1]
    ts = min(S, 512)
    per_b = S // ts
    HK = GLA_HEADS * GLA_DK
    tok = lambda n: pl.BlockSpec((ts, n), lambda b, t: (b * per_b + t, 0))
    full = lambda a: pl.BlockSpec(a.shape, lambda b, t: (0,) * a.ndim)
    return pl.pallas_call(
        _gla_kernel,
        out_shape=jax.ShapeDtypeStruct((N, D), BF16),
        grid=(B, per_b),
        in_specs=[tok(QK_W), tok(GV_W), tok(GV_W), tok(TAIL_W), full(wa2_p), full(b_a), full(g_norm), full(w_o)],
        out_specs=tok(D),
        scratch_shapes=[pltpu.VMEM((GLA_HEADS, GLA_DV, GLA_DK), F32),
                        pltpu.VMEM((ts, GLA_HEADS * GLA_DK), BF16),
                        pltpu.VMEM((ts // CHUNK, GLA_HEADS * GLA_DK), F32),
                        pltpu.VMEM((ts // CHUNK * GLA_HEADS, GLA_DV, GLA_DK), BF16),
                        pltpu.VMEM((ts, GV_W), BF16)],
        compiler_params=_cparams("arbitrary", "arbitrary"),
        name="gla",
    )(qk, gv, gr, tail, wa2_p, b_a, g_norm, w_o)


HQ = MLA_HEADS * 2 * LANES


def _prep_w_uq(w_uq):
    dh = MLA_NOPE + MLA_ROPE
    half = MLA_ROPE // 2
    nope = [w_uq[:, h * dh:h * dh + MLA_NOPE] for h in range(MLA_HEADS)]
    rope = [w_uq[:, h * dh + MLA_NOPE:(h + 1) * dh] for h in range(MLA_HEADS)]
    rope_sw = [jnp.concatenate([r[:, half:], r[:, :half]], axis=1) for r in rope]
    return jnp.concatenate(nope + rope + rope_sw, axis=1).astype(BF16)


def _prep_w_ukv(w_ukv):
    dh = MLA_NOPE + MLA_V
    kn = [w_ukv[:, h * dh:h * dh + MLA_NOPE] for h in range(MLA_HEADS)]
    vv = [w_ukv[:, h * dh + MLA_NOPE:(h + 1) * dh] for h in range(MLA_HEADS)]
    return jnp.concatenate(kn + vv, axis=1).astype(BF16)


def _rms(x, g):
    return x * lax.rsqrt(jnp.mean(x * x, axis=-1, keepdims=True) + RMS_EPS) * g


def _mla_prep_kernel(cq_ref, ckv_ref, tail_ref, cc_ref, ss_ref, gq_ref, wq_ref, gkv_ref, wkv_ref, q_ref, k_ref, v_ref):
    tm = cq_ref.shape[0]
    NP = MLA_HEADS * MLA_NOPE
    RP = MLA_HEADS * MLA_ROPE
    scale = (MLA_NOPE + MLA_ROPE) ** -0.5 * LOG2E
    cc = cc_ref[...]
    ss = ss_ref[...]
    cqn = _rms(cq_ref[...].astype(F32), gq_ref[...]).astype(BF16)
    qf = _dot(cqn, wq_ref[...]) * scale
    ckvn = _rms(ckv_ref[...].astype(F32), gkv_ref[...]).astype(BF16)
    kv = _dot(ckvn, wkv_ref[...])
    krr = (tail_ref[:, 0:LANES].astype(F32) * cc + tail_ref[:, LANES:2 * LANES].astype(F32) * ss).astype(BF16)
    lane = lax.broadcasted_iota(I32, (tm, LANES), 1)
    first = lane < MLA_ROPE
    for j in range(MLA_HEADS // 2):
        a = NP + j * LANES
        rot = qf[:, a:a + LANES] * cc + qf[:, a + RP:a + RP + LANES] * ss
        for h, keep in ((2 * j, first), (2 * j + 1, jnp.logical_not(first))):
            base = h * 2 * LANES
            q_ref[:, base:base + LANES] = qf[:, h * MLA_NOPE:(h + 1) * MLA_NOPE].astype(BF16)
            q_ref[:, base + LANES:base + 2 * LANES] = jnp.where(keep, rot, 0.0).astype(BF16)
            k_ref[:, base:base + LANES] = kv[:, h * MLA_NOPE:(h + 1) * MLA_NOPE].astype(BF16)
            k_ref[:, base + LANES:base + 2 * LANES] = krr
    v_ref[...] = kv[:, NP:].astype(BF16)


def _mla_prep(cq, ckv, tail, cc, ss, g_cq, w_uq_p, g_ckv, w_ukv_p):
    N = cq.shape[0]
    tm = min(N, 512)
    tok = lambda n: pl.BlockSpec((tm, n), lambda i: (i, 0))
    full = lambda a: pl.BlockSpec(a.shape, lambda i: (0,) * a.ndim)
    HV = MLA_HEADS * MLA_V
    return pl.pallas_call(
        _mla_prep_kernel,
        out_shape=(jax.ShapeDtypeStruct((N, HQ), BF16), jax.ShapeDtypeStruct((N, HQ), BF16),
                   jax.ShapeDtypeStruct((N, HV), BF16)),
        grid=(N // tm,),
        in_specs=[tok(MLA_Q_RANK), tok(MLA_KV_RANK), tok(TAIL_W), tok(LANES), tok(LANES),
                  full(g_cq), full(w_uq_p), full(g_ckv), full(w_ukv_p)],
        out_specs=(tok(HQ), tok(HQ), tok(HV)),
        compiler_params=_cparams("arbitrary"),
        name="mla_prep",
    )(cq, ckv, tail, cc, ss, g_cq, w_uq_p, g_ckv, w_ukv_p)


ATTN_TQ = 256


def _mla_attn_kernel(q_ref, k_ref, v_ref, o_ref):
    S = q_ref.shape[0]
    tq = min(S, ATTN_TQ)
    r = lax.broadcasted_iota(I32, (tq, tq), 0) // CHUNK
    c = lax.broadcasted_iota(I32, (tq, tq), 1) // CHUNK
    diag_mask = c <= r
    def scores(ii):
        l0 = ii * tq
        q = q_ref[l0:l0 + tq, :]
        sd = jnp.where(diag_mask, _dot_nt(q, k_ref[l0:l0 + tq, :]), -jnp.inf)
        so = _dot_nt(q, k_ref[0:l0, :]) if ii > 0 else None
        return sd, so

    def finish(ii, sd, so):
        l0 = ii * tq
        m = jnp.max(sd, axis=-1, keepdims=True)
        if so is not None:
            m = jnp.maximum(m, jnp.max(so, axis=-1, keepdims=True))
        pd = jnp.exp2(sd - m)
        l = jnp.sum(pd, axis=-1, keepdims=True)
        acc = _dot(pd.astype(BF16), v_ref[l0:l0 + tq, :])
        if so is not None:
            po = jnp.exp2(so - m)
            l = l + jnp.sum(po, axis=-1, keepdims=True)
            acc = acc + _dot(po.astype(BF16), v_ref[0:l0, :])
        o_ref[l0:l0 + tq, :] = (acc / l).astype(BF16)

    n_tiles = S // tq
    nxt = scores(0)
    for ii in range(n_tiles):
        cur = nxt
        if ii + 1 < n_tiles:
            nxt = scores(ii + 1)
        finish(ii, *cur)


def _mla_attention(qc, kc, vv, B, S):
    N = B * S
    HV = MLA_HEADS * MLA_V
    return pl.pallas_call(
        _mla_attn_kernel,
        out_shape=jax.ShapeDtypeStruct((N, HV), BF16),
        grid=(B, MLA_HEADS),
        in_specs=[
            pl.BlockSpec((S, 2 * LANES), lambda b, h: (b, h)),
            pl.BlockSpec((S, 2 * LANES), lambda b, h: (b, h)),
            pl.BlockSpec((S, MLA_V), lambda b, h: (b, h)),
        ],
        out_specs=pl.BlockSpec((S, MLA_V), lambda b, h: (b, h)),
        compiler_params=_cparams("arbitrary", "arbitrary"),
        name="mla_attn",
    )(qc, kc, vv)


def _layer_norm(u, g, b):
    mu = jnp.mean(u, axis=-1, keepdims=True)
    d = u - mu
    var = jnp.mean(d * d, axis=-1, keepdims=True)
    return d * lax.rsqrt(var + LN_EPS) * g + b


def _merge_kernel(alpha, x_ref, ya_ref, ob_ref, ga_ref, gb_ref, mod_ref, wmo_ref, wout_ref, g_ref, b_ref, wr_ref,
                  br_ref, x1_ref, h2_ref, idx_ref, wt_ref, rank_ref, cnt_ref, lg_s, carry_s):
    i = pl.program_id(0)

    @pl.when(i == 0)
    def _():
        lg_s[...] = jnp.zeros_like(lg_s)
        carry_s[...] = jnp.zeros_like(carry_s)

    idx, wt, rank, total = _route_math(lg_s[...], br_ref[...], carry_s[...])
    total = jnp.where(i > 0, total, 0.0)
    idx_ref[...] = idx
    wt_ref[...] = wt
    rank_ref[...] = rank
    carry_s[...] = total
    cnt_ref[...] = jnp.broadcast_to(total, cnt_ref.shape).astype(I32)

    gt1 = mod_ref[0, 2:3, :]
    sh2 = mod_ref[0, 3:4, :]
    sc2 = mod_ref[0, 4:5, :]
    y = (_sigmoid(ga_ref[...].astype(F32)) * ya_ref[...].astype(F32)
         + _sigmoid(gb_ref[...].astype(F32)) * _dot(ob_ref[...], wmo_ref[...]))
    mix = _dot(y.astype(BF16), wout_ref[...])
    x1 = _layer_norm(alpha * x_ref[...] + (1.0 + gt1) * mix, g_ref[...], b_ref[...])
    x1_ref[...] = x1
    h2 = x1 * (1.0 + sc2) + sh2
    h2_ref[...] = _pack_bf16_pair(h2)
    lg_s[...] = _dot_nt(wr_ref[...], h2.astype(BF16))


def _merge_route(alpha, x2, y_a, o_b, gate_a, gate_b, mod6, w_mla_o, w_out, ln_g, ln_b, wr_t, b_router, S):
    N, D = x2.shape
    E = wr_t.shape[0]
    tm = min(S, 512)
    per_b = S // tm
    n = N // tm
    cur = lambda i: jnp.minimum(i, n - 1)
    prev = lambda i: jnp.maximum(i - 1, 0)
    tok = lambda w: pl.BlockSpec((tm, w), lambda i: (cur(i), 0))
    full = lambda a: pl.BlockSpec(a.shape, lambda i: (0,) * a.ndim)
    lane_blk = pl.BlockSpec((TOP_K, tm), lambda i: (0, prev(i)))
    br = b_router.reshape(E, 1).astype(F32)
    return pl.pallas_call(
        functools.partial(_merge_kernel, alpha),
        out_shape=(jax.ShapeDtypeStruct((N, D), F32), jax.ShapeDtypeStruct((N, D // 2), I32),
                   jax.ShapeDtypeStruct((TOP_K, N), I32), jax.ShapeDtypeStruct((N, LANES), F32),
                   jax.ShapeDtypeStruct((TOP_K, N), I32), jax.ShapeDtypeStruct((E, LANES), I32)),
        grid=(n + 1,),
        in_specs=[tok(D), tok(D), tok(D), tok(D), tok(D),
                  pl.BlockSpec((1, 6, D), lambda i: (cur(i) // per_b, 0, 0)),
                  full(w_mla_o), full(w_out), full(ln_g), full(ln_b), full(wr_t), full(br)],
        out_specs=(tok(D), tok(D // 2), lane_blk, pl.BlockSpec((tm, LANES), lambda i: (prev(i), 0)), lane_blk,
                   pl.BlockSpec((E, LANES), lambda i: (0, 0))),
        scratch_shapes=[pltpu.VMEM((E, tm), F32), pltpu.VMEM((E, 1), F32)],
        compiler_params=_cparams("arbitrary"),
        name="merge_ln1_route",
    )(x2, y_a, o_b, gate_a, gate_b, mod6, w_mla_o, w_out, ln_g, ln_b, wr_t, br)


def _first_argmax(v, io, n):
    m = jnp.max(v, axis=0, keepdims=True)
    idx = jnp.min(jnp.where(v == m, io, n), axis=0, keepdims=True)
    return m, idx


def _route_math(lg, br, carry):
    E, T = lg.shape
    gsz = E // N_GROUPS
    neg = -jnp.inf
    s = _sigmoid(lg)
    biased = s + br
    eio = lax.broadcasted_iota(I32, (E, T), 0)
    gio = lax.broadcasted_iota(I32, (gsz, T), 0)

    gs = []
    for g in range(N_GROUPS):
        blk = biased[g * gsz:(g + 1) * gsz]
        m1, i1 = _first_argmax(blk, gio, gsz)
        m2 = jnp.max(jnp.where(gio == i1, neg, blk), axis=0, keepdims=True)
        gs.append(m1 + m2)
    cur = jnp.concatenate(gs, axis=0)
    nio = lax.broadcasted_iota(I32, (N_GROUPS, T), 0)
    gsel = jnp.zeros((N_GROUPS, T), F32)
    for _ in range(TOPK_GROUPS):
        _, gi = _first_argmax(cur, nio, N_GROUPS)
        hit = nio == gi
        gsel = jnp.where(hit, 1.0, gsel)
        cur = jnp.where(hit, neg, cur)
    emask = jnp.concatenate([jnp.broadcast_to(gsel[g:g + 1], (gsz, T)) for g in range(N_GROUPS)], axis=0) > 0.0

    cur = jnp.where(emask, biased, neg)
    idxs, ws = [], []
    sel = jnp.zeros((E, T), F32)
    for _ in range(TOP_K):
        _, ei = _first_argmax(cur, eio, E)
        hit = eio == ei
        idxs.append(ei)
        ws.append(jnp.sum(jnp.where(hit, s, 0.0), axis=0, keepdims=True))
        sel = jnp.where(hit, 1.0, sel)
        cur = jnp.where(hit, neg, cur)
    w = jnp.concatenate(ws, axis=0)
    w = w / jnp.sum(w, axis=0, keepdims=True) * ROUTED_SCALE

    wpad = jnp.concatenate([w, jnp.zeros((LANES - TOP_K, T), F32)], axis=0)

    r = lax.broadcasted_iota(I32, (T, T), 0)
    c = lax.broadcasted_iota(I32, (T, T), 1)
    before = (r < c).astype(BF16)
    cnt = _dot(sel.astype(BF16), before) + carry
    ranks = [jnp.sum(jnp.where(eio == idxs[k], cnt, 0.0), axis=0, keepdims=True) for k in range(TOP_K)]
    total = cnt[:, T - 1:T] + sel[:, T - 1:T]
    return jnp.concatenate(idxs, axis=0), wpad.T, jnp.concatenate(ranks, axis=0).astype(I32), total


def _dest_kernel(idx_ref, rank_ref, ps_ref, dest_ref):
    K, T = idx_ref.shape
    E = ps_ref.shape[0]
    eio = lax.broadcasted_iota(I32, (E, T), 0)
    ps = ps_ref[...]
    rows = [jnp.sum(jnp.where(eio == idx_ref[k:k + 1, :], ps, 0), axis=0, keepdims=True) for k in range(K)]
    dest_ref[...] = jnp.concatenate(rows, axis=0) + rank_ref[...]


def _dest_slots(idx, rank, pad_start):
    K, N = idx.shape
    E = pad_start.shape[0]
    T = min(N, 512)
    spec = pl.BlockSpec((K, T), lambda i: (0, i))
    return pl.pallas_call(
        _dest_kernel,
        out_shape=jax.ShapeDtypeStruct((K, N), I32),
        grid=(N // T,),
        in_specs=[spec, spec, pl.BlockSpec((E, 1), lambda i: (0, 0))],
        out_specs=spec,
        compiler_params=_cparams("arbitrary"),
        name="dest_slots",
    )(idx, rank, pad_start.reshape(E, 1))


SC_ROWS = 64


def _sc_workers():
    info = plsc.get_sparse_core_info()
    return info.num_cores, info.num_subcores


def _dispatch_rows(h2p, dest3, n_slots):
    N, W = h2p.shape
    n_chunks, K, R = dest3.shape
    nc, ns = _sc_workers()
    per_w = n_chunks // (nc * ns)
    mesh = plsc.VectorSubcoreMesh(core_axis_name="c", subcore_axis_name="s")

    @functools.partial(
        pl.kernel, mesh=mesh,
        out_type=jax.ShapeDtypeStruct((n_slots, W), I32),
        scratch_types=[pltpu.VMEM((K, R), I32), pltpu.VMEM((K, R), I32),
                       pltpu.VMEM((R, W), I32), pltpu.VMEM((R, W), I32),
                       pltpu.SemaphoreType.DMA, pltpu.SemaphoreType.DMA, pltpu.SemaphoreType.DMA],
    )
    def k(h_hbm, d_hbm, xs_hbm, idx0, idx1, rows0, rows1, lsem0, lsem1, ssem):
        idx, rows, lsem = (idx0, idx1), (rows0, rows1), (lsem0, lsem1)
        base = (lax.axis_index("s") * nc + lax.axis_index("c")) * per_w

        def loads(ch, b):
            return (pltpu.make_async_copy(d_hbm.at[ch], idx[b], lsem[b]),
                    pltpu.make_async_copy(h_hbm.at[pl.ds(ch * R, R)], rows[b], lsem[b]))

        for cp in loads(base, 0):
            cp.start()
        for cp in loads(base, 0):
            cp.wait()

        @pl.loop(0, per_w, step=2)
        def _(j):
            for b in range(2):
                ch = base + j + b
                more = j + b + 1 < per_w

                @pl.when(more)
                def _():
                    for cp in loads(ch + 1, 1 - b):
                        cp.start()

                scatters = [pltpu.make_async_copy(rows[b], xs_hbm.at[idx[b].at[kk]], ssem) for kk in range(K)]
                for cp in scatters:
                    cp.start()
                for cp in scatters:
                    cp.wait()

                @pl.when(more)
                def _():
                    for cp in loads(ch + 1, 1 - b):
                        cp.wait()

    return k(h2p, dest3)


def _combine_rows(ys, dest3, N):
    n_slots, W = ys.shape
    n_chunks, K, R = dest3.shape
    nc, ns = _sc_workers()
    per_w = n_chunks // (nc * ns)
    mesh = plsc.VectorSubcoreMesh(core_axis_name="c", subcore_axis_name="s")

    @functools.partial(
        pl.kernel, mesh=mesh,
        out_type=jax.ShapeDtypeStruct((K, N, W), I32),
        scratch_types=[pltpu.VMEM((K, R), I32), pltpu.VMEM((R, W), I32), pltpu.VMEM((R, W), I32),
                       pltpu.SemaphoreType.DMA, pltpu.SemaphoreType.DMA,
                       pltpu.SemaphoreType.DMA, pltpu.SemaphoreType.DMA],
    )
    def k(ys_hbm, d_hbm, yk_hbm, idx_v, rows0, rows1, gsem0, gsem1, wsem0, wsem1):
        rows, gsem, wsem = (rows0, rows1), (gsem0, gsem1), (wsem0, wsem1)
        base = (lax.axis_index("s") * nc + lax.axis_index("c")) * per_w

        @pl.loop(0, per_w)
        def _(j):
            ch = base + j
            pltpu.sync_copy(d_hbm.at[ch], idx_v)

            def gather(kk):
                return pltpu.make_async_copy(ys_hbm.at[idx_v.at[kk]], rows[kk % 2], gsem[kk % 2])

            def write(kk):
                return pltpu.make_async_copy(rows[kk % 2], yk_hbm.at[kk, pl.ds(ch * R, R)], wsem[kk % 2])

            gather(0).start()
            for kk in range(K):
                gather(kk).wait()
                if kk + 1 < K:
                    if kk >= 1:
                        write(kk - 1).wait()
                    gather(kk + 1).start()
                write(kk).start()
            write(K - 2).wait()
            write(K - 1).wait()

    return k(ys, dest3)


def _expert_kernel(b0_ref, nb_ref, xs_hbm, w1_ref, w3_ref, w2_ref, after_hbm, ys_hbm, w13_s, w2_s, xbuf, ybuf, sem_in, sem_out):
    e = pl.program_id(0)
    n_exp = pl.num_programs(0)
    F = w1_ref.shape[2]
    bm = xbuf.shape[1]
    half = xbuf.shape[2]
    nb = nb_ref[e]
    b0 = b0_ref[e]
    total = b0_ref[n_exp - 1] + nb_ref[n_exp - 1]

    nbuf = xbuf.shape[0]

    def in_copy(g):
        slot = g % nbuf
        return pltpu.make_async_copy(xs_hbm.at[pl.ds(g * bm, bm)], xbuf.at[slot], sem_in.at[slot])

    def out_copy(g):
        slot = g % nbuf
        return pltpu.make_async_copy(ybuf.at[slot], ys_hbm.at[pl.ds(g * bm, bm)], sem_out.at[slot])

    @pl.when(e == 0)
    def _():
        for g0 in range(nbuf - 1):
            @pl.when(g0 < total)
            def _():
                in_copy(g0).start()

    @pl.when(nb > 0)
    def _():
        w13_s[:, :F] = w1_ref[0].astype(BF16)
        w13_s[:, F:] = w3_ref[0].astype(BF16)
        w2_s[...] = w2_ref[0].astype(BF16)

    def block(j, carry):
        g = b0 + j
        slot = g % nbuf
        in_copy(g).wait()

        @pl.when(g + nbuf - 1 < total)
        def _():
            in_copy(g + nbuf - 1).start()

        @pl.when(g >= nbuf)
        def _():
            out_copy(g - nbuf).wait()

        lo, hi = _unpack_bf16_pair(xbuf[slot])
        ab = _dot(lo.astype(BF16), w13_s[:half, :]) + _dot(hi.astype(BF16), w13_s[half:, :])
        hid = (_silu(ab[:, :F]) * ab[:, F:]).astype(BF16)
        ybuf[slot] = _pack_bf16_pair(_dot(hid, w2_s[...]))
        out_copy(g).start()
        return carry

    lax.fori_loop(0, nb, block, 0)

    @pl.when(e == n_exp - 1)
    def _():
        for back in range(nbuf, 0, -1):
            @pl.when(total >= back)
            def _():
                out_copy(total - back).wait()


def _experts(xs, blk0, nblk, w_e1, w_e3, w_e2, after):
    n_slots, W = xs.shape
    E, D, F = w_e1.shape
    bm = EXPERT_BLOCK
    grid_spec = pltpu.PrefetchScalarGridSpec(
        num_scalar_prefetch=2,
        grid=(E,),
        in_specs=[
            pl.BlockSpec(memory_space=pl.ANY),
            pl.BlockSpec((1, D, F), lambda e, b0, nb: (e, 0, 0)),
            pl.BlockSpec((1, D, F), lambda e, b0, nb: (e, 0, 0)),
            pl.BlockSpec((1, F, D), lambda e, b0, nb: (e, 0, 0)),
            pl.BlockSpec(memory_space=pl.ANY),
        ],
        out_specs=pl.BlockSpec(memory_space=pl.ANY),
        scratch_shapes=[pltpu.VMEM((D, 2 * F), BF16), pltpu.VMEM((F, D), BF16),
                        pltpu.VMEM((EXPERT_RING, bm, W), I32), pltpu.VMEM((EXPERT_RING, bm, W), I32),
                        pltpu.SemaphoreType.DMA((EXPERT_RING,)), pltpu.SemaphoreType.DMA((EXPERT_RING,))],
    )
    return pl.pallas_call(
        _expert_kernel,
        out_shape=jax.ShapeDtypeStruct((n_slots, W), I32),
        grid_spec=grid_spec,
        compiler_params=_cparams("arbitrary"),
        name="experts",
    )(blk0, nblk, xs, w_e1, w_e3, w_e2, after)


def _shared_kernel(h2_ref, ws13_ref, ws2_ref, o_ref):
    F = ws2_ref.shape[0]
    half = h2_ref.shape[1]
    lo, hi = _unpack_bf16_pair(h2_ref[...])
    ab = _dot(lo.astype(BF16), ws13_ref[:half, :]) + _dot(hi.astype(BF16), ws13_ref[half:, :])
    hid = (_silu(ab[:, :F]) * ab[:, F:]).astype(BF16)
    o_ref[...] = _dot(hid, ws2_ref[...]).astype(o_ref.dtype)


def _shared_expert(h2p, ws13, ws2):
    N, W = h2p.shape
    D = ws2.shape[1]
    tm = min(N, 512)
    full = lambda a: pl.BlockSpec(a.shape, lambda i: (0,) * a.ndim)
    return pl.pallas_call(
        _shared_kernel,
        out_shape=jax.ShapeDtypeStruct((N, D), BF16),
        grid=(N // tm,),
        in_specs=[pl.BlockSpec((tm, W), lambda i: (i, 0)), full(ws13), full(ws2)],
        out_specs=pl.BlockSpec((tm, D), lambda i: (i, 0)),
        compiler_params=_cparams("arbitrary"),
        name="shared_expert",
    )(h2p, ws13, ws2)


def _final_kernel(alpha, yk_ref, wt_ref, sh_ref, x1_ref, mod_ref, g_ref, b_ref, *rest):
    o_ref = rest[-1]
    gt2 = mod_ref[0, 5:6, :]
    wt = wt_ref[...]
    mlo = jnp.zeros(yk_ref.shape[1:], F32)
    mhi = jnp.zeros(yk_ref.shape[1:], F32)
    for k in range(TOP_K):
        lo, hi = _unpack_bf16_pair(yk_ref[k])
        wk = wt[:, k:k + 1]
        mlo = mlo + wk * lo
        mhi = mhi + wk * hi
    ffn = jnp.concatenate([mlo, mhi], axis=1) + sh_ref[...].astype(F32)
    o_ref[...] = _layer_norm(alpha * x1_ref[...] + (1.0 + gt2) * ffn, g_ref[...], b_ref[...])


def _final_part(alpha, yk, wt, sh, x1, mod6, ln_g, ln_b, S, part, n_parts, prev):
    N, D = x1.shape
    K, n_part, W = yk.shape
    tm = min(S, 256)
    per_b = S // tm
    steps = n_part // tm
    off = part * steps
    tok = lambda n: pl.BlockSpec((tm, n), lambda i: (off + i, 0))
    full = lambda a: pl.BlockSpec(a.shape, lambda i: (0,) * a.ndim)
    in_specs = [pl.BlockSpec((K, tm, W), lambda i: (0, i, 0)), tok(LANES), tok(D), tok(D),
                pl.BlockSpec((1, 6, D), lambda i: ((off + i) // per_b, 0, 0)), full(ln_g), full(ln_b)]
    args = [yk, wt, sh, x1, mod6, ln_g, ln_b]
    aliases = {}
    if prev is not None:
        in_specs.append(pl.BlockSpec(memory_space=pl.ANY))
        args.append(prev)
        aliases = {len(args) - 1: 0}
    return pl.pallas_call(
        functools.partial(_final_kernel, alpha),
        out_shape=jax.ShapeDtypeStruct((N, D), F32),
        grid=(steps,),
        in_specs=in_specs,
        out_specs=tok(D),
        input_output_aliases=aliases,
        compiler_params=_cparams("arbitrary"),
        name="combine_ln2",
    )(*args)


def _moe(alpha, h2p, idx, wt, rank, cnt, x1, mod6, w_e1, w_e3, w_e2, w_s1, w_s3, w_s2, ln_g, ln_b, S):
    N = x1.shape[0]
    E = w_e1.shape[0]
    bm = EXPERT_BLOCK
    counts = cnt[:, 0]
    padded = (counts + bm - 1) // bm * bm
    pad_end = jnp.cumsum(padded)
    pad_start = pad_end - padded
    n_slots = N * TOP_K + E * bm
    dest = _dest_slots(idx, rank, pad_start.astype(I32))
    dest3 = dest.reshape(TOP_K, N // SC_ROWS, SC_ROWS).transpose(1, 0, 2)
    xs = _dispatch_rows(h2p, dest3, n_slots)
    ws13 = jnp.concatenate([w_s1, w_s3], axis=1).astype(BF16)
    sh = _shared_expert(h2p, ws13, w_s2.astype(BF16))
    ys = _experts(xs, (pad_start // bm).astype(I32), (padded // bm).astype(I32), w_e1, w_e3, w_e2, sh)
    n_chunks = dest3.shape[0]
    cpp = n_chunks // COMBINE_PARTS
    out = None
    for p in range(COMBINE_PARTS):
        yk = _combine_rows(ys, dest3[p * cpp:(p + 1) * cpp], cpp * SC_ROWS)
        out = _final_part(alpha, yk, wt, sh, x1, mod6, ln_g, ln_b, S, p, COMBINE_PARTS, out)
    return out


def kernel(x, c, positions, w_ada, b_ada, w_in, w_gla_a2, b_gla_a, g_gla_norm, w_gla_o, g_cq, w_uq, g_ckv, w_ukv, w_mla_o, w_out, ln1_g, ln1_b, w_router, b_router, w_e1, w_e3, w_e2, w_s1, w_s3, w_s2, ln2_g, ln2_b):
    B, S, D = x.shape
    N = B * S
    depth = w_ada.shape[0]
    alpha = (2.0 * depth) ** 0.25
    row = lambda a: a.reshape(1, -1)
    cc, ss = _rope_tables(positions)
    x2 = x.reshape(N, D)
    for l in range(depth):
        mod6 = _modulation(c, w_ada[l], b_ada[l]).reshape(B, 6, D)
        qk, gv, gr, cq, ckv, gate_a, gate_b, tail = _in_projection(x2, mod6, _prep_w_in(w_in[l], D), S)
        wa2_p = jnp.concatenate(
            [w_gla_a2[l], jnp.zeros((LANES - GLA_GATE_RANK, w_gla_a2.shape[2]), F32)], axis=0).astype(BF16)
        y_a = _gla(qk, gv, gr, tail, wa2_p, row(b_gla_a[l]), row(g_gla_norm[l]), w_gla_o[l].astype(BF16), B, S)
        qc, kc, vv = _mla_prep(cq, ckv, tail, cc, ss, row(g_cq[l]), _prep_w_uq(w_uq[l]),
                               row(g_ckv[l]), _prep_w_ukv(w_ukv[l]))
        o_b = _mla_attention(qc, kc, vv, B, S)
        x1, h2p, idx, wt, rank, cnt = _merge_route(
            alpha, x2, y_a, o_b, gate_a, gate_b, mod6, w_mla_o[l].astype(BF16), w_out[l].astype(BF16),
            row(ln1_g[l]), row(ln1_b[l]), w_router[l].T.astype(BF16), b_router[l], S)
        x2 = _moe(alpha, h2p, idx, wt, rank, cnt, x1, mod6, w_e1[l], w_e3[l], w_e2[l],
                  w_s1[l], w_s3[l], w_s2[l], row(ln2_g[l]), row(ln2_b[l]), S)
    return x2.reshape(B, S, D)
```

```python
import functools

import jax
import jax.numpy as jnp
from jax import lax
from jax.experimental import pallas as pl
from jax.experimental.pallas import tpu as pltpu
from jax.experimental.pallas import tpu_sc as plsc

CHUNK = 64
GLA_HEADS = 4
GLA_DK = 128
GLA_DV = 256
GLA_GATE_RANK = 16
GLA_GATE_TAU = 16.0
MLA_HEADS = 8
MLA_Q_RANK = 768
MLA_KV_RANK = 256
MLA_NOPE = 128
MLA_ROPE = 64
MLA_V = 128
ROPE_THETA = 10000.0
N_EXPERTS = 256
TOP_K = 8
N_GROUPS = 8
TOPK_GROUPS = 4
D_EXPERT = 256
ROUTED_SCALE = 2.5
LN_EPS = 1e-5
RMS_EPS = 1e-6
LOG2E = 1.4426950408889634

LANES = 128
VMEM_LIMIT = 56 * 1024 * 1024
EXPERT_BLOCK = 512
EXPERT_RING = 3
COMBINE_PARTS = 8

F32 = jnp.float32
BF16 = jnp.bfloat16
I32 = jnp.int32


def _cparams(*sem):
    return pltpu.CompilerParams(dimension_semantics=sem, vmem_limit_bytes=VMEM_LIMIT)


def _sigmoid(x):
    return 1.0 / (1.0 + jnp.exp(-x))


def _silu(x):
    return x * _sigmoid(x)


def _dot(a, b):
    return jnp.dot(a, b, preferred_element_type=F32)


def _dot_nt(a, b):
    return lax.dot_general(a, b, (((1,), (1,)), ((), ())), preferred_element_type=F32)


def _dot_tn(a, b):
    return lax.dot_general(a, b, (((0,), (0,)), ((), ())), preferred_element_type=F32)


def _pack_bf16_pair(x):
    w = x.shape[1] // 2
    u = lax.bitcast_convert_type(x.astype(BF16).astype(F32), I32)
    lo = lax.shift_right_logical(u[:, :w], jnp.int32(16))
    hi = jnp.bitwise_and(u[:, w:], jnp.int32(-65536))
    return jnp.bitwise_or(lo, hi)


def _unpack_bf16_pair(p):
    lo = lax.bitcast_convert_type(lax.shift_left(p, jnp.int32(16)), F32)
    hi = lax.bitcast_convert_type(jnp.bitwise_and(p, jnp.int32(-65536)), F32)
    return lo, hi


def _mod_kernel(c_ref, w_ref, b_ref, o_ref):
    cond = _silu(c_ref[...]).astype(BF16)
    o_ref[...] = _dot(cond, w_ref[...].astype(BF16)) + b_ref[...]


def _modulation(c, w_ada, b_ada):
    B, D = c.shape
    W = w_ada.shape[1]
    tn = D
    return pl.pallas_call(
        _mod_kernel,
        out_shape=jax.ShapeDtypeStruct((B, W), F32),
        grid=(W // tn,),
        in_specs=[
            pl.BlockSpec((B, D), lambda j: (0, 0)),
            pl.BlockSpec((D, tn), lambda j: (0, j)),
            pl.BlockSpec((1, tn), lambda j: (0, j)),
        ],
        out_specs=pl.BlockSpec((B, tn), lambda j: (0, j)),
        compiler_params=_cparams("arbitrary"),
        name="adaln_mod",
    )(c, w_ada, b_ada.reshape(1, W))


def _rope_kernel(pos_ref, f_ref, cos_ref, sin_ref):
    ang = pos_ref[...] * f_ref[...]
    cos_ref[...] = jnp.cos(ang)
    sin_ref[...] = jnp.sin(ang)


def _rope_tables(positions):
    B, S = positions.shape
    N = B * S
    half = MLA_ROPE // 2
    per_row = LANES // half
    inv_freq = ROPE_THETA ** (-jnp.arange(half, dtype=F32) * (2.0 / MLA_ROPE))
    f4 = jnp.tile(inv_freq, per_row).reshape(1, LANES)
    pos = jnp.repeat(positions.reshape(N // per_row, per_row).astype(F32), half, axis=1)
    rows = N // per_row
    tm = min(rows, 1024)
    spec = pl.BlockSpec((tm, LANES), lambda i: (i, 0))
    cos4, sin4 = pl.pallas_call(
        _rope_kernel,
        out_shape=(jax.ShapeDtypeStruct((rows, LANES), F32), jax.ShapeDtypeStruct((rows, LANES), F32)),
        grid=(rows // tm,),
        in_specs=[spec, pl.BlockSpec((1, LANES), lambda i: (0, 0))],
        out_specs=(spec, spec),
        compiler_params=_cparams("arbitrary"),
        name="rope_tables",
    )(pos, f4)
    cos = cos4.reshape(N, half)
    sin = sin4.reshape(N, half)
    return jnp.tile(cos, (1, per_row)), jnp.tile(jnp.concatenate([-sin, sin], axis=1), (1, per_row // 2))


QK_W = 2 * GLA_HEADS * GLA_DK
GV_W = GLA_HEADS * GLA_DV
TAIL_W = 3 * LANES
IN_SEG = (QK_W, GV_W, GV_W, MLA_Q_RANK, MLA_KV_RANK, 1024, 1024, TAIL_W)


def _prep_w_in(w_in, D):
    s = [GLA_HEADS * GLA_DK, GLA_HEADS * GLA_DK, GV_W, GV_W, GLA_GATE_RANK, MLA_Q_RANK, MLA_KV_RANK, MLA_ROPE, D, D]
    offs = [0]
    for n in s:
        offs.append(offs[-1] + n)
    gq, gk, gv, gr, ga, cq, ckv, kr, gate_a, gate_b = [w_in[:, offs[i]:offs[i + 1]] for i in range(10)]
    half = MLA_ROPE // 2
    kr_sw = jnp.concatenate([kr[:, half:], kr[:, :half]], axis=1)
    pad = jnp.zeros((w_in.shape[0], LANES - GLA_GATE_RANK), w_in.dtype)
    return jnp.concatenate([gq, gk, gv, gr, cq, ckv, gate_a, gate_b, kr, kr, kr_sw, kr_sw, ga, pad], axis=1).astype(BF16)


def _inproj_kernel(x_ref, mod_ref, w_ref, *out_refs):
    sh1 = mod_ref[0, 0:1, :]
    sc1 = mod_ref[0, 1:2, :]
    h = (x_ref[...] * (1.0 + sc1) + sh1).astype(BF16)
    off = 0
    for ref in out_refs:
        n = ref.shape[-1]
        ref[...] = _dot(h, w_ref[:, off:off + n]).astype(ref.dtype)
        off += n


def _in_projection(x2, mod6, w_in_p, S):
    N, D = x2.shape
    W = w_in_p.shape[1]
    tm = min(S, 512)
    per_b = S // tm
    return pl.pallas_call(
        _inproj_kernel,
        out_shape=tuple(jax.ShapeDtypeStruct((N, n), BF16) for n in IN_SEG),
        grid=(N // tm,),
        in_specs=[
            pl.BlockSpec((tm, D), lambda i: (i, 0)),
            pl.BlockSpec((1, 6, D), lambda i: (i // per_b, 0, 0)),
            pl.BlockSpec((D, W), lambda i: (0, 0), pipeline_mode=pl.Buffered(1)),
        ],
        out_specs=tuple(pl.BlockSpec((tm, n), lambda i: (i, 0)) for n in IN_SEG),
        compiler_params=_cparams("arbitrary"),
        name="in_proj",
    )(x2, mod6, w_in_p)


def _gla_kernel(qk_ref, v_ref, gr_ref, tail_ref, wa2_ref, ba_ref, gn_ref, wo_ref, y_ref,
                st_ref, kd_ref, dec_ref, sall_ref, o_ref):
    t = pl.program_id(1)

    @pl.when(t == 0)
    def _():
        st_ref[...] = jnp.zeros_like(st_ref)

    ts = qk_ref.shape[0]
    nch = ts // CHUNK
    HK = GLA_HEADS * GLA_DK
    r = lax.broadcasted_iota(I32, (CHUNK, CHUNK), 0)
    c = lax.broadcasted_iota(I32, (CHUNK, CHUNK), 1)
    tri = (r >= c).astype(BF16)
    qscale = GLA_DK ** -0.5

    z = _dot(tail_ref[:, 2 * LANES:3 * LANES], wa2_ref[...]) + ba_ref[...]
    log_a = (jnp.minimum(z, 0.0) - jnp.log(1.0 + jnp.exp(-jnp.abs(z)))) * (1.0 / GLA_GATE_TAU)
    la_hi = log_a.astype(BF16)
    la_lo = (log_a - la_hi.astype(F32)).astype(BF16)
    for n in range(nch):
        rows = slice(n * CHUNK, (n + 1) * CHUNK)
        G = _dot(tri, la_hi[rows]) + _dot(tri, la_lo[rows])
        g_end = G[CHUNK - 1:CHUNK, :]
        kd_ref[rows, :] = (qk_ref[rows, HK:2 * HK].astype(F32) * jnp.exp(g_end - G)).astype(BF16)
        dec_ref[n:n + 1, :] = jnp.exp(g_end)

    for h in range(GLA_HEADS):
        ks = slice(h * GLA_DK, (h + 1) * GLA_DK)
        vs = slice(h * GLA_DV, (h + 1) * GLA_DV)
        st = st_ref[h]
        for n in range(nch):
            rows = slice(n * CHUNK, (n + 1) * CHUNK)
            st = st * dec_ref[n:n + 1, ks] + _dot_tn(v_ref[rows, vs], kd_ref[rows, ks])
            sall_ref[n * GLA_HEADS + h] = st.astype(BF16)
        st_ref[h] = st

    for n in range(nch):
        rows = slice(n * CHUNK, (n + 1) * CHUNK)
        for h in range(GLA_HEADS):
            ks = slice(h * GLA_DK, (h + 1) * GLA_DK)
            vs = slice(h * GLA_DV, (h + 1) * GLA_DV)
            qh = (qk_ref[rows, ks].astype(F32) * qscale).astype(BF16)
            o = _dot_nt(qh, sall_ref[n * GLA_HEADS + h])
            o = o * lax.rsqrt(jnp.mean(o * o, axis=-1, keepdims=True) + RMS_EPS) * gn_ref[...]
            o_ref[rows, vs] = (o * _silu(gr_ref[rows, vs].astype(F32))).astype(BF16)
    y_ref[...] = _dot(o_ref[...], wo_ref[...]).astype(y_ref.dtype)


def _gla(qk, gv, gr, tail, wa2_p, b_a, g_norm, w_o, B, S):
    N = B * S
    D = w_o.shape[1]
    ts = min(S, 512)
    per_b = S // ts
    HK = GLA_HEADS * GLA_DK
    tok = lambda n: pl.BlockSpec((ts, n), lambda b, t: (b * per_b + t, 0))
    full = lambda a: pl.BlockSpec(a.shape, lambda b, t: (0,) * a.ndim)
    return pl.pallas_call(
        _gla_kernel,
        out_shape=jax.ShapeDtypeStruct((N, D), BF16),
        grid=(B, per_b),
        in_specs=[tok(QK_W), tok(GV_W), tok(GV_W), tok(TAIL_W), full(wa2_p), full(b_a), full(g_norm), full(w_o)],
        out_specs=tok(D),
        scratch_shapes=[pltpu.VMEM((GLA_HEADS, GLA_DV, GLA_DK), F32),
                        pltpu.VMEM((ts, GLA_HEADS * GLA_DK), BF16),
                        pltpu.VMEM((ts // CHUNK, GLA_HEADS * GLA_DK), F32),
                        pltpu.VMEM((ts // CHUNK * GLA_HEADS, GLA_DV, GLA_DK), BF16),
                        pltpu.VMEM((ts, GV_W), BF16)],
        compiler_params=_cparams("arbitrary", "arbitrary"),
        name="gla",
    )(qk, gv, gr, tail, wa2_p, b_a, g_norm, w_o)


HQ = MLA_HEADS * 2 * LANES


def _prep_w_uq(w_uq):
    dh = MLA_NOPE + MLA_ROPE
    half = MLA_ROPE // 2
    nope = [w_uq[:, h * dh:h * dh + MLA_NOPE] for h in range(MLA_HEADS)]
    rope = [w_uq[:, h * dh + MLA_NOPE:(h + 1) * dh] for h in range(MLA_HEADS)]
    rope_sw = [jnp.concatenate([r[:, half:], r[:, :half]], axis=1) for r in rope]
    return jnp.concatenate(nope + rope + rope_sw, axis=1).astype(BF16)


def _prep_w_ukv(w_ukv):
    dh = MLA_NOPE + MLA_V
    kn = [w_ukv[:, h * dh:h * dh + MLA_NOPE] for h in range(MLA_HEADS)]
    vv = [w_ukv[:, h * dh + MLA_NOPE:(h + 1) * dh] for h in range(MLA_HEADS)]
    return jnp.concatenate(kn + vv, axis=1).astype(BF16)


def _rms(x, g):
    return x * lax.rsqrt(jnp.mean(x * x, axis=-1, keepdims=True) + RMS_EPS) * g


def _mla_prep_kernel(cq_ref, ckv_ref, tail_ref, cc_ref, ss_ref, gq_ref, wq_ref, gkv_ref, wkv_ref, q_ref, k_ref, v_ref):
    tm = cq_ref.shape[0]
    NP = MLA_HEADS * MLA_NOPE
    RP = MLA_HEADS * MLA_ROPE
    scale = (MLA_NOPE + MLA_ROPE) ** -0.5 * LOG2E
    cc = cc_ref[...]
    ss = ss_ref[...]
    cqn = _rms(cq_ref[...].astype(F32), gq_ref[...]).astype(BF16)
    qf = _dot(cqn, wq_ref[...]) * scale
    ckvn = _rms(ckv_ref[...].astype(F32), gkv_ref[...]).astype(BF16)
    kv = _dot(ckvn, wkv_ref[...])
    krr = (tail_ref[:, 0:LANES].astype(F32) * cc + tail_ref[:, LANES:2 * LANES].astype(F32) * ss).astype(BF16)
    lane = lax.broadcasted_iota(I32, (tm, LANES), 1)
    first = lane < MLA_ROPE
    for j in range(MLA_HEADS // 2):
        a = NP + j * LANES
        rot = qf[:, a:a + LANES] * cc + qf[:, a + RP:a + RP + LANES] * ss
        for h, keep in ((2 * j, first), (2 * j + 1, jnp.logical_not(first))):
            base = h * 2 * LANES
            q_ref[:, base:base + LANES] = qf[:, h * MLA_NOPE:(h + 1) * MLA_NOPE].astype(BF16)
            q_ref[:, base + LANES:base + 2 * LANES] = jnp.where(keep, rot, 0.0).astype(BF16)
            k_ref[:, base:base + LANES] = kv[:, h * MLA_NOPE:(h + 1) * MLA_NOPE].astype(BF16)
            k_ref[:, base + LANES:base + 2 * LANES] = krr
    v_ref[...] = kv[:, NP:].astype(BF16)


def _mla_prep(cq, ckv, tail, cc, ss, g_cq, w_uq_p, g_ckv, w_ukv_p):
    N = cq.shape[0]
    tm = min(N, 512)
    tok = lambda n: pl.BlockSpec((tm, n), lambda i: (i, 0))
    full = lambda a: pl.BlockSpec(a.shape, lambda i: (0,) * a.ndim)
    HV = MLA_HEADS * MLA_V
    return pl.pallas_call(
        _mla_prep_kernel,
        out_shape=(jax.ShapeDtypeStruct((N, HQ), BF16), jax.ShapeDtypeStruct((N, HQ), BF16),
                   jax.ShapeDtypeStruct((N, HV), BF16)),
        grid=(N // tm,),
        in_specs=[tok(MLA_Q_RANK), tok(MLA_KV_RANK), tok(TAIL_W), tok(LANES), tok(LANES),
                  full(g_cq), full(w_uq_p), full(g_ckv), full(w_ukv_p)],
        out_specs=(tok(HQ), tok(HQ), tok(HV)),
        compiler_params=_cparams("arbitrary"),
        name="mla_prep",
    )(cq, ckv, tail, cc, ss, g_cq, w_uq_p, g_ckv, w_ukv_p)


ATTN_TQ = 256


def _mla_attn_kernel(q_ref, k_ref, v_ref, o_ref, v1_ref):
    S = q_ref.shape[0]
    tq = min(S, ATTN_TQ)
    r = lax.broadcasted_iota(I32, (tq, tq), 0) // CHUNK
    c = lax.broadcasted_iota(I32, (tq, tq), 1) // CHUNK
    diag_mask = c <= r
    v1_ref[:, :MLA_V] = v_ref[...]
    v1_ref[:, MLA_V:] = jnp.ones((S, MLA_V), BF16)

    def scores(ii):
        l0 = ii * tq
        q = q_ref[l0:l0 + tq, :]
        sd = jnp.where(diag_mask, _dot_nt(q, k_ref[l0:l0 + tq, :]), -jnp.inf)
        so = _dot_nt(q, k_ref[0:l0, :]) if ii > 0 else None
        return sd, so

    def finish(ii, sd, so):
        l0 = ii * tq
        m = jnp.max(sd, axis=-1, keepdims=True)
        if so is not None:
            m = jnp.maximum(m, jnp.max(so, axis=-1, keepdims=True))
        acc = _dot(jnp.exp2((sd - m).astype(BF16)), v1_ref[l0:l0 + tq, :])
        if so is not None:
            acc = acc + _dot(jnp.exp2((so - m).astype(BF16)), v1_ref[0:l0, :])
        o_ref[l0:l0 + tq, :] = (acc[:, :MLA_V] / acc[:, MLA_V:]).astype(BF16)

    n_tiles = S // tq
    ahead = 2
    pending = [scores(ii) for ii in range(min(ahead, n_tiles))]
    for ii in range(n_tiles):
        if ii + ahead < n_tiles:
            pending.append(scores(ii + ahead))
        finish(ii, *pending.pop(0))


def _mla_attention(qc, kc, vv, B, S):
    N = B * S
    HV = MLA_HEADS * MLA_V
    return pl.pallas_call(
        _mla_attn_kernel,
        out_shape=jax.ShapeDtypeStruct((N, HV), BF16),
        grid=(B, MLA_HEADS),
        in_specs=[
            pl.BlockSpec((S, 2 * LANES), lambda b, h: (b, h)),
            pl.BlockSpec((S, 2 * LANES), lambda b, h: (b, h)),
            pl.BlockSpec((S, MLA_V), lambda b, h: (b, h)),
        ],
        out_specs=pl.BlockSpec((S, MLA_V), lambda b, h: (b, h)),
        scratch_shapes=[pltpu.VMEM((S, 2 * MLA_V), BF16)],
        compiler_params=_cparams("arbitrary", "arbitrary"),
        name="mla_attn",
    )(qc, kc, vv)


def _layer_norm(u, g, b):
    mu = jnp.mean(u, axis=-1, keepdims=True)
    d = u - mu
    var = jnp.mean(d * d, axis=-1, keepdims=True)
    return d * lax.rsqrt(var + LN_EPS) * g + b


def _merge_kernel(alpha, x_ref, ya_ref, ob_ref, ga_ref, gb_ref, mod_ref, wmo_ref, wout_ref, g_ref, b_ref, wr_ref,
                  br_ref, x1_ref, h2_ref, idx_ref, wt_ref, rank_ref, cnt_ref, lg_s, carry_s):
    i = pl.program_id(0)

    @pl.when(i == 0)
    def _():
        lg_s[...] = jnp.zeros_like(lg_s)
        carry_s[...] = jnp.zeros_like(carry_s)

    idx, wt, rank, total = _route_math(lg_s[...], br_ref[...], carry_s[...])
    total = jnp.where(i > 0, total, 0.0)
    idx_ref[...] = idx
    wt_ref[...] = wt
    rank_ref[...] = rank
    carry_s[...] = total
    cnt_ref[...] = jnp.broadcast_to(total, cnt_ref.shape).astype(I32)

    gt1 = mod_ref[0, 2:3, :]
    sh2 = mod_ref[0, 3:4, :]
    sc2 = mod_ref[0, 4:5, :]
    y = (_sigmoid(ga_ref[...].astype(F32)) * ya_ref[...].astype(F32)
         + _sigmoid(gb_ref[...].astype(F32)) * _dot(ob_ref[...], wmo_ref[...]))
    mix = _dot(y.astype(BF16), wout_ref[...])
    x1 = _layer_norm(alpha * x_ref[...] + (1.0 + gt1) * mix, g_ref[...], b_ref[...])
    x1_ref[...] = x1
    h2 = x1 * (1.0 + sc2) + sh2
    h2_ref[...] = _pack_bf16_pair(h2)
    lg_s[...] = _dot_nt(wr_ref[...], h2.astype(BF16))


def _merge_route(alpha, x2, y_a, o_b, gate_a, gate_b, mod6, w_mla_o, w_out, ln_g, ln_b, wr_t, b_router, S):
    N, D = x2.shape
    E = wr_t.shape[0]
    tm = min(S, 512)
    per_b = S // tm
    n = N // tm
    cur = lambda i: jnp.minimum(i, n - 1)
    prev = lambda i: jnp.maximum(i - 1, 0)
    tok = lambda w: pl.BlockSpec((tm, w), lambda i: (cur(i), 0))
    full = lambda a: pl.BlockSpec(a.shape, lambda i: (0,) * a.ndim)
    lane_blk = pl.BlockSpec((TOP_K, tm), lambda i: (0, prev(i)))
    br = b_router.reshape(E, 1).astype(F32)
    return pl.pallas_call(
        functools.partial(_merge_kernel, alpha),
        out_shape=(jax.ShapeDtypeStruct((N, D), F32), jax.ShapeDtypeStruct((N, D // 2), I32),
                   jax.ShapeDtypeStruct((TOP_K, N), I32), jax.ShapeDtypeStruct((N, LANES), F32),
                   jax.ShapeDtypeStruct((TOP_K, N), I32), jax.ShapeDtypeStruct((E, LANES), I32)),
        grid=(n + 1,),
        in_specs=[tok(D), tok(D), tok(D), tok(D), tok(D),
                  pl.BlockSpec((1, 6, D), lambda i: (cur(i) // per_b, 0, 0)),
                  full(w_mla_o), full(w_out), full(ln_g), full(ln_b), full(wr_t), full(br)],
        out_specs=(tok(D), tok(D // 2), lane_blk, pl.BlockSpec((tm, LANES), lambda i: (prev(i), 0)), lane_blk,
                   pl.BlockSpec((E, LANES), lambda i: (0, 0))),
        scratch_shapes=[pltpu.VMEM((E, tm), F32), pltpu.VMEM((E, 1), F32)],
        compiler_params=_cparams("arbitrary"),
        name="merge_ln1_route",
    )(x2, y_a, o_b, gate_a, gate_b, mod6, w_mla_o, w_out, ln_g, ln_b, wr_t, br)


def _first_argmax(v, io, n):
    m = jnp.max(v, axis=0, keepdims=True)
    idx = jnp.min(jnp.where(v == m, io, n), axis=0, keepdims=True)
    return m, idx


def _route_math(lg, br, carry):
    E, T = lg.shape
    gsz = E // N_GROUPS
    neg = -jnp.inf
    s = _sigmoid(lg)
    biased = s + br
    eio = lax.broadcasted_iota(I32, (E, T), 0)
    gio = lax.broadcasted_iota(I32, (gsz, T), 0)

    gs = []
    for g in range(N_GROUPS):
        blk = biased[g * gsz:(g + 1) * gsz]
        m1, i1 = _first_argmax(blk, gio, gsz)
        m2 = jnp.max(jnp.where(gio == i1, neg, blk), axis=0, keepdims=True)
        gs.append(m1 + m2)
    cur = jnp.concatenate(gs, axis=0)
    nio = lax.broadcasted_iota(I32, (N_GROUPS, T), 0)
    gsel = jnp.zeros((N_GROUPS, T), F32)
    for _ in range(TOPK_GROUPS):
        _, gi = _first_argmax(cur, nio, N_GROUPS)
        hit = nio == gi
        gsel = jnp.where(hit, 1.0, gsel)
        cur = jnp.where(hit, neg, cur)
    emask = jnp.concatenate([jnp.broadcast_to(gsel[g:g + 1], (gsz, T)) for g in range(N_GROUPS)], axis=0) > 0.0

    cur = jnp.where(emask, biased, neg)
    idxs, ws = [], []
    sel = jnp.zeros((E, T), F32)
    for _ in range(TOP_K):
        _, ei = _first_argmax(cur, eio, E)
        hit = eio == ei
        idxs.append(ei)
        ws.append(jnp.sum(jnp.where(hit, s, 0.0), axis=0, keepdims=True))
        sel = jnp.where(hit, 1.0, sel)
        cur = jnp.where(hit, neg, cur)
    w = jnp.concatenate(ws, axis=0)
    w = w / jnp.sum(w, axis=0, keepdims=True) * ROUTED_SCALE

    wpad = jnp.concatenate([w, jnp.zeros((LANES - TOP_K, T), F32)], axis=0)

    r = lax.broadcasted_iota(I32, (T, T), 0)
    c = lax.broadcasted_iota(I32, (T, T), 1)
    before = (r < c).astype(BF16)
    cnt = _dot(sel.astype(BF16), before) + carry
    ranks = [jnp.sum(jnp.where(eio == idxs[k], cnt, 0.0), axis=0, keepdims=True) for k in range(TOP_K)]
    total = cnt[:, T - 1:T] + sel[:, T - 1:T]
    return jnp.concatenate(idxs, axis=0), wpad.T, jnp.concatenate(ranks, axis=0).astype(I32), total


def _dest_kernel(idx_ref, rank_ref, ps_ref, dest_ref):
    K, T = idx_ref.shape
    E = ps_ref.shape[0]
    eio = lax.broadcasted_iota(I32, (E, T), 0)
    ps = ps_ref[...]
    rows = [jnp.sum(jnp.where(eio == idx_ref[k:k + 1, :], ps, 0), axis=0, keepdims=True) for k in range(K)]
    dest_ref[...] = jnp.concatenate(rows, axis=0) + rank_ref[...]


def _dest_slots(idx, rank, pad_start):
    K, N = idx.shape
    E = pad_start.shape[0]
    T = min(N, 512)
    spec = pl.BlockSpec((K, T), lambda i: (0, i))
    return pl.pallas_call(
        _dest_kernel,
        out_shape=jax.ShapeDtypeStruct((K, N), I32),
        grid=(N // T,),
        in_specs=[spec, spec, pl.BlockSpec((E, 1), lambda i: (0, 0))],
        out_specs=spec,
        compiler_params=_cparams("arbitrary"),
        name="dest_slots",
    )(idx, rank, pad_start.reshape(E, 1))


SC_ROWS = 64


def _sc_workers():
    info = plsc.get_sparse_core_info()
    return info.num_cores, info.num_subcores


def _dispatch_rows(h2p, dest3, n_slots):
    N, W = h2p.shape
    n_chunks, K, R = dest3.shape
    nc, ns = _sc_workers()
    per_w = n_chunks // (nc * ns)
    mesh = plsc.VectorSubcoreMesh(core_axis_name="c", subcore_axis_name="s")

    @functools.partial(
        pl.kernel, mesh=mesh,
        out_type=jax.ShapeDtypeStruct((n_slots, W), I32),
        scratch_types=[pltpu.VMEM((K, R), I32), pltpu.VMEM((K, R), I32),
                       pltpu.VMEM((R, W), I32), pltpu.VMEM((R, W), I32),
                       pltpu.SemaphoreType.DMA, pltpu.SemaphoreType.DMA, pltpu.SemaphoreType.DMA],
    )
    def k(h_hbm, d_hbm, xs_hbm, idx0, idx1, rows0, rows1, lsem0, lsem1, ssem):
        idx, rows, lsem = (idx0, idx1), (rows0, rows1), (lsem0, lsem1)
        base = (lax.axis_index("s") * nc + lax.axis_index("c")) * per_w

        def loads(ch, b):
            return (pltpu.make_async_copy(d_hbm.at[ch], idx[b], lsem[b]),
                    pltpu.make_async_copy(h_hbm.at[pl.ds(ch * R, R)], rows[b], lsem[b]))

        for cp in loads(base, 0):
            cp.start()
        for cp in loads(base, 0):
            cp.wait()

        @pl.loop(0, per_w, step=2)
        def _(j):
            for b in range(2):
                ch = base + j + b
                more = j + b + 1 < per_w

                @pl.when(more)
                def _():
                    for cp in loads(ch + 1, 1 - b):
                        cp.start()

                scatters = [pltpu.make_async_copy(rows[b], xs_hbm.at[idx[b].at[kk]], ssem) for kk in range(K)]
                for cp in scatters:
                    cp.start()
                for cp in scatters:
                    cp.wait()

                @pl.when(more)
                def _():
                    for cp in loads(ch + 1, 1 - b):
                        cp.wait()

    return k(h2p, dest3)


def _combine_rows(ys, dest3, N):
    n_slots, W = ys.shape
    n_chunks, K, R = dest3.shape
    nc, ns = _sc_workers()
    per_w = n_chunks // (nc * ns)
    mesh = plsc.VectorSubcoreMesh(core_axis_name="c", subcore_axis_name="s")

    @functools.partial(
        pl.kernel, mesh=mesh,
        out_type=jax.ShapeDtypeStruct((K, N, W), I32),
        scratch_types=[pltpu.VMEM((K, R), I32), pltpu.VMEM((R, W), I32), pltpu.VMEM((R, W), I32),
                       pltpu.SemaphoreType.DMA, pltpu.SemaphoreType.DMA,
                       pltpu.SemaphoreType.DMA, pltpu.SemaphoreType.DMA],
    )
    def k(ys_hbm, d_hbm, yk_hbm, idx_v, rows0, rows1, gsem0, gsem1, wsem0, wsem1):
        rows, gsem, wsem = (rows0, rows1), (gsem0, gsem1), (wsem0, wsem1)
        base = (lax.axis_index("s") * nc + lax.axis_index("c")) * per_w

        @pl.loop(0, per_w)
        def _(j):
            ch = base + j
            pltpu.sync_copy(d_hbm.at[ch], idx_v)

            def gather(kk):
                return pltpu.make_async_copy(ys_hbm.at[idx_v.at[kk]], rows[kk % 2], gsem[kk % 2])

            def write(kk):
                return pltpu.make_async_copy(rows[kk % 2], yk_hbm.at[kk, pl.ds(ch * R, R)], wsem[kk % 2])

            gather(0).start()
            for kk in range(K):
                gather(kk).wait()
                if kk + 1 < K:
                    if kk >= 1:
                        write(kk - 1).wait()
                    gather(kk + 1).start()
                write(kk).start()
            write(K - 2).wait()
            write(K - 1).wait()

    return k(ys, dest3)


def _expert_kernel(b0_ref, nb_ref, xs_hbm, w1_ref, w3_ref, w2_ref, after_hbm, ys_hbm,
                   w13_s, w2_s, xbuf, ybuf, sem_in, sem_out):
    e = pl.program_id(0)
    n_exp = pl.num_programs(0)
    F = w1_ref.shape[2]
    bm = xbuf.shape[1]
    half = xbuf.shape[2]
    nb = nb_ref[e]
    b0 = b0_ref[e]
    total = b0_ref[n_exp - 1] + nb_ref[n_exp - 1]

    nbuf = xbuf.shape[0]

    def in_copy(g):
        slot = g % nbuf
        return pltpu.make_async_copy(xs_hbm.at[pl.ds(g * bm, bm)], xbuf.at[slot], sem_in.at[slot])

    def out_copy(g):
        slot = g % nbuf
        return pltpu.make_async_copy(ybuf.at[slot], ys_hbm.at[pl.ds(g * bm, bm)], sem_out.at[slot])

    @pl.when(e == 0)
    def _():
        for g0 in range(nbuf - 1):
            @pl.when(g0 < total)
            def _():
                in_copy(g0).start()

    @pl.when(nb > 0)
    def _():
        w13_s[:, :F] = w1_ref[0].astype(BF16)
        w13_s[:, F:] = w3_ref[0].astype(BF16)
        w2_s[...] = w2_ref[0].astype(BF16)

    def block(j, carry):
        g = b0 + j
        slot = g % nbuf
        in_copy(g).wait()

        @pl.when(g + nbuf - 1 < total)
        def _():
            in_copy(g + nbuf - 1).start()

        @pl.when(g >= nbuf)
        def _():
            out_copy(g - nbuf).wait()

        lo, hi = _unpack_bf16_pair(xbuf[slot])
        ab = _dot(lo.astype(BF16), w13_s[:half, :]) + _dot(hi.astype(BF16), w13_s[half:, :])
        hid = (_silu(ab[:, :F]) * ab[:, F:]).astype(BF16)
        ybuf[slot] = _pack_bf16_pair(_dot(hid, w2_s[...]))
        out_copy(g).start()
        return carry

    lax.fori_loop(0, nb, block, 0)

    @pl.when(e == n_exp - 1)
    def _():
        for back in range(nbuf, 0, -1):
            @pl.when(total >= back)
            def _():
                out_copy(total - back).wait()


def _experts(xs, blk0, nblk, w_e1, w_e3, w_e2, after):
    n_slots, W = xs.shape
    E, D, F = w_e1.shape
    bm = EXPERT_BLOCK
    grid_spec = pltpu.PrefetchScalarGridSpec(
        num_scalar_prefetch=2,
        grid=(E,),
        in_specs=[
            pl.BlockSpec(memory_space=pl.ANY),
            pl.BlockSpec((1, D, F), lambda e, b0, nb: (e, 0, 0)),
            pl.BlockSpec((1, D, F), lambda e, b0, nb: (e, 0, 0)),
            pl.BlockSpec((1, F, D), lambda e, b0, nb: (e, 0, 0)),
            pl.BlockSpec(memory_space=pl.ANY),
        ],
        out_specs=pl.BlockSpec(memory_space=pl.ANY),
        scratch_shapes=[pltpu.VMEM((D, 2 * F), BF16), pltpu.VMEM((F, D), BF16),
                        pltpu.VMEM((EXPERT_RING, bm, W), I32), pltpu.VMEM((EXPERT_RING, bm, W), I32),
                        pltpu.SemaphoreType.DMA((EXPERT_RING,)), pltpu.SemaphoreType.DMA((EXPERT_RING,))],
    )
    return pl.pallas_call(
        _expert_kernel,
        out_shape=jax.ShapeDtypeStruct((n_slots, W), I32),
        grid_spec=grid_spec,
        compiler_params=_cparams("arbitrary"),
        name="experts",
    )(blk0, nblk, xs, w_e1, w_e3, w_e2, after)


def _shared_kernel(h2_ref, ws13_ref, ws2_ref, o_ref):
    F = ws2_ref.shape[0]
    half = h2_ref.shape[1]
    lo, hi = _unpack_bf16_pair(h2_ref[...])
    ab = _dot(lo.astype(BF16), ws13_ref[:half, :]) + _dot(hi.astype(BF16), ws13_ref[half:, :])
    hid = (_silu(ab[:, :F]) * ab[:, F:]).astype(BF16)
    o_ref[...] = _dot(hid, ws2_ref[...]).astype(o_ref.dtype)


def _shared_expert(h2p, ws13, ws2):
    N, W = h2p.shape
    D = ws2.shape[1]
    tm = min(N, 512)
    full = lambda a: pl.BlockSpec(a.shape, lambda i: (0,) * a.ndim)
    return pl.pallas_call(
        _shared_kernel,
        out_shape=jax.ShapeDtypeStruct((N, D), BF16),
        grid=(N // tm,),
        in_specs=[pl.BlockSpec((tm, W), lambda i: (i, 0)), full(ws13), full(ws2)],
        out_specs=pl.BlockSpec((tm, D), lambda i: (i, 0)),
        compiler_params=_cparams("arbitrary"),
        name="shared_expert",
    )(h2p, ws13, ws2)


def _final_kernel(alpha, yk_ref, wt_ref, sh_ref, x1_ref, mod_ref, g_ref, b_ref, *rest):
    o_ref = rest[-1]
    gt2 = mod_ref[0, 5:6, :]
    wt = wt_ref[...]
    mlo = jnp.zeros(yk_ref.shape[1:], F32)
    mhi = jnp.zeros(yk_ref.shape[1:], F32)
    for k in range(TOP_K):
        lo, hi = _unpack_bf16_pair(yk_ref[k])
        wk = wt[:, k:k + 1]
        mlo = mlo + wk * lo
        mhi = mhi + wk * hi
    ffn = jnp.concatenate([mlo, mhi], axis=1) + sh_ref[...].astype(F32)
    o_ref[...] = _layer_norm(alpha * x1_ref[...] + (1.0 + gt2) * ffn, g_ref[...], b_ref[...])


def _final_part(alpha, yk, wt, sh, x1, mod6, ln_g, ln_b, S, part, n_parts, prev):
    N, D = x1.shape
    K, n_part, W = yk.shape
    tm = min(S, 256)
    per_b = S // tm
    steps = n_part // tm
    off = part * steps
    tok = lambda n: pl.BlockSpec((tm, n), lambda i: (off + i, 0))
    full = lambda a: pl.BlockSpec(a.shape, lambda i: (0,) * a.ndim)
    in_specs = [pl.BlockSpec((K, tm, W), lambda i: (0, i, 0)), tok(LANES), tok(D), tok(D),
                pl.BlockSpec((1, 6, D), lambda i: ((off + i) // per_b, 0, 0)), full(ln_g), full(ln_b)]
    args = [yk, wt, sh, x1, mod6, ln_g, ln_b]
    aliases = {}
    if prev is not None:
        in_specs.append(pl.BlockSpec(memory_space=pl.ANY))
        args.append(prev)
        aliases = {len(args) - 1: 0}
    return pl.pallas_call(
        functools.partial(_final_kernel, alpha),
        out_shape=jax.ShapeDtypeStruct((N, D), F32),
        grid=(steps,),
        in_specs=in_specs,
        out_specs=tok(D),
        input_output_aliases=aliases,
        compiler_params=_cparams("arbitrary"),
        name="combine_ln2",
    )(*args)


def _moe(alpha, h2p, idx, wt, rank, cnt, x1, mod6, w_e1, w_e3, w_e2, w_s1, w_s3, w_s2, ln_g, ln_b, S):
    N = x1.shape[0]
    E = w_e1.shape[0]
    bm = EXPERT_BLOCK
    counts = cnt[:, 0]
    padded = (counts + bm - 1) // bm * bm
    pad_end = jnp.cumsum(padded)
    pad_start = pad_end - padded
    n_slots = N * TOP_K + E * bm
    dest = _dest_slots(idx, rank, pad_start.astype(I32))
    dest3 = dest.reshape(TOP_K, N // SC_ROWS, SC_ROWS).transpose(1, 0, 2)
    xs = _dispatch_rows(h2p, dest3, n_slots)
    ws13 = jnp.concatenate([w_s1, w_s3], axis=1).astype(BF16)
    sh = _shared_expert(h2p, ws13, w_s2.astype(BF16))
    ys = _experts(xs, (pad_start // bm).astype(I32), (padded // bm).astype(I32), w_e1, w_e3, w_e2, sh)
    n_chunks = dest3.shape[0]
    cpp = n_chunks // COMBINE_PARTS
    out = None
    for p in range(COMBINE_PARTS):
        yk = _combine_rows(ys, dest3[p * cpp:(p + 1) * cpp], cpp * SC_ROWS)
        out = _final_part(alpha, yk, wt, sh, x1, mod6, ln_g, ln_b, S, p, COMBINE_PARTS, out)
    return out


def kernel(x, c, positions, w_ada, b_ada, w_in, w_gla_a2, b_gla_a, g_gla_norm, w_gla_o, g_cq, w_uq, g_ckv, w_ukv, w_mla_o, w_out, ln1_g, ln1_b, w_router, b_router, w_e1, w_e3, w_e2, w_s1, w_s3, w_s2, ln2_g, ln2_b):
    B, S, D = x.shape
    N = B * S
    depth = w_ada.shape[0]
    alpha = (2.0 * depth) ** 0.25
    row = lambda a: a.reshape(1, -1)
    cc, ss = _rope_tables(positions)
    x2 = x.reshape(N, D)
    for l in range(depth):
        mod6 = _modulation(c, w_ada[l], b_ada[l]).reshape(B, 6, D)
        qk, gv, gr, cq, ckv, gate_a, gate_b, tail = _in_projection(x2, mod6, _prep_w_in(w_in[l], D), S)
        wa2_p = jnp.concatenate(
            [w_gla_a2[l], jnp.zeros((LANES - GLA_GATE_RANK, w_gla_a2.shape[2]), F32)], axis=0).astype(BF16)
        y_a = _gla(qk, gv, gr, tail, wa2_p, row(b_gla_a[l]), row(g_gla_norm[l]), w_gla_o[l].astype(BF16), B, S)
        qc, kc, vv = _mla_prep(cq, ckv, tail, cc, ss, row(g_cq[l]), _prep_w_uq(w_uq[l]),
                               row(g_ckv[l]), _prep_w_ukv(w_ukv[l]))
        o_b = _mla_attention(qc, kc, vv, B, S)
        x1, h2p, idx, wt, rank, cnt = _merge_route(
            alpha, x2, y_a, o_b, gate_a, gate_b, mod6, w_mla_o[l].astype(BF16), w_out[l].astype(BF16),
            row(ln1_g[l]), row(ln1_b[l]), w_router[l].T.astype(BF16), b_router[l], S)
        x2 = _moe(alpha, h2p, idx, wt, rank, cnt, x1, mod6, w_e1[l], w_e3[l], w_e2[l],
                  w_s1[l], w_s3[l], w_s2[l], row(ln2_g[l]), row(ln2_b[l]), S)
    return x2.reshape(B, S, D)
```

```python
import functools

import jax
import jax.numpy as jnp
from jax import lax
from jax.experimental import pallas as pl
from jax.experimental.pallas import tpu as pltpu
from jax.experimental.pallas import tpu_sc as plsc

CHUNK = 64
GLA_HEADS = 4
GLA_DK = 128
GLA_DV = 256
GLA_GATE_RANK = 16
GLA_GATE_TAU = 16.0
MLA_HEADS = 8
MLA_Q_RANK = 768
MLA_KV_RANK = 256
MLA_NOPE = 128
MLA_ROPE = 64
MLA_V = 128
ROPE_THETA = 10000.0
N_EXPERTS = 256
TOP_K = 8
N_GROUPS = 8
TOPK_GROUPS = 4
D_EXPERT = 256
ROUTED_SCALE = 2.5
LN_EPS = 1e-5
RMS_EPS = 1e-6
LOG2E = 1.4426950408889634

LANES = 128
VMEM_LIMIT = 56 * 1024 * 1024
EXPERT_BLOCK = 512
EXPERT_RING = 3
MOE_PARTS = 2
COMBINE_PARTS = 4

F32 = jnp.float32
BF16 = jnp.bfloat16
I32 = jnp.int32


def _cparams(*sem):
    return pltpu.CompilerParams(dimension_semantics=sem, vmem_limit_bytes=VMEM_LIMIT)


def _sigmoid(x):
    return 1.0 / (1.0 + jnp.exp(-x))


def _silu(x):
    return x * _sigmoid(x)


def _dot(a, b):
    return jnp.dot(a, b, preferred_element_type=F32)


def _dot_nt(a, b):
    return lax.dot_general(a, b, (((1,), (1,)), ((), ())), preferred_element_type=F32)


def _dot_tn(a, b):
    return lax.dot_general(a, b, (((0,), (0,)), ((), ())), preferred_element_type=F32)


def _pack_bf16_pair(x):
    w = x.shape[1] // 2
    u = lax.bitcast_convert_type(x.astype(BF16).astype(F32), I32)
    lo = lax.shift_right_logical(u[:, :w], jnp.int32(16))
    hi = jnp.bitwise_and(u[:, w:], jnp.int32(-65536))
    return jnp.bitwise_or(lo, hi)


def _unpack_bf16_pair(p):
    lo = lax.bitcast_convert_type(lax.shift_left(p, jnp.int32(16)), F32)
    hi = lax.bitcast_convert_type(jnp.bitwise_and(p, jnp.int32(-65536)), F32)
    return lo, hi


def _mod_kernel(c_ref, w_ref, b_ref, o_ref):
    cond = _silu(c_ref[...]).astype(BF16)
    o_ref[...] = _dot(cond, w_ref[...].astype(BF16)) + b_ref[...]


def _modulation(c, w_ada, b_ada):
    B, D = c.shape
    W = w_ada.shape[1]
    tn = D
    return pl.pallas_call(
        _mod_kernel,
        out_shape=jax.ShapeDtypeStruct((B, W), F32),
        grid=(W // tn,),
        in_specs=[
            pl.BlockSpec((B, D), lambda j: (0, 0)),
            pl.BlockSpec((D, tn), lambda j: (0, j)),
            pl.BlockSpec((1, tn), lambda j: (0, j)),
        ],
        out_specs=pl.BlockSpec((B, tn), lambda j: (0, j)),
        compiler_params=_cparams("arbitrary"),
        name="adaln_mod",
    )(c, w_ada, b_ada.reshape(1, W))


def _rope_kernel(pos_ref, f_ref, ph_ref, cc_ref, ss_ref):
    ang = pos_ref[...] * f_ref[...]
    cc_ref[...] = jnp.cos(ang)
    ss_ref[...] = jnp.sin(ang) * ph_ref[...]


def _rope_tables(positions):
    B, S = positions.shape
    N = B * S
    half = MLA_ROPE // 2
    inv_freq = ROPE_THETA ** (-jnp.arange(half, dtype=F32) * (2.0 / MLA_ROPE))
    f4 = jnp.tile(inv_freq, LANES // half).reshape(1, LANES)
    sign = jnp.tile(jnp.concatenate([-jnp.ones((half,), F32), jnp.ones((half,), F32)]), LANES // MLA_ROPE)
    pos = jnp.broadcast_to(positions.reshape(N, 1).astype(F32), (N, LANES))
    tm = min(N, 1024)
    spec = pl.BlockSpec((tm, LANES), lambda i: (i, 0))
    vec = pl.BlockSpec((1, LANES), lambda i: (0, 0))
    return pl.pallas_call(
        _rope_kernel,
        out_shape=(jax.ShapeDtypeStruct((N, LANES), F32), jax.ShapeDtypeStruct((N, LANES), F32)),
        grid=(N // tm,),
        in_specs=[spec, vec, vec],
        out_specs=(spec, spec),
        compiler_params=_cparams("arbitrary"),
        name="rope_tables",
    )(pos, f4, sign.reshape(1, LANES))


QK_W = 2 * GLA_HEADS * GLA_DK
GV_W = GLA_HEADS * GLA_DV
TAIL_W = 3 * LANES
IN_SEG = (QK_W, GV_W, GV_W, MLA_Q_RANK, MLA_KV_RANK, 1024, 1024, TAIL_W)


def _prep_w_in(w_in, D):
    s = [GLA_HEADS * GLA_DK, GLA_HEADS * GLA_DK, GV_W, GV_W, GLA_GATE_RANK, MLA_Q_RANK, MLA_KV_RANK, MLA_ROPE, D, D]
    offs = [0]
    for n in s:
        offs.append(offs[-1] + n)
    gq, gk, gv, gr, ga, cq, ckv, kr, gate_a, gate_b = [w_in[:, offs[i]:offs[i + 1]] for i in range(10)]
    half = MLA_ROPE // 2
    kr_sw = jnp.concatenate([kr[:, half:], kr[:, :half]], axis=1)
    pad = jnp.zeros((w_in.shape[0], LANES - GLA_GATE_RANK), w_in.dtype)
    return jnp.concatenate([gq, gk, gv, gr, cq, ckv, gate_a, gate_b, kr, kr, kr_sw, kr_sw, ga, pad], axis=1).astype(BF16)


def _inproj_kernel(x_ref, mod_ref, w_ref, *out_refs):
    sh1 = mod_ref[0, 0:1, :]
    sc1 = mod_ref[0, 1:2, :]
    h = (x_ref[...] * (1.0 + sc1) + sh1).astype(BF16)
    off = 0
    for ref in out_refs:
        n = ref.shape[-1]
        ref[...] = _dot(h, w_ref[:, off:off + n]).astype(ref.dtype)
        off += n


def _in_projection(x2, mod6, w_in_p, S):
    N, D = x2.shape
    W = w_in_p.shape[1]
    tm = min(S, 512)
    per_b = S // tm
    return pl.pallas_call(
        _inproj_kernel,
        out_shape=tuple(jax.ShapeDtypeStruct((N, n), BF16) for n in IN_SEG),
        grid=(N // tm,),
        in_specs=[
            pl.BlockSpec((tm, D), lambda i: (i, 0)),
            pl.BlockSpec((1, 6, D), lambda i: (i // per_b, 0, 0)),
            pl.BlockSpec((D, W), lambda i: (0, 0), pipeline_mode=pl.Buffered(1)),
        ],
        out_specs=tuple(pl.BlockSpec((tm, n), lambda i: (i, 0)) for n in IN_SEG),
        compiler_params=_cparams("arbitrary"),
        name="in_proj",
    )(x2, mod6, w_in_p)


def _gla_kernel(qk_ref, v_ref, gr_ref, tail_ref, wa2_ref, ba_ref, gn_ref, wo_ref, y_ref,
                st_ref, kd_ref, dec_ref, sall_ref, o_ref):
    t = pl.program_id(1)

    @pl.when(t == 0)
    def _():
        st_ref[...] = jnp.zeros_like(st_ref)

    ts = qk_ref.shape[0]
    nch = ts // CHUNK
    HK = GLA_HEADS * GLA_DK
    r = lax.broadcasted_iota(I32, (CHUNK, CHUNK), 0)
    c = lax.broadcasted_iota(I32, (CHUNK, CHUNK), 1)
    tri = (r >= c).astype(BF16)
    qscale = GLA_DK ** -0.5

    z = _dot(tail_ref[:, 2 * LANES:3 * LANES], wa2_ref[...]) + ba_ref[...]
    log_a = (jnp.minimum(z, 0.0) - jnp.log(1.0 + jnp.exp(-jnp.abs(z)))) * (1.0 / GLA_GATE_TAU)
    la_hi = log_a.astype(BF16)
    la_lo = (log_a - la_hi.astype(F32)).astype(BF16)
    for n in range(nch):
        rows = slice(n * CHUNK, (n + 1) * CHUNK)
        G = _dot(tri, la_hi[rows]) + _dot(tri, la_lo[rows])
        g_end = G[CHUNK - 1:CHUNK, :]
        kd_ref[rows, :] = (qk_ref[rows, HK:2 * HK].astype(F32) * jnp.exp(g_end - G)).astype(BF16)
        dec_ref[n:n + 1, :] = jnp.exp(g_end)

    for h in range(GLA_HEADS):
        ks = slice(h * GLA_DK, (h + 1) * GLA_DK)
        vs = slice(h * GLA_DV, (h + 1) * GLA_DV)
        st = st_ref[h]
        for n in range(nch):
            rows = slice(n * CHUNK, (n + 1) * CHUNK)
            st = st * dec_ref[n:n + 1, ks] + _dot_tn(v_ref[rows, vs], kd_ref[rows, ks])
            sall_ref[n * GLA_HEADS + h] = st.astype(BF16)
        st_ref[h] = st

    for n in range(nch):
        rows = slice(n * CHUNK, (n + 1) * CHUNK)
        for h in range(GLA_HEADS):
            ks = slice(h * GLA_DK, (h + 1) * GLA_DK)
            vs = slice(h * GLA_DV, (h + 1) * GLA_DV)
            qh = (qk_ref[rows, ks].astype(F32) * qscale).astype(BF16)
            o = _dot_nt(qh, sall_ref[n * GLA_HEADS + h])
            o = o * lax.rsqrt(jnp.mean(o * o, axis=-1, keepdims=True) + RMS_EPS) * gn_ref[...]
            o_ref[rows, vs] = (o * _silu(gr_ref[rows, vs].astype(F32))).astype(BF16)
    y_ref[...] = _dot(o_ref[...], wo_ref[...]).astype(y_ref.dtype)


def _gla(qk, gv, gr, tail, wa2_p, b_a, g_norm, w_o, B, S):
    N = B * S
    D = w_o.shape[1]
    ts = min(S, 512)
    per_b = S // ts
    HK = GLA_HEADS * GLA_DK
    tok = lambda n: pl.BlockSpec((ts, n), lambda b, t: (b * per_b + t, 0))
    full = lambda a: pl.BlockSpec(a.shape, lambda b, t: (0,) * a.ndim)
    return pl.pallas_call(
        _gla_kernel,
        out_shape=jax.ShapeDtypeStruct((N, D), BF16),
        grid=(B, per_b),
        in_specs=[tok(QK_W), tok(GV_W), tok(GV_W), tok(TAIL_W), full(wa2_p), full(b_a), full(g_norm), full(w_o)],
        out_specs=tok(D),
        scratch_shapes=[pltpu.VMEM((GLA_HEADS, GLA_DV, GLA_DK), F32),
                        pltpu.VMEM((ts, GLA_HEADS * GLA_DK), BF16),
                        pltpu.VMEM((ts // CHUNK, GLA_HEADS * GLA_DK), F32),
                        pltpu.VMEM((ts // CHUNK * GLA_HEADS, GLA_DV, GLA_DK), BF16),
                        pltpu.VMEM((ts, GV_W), BF16)],
        compiler_params=_cparams("arbitrary", "arbitrary"),
        name="gla",
    )(qk, gv, gr, tail, wa2_p, b_a, g_norm, w_o)


HQ = MLA_HEADS * 2 * LANES


def _prep_w_uq(w_uq):
    dh = MLA_NOPE + MLA_ROPE
    half = MLA_ROPE // 2
    nope = [w_uq[:, h * dh:h * dh + MLA_NOPE] for h in range(MLA_HEADS)]
    rope = [w_uq[:, h * dh + MLA_NOPE:(h + 1) * dh] for h in range(MLA_HEADS)]
    rope_sw = [jnp.concatenate([r[:, half:], r[:, :half]], axis=1) for r in rope]
    return jnp.concatenate(nope + rope + rope_sw, axis=1).astype(BF16)


def _prep_w_ukv(w_ukv):
    dh = MLA_NOPE + MLA_V
    kn = [w_ukv[:, h * dh:h * dh + MLA_NOPE] for h in range(MLA_HEADS)]
    vv = [w_ukv[:, h * dh + MLA_NOPE:(h + 1) * dh] for h in range(MLA_HEADS)]
    return jnp.concatenate(kn + vv, axis=1).astype(BF16)


def _rms(x, g):
    return x * lax.rsqrt(jnp.mean(x * x, axis=-1, keepdims=True) + RMS_EPS) * g


def _mla_prep_kernel(cq_ref, ckv_ref, tail_ref, cc_ref, ss_ref, gq_ref, wq_ref, gkv_ref, wkv_ref, q_ref, k_ref, v_ref):
    tm = cq_ref.shape[0]
    NP = MLA_HEADS * MLA_NOPE
    RP = MLA_HEADS * MLA_ROPE
    scale = (MLA_NOPE + MLA_ROPE) ** -0.5 * LOG2E
    cc = cc_ref[...]
    ss = ss_ref[...]
    cqn = _rms(cq_ref[...].astype(F32), gq_ref[...]).astype(BF16)
    qf = _dot(cqn, wq_ref[...]) * scale
    ckvn = _rms(ckv_ref[...].astype(F32), gkv_ref[...]).astype(BF16)
    kv = _dot(ckvn, wkv_ref[...])
    krr = (tail_ref[:, 0:LANES].astype(F32) * cc + tail_ref[:, LANES:2 * LANES].astype(F32) * ss).astype(BF16)
    lane = lax.broadcasted_iota(I32, (tm, LANES), 1)
    first = lane < MLA_ROPE
    for j in range(MLA_HEADS // 2):
        a = NP + j * LANES
        rot = qf[:, a:a + LANES] * cc + qf[:, a + RP:a + RP + LANES] * ss
        for h, keep in ((2 * j, first), (2 * j + 1, jnp.logical_not(first))):
            base = h * 2 * LANES
            q_ref[:, base:base + LANES] = qf[:, h * MLA_NOPE:(h + 1) * MLA_NOPE].astype(BF16)
            q_ref[:, base + LANES:base + 2 * LANES] = jnp.where(keep, rot, 0.0).astype(BF16)
            k_ref[:, base:base + LANES] = kv[:, h * MLA_NOPE:(h + 1) * MLA_NOPE].astype(BF16)
            k_ref[:, base + LANES:base + 2 * LANES] = krr
    v_ref[...] = kv[:, NP:].astype(BF16)


def _mla_prep(cq, ckv, tail, cc, ss, g_cq, w_uq_p, g_ckv, w_ukv_p):
    N = cq.shape[0]
    tm = min(N, 512)
    tok = lambda n: pl.BlockSpec((tm, n), lambda i: (i, 0))
    full = lambda a: pl.BlockSpec(a.shape, lambda i: (0,) * a.ndim)
    HV = MLA_HEADS * MLA_V
    return pl.pallas_call(
        _mla_prep_kernel,
        out_shape=(jax.ShapeDtypeStruct((N, HQ), BF16), jax.ShapeDtypeStruct((N, HQ), BF16),
                   jax.ShapeDtypeStruct((N, HV), BF16)),
        grid=(N // tm,),
        in_specs=[tok(MLA_Q_RANK), tok(MLA_KV_RANK), tok(TAIL_W), tok(LANES), tok(LANES),
                  full(g_cq), full(w_uq_p), full(g_ckv), full(w_ukv_p)],
        out_specs=(tok(HQ), tok(HQ), tok(HV)),
        compiler_params=_cparams("arbitrary"),
        name="mla_prep",
    )(cq, ckv, tail, cc, ss, g_cq, w_uq_p, g_ckv, w_ukv_p)


ATTN_TQ = 256


def _mla_attn_kernel(q_ref, k_ref, v_ref, o_ref, v1_ref):
    S = q_ref.shape[0]
    tq = min(S, ATTN_TQ)
    r = lax.broadcasted_iota(I32, (tq, tq), 0) // CHUNK
    c = lax.broadcasted_iota(I32, (tq, tq), 1) // CHUNK
    diag_mask = c <= r
    v1_ref[:, :MLA_V] = v_ref[...]
    v1_ref[:, MLA_V:] = jnp.ones((S, MLA_V), BF16)

    def scores(ii):
        l0 = ii * tq
        q = q_ref[l0:l0 + tq, :]
        sd = jnp.where(diag_mask, _dot_nt(q, k_ref[l0:l0 + tq, :]), -jnp.inf)
        so = _dot_nt(q, k_ref[0:l0, :]) if ii > 0 else None
        return sd, so

    def finish(ii, sd, so):
        l0 = ii * tq
        m = jnp.max(sd, axis=-1, keepdims=True)
        if so is not None:
            m = jnp.maximum(m, jnp.max(so, axis=-1, keepdims=True))
        acc = _dot(jnp.exp2((sd - m).astype(BF16)), v1_ref[l0:l0 + tq, :])
        if so is not None:
            acc = acc + _dot(jnp.exp2((so - m).astype(BF16)), v1_ref[0:l0, :])
        o_ref[l0:l0 + tq, :] = (acc[:, :MLA_V] / acc[:, MLA_V:]).astype(BF16)

    n_tiles = S // tq
    ahead = 2
    pending = [scores(ii) for ii in range(min(ahead, n_tiles))]
    for ii in range(n_tiles):
        if ii + ahead < n_tiles:
            pending.append(scores(ii + ahead))
        finish(ii, *pending.pop(0))


def _mla_attention(qc, kc, vv, B, S):
    N = B * S
    HV = MLA_HEADS * MLA_V
    return pl.pallas_call(
        _mla_attn_kernel,
        out_shape=jax.ShapeDtypeStruct((N, HV), BF16),
        grid=(B, MLA_HEADS),
        in_specs=[
            pl.BlockSpec((S, 2 * LANES), lambda b, h: (b, h)),
            pl.BlockSpec((S, 2 * LANES), lambda b, h: (b, h)),
            pl.BlockSpec((S, MLA_V), lambda b, h: (b, h)),
        ],
        out_specs=pl.BlockSpec((S, MLA_V), lambda b, h: (b, h)),
        scratch_shapes=[pltpu.VMEM((S, 2 * MLA_V), BF16)],
        compiler_params=_cparams("arbitrary", "arbitrary"),
        name="mla_attn",
    )(qc, kc, vv)


def _layer_norm(u, g, b):
    mu = jnp.mean(u, axis=-1, keepdims=True)
    d = u - mu
    var = jnp.mean(d * d, axis=-1, keepdims=True)
    return d * lax.rsqrt(var + LN_EPS) * g + b


def _merge_kernel(alpha, x_ref, ya_ref, ob_ref, ga_ref, gb_ref, mod_ref, wmo_ref, wout_ref, g_ref, b_ref, wr_ref,
                  br_ref, x1_ref, h2_ref, idx_ref, wt_ref, rank_ref, cnt_ref, lg_s, carry_s):
    i = pl.program_id(0)

    @pl.when(i == 0)
    def _():
        lg_s[...] = jnp.zeros_like(lg_s)
        carry_s[...] = jnp.zeros_like(carry_s)

    idx, wt, rank, total = _route_math(lg_s[...], br_ref[...], carry_s[...])
    total = jnp.where(i > 0, total, 0.0)
    idx_ref[...] = idx
    wt_ref[...] = wt
    rank_ref[...] = rank
    carry_s[...] = total
    cnt_ref[...] = jnp.broadcast_to(total, cnt_ref.shape).astype(I32)

    gt1 = mod_ref[0, 2:3, :]
    sh2 = mod_ref[0, 3:4, :]
    sc2 = mod_ref[0, 4:5, :]
    y = (_sigmoid(ga_ref[...].astype(F32)) * ya_ref[...].astype(F32)
         + _sigmoid(gb_ref[...].astype(F32)) * _dot(ob_ref[...], wmo_ref[...]))
    mix = _dot(y.astype(BF16), wout_ref[...])
    x1 = _layer_norm(alpha * x_ref[...] + (1.0 + gt1) * mix, g_ref[...], b_ref[...])
    x1_ref[...] = x1
    h2 = x1 * (1.0 + sc2) + sh2
    h2_ref[...] = _pack_bf16_pair(h2)
    lg_s[...] = _dot_nt(wr_ref[...], h2.astype(BF16))


def _merge_route(alpha, x2, y_a, o_b, gate_a, gate_b, mod6, w_mla_o, w_out, ln_g, ln_b, wr_t, b_router, S, part):
    N, D = x2.shape
    E = wr_t.shape[0]
    tm = min(S, 512)
    per_b = S // tm
    n = N // tm // MOE_PARTS
    Np = n * tm
    base = part * n
    cur = lambda i: jnp.minimum(i, n - 1)
    prev = lambda i: jnp.maximum(i - 1, 0)
    tok_in = lambda w: pl.BlockSpec((tm, w), lambda i: (base + cur(i), 0))
    tok = lambda w: pl.BlockSpec((tm, w), lambda i: (cur(i), 0))
    full = lambda a: pl.BlockSpec(a.shape, lambda i: (0,) * a.ndim)
    lane_blk = pl.BlockSpec((TOP_K, tm), lambda i: (0, prev(i)))
    br = b_router.reshape(E, 1).astype(F32)
    return pl.pallas_call(
        functools.partial(_merge_kernel, alpha),
        out_shape=(jax.ShapeDtypeStruct((Np, D), F32), jax.ShapeDtypeStruct((Np, D // 2), I32),
                   jax.ShapeDtypeStruct((TOP_K, Np), I32), jax.ShapeDtypeStruct((Np, LANES), F32),
                   jax.ShapeDtypeStruct((TOP_K, Np), I32), jax.ShapeDtypeStruct((E, LANES), I32)),
        grid=(n + 1,),
        in_specs=[tok_in(D), tok_in(D), tok_in(D), tok_in(D), tok_in(D),
                  pl.BlockSpec((1, 6, D), lambda i: ((base + cur(i)) // per_b, 0, 0)),
                  full(w_mla_o), full(w_out), full(ln_g), full(ln_b), full(wr_t), full(br)],
        out_specs=(tok(D), tok(D // 2), lane_blk, pl.BlockSpec((tm, LANES), lambda i: (prev(i), 0)), lane_blk,
                   pl.BlockSpec((E, LANES), lambda i: (0, 0))),
        scratch_shapes=[pltpu.VMEM((E, tm), F32), pltpu.VMEM((E, 1), F32)],
        compiler_params=_cparams("arbitrary"),
        name="merge_ln1_route",
    )(x2, y_a, o_b, gate_a, gate_b, mod6, w_mla_o, w_out, ln_g, ln_b, wr_t, br)


def _first_argmax(v, io, n):
    m = jnp.max(v, axis=0, keepdims=True)
    idx = jnp.min(jnp.where(v == m, io, n), axis=0, keepdims=True)
    return m, idx


def _route_math(lg, br, carry):
    E, T = lg.shape
    gsz = E // N_GROUPS
    neg = -jnp.inf
    s = _sigmoid(lg)
    biased = s + br
    eio = lax.broadcasted_iota(I32, (E, T), 0)
    gio = lax.broadcasted_iota(I32, (gsz, T), 0)

    gs = []
    for g in range(N_GROUPS):
        blk = biased[g * gsz:(g + 1) * gsz]
        m1, i1 = _first_argmax(blk, gio, gsz)
        m2 = jnp.max(jnp.where(gio == i1, neg, blk), axis=0, keepdims=True)
        gs.append(m1 + m2)
    cur = jnp.concatenate(gs, axis=0)
    nio = lax.broadcasted_iota(I32, (N_GROUPS, T), 0)
    gsel = jnp.zeros((N_GROUPS, T), F32)
    for _ in range(TOPK_GROUPS):
        _, gi = _first_argmax(cur, nio, N_GROUPS)
        hit = nio == gi
        gsel = jnp.where(hit, 1.0, gsel)
        cur = jnp.where(hit, neg, cur)
    emask = jnp.concatenate([jnp.broadcast_to(gsel[g:g + 1], (gsz, T)) for g in range(N_GROUPS)], axis=0) > 0.0

    cur = jnp.where(emask, biased, neg)
    idxs, ws = [], []
    sel = jnp.zeros((E, T), F32)
    for _ in range(TOP_K):
        _, ei = _first_argmax(cur, eio, E)
        hit = eio == ei
        idxs.append(ei)
        ws.append(jnp.sum(jnp.where(hit, s, 0.0), axis=0, keepdims=True))
        sel = jnp.where(hit, 1.0, sel)
        cur = jnp.where(hit, neg, cur)
    w = jnp.concatenate(ws, axis=0)
    w = w / jnp.sum(w, axis=0, keepdims=True) * ROUTED_SCALE

    wpad = jnp.concatenate([w, jnp.zeros((LANES - TOP_K, T), F32)], axis=0)

    r = lax.broadcasted_iota(I32, (T, T), 0)
    c = lax.broadcasted_iota(I32, (T, T), 1)
    before = (r < c).astype(BF16)
    cnt = _dot(sel.astype(BF16), before) + carry
    ranks = [jnp.sum(jnp.where(eio == idxs[k], cnt, 0.0), axis=0, keepdims=True) for k in range(TOP_K)]
    total = cnt[:, T - 1:T] + sel[:, T - 1:T]
    return jnp.concatenate(idxs, axis=0), wpad.T, jnp.concatenate(ranks, axis=0).astype(I32), total


def _dest_kernel(idx_ref, rank_ref, ps_ref, dest_ref):
    K, T = idx_ref.shape
    E = ps_ref.shape[0]
    eio = lax.broadcasted_iota(I32, (E, T), 0)
    ps = ps_ref[...]
    rows = [jnp.sum(jnp.where(eio == idx_ref[k:k + 1, :], ps, 0), axis=0, keepdims=True) for k in range(K)]
    dest_ref[...] = jnp.concatenate(rows, axis=0) + rank_ref[...]


def _dest_slots(idx, rank, pad_start):
    K, N = idx.shape
    E = pad_start.shape[0]
    T = min(N, 512)
    spec = pl.BlockSpec((K, T), lambda i: (0, i))
    return pl.pallas_call(
        _dest_kernel,
        out_shape=jax.ShapeDtypeStruct((K, N), I32),
        grid=(N // T,),
        in_specs=[spec, spec, pl.BlockSpec((E, 1), lambda i: (0, 0))],
        out_specs=spec,
        compiler_params=_cparams("arbitrary"),
        name="dest_slots",
    )(idx, rank, pad_start.reshape(E, 1))


SC_ROWS = 64


def _sc_workers():
    info = plsc.get_sparse_core_info()
    return info.num_cores, info.num_subcores


def _dispatch_rows(h2p, dest3, n_slots):
    N, W = h2p.shape
    n_chunks, K, R = dest3.shape
    nc, ns = _sc_workers()
    per_w = n_chunks // (nc * ns)
    mesh = plsc.VectorSubcoreMesh(core_axis_name="c", subcore_axis_name="s")

    @functools.partial(
        pl.kernel, mesh=mesh,
        out_type=jax.ShapeDtypeStruct((n_slots, W), I32),
        scratch_types=[pltpu.VMEM((K, R), I32), pltpu.VMEM((K, R), I32),
                       pltpu.VMEM((R, W), I32), pltpu.VMEM((R, W), I32),
                       pltpu.SemaphoreType.DMA, pltpu.SemaphoreType.DMA, pltpu.SemaphoreType.DMA],
    )
    def k(h_hbm, d_hbm, xs_hbm, idx0, idx1, rows0, rows1, lsem0, lsem1, ssem):
        idx, rows, lsem = (idx0, idx1), (rows0, rows1), (lsem0, lsem1)
        base = (lax.axis_index("s") * nc + lax.axis_index("c")) * per_w

        def loads(ch, b):
            return (pltpu.make_async_copy(d_hbm.at[ch], idx[b], lsem[b]),
                    pltpu.make_async_copy(h_hbm.at[pl.ds(ch * R, R)], rows[b], lsem[b]))

        for cp in loads(base, 0):
            cp.start()
        for cp in loads(base, 0):
            cp.wait()

        @pl.loop(0, per_w, step=2)
        def _(j):
            for b in range(2):
                ch = base + j + b
                more = j + b + 1 < per_w

                @pl.when(more)
                def _():
                    for cp in loads(ch + 1, 1 - b):
                        cp.start()

                scatters = [pltpu.make_async_copy(rows[b], xs_hbm.at[idx[b].at[kk]], ssem) for kk in range(K)]
                for cp in scatters:
                    cp.start()
                for cp in scatters:
                    cp.wait()

                @pl.when(more)
                def _():
                    for cp in loads(ch + 1, 1 - b):
                        cp.wait()

    return k(h2p, dest3)


def _combine_rows(ys, dest3, N):
    n_slots, W = ys.shape
    n_chunks, K, R = dest3.shape
    nc, ns = _sc_workers()
    per_w = n_chunks // (nc * ns)
    mesh = plsc.VectorSubcoreMesh(core_axis_name="c", subcore_axis_name="s")

    @functools.partial(
        pl.kernel, mesh=mesh,
        out_type=jax.ShapeDtypeStruct((K, N, W), I32),
        scratch_types=[pltpu.VMEM((K, R), I32), pltpu.VMEM((R, W), I32), pltpu.VMEM((R, W), I32),
                       pltpu.SemaphoreType.DMA, pltpu.SemaphoreType.DMA,
                       pltpu.SemaphoreType.DMA, pltpu.SemaphoreType.DMA],
    )
    def k(ys_hbm, d_hbm, yk_hbm, idx_v, rows0, rows1, gsem0, gsem1, wsem0, wsem1):
        rows, gsem, wsem = (rows0, rows1), (gsem0, gsem1), (wsem0, wsem1)
        base = (lax.axis_index("s") * nc + lax.axis_index("c")) * per_w

        @pl.loop(0, per_w)
        def _(j):
            ch = base + j
            pltpu.sync_copy(d_hbm.at[ch], idx_v)

            def gather(kk):
                return pltpu.make_async_copy(ys_hbm.at[idx_v.at[kk]], rows[kk % 2], gsem[kk % 2])

            def write(kk):
                return pltpu.make_async_copy(rows[kk % 2], yk_hbm.at[kk, pl.ds(ch * R, R)], wsem[kk % 2])

            gather(0).start()
            for kk in range(K):
                gather(kk).wait()
                if kk + 1 < K:
                    if kk >= 1:
                        write(kk - 1).wait()
                    gather(kk + 1).start()
                write(kk).start()
            write(K - 2).wait()
            write(K - 1).wait()

    return k(ys, dest3)


def _expert_kernel(b0_ref, nb_ref, xs_hbm, w1_ref, w3_ref, w2_ref, after_hbm, ys_hbm,
                   w13_s, w2_s, xbuf, ybuf, sem_in, sem_out):
    e = pl.program_id(0)
    n_exp = pl.num_programs(0)
    F = w1_ref.shape[2]
    bm = xbuf.shape[1]
    half = xbuf.shape[2]
    nb = nb_ref[e]
    b0 = b0_ref[e]
    total = b0_ref[n_exp - 1] + nb_ref[n_exp - 1]

    nbuf = xbuf.shape[0]

    def in_copy(g):
        slot = g % nbuf
        return pltpu.make_async_copy(xs_hbm.at[pl.ds(g * bm, bm)], xbuf.at[slot], sem_in.at[slot])

    def out_copy(g):
        slot = g % nbuf
        return pltpu.make_async_copy(ybuf.at[slot], ys_hbm.at[pl.ds(g * bm, bm)], sem_out.at[slot])

    @pl.when(e == 0)
    def _():
        for g0 in range(nbuf - 1):
            @pl.when(g0 < total)
            def _():
                in_copy(g0).start()

    @pl.when(nb > 0)
    def _():
        w13_s[:, :F] = w1_ref[0].astype(BF16)
        w13_s[:, F:] = w3_ref[0].astype(BF16)
        w2_s[...] = w2_ref[0].astype(BF16)

    def block(j, carry):
        g = b0 + j
        slot = g % nbuf
        in_copy(g).wait()

        @pl.when(g + nbuf - 1 < total)
        def _():
            in_copy(g + nbuf - 1).start()

        @pl.when(g >= nbuf)
        def _():
            out_copy(g - nbuf).wait()

        lo, hi = _unpack_bf16_pair(xbuf[slot])
        ab = _dot(lo.astype(BF16), w13_s[:half, :]) + _dot(hi.astype(BF16), w13_s[half:, :])
        hid = (_silu(ab[:, :F]) * ab[:, F:]).astype(BF16)
        ybuf[slot] = _pack_bf16_pair(_dot(hid, w2_s[...]))
        out_copy(g).start()
        return carry

    lax.fori_loop(0, nb, block, 0)

    @pl.when(e == n_exp - 1)
    def _():
        for back in range(nbuf, 0, -1):
            @pl.when(total >= back)
            def _():
                out_copy(total - back).wait()


def _experts(xs, blk0, nblk, w_e1, w_e3, w_e2, after):
    n_slots, W = xs.shape
    E, D, F = w_e1.shape
    bm = EXPERT_BLOCK
    grid_spec = pltpu.PrefetchScalarGridSpec(
        num_scalar_prefetch=2,
        grid=(E,),
        in_specs=[
            pl.BlockSpec(memory_space=pl.ANY),
            pl.BlockSpec((1, D, F), lambda e, b0, nb: (e, 0, 0)),
            pl.BlockSpec((1, D, F), lambda e, b0, nb: (e, 0, 0)),
            pl.BlockSpec((1, F, D), lambda e, b0, nb: (e, 0, 0)),
            pl.BlockSpec(memory_space=pl.ANY),
        ],
        out_specs=pl.BlockSpec(memory_space=pl.ANY),
        scratch_shapes=[pltpu.VMEM((D, 2 * F), BF16), pltpu.VMEM((F, D), BF16),
                        pltpu.VMEM((EXPERT_RING, bm, W), I32), pltpu.VMEM((EXPERT_RING, bm, W), I32),
                        pltpu.SemaphoreType.DMA((EXPERT_RING,)), pltpu.SemaphoreType.DMA((EXPERT_RING,))],
    )
    return pl.pallas_call(
        _expert_kernel,
        out_shape=jax.ShapeDtypeStruct((n_slots, W), I32),
        grid_spec=grid_spec,
        compiler_params=_cparams("arbitrary"),
        name="experts",
    )(blk0, nblk, xs, w_e1, w_e3, w_e2, after)


def _shared_kernel(h2_ref, ws13_ref, ws2_ref, o_ref):
    F = ws2_ref.shape[0]
    half = h2_ref.shape[1]
    lo, hi = _unpack_bf16_pair(h2_ref[...])
    ab = _dot(lo.astype(BF16), ws13_ref[:half, :]) + _dot(hi.astype(BF16), ws13_ref[half:, :])
    hid = (_silu(ab[:, :F]) * ab[:, F:]).astype(BF16)
    o_ref[...] = _dot(hid, ws2_ref[...]).astype(o_ref.dtype)


def _shared_expert(h2p, ws13, ws2):
    N, W = h2p.shape
    D = ws2.shape[1]
    tm = min(N, 512)
    full = lambda a: pl.BlockSpec(a.shape, lambda i: (0,) * a.ndim)
    return pl.pallas_call(
        _shared_kernel,
        out_shape=jax.ShapeDtypeStruct((N, D), BF16),
        grid=(N // tm,),
        in_specs=[pl.BlockSpec((tm, W), lambda i: (i, 0)), full(ws13), full(ws2)],
        out_specs=pl.BlockSpec((tm, D), lambda i: (i, 0)),
        compiler_params=_cparams("arbitrary"),
        name="shared_expert",
    )(h2p, ws13, ws2)


def _final_kernel(alpha, yk_ref, wt_ref, sh_ref, x1_ref, mod_ref, g_ref, b_ref, *rest):
    o_ref = rest[-1]
    gt2 = mod_ref[0, 5:6, :]
    wt = wt_ref[...]
    mlo = jnp.zeros(yk_ref.shape[1:], F32)
    mhi = jnp.zeros(yk_ref.shape[1:], F32)
    for k in range(TOP_K):
        lo, hi = _unpack_bf16_pair(yk_ref[k])
        wk = wt[:, k:k + 1]
        mlo = mlo + wk * lo
        mhi = mhi + wk * hi
    ffn = jnp.concatenate([mlo, mhi], axis=1) + sh_ref[...].astype(F32)
    o_ref[...] = _layer_norm(alpha * x1_ref[...] + (1.0 + gt2) * ffn, g_ref[...], b_ref[...])


def _final_part(alpha, yk, wt, sh, x1, mod6, ln_g, ln_b, S, tok0, row0, n_total, prev):
    D = x1.shape[1]
    N = n_total
    K, n_part, W = yk.shape
    tm = min(S, 256)
    per_b = S // tm
    steps = n_part // tm
    off = tok0 // tm
    goff = row0 // tm
    tok = lambda n: pl.BlockSpec((tm, n), lambda i: (off + i, 0))
    full = lambda a: pl.BlockSpec(a.shape, lambda i: (0,) * a.ndim)
    in_specs = [pl.BlockSpec((K, tm, W), lambda i: (0, i, 0)), tok(LANES), tok(D), tok(D),
                pl.BlockSpec((1, 6, D), lambda i: ((goff + i) // per_b, 0, 0)), full(ln_g), full(ln_b)]
    args = [yk, wt, sh, x1, mod6, ln_g, ln_b]
    aliases = {}
    if prev is not None:
        in_specs.append(pl.BlockSpec(memory_space=pl.ANY))
        args.append(prev)
        aliases = {len(args) - 1: 0}
    return pl.pallas_call(
        functools.partial(_final_kernel, alpha),
        out_shape=jax.ShapeDtypeStruct((N, D), F32),
        grid=(steps,),
        in_specs=in_specs,
        out_specs=pl.BlockSpec((tm, D), lambda i: (goff + i, 0)),
        input_output_aliases=aliases,
        compiler_params=_cparams("arbitrary"),
        name="combine_ln2",
    )(*args)


def _moe(alpha, h2p, idx, wt, rank, cnt, x1, mod6, w_e1, w_e3, w_e2, ws13, ws2, ln_g, ln_b, S, row0, n_total, out):
    N = x1.shape[0]
    E = w_e1.shape[0]
    bm = EXPERT_BLOCK
    counts = cnt[:, 0]
    padded = (counts + bm - 1) // bm * bm
    pad_end = jnp.cumsum(padded)
    pad_start = pad_end - padded
    n_slots = N * TOP_K + E * bm
    dest = _dest_slots(idx, rank, pad_start.astype(I32))
    dest3 = dest.reshape(TOP_K, N // SC_ROWS, SC_ROWS).transpose(1, 0, 2)
    xs = _dispatch_rows(h2p, dest3, n_slots)
    sh = _shared_expert(h2p, ws13, ws2)
    ys = _experts(xs, (pad_start // bm).astype(I32), (padded // bm).astype(I32), w_e1, w_e3, w_e2, sh)
    n_chunks = dest3.shape[0]
    cpp = n_chunks // COMBINE_PARTS
    for p in range(COMBINE_PARTS):
        yk = _combine_rows(ys, dest3[p * cpp:(p + 1) * cpp], cpp * SC_ROWS)
        tok0 = p * cpp * SC_ROWS
        out = _final_part(alpha, yk, wt, sh, x1, mod6, ln_g, ln_b, S, tok0, row0 + tok0, n_total, out)
    return out


def kernel(x, c, positions, w_ada, b_ada, w_in, w_gla_a2, b_gla_a, g_gla_norm, w_gla_o, g_cq, w_uq, g_ckv, w_ukv, w_mla_o, w_out, ln1_g, ln1_b, w_router, b_router, w_e1, w_e3, w_e2, w_s1, w_s3, w_s2, ln2_g, ln2_b):
    B, S, D = x.shape
    N = B * S
    depth = w_ada.shape[0]
    alpha = (2.0 * depth) ** 0.25
    row = lambda a: a.reshape(1, -1)
    cc, ss = _rope_tables(positions)
    x2 = x.reshape(N, D)
    for l in range(depth):
        mod6 = _modulation(c, w_ada[l], b_ada[l]).reshape(B, 6, D)
        qk, gv, gr, cq, ckv, gate_a, gate_b, tail = _in_projection(x2, mod6, _prep_w_in(w_in[l], D), S)
        wa2_p = jnp.concatenate(
            [w_gla_a2[l], jnp.zeros((LANES - GLA_GATE_RANK, w_gla_a2.shape[2]), F32)], axis=0).astype(BF16)
        y_a = _gla(qk, gv, gr, tail, wa2_p, row(b_gla_a[l]), row(g_gla_norm[l]), w_gla_o[l].astype(BF16), B, S)
        qc, kc, vv = _mla_prep(cq, ckv, tail, cc, ss, row(g_cq[l]), _prep_w_uq(w_uq[l]),
                               row(g_ckv[l]), _prep_w_ukv(w_ukv[l]))
        o_b = _mla_attention(qc, kc, vv, B, S)
        ws13 = jnp.concatenate([w_s1[l], w_s3[l]], axis=1).astype(BF16)
        ws2 = w_s2[l].astype(BF16)
        routed = [_merge_route(alpha, x2, y_a, o_b, gate_a, gate_b, mod6, w_mla_o[l].astype(BF16),
                               w_out[l].astype(BF16), row(ln1_g[l]), row(ln1_b[l]), w_router[l].T.astype(BF16),
                               b_router[l], S, p) for p in range(MOE_PARTS)]
        out = None
        for p, (x1, h2p, idx, wt, rank, cnt) in enumerate(routed):
            out = _moe(alpha, h2p, idx, wt, rank, cnt, x1, mod6, w_e1[l], w_e3[l], w_e2[l], ws13, ws2,
                       row(ln2_g[l]), row(ln2_b[l]), S, p * (N // MOE_PARTS), N, out)
        x2 = out
    return x2.reshape(B, S, D)
```

```python
import functools

import jax
import jax.numpy as jnp
from jax import lax
from jax.experimental import pallas as pl
from jax.experimental.pallas import tpu as pltpu
from jax.experimental.pallas import tpu_sc as plsc

CHUNK = 64
GLA_HEADS = 4
GLA_DK = 128
GLA_DV = 256
GLA_GATE_RANK = 16
GLA_GATE_TAU = 16.0
MLA_HEADS = 8
MLA_Q_RANK = 768
MLA_KV_RANK = 256
MLA_NOPE = 128
MLA_ROPE = 64
MLA_V = 128
ROPE_THETA = 10000.0
N_EXPERTS = 256
TOP_K = 8
N_GROUPS = 8
TOPK_GROUPS = 4
D_EXPERT = 256
ROUTED_SCALE = 2.5
LN_EPS = 1e-5
RMS_EPS = 1e-6
LOG2E = 1.4426950408889634

LANES = 128
VMEM_LIMIT = 56 * 1024 * 1024
EXPERT_BLOCK = 512
EXPERT_RING = 3
MOE_PARTS = 2
COMBINE_PARTS = 4

F32 = jnp.float32
BF16 = jnp.bfloat16
I32 = jnp.int32


def _cparams(*sem):
    return pltpu.CompilerParams(dimension_semantics=sem, vmem_limit_bytes=VMEM_LIMIT)


def _sigmoid(x):
    return 1.0 / (1.0 + jnp.exp(-x))


def _silu(x):
    return x * _sigmoid(x)


def _dot(a, b):
    return jnp.dot(a, b, preferred_element_type=F32)


def _dot_nt(a, b):
    return lax.dot_general(a, b, (((1,), (1,)), ((), ())), preferred_element_type=F32)


def _dot_tn(a, b):
    return lax.dot_general(a, b, (((0,), (0,)), ((), ())), preferred_element_type=F32)


def _pack_bf16_pair(x):
    w = x.shape[1] // 2
    u = lax.bitcast_convert_type(x.astype(BF16).astype(F32), I32)
    lo = lax.shift_right_logical(u[:, :w], jnp.int32(16))
    hi = jnp.bitwise_and(u[:, w:], jnp.int32(-65536))
    return jnp.bitwise_or(lo, hi)


def _unpack_bf16_pair(p):
    lo = lax.bitcast_convert_type(lax.shift_left(p, jnp.int32(16)), F32)
    hi = lax.bitcast_convert_type(jnp.bitwise_and(p, jnp.int32(-65536)), F32)
    return lo, hi


def _mod_kernel(c_ref, w_ref, b_ref, o_ref):
    cond = _silu(c_ref[...]).astype(BF16)
    o_ref[...] = _dot(cond, w_ref[...].astype(BF16)) + b_ref[...]


def _modulation(c, w_ada, b_ada):
    B, D = c.shape
    W = w_ada.shape[1]
    tn = D
    return pl.pallas_call(
        _mod_kernel,
        out_shape=jax.ShapeDtypeStruct((B, W), F32),
        grid=(W // tn,),
        in_specs=[
            pl.BlockSpec((B, D), lambda j: (0, 0)),
            pl.BlockSpec((D, tn), lambda j: (0, j)),
            pl.BlockSpec((1, tn), lambda j: (0, j)),
        ],
        out_specs=pl.BlockSpec((B, tn), lambda j: (0, j)),
        compiler_params=_cparams("arbitrary"),
        name="adaln_mod",
    )(c, w_ada, b_ada.reshape(1, W))


def _rope_kernel(pos_ref, f_ref, ph_ref, cc_ref, ss_ref):
    ang = pos_ref[...] * f_ref[...]
    cc_ref[...] = jnp.cos(ang)
    ss_ref[...] = jnp.sin(ang) * ph_ref[...]


def _rope_tables(positions):
    B, S = positions.shape
    N = B * S
    half = MLA_ROPE // 2
    inv_freq = ROPE_THETA ** (-jnp.arange(half, dtype=F32) * (2.0 / MLA_ROPE))
    f4 = jnp.tile(inv_freq, LANES // half).reshape(1, LANES)
    sign = jnp.tile(jnp.concatenate([-jnp.ones((half,), F32), jnp.ones((half,), F32)]), LANES // MLA_ROPE)
    pos = jnp.broadcast_to(positions.reshape(N, 1).astype(F32), (N, LANES))
    tm = min(N, 1024)
    spec = pl.BlockSpec((tm, LANES), lambda i: (i, 0))
    vec = pl.BlockSpec((1, LANES), lambda i: (0, 0))
    return pl.pallas_call(
        _rope_kernel,
        out_shape=(jax.ShapeDtypeStruct((N, LANES), F32), jax.ShapeDtypeStruct((N, LANES), F32)),
        grid=(N // tm,),
        in_specs=[spec, vec, vec],
        out_specs=(spec, spec),
        compiler_params=_cparams("arbitrary"),
        name="rope_tables",
    )(pos, f4, sign.reshape(1, LANES))


QK_W = 2 * GLA_HEADS * GLA_DK
GV_W = GLA_HEADS * GLA_DV
TAIL_W = 3 * LANES
IN_SEG = (QK_W, GV_W, GV_W, MLA_Q_RANK, MLA_KV_RANK, 1024, 1024, TAIL_W)


def _prep_w_in(w_in, D):
    s = [GLA_HEADS * GLA_DK, GLA_HEADS * GLA_DK, GV_W, GV_W, GLA_GATE_RANK, MLA_Q_RANK, MLA_KV_RANK, MLA_ROPE, D, D]
    offs = [0]
    for n in s:
        offs.append(offs[-1] + n)
    gq, gk, gv, gr, ga, cq, ckv, kr, gate_a, gate_b = [w_in[:, offs[i]:offs[i + 1]] for i in range(10)]
    half = MLA_ROPE // 2
    kr_sw = jnp.concatenate([kr[:, half:], kr[:, :half]], axis=1)
    pad = jnp.zeros((w_in.shape[0], LANES - GLA_GATE_RANK), w_in.dtype)
    return jnp.concatenate([gq, gk, gv, gr, cq, ckv, gate_a, gate_b, kr, kr, kr_sw, kr_sw, ga, pad], axis=1).astype(BF16)


def _inproj_kernel(x_ref, mod_ref, w_ref, *out_refs):
    sh1 = mod_ref[0, 0:1, :]
    sc1 = mod_ref[0, 1:2, :]
    h = (x_ref[...] * (1.0 + sc1) + sh1).astype(BF16)
    off = 0
    for ref in out_refs:
        n = ref.shape[-1]
        ref[...] = _dot(h, w_ref[:, off:off + n]).astype(ref.dtype)
        off += n


def _in_projection(x2, mod6, w_in_p, S):
    N, D = x2.shape
    W = w_in_p.shape[1]
    tm = min(S, 512)
    per_b = S // tm
    return pl.pallas_call(
        _inproj_kernel,
        out_shape=tuple(jax.ShapeDtypeStruct((N, n), BF16) for n in IN_SEG),
        grid=(N // tm,),
        in_specs=[
            pl.BlockSpec((tm, D), lambda i: (i, 0)),
            pl.BlockSpec((1, 6, D), lambda i: (i // per_b, 0, 0)),
            pl.BlockSpec((D, W), lambda i: (0, 0), pipeline_mode=pl.Buffered(1)),
        ],
        out_specs=tuple(pl.BlockSpec((tm, n), lambda i: (i, 0)) for n in IN_SEG),
        compiler_params=_cparams("arbitrary"),
        name="in_proj",
    )(x2, mod6, w_in_p)


def _gla_kernel(qk_ref, v_ref, gr_ref, tail_ref, wa2_ref, ba_ref, gn_ref, wo_ref, y_ref,
                st_ref, kd_ref, dec_ref, sall_ref, o_ref):
    t = pl.program_id(1)

    @pl.when(t == 0)
    def _():
        st_ref[...] = jnp.zeros_like(st_ref)

    ts = qk_ref.shape[0]
    nch = ts // CHUNK
    HK = GLA_HEADS * GLA_DK
    r = lax.broadcasted_iota(I32, (CHUNK, CHUNK), 0)
    c = lax.broadcasted_iota(I32, (CHUNK, CHUNK), 1)
    tri = (r >= c).astype(BF16)
    qscale = GLA_DK ** -0.5

    z = _dot(tail_ref[:, 2 * LANES:3 * LANES], wa2_ref[...]) + ba_ref[...]
    log_a = (jnp.minimum(z, 0.0) - jnp.log(1.0 + jnp.exp(-jnp.abs(z)))) * (1.0 / GLA_GATE_TAU)
    la_hi = log_a.astype(BF16)
    la_lo = (log_a - la_hi.astype(F32)).astype(BF16)
    for n in range(nch):
        rows = slice(n * CHUNK, (n + 1) * CHUNK)
        G = _dot(tri, la_hi[rows]) + _dot(tri, la_lo[rows])
        g_end = G[CHUNK - 1:CHUNK, :]
        kd_ref[rows, :] = (qk_ref[rows, HK:2 * HK].astype(F32) * jnp.exp(g_end - G)).astype(BF16)
        dec_ref[n:n + 1, :] = jnp.exp(g_end)

    for h in range(GLA_HEADS):
        ks = slice(h * GLA_DK, (h + 1) * GLA_DK)
        vs = slice(h * GLA_DV, (h + 1) * GLA_DV)
        st = st_ref[h]
        for n in range(nch):
            rows = slice(n * CHUNK, (n + 1) * CHUNK)
            st = st * dec_ref[n:n + 1, ks] + _dot_tn(v_ref[rows, vs], kd_ref[rows, ks])
            sall_ref[n * GLA_HEADS + h] = st.astype(BF16)
        st_ref[h] = st

    for n in range(nch):
        rows = slice(n * CHUNK, (n + 1) * CHUNK)
        for h in range(GLA_HEADS):
            ks = slice(h * GLA_DK, (h + 1) * GLA_DK)
            vs = slice(h * GLA_DV, (h + 1) * GLA_DV)
            qh = (qk_ref[rows, ks].astype(F32) * qscale).astype(BF16)
            o = _dot_nt(qh, sall_ref[n * GLA_HEADS + h])
            o = o * lax.rsqrt(jnp.mean(o * o, axis=-1, keepdims=True) + RMS_EPS) * gn_ref[...]
            o_ref[rows, vs] = (o * _silu(gr_ref[rows, vs].astype(F32))).astype(BF16)
    y_ref[...] = _dot(o_ref[...], wo_ref[...]).astype(y_ref.dtype)


def _gla(qk, gv, gr, tail, wa2_p, b_a, g_norm, w_o, B, S):
    N = B * S
    D = w_o.shape[1]
    ts = min(S, 512)
    per_b = S // ts
    HK = GLA_HEADS * GLA_DK
    tok = lambda n: pl.BlockSpec((ts, n), lambda b, t: (b * per_b + t, 0))
    full = lambda a: pl.BlockSpec(a.shape, lambda b, t: (0,) * a.ndim)
    return pl.pallas_call(
        _gla_kernel,
        out_shape=jax.ShapeDtypeStruct((N, D), BF16),
        grid=(B, per_b),
        in_specs=[tok(QK_W), tok(GV_W), tok(GV_W), tok(TAIL_W), full(wa2_p), full(b_a), full(g_norm), full(w_o)],
        out_specs=tok(D),
        scratch_shapes=[pltpu.VMEM((GLA_HEADS, GLA_DV, GLA_DK), F32),
                        pltpu.VMEM((ts, GLA_HEADS * GLA_DK), BF16),
                        pltpu.VMEM((ts // CHUNK, GLA_HEADS * GLA_DK), F32),
                        pltpu.VMEM((ts // CHUNK * GLA_HEADS, GLA_DV, GLA_DK), BF16),
                        pltpu.VMEM((ts, GV_W), BF16)],
        compiler_params=_cparams("arbitrary", "arbitrary"),
        name="gla",
    )(qk, gv, gr, tail, wa2_p, b_a, g_norm, w_o)


HQ = MLA_HEADS * 2 * LANES


def _prep_w_uq(w_uq):
    dh = MLA_NOPE + MLA_ROPE
    half = MLA_ROPE // 2
    nope = [w_uq[:, h * dh:h * dh + MLA_NOPE] for h in range(MLA_HEADS)]
    rope = [w_uq[:, h * dh + MLA_NOPE:(h + 1) * dh] for h in range(MLA_HEADS)]
    rope_sw = [jnp.concatenate([r[:, half:], r[:, :half]], axis=1) for r in rope]
    return jnp.concatenate(nope + rope + rope_sw, axis=1).astype(BF16)


def _prep_w_ukv(w_ukv):
    dh = MLA_NOPE + MLA_V
    kn = [w_ukv[:, h * dh:h * dh + MLA_NOPE] for h in range(MLA_HEADS)]
    vv = [w_ukv[:, h * dh + MLA_NOPE:(h + 1) * dh] for h in range(MLA_HEADS)]
    return jnp.concatenate(kn + vv, axis=1).astype(BF16)


def _rms(x, g):
    return x * lax.rsqrt(jnp.mean(x * x, axis=-1, keepdims=True) + RMS_EPS) * g


def _mla_prep_kernel(cq_ref, ckv_ref, tail_ref, cc_ref, ss_ref, gq_ref, wq_ref, gkv_ref, wkv_ref, q_ref, k_ref, v_ref):
    tm = cq_ref.shape[0]
    NP = MLA_HEADS * MLA_NOPE
    RP = MLA_HEADS * MLA_ROPE
    scale = (MLA_NOPE + MLA_ROPE) ** -0.5 * LOG2E
    cc = cc_ref[...]
    ss = ss_ref[...]
    cqn = _rms(cq_ref[...].astype(F32), gq_ref[...]).astype(BF16)
    qf = _dot(cqn, wq_ref[...]) * scale
    ckvn = _rms(ckv_ref[...].astype(F32), gkv_ref[...]).astype(BF16)
    kv = _dot(ckvn, wkv_ref[...])
    krr = (tail_ref[:, 0:LANES].astype(F32) * cc + tail_ref[:, LANES:2 * LANES].astype(F32) * ss).astype(BF16)
    lane = lax.broadcasted_iota(I32, (tm, LANES), 1)
    first = lane < MLA_ROPE
    for j in range(MLA_HEADS // 2):
        a = NP + j * LANES
        rot = qf[:, a:a + LANES] * cc + qf[:, a + RP:a + RP + LANES] * ss
        for h, keep in ((2 * j, first), (2 * j + 1, jnp.logical_not(first))):
            base = h * 2 * LANES
            q_ref[:, base:base + LANES] = qf[:, h * MLA_NOPE:(h + 1) * MLA_NOPE].astype(BF16)
            q_ref[:, base + LANES:base + 2 * LANES] = jnp.where(keep, rot, 0.0).astype(BF16)
            k_ref[:, base:base + LANES] = kv[:, h * MLA_NOPE:(h + 1) * MLA_NOPE].astype(BF16)
            k_ref[:, base + LANES:base + 2 * LANES] = krr
    v_ref[...] = kv[:, NP:].astype(BF16)


def _mla_prep(cq, ckv, tail, cc, ss, g_cq, w_uq_p, g_ckv, w_ukv_p):
    N = cq.shape[0]
    tm = min(N, 512)
    tok = lambda n: pl.BlockSpec((tm, n), lambda i: (i, 0))
    full = lambda a: pl.BlockSpec(a.shape, lambda i: (0,) * a.ndim)
    HV = MLA_HEADS * MLA_V
    return pl.pallas_call(
        _mla_prep_kernel,
        out_shape=(jax.ShapeDtypeStruct((N, HQ), BF16), jax.ShapeDtypeStruct((N, HQ), BF16),
                   jax.ShapeDtypeStruct((N, HV), BF16)),
        grid=(N // tm,),
        in_specs=[tok(MLA_Q_RANK), tok(MLA_KV_RANK), tok(TAIL_W), tok(LANES), tok(LANES),
                  full(g_cq), full(w_uq_p), full(g_ckv), full(w_ukv_p)],
        out_specs=(tok(HQ), tok(HQ), tok(HV)),
        compiler_params=_cparams("arbitrary"),
        name="mla_prep",
    )(cq, ckv, tail, cc, ss, g_cq, w_uq_p, g_ckv, w_ukv_p)


ATTN_TQ = 256


def _mla_attn_kernel(q_ref, k_ref, v_ref, o_ref, v1_ref):
    S = q_ref.shape[0]
    tq = min(S, ATTN_TQ)
    r = lax.broadcasted_iota(I32, (tq, tq), 0) // CHUNK
    c = lax.broadcasted_iota(I32, (tq, tq), 1) // CHUNK
    diag_mask = c <= r
    v1_ref[:, :MLA_V] = v_ref[...]
    v1_ref[:, MLA_V:] = jnp.ones((S, MLA_V), BF16)

    def scores(ii):
        l0 = ii * tq
        q = q_ref[l0:l0 + tq, :]
        sd = jnp.where(diag_mask, _dot_nt(q, k_ref[l0:l0 + tq, :]), -jnp.inf)
        so = _dot_nt(q, k_ref[0:l0, :]) if ii > 0 else None
        return sd, so

    def finish(ii, sd, so):
        l0 = ii * tq
        m = jnp.max(sd, axis=-1, keepdims=True)
        if so is not None:
            m = jnp.maximum(m, jnp.max(so, axis=-1, keepdims=True))
        acc = _dot(jnp.exp2((sd - m).astype(BF16)), v1_ref[l0:l0 + tq, :])
        if so is not None:
            acc = acc + _dot(jnp.exp2((so - m).astype(BF16)), v1_ref[0:l0, :])
        o_ref[l0:l0 + tq, :] = (acc[:, :MLA_V] / acc[:, MLA_V:]).astype(BF16)

    n_tiles = S // tq
    ahead = 2
    pending = [scores(ii) for ii in range(min(ahead, n_tiles))]
    for ii in range(n_tiles):
        if ii + ahead < n_tiles:
            pending.append(scores(ii + ahead))
        finish(ii, *pending.pop(0))


def _mla_attention(qc, kc, vv, B, S):
    N = B * S
    HV = MLA_HEADS * MLA_V
    return pl.pallas_call(
        _mla_attn_kernel,
        out_shape=jax.ShapeDtypeStruct((N, HV), BF16),
        grid=(B, MLA_HEADS),
        in_specs=[
            pl.BlockSpec((S, 2 * LANES), lambda b, h: (b, h)),
            pl.BlockSpec((S, 2 * LANES), lambda b, h: (b, h)),
            pl.BlockSpec((S, MLA_V), lambda b, h: (b, h)),
        ],
        out_specs=pl.BlockSpec((S, MLA_V), lambda b, h: (b, h)),
        scratch_shapes=[pltpu.VMEM((S, 2 * MLA_V), BF16)],
        compiler_params=_cparams("arbitrary", "arbitrary"),
        name="mla_attn",
    )(qc, kc, vv)


def _layer_norm(u, g, b):
    mu = jnp.mean(u, axis=-1, keepdims=True)
    d = u - mu
    var = jnp.mean(d * d, axis=-1, keepdims=True)
    return d * lax.rsqrt(var + LN_EPS) * g + b


def _merge_kernel(alpha, x_ref, ya_ref, ob_ref, ga_ref, gb_ref, mod_ref, wmo_ref, wout_ref, g_ref, b_ref, wr_ref,
                  br_ref, x1_ref, h2_ref, idx_ref, wt_ref, rank_ref, cnt_ref, lg_s, carry_s):
    i = pl.program_id(0)

    @pl.when(i == 0)
    def _():
        lg_s[...] = jnp.zeros_like(lg_s)
        carry_s[...] = jnp.zeros_like(carry_s)

    idx, wt, rank, total = _route_math(lg_s[...], br_ref[...], carry_s[...])
    total = jnp.where(i > 0, total, 0.0)
    idx_ref[...] = idx
    wt_ref[...] = wt
    rank_ref[...] = rank
    carry_s[...] = total
    cnt_ref[...] = jnp.broadcast_to(total, cnt_ref.shape).astype(I32)

    gt1 = mod_ref[0, 2:3, :]
    sh2 = mod_ref[0, 3:4, :]
    sc2 = mod_ref[0, 4:5, :]
    y = (_sigmoid(ga_ref[...].astype(F32)) * ya_ref[...].astype(F32)
         + _sigmoid(gb_ref[...].astype(F32)) * _dot(ob_ref[...], wmo_ref[...]))
    mix = _dot(y.astype(BF16), wout_ref[...])
    x1 = _layer_norm(alpha * x_ref[...] + (1.0 + gt1) * mix, g_ref[...], b_ref[...])
    x1_ref[...] = x1
    h2 = x1 * (1.0 + sc2) + sh2
    h2_ref[...] = _pack_bf16_pair(h2)
    lg_s[...] = _dot_nt(wr_ref[...], h2.astype(BF16))


def _merge_route(alpha, x2, y_a, o_b, gate_a, gate_b, mod6, w_mla_o, w_out, ln_g, ln_b, wr_t, b_router, S, part):
    N, D = x2.shape
    E = wr_t.shape[0]
    tm = min(S, 512)
    per_b = S // tm
    n = N // tm // MOE_PARTS
    Np = n * tm
    base = part * n
    cur = lambda i: jnp.minimum(i, n - 1)
    prev = lambda i: jnp.maximum(i - 1, 0)
    tok_in = lambda w: pl.BlockSpec((tm, w), lambda i: (base + cur(i), 0))
    tok = lambda w: pl.BlockSpec((tm, w), lambda i: (cur(i), 0))
    full = lambda a: pl.BlockSpec(a.shape, lambda i: (0,) * a.ndim)
    lane_blk = pl.BlockSpec((TOP_K, tm), lambda i: (0, prev(i)))
    br = b_router.reshape(E, 1).astype(F32)
    return pl.pallas_call(
        functools.partial(_merge_kernel, alpha),
        out_shape=(jax.ShapeDtypeStruct((Np, D), F32), jax.ShapeDtypeStruct((Np, D // 2), I32),
                   jax.ShapeDtypeStruct((TOP_K, Np), I32), jax.ShapeDtypeStruct((Np, LANES), F32),
                   jax.ShapeDtypeStruct((TOP_K, Np), I32), jax.ShapeDtypeStruct((E, LANES), I32)),
        grid=(n + 1,),
        in_specs=[tok_in(D), tok_in(D), tok_in(D), tok_in(D), tok_in(D),
                  pl.BlockSpec((1, 6, D), lambda i: ((base + cur(i)) // per_b, 0, 0)),
                  full(w_mla_o), full(w_out), full(ln_g), full(ln_b), full(wr_t), full(br)],
        out_specs=(tok(D), tok(D // 2), lane_blk, pl.BlockSpec((tm, LANES), lambda i: (prev(i), 0)), lane_blk,
                   pl.BlockSpec((E, LANES), lambda i: (0, 0))),
        scratch_shapes=[pltpu.VMEM((E, tm), F32), pltpu.VMEM((E, 1), F32)],
        compiler_params=_cparams("arbitrary"),
        name="merge_ln1_route",
    )(x2, y_a, o_b, gate_a, gate_b, mod6, w_mla_o, w_out, ln_g, ln_b, wr_t, br)


def _first_argmax(v, io, n):
    m = jnp.max(v, axis=0, keepdims=True)
    idx = jnp.min(jnp.where(v == m, io, n), axis=0, keepdims=True)
    return m, idx


def _route_math(lg, br, carry):
    E, T = lg.shape
    gsz = E // N_GROUPS
    neg = -jnp.inf
    s = _sigmoid(lg)
    biased = s + br
    eio = lax.broadcasted_iota(I32, (E, T), 0)
    gio = lax.broadcasted_iota(I32, (gsz, T), 0)

    gs = []
    for g in range(N_GROUPS):
        blk = biased[g * gsz:(g + 1) * gsz]
        m1, i1 = _first_argmax(blk, gio, gsz)
        m2 = jnp.max(jnp.where(gio == i1, neg, blk), axis=0, keepdims=True)
        gs.append(m1 + m2)
    cur = jnp.concatenate(gs, axis=0)
    nio = lax.broadcasted_iota(I32, (N_GROUPS, T), 0)
    gsel = jnp.zeros((N_GROUPS, T), F32)
    for _ in range(TOPK_GROUPS):
        _, gi = _first_argmax(cur, nio, N_GROUPS)
        hit = nio == gi
        gsel = jnp.where(hit, 1.0, gsel)
        cur = jnp.where(hit, neg, cur)
    emask = jnp.concatenate([jnp.broadcast_to(gsel[g:g + 1], (gsz, T)) for g in range(N_GROUPS)], axis=0) > 0.0

    cur = jnp.where(emask, biased, neg)
    idxs, ws = [], []
    sel = jnp.zeros((E, T), F32)
    for _ in range(TOP_K):
        _, ei = _first_argmax(cur, eio, E)
        hit = eio == ei
        idxs.append(ei)
        ws.append(jnp.sum(jnp.where(hit, s, 0.0), axis=0, keepdims=True))
        sel = jnp.where(hit, 1.0, sel)
        cur = jnp.where(hit, neg, cur)
    w = jnp.concatenate(ws, axis=0)
    w = w / jnp.sum(w, axis=0, keepdims=True) * ROUTED_SCALE

    wpad = jnp.concatenate([w, jnp.zeros((LANES - TOP_K, T), F32)], axis=0)

    r = lax.broadcasted_iota(I32, (T, T), 0)
    c = lax.broadcasted_iota(I32, (T, T), 1)
    before = (r < c).astype(BF16)
    cnt = _dot(sel.astype(BF16), before) + carry
    ranks = [jnp.sum(jnp.where(eio == idxs[k], cnt, 0.0), axis=0, keepdims=True) for k in range(TOP_K)]
    total = cnt[:, T - 1:T] + sel[:, T - 1:T]
    return jnp.concatenate(idxs, axis=0), wpad.T, jnp.concatenate(ranks, axis=0).astype(I32), total


def _dest_kernel(idx_ref, rank_ref, ps_ref, dest_ref):
    K, T = idx_ref.shape
    E = ps_ref.shape[0]
    eio = lax.broadcasted_iota(I32, (E, T), 0)
    ps = ps_ref[...]
    rows = [jnp.sum(jnp.where(eio == idx_ref[k:k + 1, :], ps, 0), axis=0, keepdims=True) for k in range(K)]
    dest_ref[...] = jnp.concatenate(rows, axis=0) + rank_ref[...]


def _dest_slots(idx, rank, pad_start):
    K, N = idx.shape
    E = pad_start.shape[0]
    T = min(N, 512)
    spec = pl.BlockSpec((K, T), lambda i: (0, i))
    return pl.pallas_call(
        _dest_kernel,
        out_shape=jax.ShapeDtypeStruct((K, N), I32),
        grid=(N // T,),
        in_specs=[spec, spec, pl.BlockSpec((E, 1), lambda i: (0, 0))],
        out_specs=spec,
        compiler_params=_cparams("arbitrary"),
        name="dest_slots",
    )(idx, rank, pad_start.reshape(E, 1))


SC_ROWS = 64


def _sc_workers():
    info = plsc.get_sparse_core_info()
    return info.num_cores, info.num_subcores


def _dispatch_rows(h2p, dest3, n_slots):
    N, W = h2p.shape
    n_chunks, K, R = dest3.shape
    nc, ns = _sc_workers()
    per_w = n_chunks // (nc * ns)
    mesh = plsc.VectorSubcoreMesh(core_axis_name="c", subcore_axis_name="s")

    @functools.partial(
        pl.kernel, mesh=mesh,
        out_type=jax.ShapeDtypeStruct((n_slots, W), I32),
        scratch_types=[pltpu.VMEM((K, R), I32), pltpu.VMEM((K, R), I32),
                       pltpu.VMEM((R, W), I32), pltpu.VMEM((R, W), I32),
                       pltpu.SemaphoreType.DMA, pltpu.SemaphoreType.DMA, pltpu.SemaphoreType.DMA],
    )
    def k(h_hbm, d_hbm, xs_hbm, idx0, idx1, rows0, rows1, lsem0, lsem1, ssem):
        idx, rows, lsem = (idx0, idx1), (rows0, rows1), (lsem0, lsem1)
        base = (lax.axis_index("s") * nc + lax.axis_index("c")) * per_w

        def loads(ch, b):
            return (pltpu.make_async_copy(d_hbm.at[ch], idx[b], lsem[b]),
                    pltpu.make_async_copy(h_hbm.at[pl.ds(ch * R, R)], rows[b], lsem[b]))

        for cp in loads(base, 0):
            cp.start()
        for cp in loads(base, 0):
            cp.wait()

        @pl.loop(0, per_w, step=2)
        def _(j):
            for b in range(2):
                ch = base + j + b
                more = j + b + 1 < per_w

                @pl.when(more)
                def _():
                    for cp in loads(ch + 1, 1 - b):
                        cp.start()

                scatters = [pltpu.make_async_copy(rows[b], xs_hbm.at[idx[b].at[kk]], ssem) for kk in range(K)]
                for cp in scatters:
                    cp.start()
                for cp in scatters:
                    cp.wait()

                @pl.when(more)
                def _():
                    for cp in loads(ch + 1, 1 - b):
                        cp.wait()

    return k(h2p, dest3)


def _combine_rows(ys, dest3, N):
    n_slots, W = ys.shape
    n_chunks, K, R = dest3.shape
    nc, ns = _sc_workers()
    per_w = n_chunks // (nc * ns)
    mesh = plsc.VectorSubcoreMesh(core_axis_name="c", subcore_axis_name="s")

    @functools.partial(
        pl.kernel, mesh=mesh,
        out_type=jax.ShapeDtypeStruct((K, N, W), I32),
        scratch_types=[pltpu.VMEM((K, R), I32), pltpu.VMEM((R, W), I32), pltpu.VMEM((R, W), I32),
                       pltpu.SemaphoreType.DMA, pltpu.SemaphoreType.DMA,
                       pltpu.SemaphoreType.DMA, pltpu.SemaphoreType.DMA],
    )
    def k(ys_hbm, d_hbm, yk_hbm, idx_v, rows0, rows1, gsem0, gsem1, wsem0, wsem1):
        rows, gsem, wsem = (rows0, rows1), (gsem0, gsem1), (wsem0, wsem1)
        base = (lax.axis_index("s") * nc + lax.axis_index("c")) * per_w

        @pl.loop(0, per_w)
        def _(j):
            ch = base + j
            pltpu.sync_copy(d_hbm.at[ch], idx_v)

            def gather(kk):
                return pltpu.make_async_copy(ys_hbm.at[idx_v.at[kk]], rows[kk % 2], gsem[kk % 2])

            def write(kk):
                return pltpu.make_async_copy(rows[kk % 2], yk_hbm.at[kk, pl.ds(ch * R, R)], wsem[kk % 2])

            gather(0).start()
            for kk in range(K):
                gather(kk).wait()
                if kk + 1 < K:
                    if kk >= 1:
                        write(kk - 1).wait()
                    gather(kk + 1).start()
                write(kk).start()
            write(K - 2).wait()
            write(K - 1).wait()

    return k(ys, dest3)


def _expert_kernel(b0_ref, nb_ref, cnt_ref, xs_hbm, w1_ref, w3_ref, w2_ref, after_hbm, ys_hbm,
                   w13_s, w2_s, xbuf, ybuf, sem_in, sem_out):
    e = pl.program_id(0)
    n_exp = pl.num_programs(0)
    F = w1_ref.shape[2]
    bm = xbuf.shape[1]
    half = xbuf.shape[2]
    nb = nb_ref[e]
    b0 = b0_ref[e]
    total = b0_ref[n_exp - 1] + nb_ref[n_exp - 1]

    nbuf = xbuf.shape[0]

    def in_copy(g):
        slot = g % nbuf
        return pltpu.make_async_copy(xs_hbm.at[pl.ds(g * bm, bm)], xbuf.at[slot], sem_in.at[slot])

    def out_copy(g):
        slot = g % nbuf
        return pltpu.make_async_copy(ybuf.at[slot], ys_hbm.at[pl.ds(g * bm, bm)], sem_out.at[slot])

    @pl.when(e == 0)
    def _():
        for g0 in range(nbuf - 1):
            @pl.when(g0 < total)
            def _():
                in_copy(g0).start()

    @pl.when(nb > 0)
    def _():
        w13_s[:, :F] = w1_ref[0].astype(BF16)
        w13_s[:, F:] = w3_ref[0].astype(BF16)
        w2_s[...] = w2_ref[0].astype(BF16)

    def block(j, carry):
        g = b0 + j
        slot = g % nbuf
        in_copy(g).wait()

        @pl.when(g + nbuf - 1 < total)
        def _():
            in_copy(g + nbuf - 1).start()

        @pl.when(g >= nbuf)
        def _():
            out_copy(g - nbuf).wait()

        def swiglu(rows):
            lo, hi = _unpack_bf16_pair(xbuf[slot, :rows, :])
            ab = _dot(lo.astype(BF16), w13_s[:half, :]) + _dot(hi.astype(BF16), w13_s[half:, :])
            hid = (_silu(ab[:, :F]) * ab[:, F:]).astype(BF16)
            ybuf[slot, :rows, :] = _pack_bf16_pair(_dot(hid, w2_s[...]))

        rows_left = cnt_ref[e] - j * bm

        @pl.when(rows_left > bm // 2)
        def _():
            swiglu(bm)

        @pl.when(rows_left <= bm // 2)
        def _():
            swiglu(bm // 2)
            ybuf[slot, bm // 2:, :] = jnp.zeros((bm - bm // 2, half), I32)

        out_copy(g).start()
        return carry

    lax.fori_loop(0, nb, block, 0)

    @pl.when(e == n_exp - 1)
    def _():
        for back in range(nbuf, 0, -1):
            @pl.when(total >= back)
            def _():
                out_copy(total - back).wait()


def _experts(xs, blk0, nblk, counts, w_e1, w_e3, w_e2, after):
    n_slots, W = xs.shape
    E, D, F = w_e1.shape
    bm = EXPERT_BLOCK
    grid_spec = pltpu.PrefetchScalarGridSpec(
        num_scalar_prefetch=3,
        grid=(E,),
        in_specs=[
            pl.BlockSpec(memory_space=pl.ANY),
            pl.BlockSpec((1, D, F), lambda e, b0, nb, cnt: (e, 0, 0)),
            pl.BlockSpec((1, D, F), lambda e, b0, nb, cnt: (e, 0, 0)),
            pl.BlockSpec((1, F, D), lambda e, b0, nb, cnt: (e, 0, 0)),
            pl.BlockSpec(memory_space=pl.ANY),
        ],
        out_specs=pl.BlockSpec(memory_space=pl.ANY),
        scratch_shapes=[pltpu.VMEM((D, 2 * F), BF16), pltpu.VMEM((F, D), BF16),
                        pltpu.VMEM((EXPERT_RING, bm, W), I32), pltpu.VMEM((EXPERT_RING, bm, W), I32),
                        pltpu.SemaphoreType.DMA((EXPERT_RING,)), pltpu.SemaphoreType.DMA((EXPERT_RING,))],
    )
    return pl.pallas_call(
        _expert_kernel,
        out_shape=jax.ShapeDtypeStruct((n_slots, W), I32),
        grid_spec=grid_spec,
        compiler_params=_cparams("arbitrary"),
        name="experts",
    )(blk0, nblk, counts, xs, w_e1, w_e3, w_e2, after)


def _shared_kernel(h2_ref, ws13_ref, ws2_ref, o_ref):
    F = ws2_ref.shape[0]
    half = h2_ref.shape[1]
    lo, hi = _unpack_bf16_pair(h2_ref[...])
    ab = _dot(lo.astype(BF16), ws13_ref[:half, :]) + _dot(hi.astype(BF16), ws13_ref[half:, :])
    hid = (_silu(ab[:, :F]) * ab[:, F:]).astype(BF16)
    o_ref[...] = _dot(hid, ws2_ref[...]).astype(o_ref.dtype)


def _shared_expert(h2p, ws13, ws2):
    N, W = h2p.shape
    D = ws2.shape[1]
    tm = min(N, 512)
    full = lambda a: pl.BlockSpec(a.shape, lambda i: (0,) * a.ndim)
    return pl.pallas_call(
        _shared_kernel,
        out_shape=jax.ShapeDtypeStruct((N, D), BF16),
        grid=(N // tm,),
        in_specs=[pl.BlockSpec((tm, W), lambda i: (i, 0)), full(ws13), full(ws2)],
        out_specs=pl.BlockSpec((tm, D), lambda i: (i, 0)),
        compiler_params=_cparams("arbitrary"),
        name="shared_expert",
    )(h2p, ws13, ws2)


def _final_kernel(alpha, yk_ref, wt_ref, sh_ref, x1_ref, mod_ref, g_ref, b_ref, *rest):
    o_ref = rest[-1]
    gt2 = mod_ref[0, 5:6, :]
    wt = wt_ref[...]
    mlo = jnp.zeros(yk_ref.shape[1:], F32)
    mhi = jnp.zeros(yk_ref.shape[1:], F32)
    for k in range(TOP_K):
        lo, hi = _unpack_bf16_pair(yk_ref[k])
        wk = wt[:, k:k + 1]
        mlo = mlo + wk * lo
        mhi = mhi + wk * hi
    ffn = jnp.concatenate([mlo, mhi], axis=1) + sh_ref[...].astype(F32)
    o_ref[...] = _layer_norm(alpha * x1_ref[...] + (1.0 + gt2) * ffn, g_ref[...], b_ref[...])


def _final_part(alpha, yk, wt, sh, x1, mod6, ln_g, ln_b, S, tok0, row0, n_total, prev):
    D = x1.shape[1]
    N = n_total
    K, n_part, W = yk.shape
    tm = min(S, 256)
    per_b = S // tm
    steps = n_part // tm
    off = tok0 // tm
    goff = row0 // tm
    tok = lambda n: pl.BlockSpec((tm, n), lambda i: (off + i, 0))
    full = lambda a: pl.BlockSpec(a.shape, lambda i: (0,) * a.ndim)
    in_specs = [pl.BlockSpec((K, tm, W), lambda i: (0, i, 0)), tok(LANES), tok(D), tok(D),
                pl.BlockSpec((1, 6, D), lambda i: ((goff + i) // per_b, 0, 0)), full(ln_g), full(ln_b)]
    args = [yk, wt, sh, x1, mod6, ln_g, ln_b]
    aliases = {}
    if prev is not None:
        in_specs.append(pl.BlockSpec(memory_space=pl.ANY))
        args.append(prev)
        aliases = {len(args) - 1: 0}
    return pl.pallas_call(
        functools.partial(_final_kernel, alpha),
        out_shape=jax.ShapeDtypeStruct((N, D), F32),
        grid=(steps,),
        in_specs=in_specs,
        out_specs=pl.BlockSpec((tm, D), lambda i: (goff + i, 0)),
        input_output_aliases=aliases,
        compiler_params=_cparams("arbitrary"),
        name="combine_ln2",
    )(*args)


def _moe(alpha, h2p, idx, wt, rank, cnt, x1, mod6, w_e1, w_e3, w_e2, ws13, ws2, ln_g, ln_b, S, row0, n_total, out):
    N = x1.shape[0]
    E = w_e1.shape[0]
    bm = EXPERT_BLOCK
    counts = cnt[:, 0]
    padded = (counts + bm - 1) // bm * bm
    pad_end = jnp.cumsum(padded)
    pad_start = pad_end - padded
    n_slots = N * TOP_K + E * bm
    dest = _dest_slots(idx, rank, pad_start.astype(I32))
    dest3 = dest.reshape(TOP_K, N // SC_ROWS, SC_ROWS).transpose(1, 0, 2)
    xs = _dispatch_rows(h2p, dest3, n_slots)
    sh = _shared_expert(h2p, ws13, ws2)
    ys = _experts(xs, (pad_start // bm).astype(I32), (padded // bm).astype(I32), counts.astype(I32),
                  w_e1, w_e3, w_e2, sh)
    n_chunks = dest3.shape[0]
    cpp = n_chunks // COMBINE_PARTS
    for p in range(COMBINE_PARTS):
        yk = _combine_rows(ys, dest3[p * cpp:(p + 1) * cpp], cpp * SC_ROWS)
        tok0 = p * cpp * SC_ROWS
        out = _final_part(alpha, yk, wt, sh, x1, mod6, ln_g, ln_b, S, tok0, row0 + tok0, n_total, out)
    return out


def kernel(x, c, positions, w_ada, b_ada, w_in, w_gla_a2, b_gla_a, g_gla_norm, w_gla_o, g_cq, w_uq, g_ckv, w_ukv, w_mla_o, w_out, ln1_g, ln1_b, w_router, b_router, w_e1, w_e3, w_e2, w_s1, w_s3, w_s2, ln2_g, ln2_b):
    B, S, D = x.shape
    N = B * S
    depth = w_ada.shape[0]
    alpha = (2.0 * depth) ** 0.25
    row = lambda a: a.reshape(1, -1)
    cc, ss = _rope_tables(positions)
    x2 = x.reshape(N, D)
    for l in range(depth):
        mod6 = _modulation(c, w_ada[l], b_ada[l]).reshape(B, 6, D)
        qk, gv, gr, cq, ckv, gate_a, gate_b, tail = _in_projection(x2, mod6, _prep_w_in(w_in[l], D), S)
        wa2_p = jnp.concatenate(
            [w_gla_a2[l], jnp.zeros((LANES - GLA_GATE_RANK, w_gla_a2.shape[2]), F32)], axis=0).astype(BF16)
        y_a = _gla(qk, gv, gr, tail, wa2_p, row(b_gla_a[l]), row(g_gla_norm[l]), w_gla_o[l].astype(BF16), B, S)
        qc, kc, vv = _mla_prep(cq, ckv, tail, cc, ss, row(g_cq[l]), _prep_w_uq(w_uq[l]),
                               row(g_ckv[l]), _prep_w_ukv(w_ukv[l]))
        o_b = _mla_attention(qc, kc, vv, B, S)
        ws13 = jnp.concatenate([w_s1[l], w_s3[l]], axis=1).astype(BF16)
        ws2 = w_s2[l].astype(BF16)
        routed = [_merge_route(alpha, x2, y_a, o_b, gate_a, gate_b, mod6, w_mla_o[l].astype(BF16),
                               w_out[l].astype(BF16), row(ln1_g[l]), row(ln1_b[l]), w_router[l].T.astype(BF16),
                               b_router[l], S, p) for p in range(MOE_PARTS)]
        out = None
        for p, (x1, h2p, idx, wt, rank, cnt) in enumerate(routed):
            out = _moe(alpha, h2p, idx, wt, rank, cnt, x1, mod6, w_e1[l], w_e3[l], w_e2[l], ws13, ws2,
                       row(ln2_g[l]), row(ln2_b[l]), S, p * (N // MOE_PARTS), N, out)
        x2 = out
    return x2.reshape(B, S, D)
```

```python
import functools

import jax
import jax.numpy as jnp
from jax import lax
from jax.experimental import pallas as pl
from jax.experimental.pallas import tpu as pltpu
from jax.experimental.pallas import tpu_sc as plsc

CHUNK = 64
GLA_HEADS = 4
GLA_DK = 128
GLA_DV = 256
GLA_GATE_RANK = 16
GLA_GATE_TAU = 16.0
MLA_HEADS = 8
MLA_Q_RANK = 768
MLA_KV_RANK = 256
MLA_NOPE = 128
MLA_ROPE = 64
MLA_V = 128
ROPE_THETA = 10000.0
N_EXPERTS = 256
TOP_K = 8
N_GROUPS = 8
TOPK_GROUPS = 4
D_EXPERT = 256
ROUTED_SCALE = 2.5
LN_EPS = 1e-5
RMS_EPS = 1e-6
LOG2E = 1.4426950408889634

LANES = 128
VMEM_LIMIT = 56 * 1024 * 1024
EXPERT_BLOCK = 512
EXPERT_RING = 3
MOE_PARTS = 2
COMBINE_PARTS = 4

F32 = jnp.float32
BF16 = jnp.bfloat16
I32 = jnp.int32


def _cparams(*sem):
    return pltpu.CompilerParams(dimension_semantics=sem, vmem_limit_bytes=VMEM_LIMIT)


def _sigmoid(x):
    return 1.0 / (1.0 + jnp.exp(-x))


def _silu(x):
    return x * _sigmoid(x)


def _dot(a, b):
    return jnp.dot(a, b, preferred_element_type=F32)


def _dot_nt(a, b):
    return lax.dot_general(a, b, (((1,), (1,)), ((), ())), preferred_element_type=F32)


def _dot_tn(a, b):
    return lax.dot_general(a, b, (((0,), (0,)), ((), ())), preferred_element_type=F32)


def _pack_bf16_pair(x):
    w = x.shape[1] // 2
    u = lax.bitcast_convert_type(x.astype(BF16).astype(F32), I32)
    lo = lax.shift_right_logical(u[:, :w], jnp.int32(16))
    hi = jnp.bitwise_and(u[:, w:], jnp.int32(-65536))
    return jnp.bitwise_or(lo, hi)


def _unpack_bf16_pair(p):
    lo = lax.bitcast_convert_type(lax.shift_left(p, jnp.int32(16)), F32)
    hi = lax.bitcast_convert_type(jnp.bitwise_and(p, jnp.int32(-65536)), F32)
    return lo, hi


def _mod_kernel(c_ref, w_ref, b_ref, o_ref):
    cond = _silu(c_ref[...]).astype(BF16)
    o_ref[...] = _dot(cond, w_ref[...].astype(BF16)) + b_ref[...]


def _modulation(c, w_ada, b_ada):
    B, D = c.shape
    W = w_ada.shape[1]
    tn = D
    return pl.pallas_call(
        _mod_kernel,
        out_shape=jax.ShapeDtypeStruct((B, W), F32),
        grid=(W // tn,),
        in_specs=[
            pl.BlockSpec((B, D), lambda j: (0, 0)),
            pl.BlockSpec((D, tn), lambda j: (0, j)),
            pl.BlockSpec((1, tn), lambda j: (0, j)),
        ],
        out_specs=pl.BlockSpec((B, tn), lambda j: (0, j)),
        compiler_params=_cparams("arbitrary"),
        name="adaln_mod",
    )(c, w_ada, b_ada.reshape(1, W))


def _rope_kernel(pos_ref, f_ref, ph_ref, cc_ref, ss_ref):
    ang = pos_ref[...] * f_ref[...]
    cc_ref[...] = jnp.cos(ang)
    ss_ref[...] = jnp.sin(ang) * ph_ref[...]


def _rope_tables(positions):
    B, S = positions.shape
    N = B * S
    half = MLA_ROPE // 2
    inv_freq = ROPE_THETA ** (-jnp.arange(half, dtype=F32) * (2.0 / MLA_ROPE))
    f4 = jnp.tile(inv_freq, LANES // half).reshape(1, LANES)
    sign = jnp.tile(jnp.concatenate([-jnp.ones((half,), F32), jnp.ones((half,), F32)]), LANES // MLA_ROPE)
    pos = jnp.broadcast_to(positions.reshape(N, 1).astype(F32), (N, LANES))
    tm = min(N, 1024)
    spec = pl.BlockSpec((tm, LANES), lambda i: (i, 0))
    vec = pl.BlockSpec((1, LANES), lambda i: (0, 0))
    return pl.pallas_call(
        _rope_kernel,
        out_shape=(jax.ShapeDtypeStruct((N, LANES), F32), jax.ShapeDtypeStruct((N, LANES), F32)),
        grid=(N // tm,),
        in_specs=[spec, vec, vec],
        out_specs=(spec, spec),
        compiler_params=_cparams("arbitrary"),
        name="rope_tables",
    )(pos, f4, sign.reshape(1, LANES))


QK_W = 2 * GLA_HEADS * GLA_DK
GV_W = GLA_HEADS * GLA_DV
TAIL_W = 3 * LANES
IN_SEG = (QK_W, GV_W, GV_W, MLA_Q_RANK, MLA_KV_RANK, 1024, 1024, TAIL_W)


def _prep_w_in(w_in, D):
    s = [GLA_HEADS * GLA_DK, GLA_HEADS * GLA_DK, GV_W, GV_W, GLA_GATE_RANK, MLA_Q_RANK, MLA_KV_RANK, MLA_ROPE, D, D]
    offs = [0]
    for n in s:
        offs.append(offs[-1] + n)
    gq, gk, gv, gr, ga, cq, ckv, kr, gate_a, gate_b = [w_in[:, offs[i]:offs[i + 1]] for i in range(10)]
    half = MLA_ROPE // 2
    kr_sw = jnp.concatenate([kr[:, half:], kr[:, :half]], axis=1)
    pad = jnp.zeros((w_in.shape[0], LANES - GLA_GATE_RANK), w_in.dtype)
    return jnp.concatenate([gq, gk, gv, gr, cq, ckv, gate_a, gate_b, kr, kr, kr_sw, kr_sw, ga, pad], axis=1).astype(BF16)


def _inproj_kernel(x_ref, mod_ref, w_ref, *out_refs):
    sh1 = mod_ref[0, 0:1, :]
    sc1 = mod_ref[0, 1:2, :]
    h = (x_ref[...] * (1.0 + sc1) + sh1).astype(BF16)
    off = 0
    for ref in out_refs:
        n = ref.shape[-1]
        ref[...] = _dot(h, w_ref[:, off:off + n]).astype(ref.dtype)
        off += n


def _in_projection(x2, mod6, w_in_p, S):
    N, D = x2.shape
    W = w_in_p.shape[1]
    tm = min(S, 512)
    per_b = S // tm
    return pl.pallas_call(
        _inproj_kernel,
        out_shape=tuple(jax.ShapeDtypeStruct((N, n), BF16) for n in IN_SEG),
        grid=(N // tm,),
        in_specs=[
            pl.BlockSpec((tm, D), lambda i: (i, 0)),
            pl.BlockSpec((1, 6, D), lambda i: (i // per_b, 0, 0)),
            pl.BlockSpec((D, W), lambda i: (0, 0), pipeline_mode=pl.Buffered(1)),
        ],
        out_specs=tuple(pl.BlockSpec((tm, n), lambda i: (i, 0)) for n in IN_SEG),
        compiler_params=_cparams("arbitrary"),
        name="in_proj",
    )(x2, mod6, w_in_p)


def _gla_kernel(qk_ref, v_ref, gr_ref, tail_ref, wa2_ref, ba_ref, gn_ref, wo_ref, y_ref,
                st_ref, kd_ref, dec_ref, sall_ref, o_ref):
    t = pl.program_id(1)

    @pl.when(t == 0)
    def _():
        st_ref[...] = jnp.zeros_like(st_ref)

    ts = qk_ref.shape[0]
    nch = ts // CHUNK
    HK = GLA_HEADS * GLA_DK
    r = lax.broadcasted_iota(I32, (CHUNK, CHUNK), 0)
    c = lax.broadcasted_iota(I32, (CHUNK, CHUNK), 1)
    tri = (r >= c).astype(BF16)
    qscale = GLA_DK ** -0.5

    z = _dot(tail_ref[:, 2 * LANES:3 * LANES], wa2_ref[...]) + ba_ref[...]
    log_a = (jnp.minimum(z, 0.0) - jnp.log(1.0 + jnp.exp(-jnp.abs(z)))) * (1.0 / GLA_GATE_TAU)
    la_hi = log_a.astype(BF16)
    la_lo = (log_a - la_hi.astype(F32)).astype(BF16)
    for n in range(nch):
        rows = slice(n * CHUNK, (n + 1) * CHUNK)
        G = _dot(tri, la_hi[rows]) + _dot(tri, la_lo[rows])
        g_end = G[CHUNK - 1:CHUNK, :]
        kd_ref[rows, :] = (qk_ref[rows, HK:2 * HK].astype(F32) * jnp.exp(g_end - G)).astype(BF16)
        dec_ref[n:n + 1, :] = jnp.exp(g_end)

    for h in range(GLA_HEADS):
        ks = slice(h * GLA_DK, (h + 1) * GLA_DK)
        vs = slice(h * GLA_DV, (h + 1) * GLA_DV)
        st = st_ref[h]
        for n in range(nch):
            rows = slice(n * CHUNK, (n + 1) * CHUNK)
            st = st * dec_ref[n:n + 1, ks] + _dot_tn(v_ref[rows, vs], kd_ref[rows, ks])
            sall_ref[n * GLA_HEADS + h] = st.astype(BF16)
        st_ref[h] = st

    for n in range(nch):
        rows = slice(n * CHUNK, (n + 1) * CHUNK)
        for h in range(GLA_HEADS):
            ks = slice(h * GLA_DK, (h + 1) * GLA_DK)
            vs = slice(h * GLA_DV, (h + 1) * GLA_DV)
            qh = (qk_ref[rows, ks].astype(F32) * qscale).astype(BF16)
            o = _dot_nt(qh, sall_ref[n * GLA_HEADS + h])
            o = o * lax.rsqrt(jnp.mean(o * o, axis=-1, keepdims=True) + RMS_EPS) * gn_ref[...]
            o_ref[rows, vs] = (o * _silu(gr_ref[rows, vs].astype(F32))).astype(BF16)
    y_ref[...] = _dot(o_ref[...], wo_ref[...]).astype(y_ref.dtype)


def _gla(qk, gv, gr, tail, wa2_p, b_a, g_norm, w_o, B, S):
    N = B * S
    D = w_o.shape[1]
    ts = min(S, 512)
    per_b = S // ts
    HK = GLA_HEADS * GLA_DK
    tok = lambda n: pl.BlockSpec((ts, n), lambda b, t: (b * per_b + t, 0))
    full = lambda a: pl.BlockSpec(a.shape, lambda b, t: (0,) * a.ndim)
    return pl.pallas_call(
        _gla_kernel,
        out_shape=jax.ShapeDtypeStruct((N, D), BF16),
        grid=(B, per_b),
        in_specs=[tok(QK_W), tok(GV_W), tok(GV_W), tok(TAIL_W), full(wa2_p), full(b_a), full(g_norm), full(w_o)],
        out_specs=tok(D),
        scratch_shapes=[pltpu.VMEM((GLA_HEADS, GLA_DV, GLA_DK), F32),
                        pltpu.VMEM((ts, GLA_HEADS * GLA_DK), BF16),
                        pltpu.VMEM((ts // CHUNK, GLA_HEADS * GLA_DK), F32),
                        pltpu.VMEM((ts // CHUNK * GLA_HEADS, GLA_DV, GLA_DK), BF16),
                        pltpu.VMEM((ts, GV_W), BF16)],
        compiler_params=_cparams("arbitrary", "arbitrary"),
        name="gla",
    )(qk, gv, gr, tail, wa2_p, b_a, g_norm, w_o)


HQ = MLA_HEADS * 2 * LANES


def _prep_w_uq(w_uq):
    dh = MLA_NOPE + MLA_ROPE
    half = MLA_ROPE // 2
    nope = [w_uq[:, h * dh:h * dh + MLA_NOPE] for h in range(MLA_HEADS)]
    rope = [w_uq[:, h * dh + MLA_NOPE:(h + 1) * dh] for h in range(MLA_HEADS)]
    rope_sw = [jnp.concatenate([r[:, half:], r[:, :half]], axis=1) for r in rope]
    return jnp.concatenate(nope + rope + rope_sw, axis=1).astype(BF16)


def _prep_w_ukv(w_ukv):
    dh = MLA_NOPE + MLA_V
    kn = [w_ukv[:, h * dh:h * dh + MLA_NOPE] for h in range(MLA_HEADS)]
    vv = [w_ukv[:, h * dh + MLA_NOPE:(h + 1) * dh] for h in range(MLA_HEADS)]
    return jnp.concatenate(kn + vv, axis=1).astype(BF16)


def _rms(x, g):
    return x * lax.rsqrt(jnp.mean(x * x, axis=-1, keepdims=True) + RMS_EPS) * g


def _mla_prep_kernel(cq_ref, ckv_ref, tail_ref, cc_ref, ss_ref, gq_ref, wq_ref, gkv_ref, wkv_ref, q_ref, k_ref, v_ref):
    tm = cq_ref.shape[0]
    NP = MLA_HEADS * MLA_NOPE
    RP = MLA_HEADS * MLA_ROPE
    scale = (MLA_NOPE + MLA_ROPE) ** -0.5 * LOG2E
    cc = cc_ref[...]
    ss = ss_ref[...]
    cqn = _rms(cq_ref[...].astype(F32), gq_ref[...]).astype(BF16)
    qf = _dot(cqn, wq_ref[...]) * scale
    ckvn = _rms(ckv_ref[...].astype(F32), gkv_ref[...]).astype(BF16)
    kv = _dot(ckvn, wkv_ref[...])
    krr = (tail_ref[:, 0:LANES].astype(F32) * cc + tail_ref[:, LANES:2 * LANES].astype(F32) * ss).astype(BF16)
    lane = lax.broadcasted_iota(I32, (tm, LANES), 1)
    first = lane < MLA_ROPE
    for j in range(MLA_HEADS // 2):
        a = NP + j * LANES
        rot = qf[:, a:a + LANES] * cc + qf[:, a + RP:a + RP + LANES] * ss
        for h, keep in ((2 * j, first), (2 * j + 1, jnp.logical_not(first))):
            base = h * 2 * LANES
            q_ref[:, base:base + LANES] = qf[:, h * MLA_NOPE:(h + 1) * MLA_NOPE].astype(BF16)
            q_ref[:, base + LANES:base + 2 * LANES] = jnp.where(keep, rot, 0.0).astype(BF16)
            k_ref[:, base:base + LANES] = kv[:, h * MLA_NOPE:(h + 1) * MLA_NOPE].astype(BF16)
            k_ref[:, base + LANES:base + 2 * LANES] = krr
    v_ref[...] = kv[:, NP:].astype(BF16)


def _mla_prep(cq, ckv, tail, cc, ss, g_cq, w_uq_p, g_ckv, w_ukv_p):
    N = cq.shape[0]
    tm = min(N, 512)
    tok = lambda n: pl.BlockSpec((tm, n), lambda i: (i, 0))
    full = lambda a: pl.BlockSpec(a.shape, lambda i: (0,) * a.ndim)
    HV = MLA_HEADS * MLA_V
    return pl.pallas_call(
        _mla_prep_kernel,
        out_shape=(jax.ShapeDtypeStruct((N, HQ), BF16), jax.ShapeDtypeStruct((N, HQ), BF16),
                   jax.ShapeDtypeStruct((N, HV), BF16)),
        grid=(N // tm,),
        in_specs=[tok(MLA_Q_RANK), tok(MLA_KV_RANK), tok(TAIL_W), tok(LANES), tok(LANES),
                  full(g_cq), full(w_uq_p), full(g_ckv), full(w_ukv_p)],
        out_specs=(tok(HQ), tok(HQ), tok(HV)),
        compiler_params=_cparams("arbitrary"),
        name="mla_prep",
    )(cq, ckv, tail, cc, ss, g_cq, w_uq_p, g_ckv, w_ukv_p)


ATTN_TQ = 256


def _mla_attn_kernel(q_ref, k_ref, v_ref, o_ref, v1_ref):
    S = q_ref.shape[0]
    tq = min(S, ATTN_TQ)
    r = lax.broadcasted_iota(I32, (tq, tq), 0) // CHUNK
    c = lax.broadcasted_iota(I32, (tq, tq), 1) // CHUNK
    diag_mask = c <= r
    v1_ref[:, :MLA_V] = v_ref[...]
    v1_ref[:, MLA_V:] = jnp.ones((S, MLA_V), BF16)

    def scores(ii):
        l0 = ii * tq
        q = q_ref[l0:l0 + tq, :]
        sd = jnp.where(diag_mask, _dot_nt(q, k_ref[l0:l0 + tq, :]), -jnp.inf)
        so = _dot_nt(q, k_ref[0:l0, :]) if ii > 0 else None
        return sd, so

    def finish(ii, sd, so):
        l0 = ii * tq
        m = jnp.max(sd, axis=-1, keepdims=True)
        if so is not None:
            m = jnp.maximum(m, jnp.max(so, axis=-1, keepdims=True))
        acc = _dot(jnp.exp2((sd - m).astype(BF16)), v1_ref[l0:l0 + tq, :])
        if so is not None:
            acc = acc + _dot(jnp.exp2((so - m).astype(BF16)), v1_ref[0:l0, :])
        o_ref[l0:l0 + tq, :] = (acc[:, :MLA_V] / acc[:, MLA_V:]).astype(BF16)

    n_tiles = S // tq
    ahead = 2
    pending = [scores(ii) for ii in range(min(ahead, n_tiles))]
    for ii in range(n_tiles):
        if ii + ahead < n_tiles:
            pending.append(scores(ii + ahead))
        finish(ii, *pending.pop(0))


def _mla_attention(qc, kc, vv, B, S):
    N = B * S
    HV = MLA_HEADS * MLA_V
    return pl.pallas_call(
        _mla_attn_kernel,
        out_shape=jax.ShapeDtypeStruct((N, HV), BF16),
        grid=(B, MLA_HEADS),
        in_specs=[
            pl.BlockSpec((S, 2 * LANES), lambda b, h: (b, h)),
            pl.BlockSpec((S, 2 * LANES), lambda b, h: (b, h)),
            pl.BlockSpec((S, MLA_V), lambda b, h: (b, h)),
        ],
        out_specs=pl.BlockSpec((S, MLA_V), lambda b, h: (b, h)),
        scratch_shapes=[pltpu.VMEM((S, 2 * MLA_V), BF16)],
        compiler_params=_cparams("arbitrary", "arbitrary"),
        name="mla_attn",
    )(qc, kc, vv)


def _layer_norm(u, g, b):
    mu = jnp.mean(u, axis=-1, keepdims=True)
    d = u - mu
    var = jnp.mean(d * d, axis=-1, keepdims=True)
    return d * lax.rsqrt(var + LN_EPS) * g + b


def _merge_kernel(alpha, x_ref, ya_ref, ob_ref, ga_ref, gb_ref, mod_ref, wmo_ref, wout_ref, g_ref, b_ref, wr_ref,
                  br_ref, x1_ref, h2_ref, idx_ref, wt_ref, rank_ref, cnt_ref, lg_s, carry_s):
    i = pl.program_id(0)

    @pl.when(i == 0)
    def _():
        lg_s[...] = jnp.zeros_like(lg_s)
        carry_s[...] = jnp.zeros_like(carry_s)

    idx, wt, rank, total = _route_math(lg_s[...], br_ref[...], carry_s[...])
    total = jnp.where(i > 0, total, 0.0)
    idx_ref[...] = idx
    wt_ref[...] = wt
    rank_ref[...] = rank
    carry_s[...] = total
    cnt_ref[...] = jnp.broadcast_to(total, cnt_ref.shape).astype(I32)

    gt1 = mod_ref[0, 2:3, :]
    sh2 = mod_ref[0, 3:4, :]
    sc2 = mod_ref[0, 4:5, :]
    y = (_sigmoid(ga_ref[...].astype(F32)) * ya_ref[...].astype(F32)
         + _sigmoid(gb_ref[...].astype(F32)) * _dot(ob_ref[...], wmo_ref[...]))
    mix = _dot(y.astype(BF16), wout_ref[...])
    x1 = _layer_norm(alpha * x_ref[...] + (1.0 + gt1) * mix, g_ref[...], b_ref[...])
    x1_ref[...] = x1
    h2 = x1 * (1.0 + sc2) + sh2
    h2_ref[...] = _pack_bf16_pair(h2)
    lg_s[...] = _dot_nt(wr_ref[...], h2.astype(BF16))


def _merge_route(alpha, x2, y_a, o_b, gate_a, gate_b, mod6, w_mla_o, w_out, ln_g, ln_b, wr_t, b_router, S, part):
    N, D = x2.shape
    E = wr_t.shape[0]
    tm = min(S, 512)
    per_b = S // tm
    n = N // tm // MOE_PARTS
    Np = n * tm
    base = part * n
    cur = lambda i: jnp.minimum(i, n - 1)
    prev = lambda i: jnp.maximum(i - 1, 0)
    tok_in = lambda w: pl.BlockSpec((tm, w), lambda i: (base + cur(i), 0))
    tok = lambda w: pl.BlockSpec((tm, w), lambda i: (cur(i), 0))
    full = lambda a: pl.BlockSpec(a.shape, lambda i: (0,) * a.ndim)
    lane_blk = pl.BlockSpec((TOP_K, tm), lambda i: (0, prev(i)))
    br = b_router.reshape(E, 1).astype(F32)
    return pl.pallas_call(
        functools.partial(_merge_kernel, alpha),
        out_shape=(jax.ShapeDtypeStruct((Np, D), F32), jax.ShapeDtypeStruct((Np, D // 2), I32),
                   jax.ShapeDtypeStruct((TOP_K, Np), I32), jax.ShapeDtypeStruct((Np, LANES), F32),
                   jax.ShapeDtypeStruct((TOP_K, Np), I32), jax.ShapeDtypeStruct((E, LANES), I32)),
        grid=(n + 1,),
        in_specs=[tok_in(D), tok_in(D), tok_in(D), tok_in(D), tok_in(D),
                  pl.BlockSpec((1, 6, D), lambda i: ((base + cur(i)) // per_b, 0, 0)),
                  full(w_mla_o), full(w_out), full(ln_g), full(ln_b), full(wr_t), full(br)],
        out_specs=(tok(D), tok(D // 2), lane_blk, pl.BlockSpec((tm, LANES), lambda i: (prev(i), 0)), lane_blk,
                   pl.BlockSpec((E, LANES), lambda i: (0, 0))),
        scratch_shapes=[pltpu.VMEM((E, tm), F32), pltpu.VMEM((E, 1), F32)],
        compiler_params=_cparams("arbitrary"),
        name="merge_ln1_route",
    )(x2, y_a, o_b, gate_a, gate_b, mod6, w_mla_o, w_out, ln_g, ln_b, wr_t, br)


def _first_argmax(v, io, n):
    m = jnp.max(v, axis=0, keepdims=True)
    idx = jnp.min(jnp.where(v == m, io, n), axis=0, keepdims=True)
    return m, idx


def _route_math(lg, br, carry):
    E, T = lg.shape
    gsz = E // N_GROUPS
    neg = -jnp.inf
    s = _sigmoid(lg)
    biased = s + br
    eio = lax.broadcasted_iota(I32, (E, T), 0)
    gio = lax.broadcasted_iota(I32, (gsz, T), 0)

    gs = []
    for g in range(N_GROUPS):
        blk = biased[g * gsz:(g + 1) * gsz]
        m1, i1 = _first_argmax(blk, gio, gsz)
        m2 = jnp.max(jnp.where(gio == i1, neg, blk), axis=0, keepdims=True)
        gs.append(m1 + m2)
    cur = jnp.concatenate(gs, axis=0)
    nio = lax.broadcasted_iota(I32, (N_GROUPS, T), 0)
    gsel = jnp.zeros((N_GROUPS, T), F32)
    for _ in range(TOPK_GROUPS):
        _, gi = _first_argmax(cur, nio, N_GROUPS)
        hit = nio == gi
        gsel = jnp.where(hit, 1.0, gsel)
        cur = jnp.where(hit, neg, cur)
    emask = jnp.concatenate([jnp.broadcast_to(gsel[g:g + 1], (gsz, T)) for g in range(N_GROUPS)], axis=0) > 0.0

    cur = jnp.where(emask, biased, neg)
    idxs, ws = [], []
    sel = jnp.zeros((E, T), F32)
    for _ in range(TOP_K):
        _, ei = _first_argmax(cur, eio, E)
        hit = eio == ei
        idxs.append(ei)
        ws.append(jnp.sum(jnp.where(hit, s, 0.0), axis=0, keepdims=True))
        sel = jnp.where(hit, 1.0, sel)
        cur = jnp.where(hit, neg, cur)
    w = jnp.concatenate(ws, axis=0)
    w = w / jnp.sum(w, axis=0, keepdims=True) * ROUTED_SCALE

    wpad = jnp.concatenate([w, jnp.zeros((LANES - TOP_K, T), F32)], axis=0)

    r = lax.broadcasted_iota(I32, (T, T), 0)
    c = lax.broadcasted_iota(I32, (T, T), 1)
    before = (r < c).astype(BF16)
    cnt = _dot(sel.astype(BF16), before) + carry
    ranks = [jnp.sum(jnp.where(eio == idxs[k], cnt, 0.0), axis=0, keepdims=True) for k in range(TOP_K)]
    total = cnt[:, T - 1:T] + sel[:, T - 1:T]
    return jnp.concatenate(idxs, axis=0), wpad.T, jnp.concatenate(ranks, axis=0).astype(I32), total


def _dest_kernel(idx_ref, rank_ref, ps_ref, dest_ref):
    K, T = idx_ref.shape
    E = ps_ref.shape[0]
    eio = lax.broadcasted_iota(I32, (E, T), 0)
    ps = ps_ref[...]
    rows = [jnp.sum(jnp.where(eio == idx_ref[k:k + 1, :], ps, 0), axis=0, keepdims=True) for k in range(K)]
    dest_ref[...] = jnp.concatenate(rows, axis=0) + rank_ref[...]


def _dest_slots(idx, rank, pad_start):
    K, N = idx.shape
    E = pad_start.shape[0]
    T = min(N, 512)
    spec = pl.BlockSpec((K, T), lambda i: (0, i))
    return pl.pallas_call(
        _dest_kernel,
        out_shape=jax.ShapeDtypeStruct((K, N), I32),
        grid=(N // T,),
        in_specs=[spec, spec, pl.BlockSpec((E, 1), lambda i: (0, 0))],
        out_specs=spec,
        compiler_params=_cparams("arbitrary"),
        name="dest_slots",
    )(idx, rank, pad_start.reshape(E, 1))


SC_ROWS = 64


def _sc_workers():
    info = plsc.get_sparse_core_info()
    return info.num_cores, info.num_subcores


def _dispatch_rows(h2p, dest3, n_slots):
    N, W = h2p.shape
    n_chunks, K, R = dest3.shape
    nc, ns = _sc_workers()
    per_w = n_chunks // (nc * ns)
    mesh = plsc.VectorSubcoreMesh(core_axis_name="c", subcore_axis_name="s")

    @functools.partial(
        pl.kernel, mesh=mesh,
        out_type=jax.ShapeDtypeStruct((n_slots, W), I32),
        scratch_types=[pltpu.VMEM((K, R), I32), pltpu.VMEM((K, R), I32),
                       pltpu.VMEM((R, W), I32), pltpu.VMEM((R, W), I32),
                       pltpu.SemaphoreType.DMA, pltpu.SemaphoreType.DMA, pltpu.SemaphoreType.DMA],
    )
    def k(h_hbm, d_hbm, xs_hbm, idx0, idx1, rows0, rows1, lsem0, lsem1, ssem):
        idx, rows, lsem = (idx0, idx1), (rows0, rows1), (lsem0, lsem1)
        base = (lax.axis_index("s") * nc + lax.axis_index("c")) * per_w

        def loads(ch, b):
            return (pltpu.make_async_copy(d_hbm.at[ch], idx[b], lsem[b]),
                    pltpu.make_async_copy(h_hbm.at[pl.ds(ch * R, R)], rows[b], lsem[b]))

        for cp in loads(base, 0):
            cp.start()
        for cp in loads(base, 0):
            cp.wait()

        @pl.loop(0, per_w, step=2)
        def _(j):
            for b in range(2):
                ch = base + j + b
                more = j + b + 1 < per_w

                @pl.when(more)
                def _():
                    for cp in loads(ch + 1, 1 - b):
                        cp.start()

                scatters = [pltpu.make_async_copy(rows[b], xs_hbm.at[idx[b].at[kk]], ssem) for kk in range(K)]
                for cp in scatters:
                    cp.start()
                for cp in scatters:
                    cp.wait()

                @pl.when(more)
                def _():
                    for cp in loads(ch + 1, 1 - b):
                        cp.wait()

    return k(h2p, dest3)


COMBINE_ROWS = 8


def _combine_sum(ys, dest_flat, wexp, n_tok):
    W = ys.shape[1]
    n_chunks, K, R, L = wexp.shape
    nc, ns = _sc_workers()
    per_w = n_chunks // (nc * ns)
    mesh = plsc.VectorSubcoreMesh(core_axis_name="c", subcore_axis_name="s")

    @functools.partial(
        pl.kernel, mesh=mesh, compiler_params=pltpu.CompilerParams(needs_layout_passes=False),
        out_type=jax.ShapeDtypeStruct((n_tok, W), I32),
        scratch_types=[pltpu.VMEM((per_w * K * R,), I32),
                       pltpu.VMEM((K, R, L), I32), pltpu.VMEM((K, R, L), I32),
                       pltpu.VMEM((K, R, W), I32), pltpu.VMEM((K, R, W), I32),
                       pltpu.VMEM((R, W), I32), pltpu.VMEM((R, W), I32),
                       pltpu.SemaphoreType.DMA, pltpu.SemaphoreType.DMA,
                       pltpu.SemaphoreType.DMA, pltpu.SemaphoreType.DMA],
    )
    def k(ys_hbm, d_hbm, w_hbm, o_hbm, idx_all, w0, w1, buf0, buf1, out0, out1, gsem0, gsem1, osem0, osem1):
        wv, buf, outv, gsem, osem = (w0, w1), (buf0, buf1), (out0, out1), (gsem0, gsem1), (osem0, osem1)
        base = (lax.axis_index("s") * nc + lax.axis_index("c")) * per_w
        pltpu.sync_copy(d_hbm.at[pl.ds(pl.multiple_of(base * (K * R), 8), per_w * K * R)], idx_all)

        def fetch(j, b):
            cps = [pltpu.make_async_copy(
                ys_hbm.at[idx_all.at[pl.ds(pl.multiple_of(j * (K * R) + kk * R, 8), R)]], buf[b].at[kk], gsem[b])
                for kk in range(K)]
            cps.append(pltpu.make_async_copy(w_hbm.at[base + j], wv[b], gsem[b]))
            return cps

        def store(j, b):
            return pltpu.make_async_copy(outv[b], o_hbm.at[pl.ds(pl.multiple_of((base + j) * R, 8), R)], osem[b])

        for cp in fetch(0, 0):
            cp.start()

        @pl.loop(0, per_w, step=2)
        def _(j0):
            for b in range(2):
                j = j0 + b

                @pl.when(j + 1 < per_w)
                def _():
                    for cp in fetch(j + 1, 1 - b):
                        cp.start()

                for cp in fetch(j, b):
                    cp.wait()

                @pl.when(j >= 2)
                def _():
                    store(j - 2, b).wait()

                @pl.loop(0, R)
                def _(r):
                    ws = [plsc.bitcast(wv[b][kk, r, :], BF16) for kk in range(K)]

                    @pl.loop(0, W // L)
                    def _(q):
                        sl = pl.ds(q * L, L)
                        terms = [ws[kk] * plsc.bitcast(buf[b][kk, r, sl], BF16) for kk in range(K)]
                        while len(terms) > 1:
                            terms = [terms[i] + terms[i + 1] for i in range(0, len(terms), 2)]
                        outv[b][r, sl] = plsc.bitcast(terms[0], I32)

                store(j, b).start()

        store(per_w - 2, 0).wait()
        store(per_w - 1, 1).wait()

    return k(ys, dest_flat, wexp)


def _expert_kernel(b0_ref, nb_ref, cnt_ref, xs_hbm, w1_ref, w3_ref, w2_ref, after_hbm, ys_hbm,
                   w13_s, w2_s, xbuf, ybuf, sem_in, sem_out):
    e = pl.program_id(0)
    n_exp = pl.num_programs(0)
    F = w1_ref.shape[2]
    bm = xbuf.shape[1]
    half = xbuf.shape[2]
    nb = nb_ref[e]
    b0 = b0_ref[e]
    total = b0_ref[n_exp - 1] + nb_ref[n_exp - 1]

    nbuf = xbuf.shape[0]

    def in_copy(g):
        slot = g % nbuf
        return pltpu.make_async_copy(xs_hbm.at[pl.ds(g * bm, bm)], xbuf.at[slot], sem_in.at[slot])

    def out_copy(g):
        slot = g % nbuf
        return pltpu.make_async_copy(ybuf.at[slot], ys_hbm.at[pl.ds(g * bm, bm)], sem_out.at[slot])

    @pl.when(e == 0)
    def _():
        for g0 in range(nbuf - 1):
            @pl.when(g0 < total)
            def _():
                in_copy(g0).start()

    @pl.when(nb > 0)
    def _():
        w13_s[:, :F] = w1_ref[0].astype(BF16)
        w13_s[:, F:] = w3_ref[0].astype(BF16)
        w2_s[...] = w2_ref[0].astype(BF16)

    def block(j, carry):
        g = b0 + j
        slot = g % nbuf
        in_copy(g).wait()

        @pl.when(g + nbuf - 1 < total)
        def _():
            in_copy(g + nbuf - 1).start()

        @pl.when(g >= nbuf)
        def _():
            out_copy(g - nbuf).wait()

        def swiglu(rows):
            lo, hi = _unpack_bf16_pair(xbuf[slot, :rows, :])
            ab = _dot(lo.astype(BF16), w13_s[:half, :]) + _dot(hi.astype(BF16), w13_s[half:, :])
            hid = (_silu(ab[:, :F]) * ab[:, F:]).astype(BF16)
            ybuf[slot, :rows, :] = _pack_bf16_pair(_dot(hid, w2_s[...]))

        rows_left = cnt_ref[e] - j * bm

        @pl.when(rows_left > bm // 2)
        def _():
            swiglu(bm)

        @pl.when(rows_left <= bm // 2)
        def _():
            swiglu(bm // 2)
            ybuf[slot, bm // 2:, :] = jnp.zeros((bm - bm // 2, half), I32)

        out_copy(g).start()
        return carry

    lax.fori_loop(0, nb, block, 0)

    @pl.when(e == n_exp - 1)
    def _():
        for back in range(nbuf, 0, -1):
            @pl.when(total >= back)
            def _():
                out_copy(total - back).wait()


def _experts(xs, blk0, nblk, counts, w_e1, w_e3, w_e2, after):
    n_slots, W = xs.shape
    E, D, F = w_e1.shape
    bm = EXPERT_BLOCK
    grid_spec = pltpu.PrefetchScalarGridSpec(
        num_scalar_prefetch=3,
        grid=(E,),
        in_specs=[
            pl.BlockSpec(memory_space=pl.ANY),
            pl.BlockSpec((1, D, F), lambda e, b0, nb, cnt: (e, 0, 0)),
            pl.BlockSpec((1, D, F), lambda e, b0, nb, cnt: (e, 0, 0)),
            pl.BlockSpec((1, F, D), lambda e, b0, nb, cnt: (e, 0, 0)),
            pl.BlockSpec(memory_space=pl.ANY),
        ],
        out_specs=pl.BlockSpec(memory_space=pl.ANY),
        scratch_shapes=[pltpu.VMEM((D, 2 * F), BF16), pltpu.VMEM((F, D), BF16),
                        pltpu.VMEM((EXPERT_RING, bm, W), I32), pltpu.VMEM((EXPERT_RING, bm, W), I32),
                        pltpu.SemaphoreType.DMA((EXPERT_RING,)), pltpu.SemaphoreType.DMA((EXPERT_RING,))],
    )
    return pl.pallas_call(
        _expert_kernel,
        out_shape=jax.ShapeDtypeStruct((n_slots, W), I32),
        grid_spec=grid_spec,
        compiler_params=_cparams("arbitrary"),
        name="experts",
    )(blk0, nblk, counts, xs, w_e1, w_e3, w_e2, after)


def _shared_kernel(h2_ref, ws13_ref, ws2_ref, o_ref):
    F = ws2_ref.shape[0]
    half = h2_ref.shape[1]
    lo, hi = _unpack_bf16_pair(h2_ref[...])
    ab = _dot(lo.astype(BF16), ws13_ref[:half, :]) + _dot(hi.astype(BF16), ws13_ref[half:, :])
    hid = (_silu(ab[:, :F]) * ab[:, F:]).astype(BF16)
    o_ref[...] = _dot(hid, ws2_ref[...]).astype(o_ref.dtype)


def _shared_expert(h2p, ws13, ws2):
    N, W = h2p.shape
    D = ws2.shape[1]
    tm = min(N, 512)
    full = lambda a: pl.BlockSpec(a.shape, lambda i: (0,) * a.ndim)
    return pl.pallas_call(
        _shared_kernel,
        out_shape=jax.ShapeDtypeStruct((N, D), BF16),
        grid=(N // tm,),
        in_specs=[pl.BlockSpec((tm, W), lambda i: (i, 0)), full(ws13), full(ws2)],
        out_specs=pl.BlockSpec((tm, D), lambda i: (i, 0)),
        compiler_params=_cparams("arbitrary"),
        name="shared_expert",
    )(h2p, ws13, ws2)


def _final_kernel(alpha, ms_ref, sh_ref, x1_ref, mod_ref, g_ref, b_ref, *rest):
    o_ref = rest[-1]
    gt2 = mod_ref[0, 5:6, :]
    lo, hi = _unpack_bf16_pair(ms_ref[...])
    ffn = jnp.concatenate([lo, hi], axis=1) + sh_ref[...].astype(F32)
    o_ref[...] = _layer_norm(alpha * x1_ref[...] + (1.0 + gt2) * ffn, g_ref[...], b_ref[...])


def _final_part(alpha, ms, sh, x1, mod6, ln_g, ln_b, S, tok0, row0, n_total, prev):
    D = x1.shape[1]
    N = n_total
    n_part, W = ms.shape
    tm = min(S, 256)
    per_b = S // tm
    steps = n_part // tm
    off = tok0 // tm
    goff = row0 // tm
    tok = lambda n: pl.BlockSpec((tm, n), lambda i: (off + i, 0))
    full = lambda a: pl.BlockSpec(a.shape, lambda i: (0,) * a.ndim)
    in_specs = [pl.BlockSpec((tm, W), lambda i: (i, 0)), tok(D), tok(D),
                pl.BlockSpec((1, 6, D), lambda i: ((goff + i) // per_b, 0, 0)), full(ln_g), full(ln_b)]
    args = [ms, sh, x1, mod6, ln_g, ln_b]
    aliases = {}
    if prev is not None:
        in_specs.append(pl.BlockSpec(memory_space=pl.ANY))
        args.append(prev)
        aliases = {len(args) - 1: 0}
    return pl.pallas_call(
        functools.partial(_final_kernel, alpha),
        out_shape=jax.ShapeDtypeStruct((N, D), F32),
        grid=(steps,),
        in_specs=in_specs,
        out_specs=pl.BlockSpec((tm, D), lambda i: (goff + i, 0)),
        input_output_aliases=aliases,
        compiler_params=_cparams("arbitrary"),
        name="combine_ln2",
    )(*args)


def _moe(alpha, h2p, idx, wt, rank, cnt, x1, mod6, w_e1, w_e3, w_e2, ws13, ws2, ln_g, ln_b, S, row0, n_total, out):
    N = x1.shape[0]
    E = w_e1.shape[0]
    bm = EXPERT_BLOCK
    counts = cnt[:, 0]
    padded = (counts + bm - 1) // bm * bm
    pad_end = jnp.cumsum(padded)
    pad_start = pad_end - padded
    n_slots = N * TOP_K + E * bm
    dest = _dest_slots(idx, rank, pad_start.astype(I32))
    dest3 = dest.reshape(TOP_K, N // SC_ROWS, SC_ROWS).transpose(1, 0, 2)
    xs = _dispatch_rows(h2p, dest3, n_slots)
    sh = _shared_expert(h2p, ws13, ws2)
    ys = _experts(xs, (pad_start // bm).astype(I32), (padded // bm).astype(I32), counts.astype(I32),
                  w_e1, w_e3, w_e2, sh)
    R = COMBINE_ROWS
    wbits = lax.bitcast_convert_type(wt[:, :TOP_K].astype(BF16).astype(F32), I32)
    wword = jnp.bitwise_or(lax.shift_right_logical(wbits, jnp.int32(16)), jnp.bitwise_and(wbits, jnp.int32(-65536)))
    wexp = jnp.broadcast_to(wword.reshape(N // R, R, TOP_K).transpose(0, 2, 1)[..., None],
                            (N // R, TOP_K, R, 16))
    destc = dest.reshape(TOP_K, N // R, R).transpose(1, 0, 2)
    cpp = N // R // COMBINE_PARTS
    for p in range(COMBINE_PARTS):
        ms = _combine_sum(ys, destc[p * cpp:(p + 1) * cpp].reshape(-1), wexp[p * cpp:(p + 1) * cpp], cpp * R)
        tok0 = p * cpp * R
        out = _final_part(alpha, ms, sh, x1, mod6, ln_g, ln_b, S, tok0, row0 + tok0, n_total, out)
    return out


def kernel(x, c, positions, w_ada, b_ada, w_in, w_gla_a2, b_gla_a, g_gla_norm, w_gla_o, g_cq, w_uq, g_ckv, w_ukv, w_mla_o, w_out, ln1_g, ln1_b, w_router, b_router, w_e1, w_e3, w_e2, w_s1, w_s3, w_s2, ln2_g, ln2_b):
    B, S, D = x.shape
    N = B * S
    depth = w_ada.shape[0]
    alpha = (2.0 * depth) ** 0.25
    row = lambda a: a.reshape(1, -1)
    cc, ss = _rope_tables(positions)
    x2 = x.reshape(N, D)
    for l in range(depth):
        mod6 = _modulation(c, w_ada[l], b_ada[l]).reshape(B, 6, D)
        qk, gv, gr, cq, ckv, gate_a, gate_b, tail = _in_projection(x2, mod6, _prep_w_in(w_in[l], D), S)
        wa2_p = jnp.concatenate(
            [w_gla_a2[l], jnp.zeros((LANES - GLA_GATE_RANK, w_gla_a2.shape[2]), F32)], axis=0).astype(BF16)
        y_a = _gla(qk, gv, gr, tail, wa2_p, row(b_gla_a[l]), row(g_gla_norm[l]), w_gla_o[l].astype(BF16), B, S)
        qc, kc, vv = _mla_prep(cq, ckv, tail, cc, ss, row(g_cq[l]), _prep_w_uq(w_uq[l]),
                               row(g_ckv[l]), _prep_w_ukv(w_ukv[l]))
        o_b = _mla_attention(qc, kc, vv, B, S)
        ws13 = jnp.concatenate([w_s1[l], w_s3[l]], axis=1).astype(BF16)
        ws2 = w_s2[l].astype(BF16)
        routed = [_merge_route(alpha, x2, y_a, o_b, gate_a, gate_b, mod6, w_mla_o[l].astype(BF16),
                               w_out[l].astype(BF16), row(ln1_g[l]), row(ln1_b[l]), w_router[l].T.astype(BF16),
                               b_router[l], S, p) for p in range(MOE_PARTS)]
        out = None
        for p, (x1, h2p, idx, wt, rank, cnt) in enumerate(routed):
            out = _moe(alpha, h2p, idx, wt, rank, cnt, x1, mod6, w_e1[l], w_e3[l], w_e2[l], ws13, ws2,
                       row(ln2_g[l]), row(ln2_b[l]), S, p * (N // MOE_PARTS), N, out)
        x2 = out
    return x2.reshape(B, S, D)
```

```python
import functools

import jax
import jax.numpy as jnp
from jax import lax
from jax.experimental import pallas as pl
from jax.experimental.pallas import tpu as pltpu
from jax.experimental.pallas import tpu_sc as plsc

CHUNK = 64
GLA_HEADS = 4
GLA_DK = 128
GLA_DV = 256
GLA_GATE_RANK = 16
GLA_GATE_TAU = 16.0
MLA_HEADS = 8
MLA_Q_RANK = 768
MLA_KV_RANK = 256
MLA_NOPE = 128
MLA_ROPE = 64
MLA_V = 128
ROPE_THETA = 10000.0
N_EXPERTS = 256
TOP_K = 8
N_GROUPS = 8
TOPK_GROUPS = 4
D_EXPERT = 256
ROUTED_SCALE = 2.5
LN_EPS = 1e-5
RMS_EPS = 1e-6
LOG2E = 1.4426950408889634

LANES = 128
VMEM_LIMIT = 56 * 1024 * 1024
EXPERT_BLOCK = 512
EXPERT_RING = 3
MOE_PARTS = 2
COMBINE_PARTS = 4

F32 = jnp.float32
BF16 = jnp.bfloat16
I32 = jnp.int32


def _cparams(*sem):
    return pltpu.CompilerParams(dimension_semantics=sem, vmem_limit_bytes=VMEM_LIMIT)


def _sigmoid(x):
    return 1.0 / (1.0 + jnp.exp(-x))


def _silu(x):
    return x * _sigmoid(x)


def _dot(a, b):
    return jnp.dot(a, b, preferred_element_type=F32)


def _dot_nt(a, b):
    return lax.dot_general(a, b, (((1,), (1,)), ((), ())), preferred_element_type=F32)


def _dot_tn(a, b):
    return lax.dot_general(a, b, (((0,), (0,)), ((), ())), preferred_element_type=F32)


def _pack_bf16_pair(x):
    w = x.shape[1] // 2
    u = lax.bitcast_convert_type(x.astype(BF16).astype(F32), I32)
    lo = lax.shift_right_logical(u[:, :w], jnp.int32(16))
    hi = jnp.bitwise_and(u[:, w:], jnp.int32(-65536))
    return jnp.bitwise_or(lo, hi)


def _unpack_bf16_pair(p):
    lo = lax.bitcast_convert_type(lax.shift_left(p, jnp.int32(16)), F32)
    hi = lax.bitcast_convert_type(jnp.bitwise_and(p, jnp.int32(-65536)), F32)
    return lo, hi


def _mod_kernel(c_ref, w_ref, b_ref, o_ref):
    cond = _silu(c_ref[...]).astype(BF16)
    o_ref[...] = _dot(cond, w_ref[...].astype(BF16)) + b_ref[...]


def _modulation(c, w_ada, b_ada):
    B, D = c.shape
    W = w_ada.shape[1]
    tn = D
    return pl.pallas_call(
        _mod_kernel,
        out_shape=jax.ShapeDtypeStruct((B, W), F32),
        grid=(W // tn,),
        in_specs=[
            pl.BlockSpec((B, D), lambda j: (0, 0)),
            pl.BlockSpec((D, tn), lambda j: (0, j)),
            pl.BlockSpec((1, tn), lambda j: (0, j)),
        ],
        out_specs=pl.BlockSpec((B, tn), lambda j: (0, j)),
        compiler_params=_cparams("arbitrary"),
        name="adaln_mod",
    )(c, w_ada, b_ada.reshape(1, W))


def _rope_kernel(pos_ref, f_ref, ph_ref, cc_ref, ss_ref):
    ang = pos_ref[...] * f_ref[...]
    cc_ref[...] = jnp.cos(ang)
    ss_ref[...] = jnp.sin(ang) * ph_ref[...]


def _rope_tables(positions):
    B, S = positions.shape
    N = B * S
    half = MLA_ROPE // 2
    inv_freq = ROPE_THETA ** (-jnp.arange(half, dtype=F32) * (2.0 / MLA_ROPE))
    f4 = jnp.tile(inv_freq, LANES // half).reshape(1, LANES)
    sign = jnp.tile(jnp.concatenate([-jnp.ones((half,), F32), jnp.ones((half,), F32)]), LANES // MLA_ROPE)
    pos = jnp.broadcast_to(positions.reshape(N, 1).astype(F32), (N, LANES))
    tm = min(N, 1024)
    spec = pl.BlockSpec((tm, LANES), lambda i: (i, 0))
    vec = pl.BlockSpec((1, LANES), lambda i: (0, 0))
    return pl.pallas_call(
        _rope_kernel,
        out_shape=(jax.ShapeDtypeStruct((N, LANES), F32), jax.ShapeDtypeStruct((N, LANES), F32)),
        grid=(N // tm,),
        in_specs=[spec, vec, vec],
        out_specs=(spec, spec),
        compiler_params=_cparams("arbitrary"),
        name="rope_tables",
    )(pos, f4, sign.reshape(1, LANES))


QK_W = 2 * GLA_HEADS * GLA_DK
GV_W = GLA_HEADS * GLA_DV
TAIL_W = 3 * LANES
IN_SEG = (QK_W, GV_W, GV_W, MLA_Q_RANK, MLA_KV_RANK, 1024, 1024, TAIL_W)


def _prep_w_in(w_in, D):
    s = [GLA_HEADS * GLA_DK, GLA_HEADS * GLA_DK, GV_W, GV_W, GLA_GATE_RANK, MLA_Q_RANK, MLA_KV_RANK, MLA_ROPE, D, D]
    offs = [0]
    for n in s:
        offs.append(offs[-1] + n)
    gq, gk, gv, gr, ga, cq, ckv, kr, gate_a, gate_b = [w_in[:, offs[i]:offs[i + 1]] for i in range(10)]
    half = MLA_ROPE // 2
    kr_sw = jnp.concatenate([kr[:, half:], kr[:, :half]], axis=1)
    pad = jnp.zeros((w_in.shape[0], LANES - GLA_GATE_RANK), w_in.dtype)
    return jnp.concatenate([gq, gk, gv, gr, cq, ckv, gate_a, gate_b, kr, kr, kr_sw, kr_sw, ga, pad], axis=1).astype(BF16)


def _inproj_kernel(x_ref, mod_ref, w_ref, *out_refs):
    sh1 = mod_ref[0, 0:1, :]
    sc1 = mod_ref[0, 1:2, :]
    h = (x_ref[...] * (1.0 + sc1) + sh1).astype(BF16)
    off = 0
    for ref in out_refs:
        n = ref.shape[-1]
        ref[...] = _dot(h, w_ref[:, off:off + n]).astype(ref.dtype)
        off += n


def _in_projection(x2, mod6, w_in_p, S):
    N, D = x2.shape
    W = w_in_p.shape[1]
    tm = min(S, 512)
    per_b = S // tm
    return pl.pallas_call(
        _inproj_kernel,
        out_shape=tuple(jax.ShapeDtypeStruct((N, n), BF16) for n in IN_SEG),
        grid=(N // tm,),
        in_specs=[
            pl.BlockSpec((tm, D), lambda i: (i, 0)),
            pl.BlockSpec((1, 6, D), lambda i: (i // per_b, 0, 0)),
            pl.BlockSpec((D, W), lambda i: (0, 0), pipeline_mode=pl.Buffered(1)),
        ],
        out_specs=tuple(pl.BlockSpec((tm, n), lambda i: (i, 0)) for n in IN_SEG),
        compiler_params=_cparams("arbitrary"),
        name="in_proj",
    )(x2, mod6, w_in_p)


def _gla_kernel(qk_ref, v_ref, gr_ref, tail_ref, wa2_ref, ba_ref, gn_ref, wo_ref, y_ref,
                st_ref, kd_ref, dec_ref, sall_ref, o_ref):
    t = pl.program_id(1)

    @pl.when(t == 0)
    def _():
        st_ref[...] = jnp.zeros_like(st_ref)

    ts = qk_ref.shape[0]
    nch = ts // CHUNK
    HK = GLA_HEADS * GLA_DK
    r = lax.broadcasted_iota(I32, (CHUNK, CHUNK), 0)
    c = lax.broadcasted_iota(I32, (CHUNK, CHUNK), 1)
    tri = (r >= c).astype(BF16)
    qscale = GLA_DK ** -0.5

    z = _dot(tail_ref[:, 2 * LANES:3 * LANES], wa2_ref[...]) + ba_ref[...]
    log_a = (jnp.minimum(z, 0.0) - jnp.log(1.0 + jnp.exp(-jnp.abs(z)))) * (1.0 / GLA_GATE_TAU)
    la_hi = log_a.astype(BF16)
    la_lo = (log_a - la_hi.astype(F32)).astype(BF16)
    for n in range(nch):
        rows = slice(n * CHUNK, (n + 1) * CHUNK)
        G = _dot(tri, la_hi[rows]) + _dot(tri, la_lo[rows])
        g_end = G[CHUNK - 1:CHUNK, :]
        kd_ref[rows, :] = (qk_ref[rows, HK:2 * HK].astype(F32) * jnp.exp(g_end - G)).astype(BF16)
        dec_ref[n:n + 1, :] = jnp.exp(g_end)

    for h in range(GLA_HEADS):
        ks = slice(h * GLA_DK, (h + 1) * GLA_DK)
        vs = slice(h * GLA_DV, (h + 1) * GLA_DV)
        st = st_ref[h]
        for n in range(nch):
            rows = slice(n * CHUNK, (n + 1) * CHUNK)
            st = st * dec_ref[n:n + 1, ks] + _dot_tn(v_ref[rows, vs], kd_ref[rows, ks])
            sall_ref[n * GLA_HEADS + h] = st.astype(BF16)
        st_ref[h] = st

    for n in range(nch):
        rows = slice(n * CHUNK, (n + 1) * CHUNK)
        for h in range(GLA_HEADS):
            ks = slice(h * GLA_DK, (h + 1) * GLA_DK)
            vs = slice(h * GLA_DV, (h + 1) * GLA_DV)
            qh = (qk_ref[rows, ks].astype(F32) * qscale).astype(BF16)
            o = _dot_nt(qh, sall_ref[n * GLA_HEADS + h])
            o = o * lax.rsqrt(jnp.mean(o * o, axis=-1, keepdims=True) + RMS_EPS) * gn_ref[...]
            o_ref[rows, vs] = (o * _silu(gr_ref[rows, vs].astype(F32))).astype(BF16)
    y_ref[...] = _dot(o_ref[...], wo_ref[...]).astype(y_ref.dtype)


def _gla(qk, gv, gr, tail, wa2_p, b_a, g_norm, w_o, B, S):
    N = B * S
    D = w_o.shape[1]
    ts = min(S, 512)
    per_b = S // ts
    HK = GLA_HEADS * GLA_DK
    tok = lambda n: pl.BlockSpec((ts, n), lambda b, t: (b * per_b + t, 0))
    full = lambda a: pl.BlockSpec(a.shape, lambda b, t: (0,) * a.ndim)
    return pl.pallas_call(
        _gla_kernel,
        out_shape=jax.ShapeDtypeStruct((N, D), BF16),
        grid=(B, per_b),
        in_specs=[tok(QK_W), tok(GV_W), tok(GV_W), tok(TAIL_W), full(wa2_p), full(b_a), full(g_norm), full(w_o)],
        out_specs=tok(D),
        scratch_shapes=[pltpu.VMEM((GLA_HEADS, GLA_DV, GLA_DK), F32),
                        pltpu.VMEM((ts, GLA_HEADS * GLA_DK), BF16),
                        pltpu.VMEM((ts // CHUNK, GLA_HEADS * GLA_DK), F32),
                        pltpu.VMEM((ts // CHUNK * GLA_HEADS, GLA_DV, GLA_DK), BF16),
                        pltpu.VMEM((ts, GV_W), BF16)],
        compiler_params=_cparams("arbitrary", "arbitrary"),
        name="gla",
    )(qk, gv, gr, tail, wa2_p, b_a, g_norm, w_o)


HQ = MLA_HEADS * 2 * LANES


def _prep_w_uq(w_uq):
    dh = MLA_NOPE + MLA_ROPE
    half = MLA_ROPE // 2
    nope = [w_uq[:, h * dh:h * dh + MLA_NOPE] for h in range(MLA_HEADS)]
    rope = [w_uq[:, h * dh + MLA_NOPE:(h + 1) * dh] for h in range(MLA_HEADS)]
    rope_sw = [jnp.concatenate([r[:, half:], r[:, :half]], axis=1) for r in rope]
    return jnp.concatenate(nope + rope + rope_sw, axis=1).astype(BF16)


def _prep_w_ukv(w_ukv):
    dh = MLA_NOPE + MLA_V
    kn = [w_ukv[:, h * dh:h * dh + MLA_NOPE] for h in range(MLA_HEADS)]
    vv = [w_ukv[:, h * dh + MLA_NOPE:(h + 1) * dh] for h in range(MLA_HEADS)]
    return jnp.concatenate(kn + vv, axis=1).astype(BF16)


def _rms(x, g):
    return x * lax.rsqrt(jnp.mean(x * x, axis=-1, keepdims=True) + RMS_EPS) * g


def _mla_prep_kernel(cq_ref, ckv_ref, tail_ref, cc_ref, ss_ref, gq_ref, wq_ref, gkv_ref, wkv_ref, q_ref, k_ref, v_ref):
    tm = cq_ref.shape[0]
    NP = MLA_HEADS * MLA_NOPE
    RP = MLA_HEADS * MLA_ROPE
    scale = (MLA_NOPE + MLA_ROPE) ** -0.5 * LOG2E
    cc = cc_ref[...]
    ss = ss_ref[...]
    cqn = _rms(cq_ref[...].astype(F32), gq_ref[...]).astype(BF16)
    qf = _dot(cqn, wq_ref[...]) * scale
    ckvn = _rms(ckv_ref[...].astype(F32), gkv_ref[...]).astype(BF16)
    kv = _dot(ckvn, wkv_ref[...])
    krr = (tail_ref[:, 0:LANES].astype(F32) * cc + tail_ref[:, LANES:2 * LANES].astype(F32) * ss).astype(BF16)
    lane = lax.broadcasted_iota(I32, (tm, LANES), 1)
    first = lane < MLA_ROPE
    for j in range(MLA_HEADS // 2):
        a = NP + j * LANES
        rot = qf[:, a:a + LANES] * cc + qf[:, a + RP:a + RP + LANES] * ss
        for h, keep in ((2 * j, first), (2 * j + 1, jnp.logical_not(first))):
            base = h * 2 * LANES
            q_ref[:, base:base + LANES] = qf[:, h * MLA_NOPE:(h + 1) * MLA_NOPE].astype(BF16)
            q_ref[:, base + LANES:base + 2 * LANES] = jnp.where(keep, rot, 0.0).astype(BF16)
            k_ref[:, base:base + LANES] = kv[:, h * MLA_NOPE:(h + 1) * MLA_NOPE].astype(BF16)
            k_ref[:, base + LANES:base + 2 * LANES] = krr
    v_ref[...] = kv[:, NP:].astype(BF16)


def _mla_prep(cq, ckv, tail, cc, ss, g_cq, w_uq_p, g_ckv, w_ukv_p):
    N = cq.shape[0]
    tm = min(N, 512)
    tok = lambda n: pl.BlockSpec((tm, n), lambda i: (i, 0))
    full = lambda a: pl.BlockSpec(a.shape, lambda i: (0,) * a.ndim)
    HV = MLA_HEADS * MLA_V
    return pl.pallas_call(
        _mla_prep_kernel,
        out_shape=(jax.ShapeDtypeStruct((N, HQ), BF16), jax.ShapeDtypeStruct((N, HQ), BF16),
                   jax.ShapeDtypeStruct((N, HV), BF16)),
        grid=(N // tm,),
        in_specs=[tok(MLA_Q_RANK), tok(MLA_KV_RANK), tok(TAIL_W), tok(LANES), tok(LANES),
                  full(g_cq), full(w_uq_p), full(g_ckv), full(w_ukv_p)],
        out_specs=(tok(HQ), tok(HQ), tok(HV)),
        compiler_params=_cparams("arbitrary"),
        name="mla_prep",
    )(cq, ckv, tail, cc, ss, g_cq, w_uq_p, g_ckv, w_ukv_p)


ATTN_TQ = 256


def _mla_attn_kernel(q_ref, k_ref, v_ref, o_ref, v1_ref):
    S = q_ref.shape[0]
    tq = min(S, ATTN_TQ)
    r = lax.broadcasted_iota(I32, (tq, tq), 0) // CHUNK
    c = lax.broadcasted_iota(I32, (tq, tq), 1) // CHUNK
    diag_mask = c <= r
    v1_ref[:, :MLA_V] = v_ref[...]
    v1_ref[:, MLA_V:] = jnp.ones((S, MLA_V), BF16)

    def scores(ii):
        l0 = ii * tq
        q = q_ref[l0:l0 + tq, :]
        sd = jnp.where(diag_mask, _dot_nt(q, k_ref[l0:l0 + tq, :]), -jnp.inf)
        so = _dot_nt(q, k_ref[0:l0, :]) if ii > 0 else None
        return sd, so

    def finish(ii, sd, so):
        l0 = ii * tq
        m = jnp.max(sd, axis=-1, keepdims=True)
        if so is not None:
            m = jnp.maximum(m, jnp.max(so, axis=-1, keepdims=True))
        acc = _dot(jnp.exp2((sd - m).astype(BF16)), v1_ref[l0:l0 + tq, :])
        if so is not None:
            acc = acc + _dot(jnp.exp2((so - m).astype(BF16)), v1_ref[0:l0, :])
        o_ref[l0:l0 + tq, :] = (acc[:, :MLA_V] / acc[:, MLA_V:]).astype(BF16)

    n_tiles = S // tq
    ahead = 2
    pending = [scores(ii) for ii in range(min(ahead, n_tiles))]
    for ii in range(n_tiles):
        if ii + ahead < n_tiles:
            pending.append(scores(ii + ahead))
        finish(ii, *pending.pop(0))


def _mla_attention(qc, kc, vv, B, S):
    N = B * S
    HV = MLA_HEADS * MLA_V
    return pl.pallas_call(
        _mla_attn_kernel,
        out_shape=jax.ShapeDtypeStruct((N, HV), BF16),
        grid=(B, MLA_HEADS),
        in_specs=[
            pl.BlockSpec((S, 2 * LANES), lambda b, h: (b, h)),
            pl.BlockSpec((S, 2 * LANES), lambda b, h: (b, h)),
            pl.BlockSpec((S, MLA_V), lambda b, h: (b, h)),
        ],
        out_specs=pl.BlockSpec((S, MLA_V), lambda b, h: (b, h)),
        scratch_shapes=[pltpu.VMEM((S, 2 * MLA_V), BF16)],
        compiler_params=_cparams("arbitrary", "arbitrary"),
        name="mla_attn",
    )(qc, kc, vv)


def _layer_norm(u, g, b):
    mu = jnp.mean(u, axis=-1, keepdims=True)
    d = u - mu
    var = jnp.mean(d * d, axis=-1, keepdims=True)
    return d * lax.rsqrt(var + LN_EPS) * g + b


def _merge_kernel(alpha, x_ref, ya_ref, ob_ref, ga_ref, gb_ref, mod_ref, wmo_ref, wout_ref, g_ref, b_ref, wr_ref,
                  br_ref, x1_ref, h2_ref, idx_ref, wt_ref, rank_ref, cnt_ref, lg_s, carry_s):
    i = pl.program_id(0)

    @pl.when(i == 0)
    def _():
        lg_s[...] = jnp.zeros_like(lg_s)
        carry_s[...] = jnp.zeros_like(carry_s)

    idx, wt, rank, total = _route_math(lg_s[...], br_ref[...], carry_s[...])
    total = jnp.where(i > 0, total, 0.0)
    idx_ref[...] = idx
    wt_ref[...] = wt
    rank_ref[...] = rank
    carry_s[...] = total
    cnt_ref[...] = jnp.broadcast_to(total, cnt_ref.shape).astype(I32)

    gt1 = mod_ref[0, 2:3, :]
    sh2 = mod_ref[0, 3:4, :]
    sc2 = mod_ref[0, 4:5, :]
    y = (_sigmoid(ga_ref[...].astype(F32)) * ya_ref[...].astype(F32)
         + _sigmoid(gb_ref[...].astype(F32)) * _dot(ob_ref[...], wmo_ref[...]))
    mix = _dot(y.astype(BF16), wout_ref[...])
    x1 = _layer_norm(alpha * x_ref[...] + (1.0 + gt1) * mix, g_ref[...], b_ref[...])
    x1_ref[...] = x1
    h2 = x1 * (1.0 + sc2) + sh2
    h2_ref[...] = _pack_bf16_pair(h2)
    lg_s[...] = _dot_nt(wr_ref[...], h2.astype(BF16))


def _merge_route(alpha, x2, y_a, o_b, gate_a, gate_b, mod6, w_mla_o, w_out, ln_g, ln_b, wr_t, b_router, S, part):
    N, D = x2.shape
    E = wr_t.shape[0]
    tm = min(S, 512)
    per_b = S // tm
    n = N // tm // MOE_PARTS
    Np = n * tm
    base = part * n
    cur = lambda i: jnp.minimum(i, n - 1)
    prev = lambda i: jnp.maximum(i - 1, 0)
    tok_in = lambda w: pl.BlockSpec((tm, w), lambda i: (base + cur(i), 0))
    tok = lambda w: pl.BlockSpec((tm, w), lambda i: (cur(i), 0))
    full = lambda a: pl.BlockSpec(a.shape, lambda i: (0,) * a.ndim)
    lane_blk = pl.BlockSpec((TOP_K, tm), lambda i: (0, prev(i)))
    br = b_router.reshape(E, 1).astype(F32)
    return pl.pallas_call(
        functools.partial(_merge_kernel, alpha),
        out_shape=(jax.ShapeDtypeStruct((Np, D), F32), jax.ShapeDtypeStruct((Np, D // 2), I32),
                   jax.ShapeDtypeStruct((TOP_K, Np), I32), jax.ShapeDtypeStruct((TOP_K, Np), I32),
                   jax.ShapeDtypeStruct((TOP_K, Np), I32), jax.ShapeDtypeStruct((E, LANES), I32)),
        grid=(n + 1,),
        in_specs=[tok_in(D), tok_in(D), tok_in(D), tok_in(D), tok_in(D),
                  pl.BlockSpec((1, 6, D), lambda i: ((base + cur(i)) // per_b, 0, 0)),
                  full(w_mla_o), full(w_out), full(ln_g), full(ln_b), full(wr_t), full(br)],
        out_specs=(tok(D), tok(D // 2), lane_blk, lane_blk, lane_blk,
                   pl.BlockSpec((E, LANES), lambda i: (0, 0))),
        scratch_shapes=[pltpu.VMEM((E, tm), F32), pltpu.VMEM((E, 1), F32)],
        compiler_params=_cparams("arbitrary"),
        name="merge_ln1_route",
    )(x2, y_a, o_b, gate_a, gate_b, mod6, w_mla_o, w_out, ln_g, ln_b, wr_t, br)


def _first_argmax(v, io, n):
    m = jnp.max(v, axis=0, keepdims=True)
    idx = jnp.min(jnp.where(v == m, io, n), axis=0, keepdims=True)
    return m, idx


def _route_math(lg, br, carry):
    E, T = lg.shape
    gsz = E // N_GROUPS
    neg = -jnp.inf
    s = _sigmoid(lg)
    biased = s + br
    eio = lax.broadcasted_iota(I32, (E, T), 0)
    gio = lax.broadcasted_iota(I32, (gsz, T), 0)

    gs = []
    for g in range(N_GROUPS):
        blk = biased[g * gsz:(g + 1) * gsz]
        m1, i1 = _first_argmax(blk, gio, gsz)
        m2 = jnp.max(jnp.where(gio == i1, neg, blk), axis=0, keepdims=True)
        gs.append(m1 + m2)
    cur = jnp.concatenate(gs, axis=0)
    nio = lax.broadcasted_iota(I32, (N_GROUPS, T), 0)
    gsel = jnp.zeros((N_GROUPS, T), F32)
    for _ in range(TOPK_GROUPS):
        _, gi = _first_argmax(cur, nio, N_GROUPS)
        hit = nio == gi
        gsel = jnp.where(hit, 1.0, gsel)
        cur = jnp.where(hit, neg, cur)
    emask = jnp.concatenate([jnp.broadcast_to(gsel[g:g + 1], (gsz, T)) for g in range(N_GROUPS)], axis=0) > 0.0

    cur = jnp.where(emask, biased, neg)
    idxs, ws = [], []
    sel = jnp.zeros((E, T), F32)
    for _ in range(TOP_K):
        _, ei = _first_argmax(cur, eio, E)
        hit = eio == ei
        idxs.append(ei)
        ws.append(jnp.sum(jnp.where(hit, s, 0.0), axis=0, keepdims=True))
        sel = jnp.where(hit, 1.0, sel)
        cur = jnp.where(hit, neg, cur)
    w = jnp.concatenate(ws, axis=0)
    w = w / jnp.sum(w, axis=0, keepdims=True) * ROUTED_SCALE
    wbits = lax.bitcast_convert_type(w.astype(BF16).astype(F32), I32)
    wword = jnp.bitwise_or(lax.shift_right_logical(wbits, jnp.int32(16)), jnp.bitwise_and(wbits, jnp.int32(-65536)))

    r = lax.broadcasted_iota(I32, (T, T), 0)
    c = lax.broadcasted_iota(I32, (T, T), 1)
    before = (r < c).astype(BF16)
    cnt = _dot(sel.astype(BF16), before) + carry
    ranks = [jnp.sum(jnp.where(eio == idxs[k], cnt, 0.0), axis=0, keepdims=True) for k in range(TOP_K)]
    total = cnt[:, T - 1:T] + sel[:, T - 1:T]
    return jnp.concatenate(idxs, axis=0), wword, jnp.concatenate(ranks, axis=0).astype(I32), total


def _dest_kernel(idx_ref, rank_ref, ps_ref, dest_ref):
    K, T = idx_ref.shape
    E = ps_ref.shape[0]
    eio = lax.broadcasted_iota(I32, (E, T), 0)
    ps = ps_ref[...]
    rows = [jnp.sum(jnp.where(eio == idx_ref[k:k + 1, :], ps, 0), axis=0, keepdims=True) for k in range(K)]
    dest_ref[...] = jnp.concatenate(rows, axis=0) + rank_ref[...]


def _dest_slots(idx, rank, pad_start):
    K, N = idx.shape
    E = pad_start.shape[0]
    T = min(N, 512)
    spec = pl.BlockSpec((K, T), lambda i: (0, i))
    return pl.pallas_call(
        _dest_kernel,
        out_shape=jax.ShapeDtypeStruct((K, N), I32),
        grid=(N // T,),
        in_specs=[spec, spec, pl.BlockSpec((E, 1), lambda i: (0, 0))],
        out_specs=spec,
        compiler_params=_cparams("arbitrary"),
        name="dest_slots",
    )(idx, rank, pad_start.reshape(E, 1))


SC_ROWS = 64


def _sc_workers():
    info = plsc.get_sparse_core_info()
    return info.num_cores, info.num_subcores


def _dispatch_rows(h2p, dest, n_slots):
    N, W = h2p.shape
    K = dest.shape[0]
    R = SC_ROWS
    nc, ns = _sc_workers()
    per_w = N // R // (nc * ns)
    mesh = plsc.VectorSubcoreMesh(core_axis_name="c", subcore_axis_name="s")

    @functools.partial(
        pl.kernel, mesh=mesh,
        out_type=jax.ShapeDtypeStruct((n_slots, W), I32),
        scratch_types=[pltpu.VMEM((K, R), I32), pltpu.VMEM((K, R), I32),
                       pltpu.VMEM((R, W), I32), pltpu.VMEM((R, W), I32),
                       pltpu.SemaphoreType.DMA, pltpu.SemaphoreType.DMA, pltpu.SemaphoreType.DMA],
    )
    def k(h_hbm, d_hbm, xs_hbm, idx0, idx1, rows0, rows1, lsem0, lsem1, ssem):
        idx, rows, lsem = (idx0, idx1), (rows0, rows1), (lsem0, lsem1)
        base = (lax.axis_index("s") * nc + lax.axis_index("c")) * per_w

        def loads(ch, b):
            tok = pl.ds(pl.multiple_of(ch * R, 8), R)
            return [pltpu.make_async_copy(d_hbm.at[kk, tok], idx[b].at[kk], lsem[b]) for kk in range(K)] + [
                pltpu.make_async_copy(h_hbm.at[tok], rows[b], lsem[b])]

        for cp in loads(base, 0):
            cp.start()
        for cp in loads(base, 0):
            cp.wait()

        @pl.loop(0, per_w, step=2)
        def _(j):
            for b in range(2):
                ch = base + j + b
                more = j + b + 1 < per_w

                @pl.when(more)
                def _():
                    for cp in loads(ch + 1, 1 - b):
                        cp.start()

                scatters = [pltpu.make_async_copy(rows[b], xs_hbm.at[idx[b].at[kk]], ssem) for kk in range(K)]
                for cp in scatters:
                    cp.start()
                for cp in scatters:
                    cp.wait()

                @pl.when(more)
                def _():
                    for cp in loads(ch + 1, 1 - b):
                        cp.wait()

    return k(h2p, dest)


COMBINE_ROWS = 8


def _combine_sum(ys, dest, ww, tok0, n_tok):
    W = ys.shape[1]
    K = dest.shape[0]
    R = COMBINE_ROWS
    L = 16
    nc, ns = _sc_workers()
    per_w = n_tok // R // (nc * ns)
    T = per_w * R
    mesh = plsc.VectorSubcoreMesh(core_axis_name="c", subcore_axis_name="s")

    @functools.partial(
        pl.kernel, mesh=mesh, compiler_params=pltpu.CompilerParams(needs_layout_passes=False),
        out_type=jax.ShapeDtypeStruct((n_tok, W), I32),
        scratch_types=[pltpu.VMEM((K * T,), I32), pltpu.VMEM((K * T,), I32),
                       pltpu.VMEM((K, R, W), I32), pltpu.VMEM((K, R, W), I32),
                       pltpu.VMEM((R, W), I32), pltpu.VMEM((R, W), I32),
                       pltpu.SemaphoreType.DMA, pltpu.SemaphoreType.DMA,
                       pltpu.SemaphoreType.DMA, pltpu.SemaphoreType.DMA],
    )
    def k(ys_hbm, d_hbm, w_hbm, o_hbm, idx_all, w_all, buf0, buf1, out0, out1, gsem0, gsem1, osem0, osem1):
        buf, outv, gsem, osem = (buf0, buf1), (out0, out1), (gsem0, gsem1), (osem0, osem1)
        wid = lax.axis_index("s") * nc + lax.axis_index("c")
        t0 = pl.multiple_of(tok0 + wid * T, 8)
        for kk in range(K):
            pltpu.sync_copy(d_hbm.at[kk, pl.ds(t0, T)], idx_all.at[pl.ds(kk * T, T)])
            pltpu.sync_copy(w_hbm.at[kk, pl.ds(t0, T)], w_all.at[pl.ds(kk * T, T)])

        def fetch(j, b):
            return [pltpu.make_async_copy(
                ys_hbm.at[idx_all.at[pl.ds(pl.multiple_of(kk * T + j * R, 8), R)]], buf[b].at[kk], gsem[b])
                for kk in range(K)]

        def store(j, b):
            return pltpu.make_async_copy(outv[b], o_hbm.at[pl.ds(pl.multiple_of(wid * T + j * R, 8), R)], osem[b])

        for cp in fetch(0, 0):
            cp.start()

        @pl.loop(0, per_w, step=2)
        def _(j0):
            for b in range(2):
                j = j0 + b

                @pl.when(j + 1 < per_w)
                def _():
                    for cp in fetch(j + 1, 1 - b):
                        cp.start()

                for cp in fetch(j, b):
                    cp.wait()

                @pl.when(j >= 2)
                def _():
                    store(j - 2, b).wait()

                @pl.loop(0, R)
                def _(r):
                    ws = [plsc.bitcast(plsc.load_gather(w_all, [jnp.full((L,), kk * T + j * R + r, I32)]), BF16)
                          for kk in range(K)]

                    @pl.loop(0, W // L)
                    def _(q):
                        sl = pl.ds(q * L, L)
                        terms = [ws[kk] * plsc.bitcast(buf[b][kk, r, sl], BF16) for kk in range(K)]
                        while len(terms) > 1:
                            terms = [terms[i] + terms[i + 1] for i in range(0, len(terms), 2)]
                        outv[b][r, sl] = plsc.bitcast(terms[0], I32)

                store(j, b).start()

        store(per_w - 2, 0).wait()
        store(per_w - 1, 1).wait()

    return k(ys, dest, ww)


def _expert_kernel(b0_ref, nb_ref, cnt_ref, xs_hbm, w1_ref, w3_ref, w2_ref, after_hbm, ys_hbm,
                   w13_s, w2_s, xbuf, ybuf, sem_in, sem_out):
    e = pl.program_id(0)
    n_exp = pl.num_programs(0)
    F = w1_ref.shape[2]
    bm = xbuf.shape[1]
    half = xbuf.shape[2]
    nb = nb_ref[e]
    b0 = b0_ref[e]
    total = b0_ref[n_exp - 1] + nb_ref[n_exp - 1]

    nbuf = xbuf.shape[0]

    def in_copy(g):
        slot = g % nbuf
        return pltpu.make_async_copy(xs_hbm.at[pl.ds(g * bm, bm)], xbuf.at[slot], sem_in.at[slot])

    def out_copy(g):
        slot = g % nbuf
        return pltpu.make_async_copy(ybuf.at[slot], ys_hbm.at[pl.ds(g * bm, bm)], sem_out.at[slot])

    @pl.when(e == 0)
    def _():
        for g0 in range(nbuf - 1):
            @pl.when(g0 < total)
            def _():
                in_copy(g0).start()

    @pl.when(nb > 0)
    def _():
        w13_s[:, :F] = w1_ref[0].astype(BF16)
        w13_s[:, F:] = w3_ref[0].astype(BF16)
        w2_s[...] = w2_ref[0].astype(BF16)

    def block(j, carry):
        g = b0 + j
        slot = g % nbuf
        in_copy(g).wait()

        @pl.when(g + nbuf - 1 < total)
        def _():
            in_copy(g + nbuf - 1).start()

        @pl.when(g >= nbuf)
        def _():
            out_copy(g - nbuf).wait()

        def swiglu(rows):
            lo, hi = _unpack_bf16_pair(xbuf[slot, :rows, :])
            ab = _dot(lo.astype(BF16), w13_s[:half, :]) + _dot(hi.astype(BF16), w13_s[half:, :])
            hid = (_silu(ab[:, :F]) * ab[:, F:]).astype(BF16)
            ybuf[slot, :rows, :] = _pack_bf16_pair(_dot(hid, w2_s[...]))

        rows_left = cnt_ref[e] - j * bm

        @pl.when(rows_left > bm // 2)
        def _():
            swiglu(bm)

        @pl.when(rows_left <= bm // 2)
        def _():
            swiglu(bm // 2)
            ybuf[slot, bm // 2:, :] = jnp.zeros((bm - bm // 2, half), I32)

        out_copy(g).start()
        return carry

    lax.fori_loop(0, nb, block, 0)

    @pl.when(e == n_exp - 1)
    def _():
        for back in range(nbuf, 0, -1):
            @pl.when(total >= back)
            def _():
                out_copy(total - back).wait()


def _experts(xs, blk0, nblk, counts, w_e1, w_e3, w_e2, after):
    n_slots, W = xs.shape
    E, D, F = w_e1.shape
    bm = EXPERT_BLOCK
    grid_spec = pltpu.PrefetchScalarGridSpec(
        num_scalar_prefetch=3,
        grid=(E,),
        in_specs=[
            pl.BlockSpec(memory_space=pl.ANY),
            pl.BlockSpec((1, D, F), lambda e, b0, nb, cnt: (e, 0, 0)),
            pl.BlockSpec((1, D, F), lambda e, b0, nb, cnt: (e, 0, 0)),
            pl.BlockSpec((1, F, D), lambda e, b0, nb, cnt: (e, 0, 0)),
            pl.BlockSpec(memory_space=pl.ANY),
        ],
        out_specs=pl.BlockSpec(memory_space=pl.ANY),
        scratch_shapes=[pltpu.VMEM((D, 2 * F), BF16), pltpu.VMEM((F, D), BF16),
                        pltpu.VMEM((EXPERT_RING, bm, W), I32), pltpu.VMEM((EXPERT_RING, bm, W), I32),
                        pltpu.SemaphoreType.DMA((EXPERT_RING,)), pltpu.SemaphoreType.DMA((EXPERT_RING,))],
    )
    return pl.pallas_call(
        _expert_kernel,
        out_shape=jax.ShapeDtypeStruct((n_slots, W), I32),
        grid_spec=grid_spec,
        compiler_params=_cparams("arbitrary"),
        name="experts",
    )(blk0, nblk, counts, xs, w_e1, w_e3, w_e2, after)


def _shared_kernel(h2_ref, ws13_ref, ws2_ref, o_ref):
    F = ws2_ref.shape[0]
    half = h2_ref.shape[1]
    lo, hi = _unpack_bf16_pair(h2_ref[...])
    ab = _dot(lo.astype(BF16), ws13_ref[:half, :]) + _dot(hi.astype(BF16), ws13_ref[half:, :])
    hid = (_silu(ab[:, :F]) * ab[:, F:]).astype(BF16)
    o_ref[...] = _dot(hid, ws2_ref[...]).astype(o_ref.dtype)


def _shared_expert(h2p, ws13, ws2):
    N, W = h2p.shape
    D = ws2.shape[1]
    tm = min(N, 512)
    full = lambda a: pl.BlockSpec(a.shape, lambda i: (0,) * a.ndim)
    return pl.pallas_call(
        _shared_kernel,
        out_shape=jax.ShapeDtypeStruct((N, D), BF16),
        grid=(N // tm,),
        in_specs=[pl.BlockSpec((tm, W), lambda i: (i, 0)), full(ws13), full(ws2)],
        out_specs=pl.BlockSpec((tm, D), lambda i: (i, 0)),
        compiler_params=_cparams("arbitrary"),
        name="shared_expert",
    )(h2p, ws13, ws2)


def _final_kernel(alpha, ms_ref, sh_ref, x1_ref, mod_ref, g_ref, b_ref, *rest):
    o_ref = rest[-1]
    gt2 = mod_ref[0, 5:6, :]
    lo, hi = _unpack_bf16_pair(ms_ref[...])
    ffn = jnp.concatenate([lo, hi], axis=1) + sh_ref[...].astype(F32)
    o_ref[...] = _layer_norm(alpha * x1_ref[...] + (1.0 + gt2) * ffn, g_ref[...], b_ref[...])


def _final_part(alpha, ms, sh, x1, mod6, ln_g, ln_b, S, tok0, row0, n_total, prev):
    D = x1.shape[1]
    N = n_total
    n_part, W = ms.shape
    tm = min(S, 512, n_part)
    assert n_part % tm == 0 and S % tm == 0 and tok0 % tm == 0 and row0 % tm == 0
    per_b = S // tm
    steps = n_part // tm
    off = tok0 // tm
    goff = row0 // tm
    tok = lambda n: pl.BlockSpec((tm, n), lambda i: (off + i, 0))
    full = lambda a: pl.BlockSpec(a.shape, lambda i: (0,) * a.ndim)
    in_specs = [pl.BlockSpec((tm, W), lambda i: (i, 0)), tok(D), tok(D),
                pl.BlockSpec((1, 6, D), lambda i: ((goff + i) // per_b, 0, 0)), full(ln_g), full(ln_b)]
    args = [ms, sh, x1, mod6, ln_g, ln_b]
    aliases = {}
    if prev is not None:
        in_specs.append(pl.BlockSpec(memory_space=pl.ANY))
        args.append(prev)
        aliases = {len(args) - 1: 0}
    return pl.pallas_call(
        functools.partial(_final_kernel, alpha),
        out_shape=jax.ShapeDtypeStruct((N, D), F32),
        grid=(steps,),
        in_specs=in_specs,
        out_specs=pl.BlockSpec((tm, D), lambda i: (goff + i, 0)),
        input_output_aliases=aliases,
        compiler_params=_cparams("arbitrary"),
        name="combine_ln2",
    )(*args)


def _moe(alpha, h2p, idx, ww, rank, cnt, x1, mod6, w_e1, w_e3, w_e2, ws13, ws2, ln_g, ln_b, S, row0, n_total, out):
    N = x1.shape[0]
    E = w_e1.shape[0]
    bm = EXPERT_BLOCK
    counts = cnt[:, 0]
    padded = (counts + bm - 1) // bm * bm
    pad_end = jnp.cumsum(padded)
    pad_start = pad_end - padded
    n_slots = N * TOP_K + E * bm
    dest = _dest_slots(idx, rank, pad_start.astype(I32))
    xs = _dispatch_rows(h2p, dest, n_slots)
    sh = _shared_expert(h2p, ws13, ws2)
    ys = _experts(xs, (pad_start // bm).astype(I32), (padded // bm).astype(I32), counts.astype(I32),
                  w_e1, w_e3, w_e2, sh)
    n_part = N // COMBINE_PARTS
    for p in range(COMBINE_PARTS):
        tok0 = p * n_part
        ms = _combine_sum(ys, dest, ww, tok0, n_part)
        out = _final_part(alpha, ms, sh, x1, mod6, ln_g, ln_b, S, tok0, row0 + tok0, n_total, out)
    return out


def kernel(x, c, positions, w_ada, b_ada, w_in, w_gla_a2, b_gla_a, g_gla_norm, w_gla_o, g_cq, w_uq, g_ckv, w_ukv, w_mla_o, w_out, ln1_g, ln1_b, w_router, b_router, w_e1, w_e3, w_e2, w_s1, w_s3, w_s2, ln2_g, ln2_b):
    B, S, D = x.shape
    N = B * S
    depth = w_ada.shape[0]
    alpha = (2.0 * depth) ** 0.25
    row = lambda a: a.reshape(1, -1)
    cc, ss = _rope_tables(positions)
    x2 = x.reshape(N, D)
    for l in range(depth):
        mod6 = _modulation(c, w_ada[l], b_ada[l]).reshape(B, 6, D)
        qk, gv, gr, cq, ckv, gate_a, gate_b, tail = _in_projection(x2, mod6, _prep_w_in(w_in[l], D), S)
        wa2_p = jnp.concatenate(
            [w_gla_a2[l], jnp.zeros((LANES - GLA_GATE_RANK, w_gla_a2.shape[2]), F32)], axis=0).astype(BF16)
        y_a = _gla(qk, gv, gr, tail, wa2_p, row(b_gla_a[l]), row(g_gla_norm[l]), w_gla_o[l].astype(BF16), B, S)
        qc, kc, vv = _mla_prep(cq, ckv, tail, cc, ss, row(g_cq[l]), _prep_w_uq(w_uq[l]),
                               row(g_ckv[l]), _prep_w_ukv(w_ukv[l]))
        o_b = _mla_attention(qc, kc, vv, B, S)
        ws13 = jnp.concatenate([w_s1[l], w_s3[l]], axis=1).astype(BF16)
        ws2 = w_s2[l].astype(BF16)
        routed = [_merge_route(alpha, x2, y_a, o_b, gate_a, gate_b, mod6, w_mla_o[l].astype(BF16),
                               w_out[l].astype(BF16), row(ln1_g[l]), row(ln1_b[l]), w_router[l].T.astype(BF16),
                               b_router[l], S, p) for p in range(MOE_PARTS)]
        out = None
        for p, (x1, h2p, idx, wt, rank, cnt) in enumerate(routed):
            out = _moe(alpha, h2p, idx, wt, rank, cnt, x1, mod6, w_e1[l], w_e3[l], w_e2[l], ws13, ws2,
                       row(ln2_g[l]), row(ln2_b[l]), S, p * (N // MOE_PARTS), N, out)
        x2 = out
    return x2.reshape(B, S, D)
```

```python
import functools

import jax
import jax.numpy as jnp
from jax import lax
from jax.experimental import pallas as pl
from jax.experimental.pallas import tpu as pltpu
from jax.experimental.pallas import tpu_sc as plsc

CHUNK = 64
GLA_HEADS = 4
GLA_DK = 128
GLA_DV = 256
GLA_GATE_RANK = 16
GLA_GATE_TAU = 16.0
MLA_HEADS = 8
MLA_Q_RANK = 768
MLA_KV_RANK = 256
MLA_NOPE = 128
MLA_ROPE = 64
MLA_V = 128
ROPE_THETA = 10000.0
N_EXPERTS = 256
TOP_K = 8
N_GROUPS = 8
TOPK_GROUPS = 4
D_EXPERT = 256
ROUTED_SCALE = 2.5
LN_EPS = 1e-5
RMS_EPS = 1e-6
LOG2E = 1.4426950408889634

LANES = 128
VMEM_LIMIT = 56 * 1024 * 1024
EXPERT_BLOCK = 512
EXPERT_RING = 3
MOE_FIRST_EIGHTHS = 5
COMBINE_TOKENS = 8192

F32 = jnp.float32
BF16 = jnp.bfloat16
I32 = jnp.int32


def _cparams(*sem):
    return pltpu.CompilerParams(dimension_semantics=sem, vmem_limit_bytes=VMEM_LIMIT)


def _sigmoid(x):
    return 1.0 / (1.0 + jnp.exp(-x))


def _silu(x):
    return x * _sigmoid(x)


def _dot(a, b):
    return jnp.dot(a, b, preferred_element_type=F32)


def _dot_nt(a, b):
    return lax.dot_general(a, b, (((1,), (1,)), ((), ())), preferred_element_type=F32)


def _dot_tn(a, b):
    return lax.dot_general(a, b, (((0,), (0,)), ((), ())), preferred_element_type=F32)


def _pack_bf16_pair(x):
    w = x.shape[1] // 2
    u = lax.bitcast_convert_type(x.astype(BF16).astype(F32), I32)
    lo = lax.shift_right_logical(u[:, :w], jnp.int32(16))
    hi = jnp.bitwise_and(u[:, w:], jnp.int32(-65536))
    return jnp.bitwise_or(lo, hi)


def _unpack_bf16_pair(p):
    lo = lax.bitcast_convert_type(lax.shift_left(p, jnp.int32(16)), F32)
    hi = lax.bitcast_convert_type(jnp.bitwise_and(p, jnp.int32(-65536)), F32)
    return lo, hi


def _mod_kernel(c_ref, w_ref, b_ref, o_ref):
    cond = _silu(c_ref[...]).astype(BF16)
    o_ref[...] = _dot(cond, w_ref[...].astype(BF16)) + b_ref[...]


def _modulation(c, w_ada, b_ada):
    B, D = c.shape
    W = w_ada.shape[1]
    tn = D
    return pl.pallas_call(
        _mod_kernel,
        out_shape=jax.ShapeDtypeStruct((B, W), F32),
        grid=(W // tn,),
        in_specs=[
            pl.BlockSpec((B, D), lambda j: (0, 0)),
            pl.BlockSpec((D, tn), lambda j: (0, j)),
            pl.BlockSpec((1, tn), lambda j: (0, j)),
        ],
        out_specs=pl.BlockSpec((B, tn), lambda j: (0, j)),
        compiler_params=_cparams("arbitrary"),
        name="adaln_mod",
    )(c, w_ada, b_ada.reshape(1, W))


def _rope_kernel(pos_ref, f_ref, ph_ref, cc_ref, ss_ref):
    ang = pos_ref[...] * f_ref[...]
    cc_ref[...] = jnp.cos(ang)
    ss_ref[...] = jnp.sin(ang) * ph_ref[...]


def _rope_tables(positions):
    B, S = positions.shape
    N = B * S
    half = MLA_ROPE // 2
    inv_freq = ROPE_THETA ** (-jnp.arange(half, dtype=F32) * (2.0 / MLA_ROPE))
    f4 = jnp.tile(inv_freq, LANES // half).reshape(1, LANES)
    sign = jnp.tile(jnp.concatenate([-jnp.ones((half,), F32), jnp.ones((half,), F32)]), LANES // MLA_ROPE)
    pos = jnp.broadcast_to(positions.reshape(N, 1).astype(F32), (N, LANES))
    tm = min(N, 1024)
    spec = pl.BlockSpec((tm, LANES), lambda i: (i, 0))
    vec = pl.BlockSpec((1, LANES), lambda i: (0, 0))
    return pl.pallas_call(
        _rope_kernel,
        out_shape=(jax.ShapeDtypeStruct((N, LANES), F32), jax.ShapeDtypeStruct((N, LANES), F32)),
        grid=(N // tm,),
        in_specs=[spec, vec, vec],
        out_specs=(spec, spec),
        compiler_params=_cparams("arbitrary"),
        name="rope_tables",
    )(pos, f4, sign.reshape(1, LANES))


QK_W = 2 * GLA_HEADS * GLA_DK
GV_W = GLA_HEADS * GLA_DV
TAIL_W = 3 * LANES
IN_SEG = (QK_W, GV_W, GV_W, MLA_Q_RANK, MLA_KV_RANK, 1024, 1024, TAIL_W)


def _prep_w_in(w_in, D):
    s = [GLA_HEADS * GLA_DK, GLA_HEADS * GLA_DK, GV_W, GV_W, GLA_GATE_RANK, MLA_Q_RANK, MLA_KV_RANK, MLA_ROPE, D, D]
    offs = [0]
    for n in s:
        offs.append(offs[-1] + n)
    gq, gk, gv, gr, ga, cq, ckv, kr, gate_a, gate_b = [w_in[:, offs[i]:offs[i + 1]] for i in range(10)]
    half = MLA_ROPE // 2
    kr_sw = jnp.concatenate([kr[:, half:], kr[:, :half]], axis=1)
    pad = jnp.zeros((w_in.shape[0], LANES - GLA_GATE_RANK), w_in.dtype)
    return jnp.concatenate([gq, gk, gv, gr, cq, ckv, gate_a, gate_b, kr, kr, kr_sw, kr_sw, ga, pad], axis=1).astype(BF16)


def _inproj_kernel(x_ref, mod_ref, w_ref, *out_refs):
    sh1 = mod_ref[0, 0:1, :]
    sc1 = mod_ref[0, 1:2, :]
    h = (x_ref[...] * (1.0 + sc1) + sh1).astype(BF16)
    off = 0
    for ref in out_refs:
        n = ref.shape[-1]
        ref[...] = _dot(h, w_ref[:, off:off + n]).astype(ref.dtype)
        off += n


def _in_projection(x2, mod6, w_in_p, S):
    N, D = x2.shape
    W = w_in_p.shape[1]
    tm = min(S, 512)
    per_b = S // tm
    return pl.pallas_call(
        _inproj_kernel,
        out_shape=tuple(jax.ShapeDtypeStruct((N, n), BF16) for n in IN_SEG),
        grid=(N // tm,),
        in_specs=[
            pl.BlockSpec((tm, D), lambda i: (i, 0)),
            pl.BlockSpec((1, 6, D), lambda i: (i // per_b, 0, 0)),
            pl.BlockSpec((D, W), lambda i: (0, 0), pipeline_mode=pl.Buffered(1)),
        ],
        out_specs=tuple(pl.BlockSpec((tm, n), lambda i: (i, 0)) for n in IN_SEG),
        compiler_params=_cparams("arbitrary"),
        name="in_proj",
    )(x2, mod6, w_in_p)


def _gla_kernel(qk_ref, v_ref, gr_ref, tail_ref, wa2_ref, ba_ref, gn_ref, wo_ref, y_ref,
                st_ref, kd_ref, dec_ref, sall_ref, o_ref):
    t = pl.program_id(1)

    @pl.when(t == 0)
    def _():
        st_ref[...] = jnp.zeros_like(st_ref)

    ts = qk_ref.shape[0]
    nch = ts // CHUNK
    HK = GLA_HEADS * GLA_DK
    r = lax.broadcasted_iota(I32, (CHUNK, CHUNK), 0)
    c = lax.broadcasted_iota(I32, (CHUNK, CHUNK), 1)
    tri = (r >= c).astype(BF16)
    qscale = GLA_DK ** -0.5

    z = _dot(tail_ref[:, 2 * LANES:3 * LANES], wa2_ref[...]) + ba_ref[...]
    log_a = (jnp.minimum(z, 0.0) - jnp.log(1.0 + jnp.exp(-jnp.abs(z)))) * (1.0 / GLA_GATE_TAU)
    la_hi = log_a.astype(BF16)
    la_lo = (log_a - la_hi.astype(F32)).astype(BF16)
    for n in range(nch):
        rows = slice(n * CHUNK, (n + 1) * CHUNK)
        G = _dot(tri, la_hi[rows]) + _dot(tri, la_lo[rows])
        g_end = G[CHUNK - 1:CHUNK, :]
        kd_ref[rows, :] = (qk_ref[rows, HK:2 * HK].astype(F32) * jnp.exp(g_end - G)).astype(BF16)
        dec_ref[n:n + 1, :] = jnp.exp(g_end)

    for h in range(GLA_HEADS):
        ks = slice(h * GLA_DK, (h + 1) * GLA_DK)
        vs = slice(h * GLA_DV, (h + 1) * GLA_DV)
        st = st_ref[h]
        for n in range(nch):
            rows = slice(n * CHUNK, (n + 1) * CHUNK)
            st = st * dec_ref[n:n + 1, ks] + _dot_tn(v_ref[rows, vs], kd_ref[rows, ks])
            sall_ref[n * GLA_HEADS + h] = st.astype(BF16)
        st_ref[h] = st

    for n in range(nch):
        rows = slice(n * CHUNK, (n + 1) * CHUNK)
        for h in range(GLA_HEADS):
            ks = slice(h * GLA_DK, (h + 1) * GLA_DK)
            vs = slice(h * GLA_DV, (h + 1) * GLA_DV)
            qh = (qk_ref[rows, ks].astype(F32) * qscale).astype(BF16)
            o = _dot_nt(qh, sall_ref[n * GLA_HEADS + h])
            o = o * lax.rsqrt(jnp.mean(o * o, axis=-1, keepdims=True) + RMS_EPS) * gn_ref[...]
            o_ref[rows, vs] = (o * _silu(gr_ref[rows, vs].astype(F32))).astype(BF16)
    y_ref[...] = _dot(o_ref[...], wo_ref[...]).astype(y_ref.dtype)


def _gla(qk, gv, gr, tail, wa2_p, b_a, g_norm, w_o, B, S):
    N = B * S
    D = w_o.shape[1]
    ts = min(S, 512)
    per_b = S // ts
    HK = GLA_HEADS * GLA_DK
    tok = lambda n: pl.BlockSpec((ts, n), lambda b, t: (b * per_b + t, 0))
    full = lambda a: pl.BlockSpec(a.shape, lambda b, t: (0,) * a.ndim)
    return pl.pallas_call(
        _gla_kernel,
        out_shape=jax.ShapeDtypeStruct((N, D), BF16),
        grid=(B, per_b),
        in_specs=[tok(QK_W), tok(GV_W), tok(GV_W), tok(TAIL_W), full(wa2_p), full(b_a), full(g_norm), full(w_o)],
        out_specs=tok(D),
        scratch_shapes=[pltpu.VMEM((GLA_HEADS, GLA_DV, GLA_DK), F32),
                        pltpu.VMEM((ts, GLA_HEADS * GLA_DK), BF16),
                        pltpu.VMEM((ts // CHUNK, GLA_HEADS * GLA_DK), F32),
                        pltpu.VMEM((ts // CHUNK * GLA_HEADS, GLA_DV, GLA_DK), BF16),
                        pltpu.VMEM((ts, GV_W), BF16)],
        compiler_params=_cparams("arbitrary", "arbitrary"),
        name="gla",
    )(qk, gv, gr, tail, wa2_p, b_a, g_norm, w_o)


HQ = MLA_HEADS * 2 * LANES


def _prep_w_uq(w_uq):
    dh = MLA_NOPE + MLA_ROPE
    half = MLA_ROPE // 2
    nope = [w_uq[:, h * dh:h * dh + MLA_NOPE] for h in range(MLA_HEADS)]
    rope = [w_uq[:, h * dh + MLA_NOPE:(h + 1) * dh] for h in range(MLA_HEADS)]
    rope_sw = [jnp.concatenate([r[:, half:], r[:, :half]], axis=1) for r in rope]
    return jnp.concatenate(nope + rope + rope_sw, axis=1).astype(BF16)


def _prep_w_ukv(w_ukv):
    dh = MLA_NOPE + MLA_V
    kn = [w_ukv[:, h * dh:h * dh + MLA_NOPE] for h in range(MLA_HEADS)]
    vv = [w_ukv[:, h * dh + MLA_NOPE:(h + 1) * dh] for h in range(MLA_HEADS)]
    return jnp.concatenate(kn + vv, axis=1).astype(BF16)


def _rms(x, g):
    return x * lax.rsqrt(jnp.mean(x * x, axis=-1, keepdims=True) + RMS_EPS) * g


def _mla_prep_kernel(cq_ref, ckv_ref, tail_ref, cc_ref, ss_ref, gq_ref, wq_ref, gkv_ref, wkv_ref, q_ref, k_ref, v_ref):
    tm = cq_ref.shape[0]
    NP = MLA_HEADS * MLA_NOPE
    RP = MLA_HEADS * MLA_ROPE
    scale = (MLA_NOPE + MLA_ROPE) ** -0.5 * LOG2E
    cc = cc_ref[...]
    ss = ss_ref[...]
    cqn = _rms(cq_ref[...].astype(F32), gq_ref[...]).astype(BF16)
    qf = _dot(cqn, wq_ref[...]) * scale
    ckvn = _rms(ckv_ref[...].astype(F32), gkv_ref[...]).astype(BF16)
    kv = _dot(ckvn, wkv_ref[...])
    krr = (tail_ref[:, 0:LANES].astype(F32) * cc + tail_ref[:, LANES:2 * LANES].astype(F32) * ss).astype(BF16)
    lane = lax.broadcasted_iota(I32, (tm, LANES), 1)
    first = lane < MLA_ROPE
    for j in range(MLA_HEADS // 2):
        a = NP + j * LANES
        rot = qf[:, a:a + LANES] * cc + qf[:, a + RP:a + RP + LANES] * ss
        for h, keep in ((2 * j, first), (2 * j + 1, jnp.logical_not(first))):
            base = h * 2 * LANES
            q_ref[:, base:base + LANES] = qf[:, h * MLA_NOPE:(h + 1) * MLA_NOPE].astype(BF16)
            q_ref[:, base + LANES:base + 2 * LANES] = jnp.where(keep, rot, 0.0).astype(BF16)
            k_ref[:, base:base + LANES] = kv[:, h * MLA_NOPE:(h + 1) * MLA_NOPE].astype(BF16)
            k_ref[:, base + LANES:base + 2 * LANES] = krr
    v_ref[...] = kv[:, NP:].astype(BF16)


def _mla_prep(cq, ckv, tail, cc, ss, g_cq, w_uq_p, g_ckv, w_ukv_p):
    N = cq.shape[0]
    tm = min(N, 512)
    tok = lambda n: pl.BlockSpec((tm, n), lambda i: (i, 0))
    full = lambda a: pl.BlockSpec(a.shape, lambda i: (0,) * a.ndim)
    HV = MLA_HEADS * MLA_V
    return pl.pallas_call(
        _mla_prep_kernel,
        out_shape=(jax.ShapeDtypeStruct((N, HQ), BF16), jax.ShapeDtypeStruct((N, HQ), BF16),
                   jax.ShapeDtypeStruct((N, HV), BF16)),
        grid=(N // tm,),
        in_specs=[tok(MLA_Q_RANK), tok(MLA_KV_RANK), tok(TAIL_W), tok(LANES), tok(LANES),
                  full(g_cq), full(w_uq_p), full(g_ckv), full(w_ukv_p)],
        out_specs=(tok(HQ), tok(HQ), tok(HV)),
        compiler_params=_cparams("arbitrary"),
        name="mla_prep",
    )(cq, ckv, tail, cc, ss, g_cq, w_uq_p, g_ckv, w_ukv_p)


ATTN_TQ = 256


def _mla_attn_kernel(q_ref, k_ref, v_ref, o_ref, v1_ref):
    S = q_ref.shape[0]
    tq = min(S, ATTN_TQ)
    r = lax.broadcasted_iota(I32, (tq, tq), 0) // CHUNK
    c = lax.broadcasted_iota(I32, (tq, tq), 1) // CHUNK
    diag_mask = c <= r
    v1_ref[:, :MLA_V] = v_ref[...]
    v1_ref[:, MLA_V:] = jnp.ones((S, MLA_V), BF16)

    def scores(ii):
        l0 = ii * tq
        q = q_ref[l0:l0 + tq, :]
        sd = jnp.where(diag_mask, _dot_nt(q, k_ref[l0:l0 + tq, :]), -jnp.inf)
        so = _dot_nt(q, k_ref[0:l0, :]) if ii > 0 else None
        return sd, so

    def finish(ii, sd, so):
        l0 = ii * tq
        m = jnp.max(sd, axis=-1, keepdims=True)
        if so is not None:
            m = jnp.maximum(m, jnp.max(so, axis=-1, keepdims=True))
        acc = _dot(jnp.exp2((sd - m).astype(BF16)), v1_ref[l0:l0 + tq, :])
        if so is not None:
            acc = acc + _dot(jnp.exp2((so - m).astype(BF16)), v1_ref[0:l0, :])
        o_ref[l0:l0 + tq, :] = (acc[:, :MLA_V] / acc[:, MLA_V:]).astype(BF16)

    n_tiles = S // tq
    ahead = 2
    pending = [scores(ii) for ii in range(min(ahead, n_tiles))]
    for ii in range(n_tiles):
        if ii + ahead < n_tiles:
            pending.append(scores(ii + ahead))
        finish(ii, *pending.pop(0))


def _mla_attention(qc, kc, vv, B, S):
    N = B * S
    HV = MLA_HEADS * MLA_V
    return pl.pallas_call(
        _mla_attn_kernel,
        out_shape=jax.ShapeDtypeStruct((N, HV), BF16),
        grid=(B, MLA_HEADS),
        in_specs=[
            pl.BlockSpec((S, 2 * LANES), lambda b, h: (b, h)),
            pl.BlockSpec((S, 2 * LANES), lambda b, h: (b, h)),
            pl.BlockSpec((S, MLA_V), lambda b, h: (b, h)),
        ],
        out_specs=pl.BlockSpec((S, MLA_V), lambda b, h: (b, h)),
        scratch_shapes=[pltpu.VMEM((S, 2 * MLA_V), BF16)],
        compiler_params=_cparams("arbitrary", "arbitrary"),
        name="mla_attn",
    )(qc, kc, vv)


def _layer_norm(u, g, b):
    mu = jnp.mean(u, axis=-1, keepdims=True)
    d = u - mu
    var = jnp.mean(d * d, axis=-1, keepdims=True)
    return d * lax.rsqrt(var + LN_EPS) * g + b


def _merge_kernel(alpha, x_ref, ya_ref, ob_ref, ga_ref, gb_ref, mod_ref, wmo_ref, wout_ref, g_ref, b_ref, wr_ref,
                  br_ref, x1_ref, h2_ref, idx_ref, wt_ref, rank_ref, cnt_ref, lg_s, carry_s):
    i = pl.program_id(0)

    @pl.when(i == 0)
    def _():
        lg_s[...] = jnp.zeros_like(lg_s)
        carry_s[...] = jnp.zeros_like(carry_s)

    idx, wt, rank, total = _route_math(lg_s[...], br_ref[...], carry_s[...])
    total = jnp.where(i > 0, total, 0.0)
    idx_ref[...] = idx
    wt_ref[...] = wt
    rank_ref[...] = rank
    carry_s[...] = total
    cnt_ref[...] = jnp.broadcast_to(total, cnt_ref.shape).astype(I32)

    gt1 = mod_ref[0, 2:3, :]
    sh2 = mod_ref[0, 3:4, :]
    sc2 = mod_ref[0, 4:5, :]
    y = (_sigmoid(ga_ref[...].astype(F32)) * ya_ref[...].astype(F32)
         + _sigmoid(gb_ref[...].astype(F32)) * _dot(ob_ref[...], wmo_ref[...]))
    mix = _dot(y.astype(BF16), wout_ref[...])
    x1 = _layer_norm(alpha * x_ref[...] + (1.0 + gt1) * mix, g_ref[...], b_ref[...])
    x1_ref[...] = x1
    h2 = x1 * (1.0 + sc2) + sh2
    h2_ref[...] = _pack_bf16_pair(h2)
    lg_s[...] = _dot_nt(wr_ref[...], h2.astype(BF16))


def _merge_route(alpha, x2, y_a, o_b, gate_a, gate_b, mod6, w_mla_o, w_out, ln_g, ln_b, wr_t, b_router, S, tok0, Np):
    N, D = x2.shape
    E = wr_t.shape[0]
    tm = min(S, 512)
    per_b = S // tm
    assert Np % tm == 0 and tok0 % tm == 0
    n = Np // tm
    base = tok0 // tm
    cur = lambda i: jnp.minimum(i, n - 1)
    prev = lambda i: jnp.maximum(i - 1, 0)
    tok_in = lambda w: pl.BlockSpec((tm, w), lambda i: (base + cur(i), 0))
    tok = lambda w: pl.BlockSpec((tm, w), lambda i: (cur(i), 0))
    full = lambda a: pl.BlockSpec(a.shape, lambda i: (0,) * a.ndim)
    lane_blk = pl.BlockSpec((TOP_K, tm), lambda i: (0, prev(i)))
    br = b_router.reshape(E, 1).astype(F32)
    return pl.pallas_call(
        functools.partial(_merge_kernel, alpha),
        out_shape=(jax.ShapeDtypeStruct((Np, D), F32), jax.ShapeDtypeStruct((Np, D // 2), I32),
                   jax.ShapeDtypeStruct((TOP_K, Np), I32), jax.ShapeDtypeStruct((TOP_K, Np), I32),
                   jax.ShapeDtypeStruct((TOP_K, Np), I32), jax.ShapeDtypeStruct((E, LANES), I32)),
        grid=(n + 1,),
        in_specs=[tok_in(D), tok_in(D), tok_in(D), tok_in(D), tok_in(D),
                  pl.BlockSpec((1, 6, D), lambda i: ((base + cur(i)) // per_b, 0, 0)),
                  full(w_mla_o), full(w_out), full(ln_g), full(ln_b), full(wr_t), full(br)],
        out_specs=(tok(D), tok(D // 2), lane_blk, lane_blk, lane_blk,
                   pl.BlockSpec((E, LANES), lambda i: (0, 0))),
        scratch_shapes=[pltpu.VMEM((E, tm), F32), pltpu.VMEM((E, 1), F32)],
        compiler_params=_cparams("arbitrary"),
        name="merge_ln1_route",
    )(x2, y_a, o_b, gate_a, gate_b, mod6, w_mla_o, w_out, ln_g, ln_b, wr_t, br)


def _first_argmax(v, io, n):
    m = jnp.max(v, axis=0, keepdims=True)
    idx = jnp.min(jnp.where(v == m, io, n), axis=0, keepdims=True)
    return m, idx


def _route_math(lg, br, carry):
    E, T = lg.shape
    gsz = E // N_GROUPS
    neg = -jnp.inf
    s = _sigmoid(lg)
    biased = s + br
    eio = lax.broadcasted_iota(I32, (E, T), 0)
    gio = lax.broadcasted_iota(I32, (gsz, T), 0)

    gs = []
    for g in range(N_GROUPS):
        blk = biased[g * gsz:(g + 1) * gsz]
        m1, i1 = _first_argmax(blk, gio, gsz)
        m2 = jnp.max(jnp.where(gio == i1, neg, blk), axis=0, keepdims=True)
        gs.append(m1 + m2)
    cur = jnp.concatenate(gs, axis=0)
    nio = lax.broadcasted_iota(I32, (N_GROUPS, T), 0)
    gsel = jnp.zeros((N_GROUPS, T), F32)
    for _ in range(TOPK_GROUPS):
        _, gi = _first_argmax(cur, nio, N_GROUPS)
        hit = nio == gi
        gsel = jnp.where(hit, 1.0, gsel)
        cur = jnp.where(hit, neg, cur)
    emask = jnp.concatenate([jnp.broadcast_to(gsel[g:g + 1], (gsz, T)) for g in range(N_GROUPS)], axis=0) > 0.0

    cur = jnp.where(emask, biased, neg)
    idxs, ws = [], []
    sel = jnp.zeros((E, T), F32)
    for _ in range(TOP_K):
        _, ei = _first_argmax(cur, eio, E)
        hit = eio == ei
        idxs.append(ei)
        ws.append(jnp.sum(jnp.where(hit, s, 0.0), axis=0, keepdims=True))
        sel = jnp.where(hit, 1.0, sel)
        cur = jnp.where(hit, neg, cur)
    w = jnp.concatenate(ws, axis=0)
    w = w / jnp.sum(w, axis=0, keepdims=True) * ROUTED_SCALE
    wbits = lax.bitcast_convert_type(w.astype(BF16).astype(F32), I32)
    wword = jnp.bitwise_or(lax.shift_right_logical(wbits, jnp.int32(16)), jnp.bitwise_and(wbits, jnp.int32(-65536)))

    r = lax.broadcasted_iota(I32, (T, T), 0)
    c = lax.broadcasted_iota(I32, (T, T), 1)
    before = (r < c).astype(BF16)
    cnt = _dot(sel.astype(BF16), before) + carry
    ranks = [jnp.sum(jnp.where(eio == idxs[k], cnt, 0.0), axis=0, keepdims=True) for k in range(TOP_K)]
    total = cnt[:, T - 1:T] + sel[:, T - 1:T]
    return jnp.concatenate(idxs, axis=0), wword, jnp.concatenate(ranks, axis=0).astype(I32), total


def _dest_kernel(idx_ref, rank_ref, ps_ref, dest_ref):
    K, T = idx_ref.shape
    E = ps_ref.shape[0]
    eio = lax.broadcasted_iota(I32, (E, T), 0)
    ps = ps_ref[...]
    rows = [jnp.sum(jnp.where(eio == idx_ref[k:k + 1, :], ps, 0), axis=0, keepdims=True) for k in range(K)]
    dest_ref[...] = jnp.concatenate(rows, axis=0) + rank_ref[...]


def _dest_slots(idx, rank, pad_start):
    K, N = idx.shape
    E = pad_start.shape[0]
    T = min(N, 512)
    spec = pl.BlockSpec((K, T), lambda i: (0, i))
    return pl.pallas_call(
        _dest_kernel,
        out_shape=jax.ShapeDtypeStruct((K, N), I32),
        grid=(N // T,),
        in_specs=[spec, spec, pl.BlockSpec((E, 1), lambda i: (0, 0))],
        out_specs=spec,
        compiler_params=_cparams("arbitrary"),
        name="dest_slots",
    )(idx, rank, pad_start.reshape(E, 1))


SC_ROWS = 64


def _sc_workers():
    info = plsc.get_sparse_core_info()
    return info.num_cores, info.num_subcores


def _dispatch_rows(h2p, dest, n_slots):
    N, W = h2p.shape
    K = dest.shape[0]
    R = SC_ROWS
    nc, ns = _sc_workers()
    per_w = N // R // (nc * ns)
    assert per_w * R * nc * ns == N and per_w % 2 == 0, "token count must split into chunk pairs per subcore"
    mesh = plsc.VectorSubcoreMesh(core_axis_name="c", subcore_axis_name="s")

    @functools.partial(
        pl.kernel, mesh=mesh,
        out_type=jax.ShapeDtypeStruct((n_slots, W), I32),
        scratch_types=[pltpu.VMEM((K, R), I32), pltpu.VMEM((K, R), I32),
                       pltpu.VMEM((R, W), I32), pltpu.VMEM((R, W), I32),
                       pltpu.SemaphoreType.DMA, pltpu.SemaphoreType.DMA, pltpu.SemaphoreType.DMA],
    )
    def k(h_hbm, d_hbm, xs_hbm, idx0, idx1, rows0, rows1, lsem0, lsem1, ssem):
        idx, rows, lsem = (idx0, idx1), (rows0, rows1), (lsem0, lsem1)
        base = (lax.axis_index("s") * nc + lax.axis_index("c")) * per_w

        def loads(ch, b):
            tok = pl.ds(pl.multiple_of(ch * R, 8), R)
            return [pltpu.make_async_copy(d_hbm.at[kk, tok], idx[b].at[kk], lsem[b]) for kk in range(K)] + [
                pltpu.make_async_copy(h_hbm.at[tok], rows[b], lsem[b])]

        for cp in loads(base, 0):
            cp.start()
        for cp in loads(base, 0):
            cp.wait()

        @pl.loop(0, per_w, step=2)
        def _(j):
            for b in range(2):
                ch = base + j + b
                more = j + b + 1 < per_w

                @pl.when(more)
                def _():
                    for cp in loads(ch + 1, 1 - b):
                        cp.start()

                scatters = [pltpu.make_async_copy(rows[b], xs_hbm.at[idx[b].at[kk]], ssem) for kk in range(K)]
                for cp in scatters:
                    cp.start()
                for cp in scatters:
                    cp.wait()

                @pl.when(more)
                def _():
                    for cp in loads(ch + 1, 1 - b):
                        cp.wait()

    return k(h2p, dest)


COMBINE_ROWS = 8


def _combine_sum(ys, dest, ww, tok0, n_tok):
    W = ys.shape[1]
    K = dest.shape[0]
    R = COMBINE_ROWS
    L = 16
    nc, ns = _sc_workers()
    per_w = n_tok // R // (nc * ns)
    assert per_w * R * nc * ns == n_tok and per_w % 2 == 0, "token count must split into chunk pairs per subcore"
    T = per_w * R
    mesh = plsc.VectorSubcoreMesh(core_axis_name="c", subcore_axis_name="s")

    @functools.partial(
        pl.kernel, mesh=mesh, compiler_params=pltpu.CompilerParams(needs_layout_passes=False),
        out_type=jax.ShapeDtypeStruct((n_tok, W), I32),
        scratch_types=[pltpu.VMEM((K * T,), I32), pltpu.VMEM((K * T,), I32),
                       pltpu.VMEM((K, R, W), I32), pltpu.VMEM((K, R, W), I32),
                       pltpu.VMEM((R, W), I32), pltpu.VMEM((R, W), I32),
                       pltpu.SemaphoreType.DMA, pltpu.SemaphoreType.DMA,
                       pltpu.SemaphoreType.DMA, pltpu.SemaphoreType.DMA],
    )
    def k(ys_hbm, d_hbm, w_hbm, o_hbm, idx_all, w_all, buf0, buf1, out0, out1, gsem0, gsem1, osem0, osem1):
        buf, outv, gsem, osem = (buf0, buf1), (out0, out1), (gsem0, gsem1), (osem0, osem1)
        wid = lax.axis_index("s") * nc + lax.axis_index("c")
        t0 = pl.multiple_of(tok0 + wid * T, 8)
        for kk in range(K):
            pltpu.sync_copy(d_hbm.at[kk, pl.ds(t0, T)], idx_all.at[pl.ds(kk * T, T)])
            pltpu.sync_copy(w_hbm.at[kk, pl.ds(t0, T)], w_all.at[pl.ds(kk * T, T)])

        def fetch(j, b):
            return [pltpu.make_async_copy(
                ys_hbm.at[idx_all.at[pl.ds(pl.multiple_of(kk * T + j * R, 8), R)]], buf[b].at[kk], gsem[b])
                for kk in range(K)]

        def store(j, b):
            return pltpu.make_async_copy(outv[b], o_hbm.at[pl.ds(pl.multiple_of(wid * T + j * R, 8), R)], osem[b])

        for cp in fetch(0, 0):
            cp.start()

        @pl.loop(0, per_w, step=2)
        def _(j0):
            for b in range(2):
                j = j0 + b

                @pl.when(j + 1 < per_w)
                def _():
                    for cp in fetch(j + 1, 1 - b):
                        cp.start()

                for cp in fetch(j, b):
                    cp.wait()

                @pl.when(j >= 2)
                def _():
                    store(j - 2, b).wait()

                @pl.loop(0, R)
                def _(r):
                    ws = [plsc.bitcast(plsc.load_gather(w_all, [jnp.full((L,), kk * T + j * R + r, I32)]), BF16)
                          for kk in range(K)]

                    @pl.loop(0, W // L)
                    def _(q):
                        sl = pl.ds(q * L, L)
                        terms = [ws[kk] * plsc.bitcast(buf[b][kk, r, sl], BF16) for kk in range(K)]
                        while len(terms) > 1:
                            terms = [terms[i] + terms[i + 1] for i in range(0, len(terms), 2)]
                        outv[b][r, sl] = plsc.bitcast(terms[0], I32)

                store(j, b).start()

        store(per_w - 2, 0).wait()
        store(per_w - 1, 1).wait()

    return k(ys, dest, ww)


def _expert_kernel(b0_ref, nb_ref, cnt_ref, xs_hbm, w1_ref, w3_ref, w2_ref, after_hbm, ys_hbm,
                   w13_s, w2_s, xbuf, ybuf, sem_in, sem_out):
    e = pl.program_id(0)
    n_exp = pl.num_programs(0)
    F = w1_ref.shape[2]
    bm = xbuf.shape[1]
    half = xbuf.shape[2]
    nb = nb_ref[e]
    b0 = b0_ref[e]
    total = b0_ref[n_exp - 1] + nb_ref[n_exp - 1]

    nbuf = xbuf.shape[0]

    def in_copy(g):
        slot = g % nbuf
        return pltpu.make_async_copy(xs_hbm.at[pl.ds(g * bm, bm)], xbuf.at[slot], sem_in.at[slot])

    def out_copy(g):
        slot = g % nbuf
        return pltpu.make_async_copy(ybuf.at[slot], ys_hbm.at[pl.ds(g * bm, bm)], sem_out.at[slot])

    @pl.when(e == 0)
    def _():
        for g0 in range(nbuf - 1):
            @pl.when(g0 < total)
            def _():
                in_copy(g0).start()

    @pl.when(nb > 0)
    def _():
        w13_s[:, :F] = w1_ref[0].astype(BF16)
        w13_s[:, F:] = w3_ref[0].astype(BF16)
        w2_s[...] = w2_ref[0].astype(BF16)

    def block(j, carry):
        g = b0 + j
        slot = g % nbuf
        in_copy(g).wait()

        @pl.when(g + nbuf - 1 < total)
        def _():
            in_copy(g + nbuf - 1).start()

        @pl.when(g >= nbuf)
        def _():
            out_copy(g - nbuf).wait()

        def swiglu(rows):
            lo, hi = _unpack_bf16_pair(xbuf[slot, :rows, :])
            ab = _dot(lo.astype(BF16), w13_s[:half, :]) + _dot(hi.astype(BF16), w13_s[half:, :])
            hid = (_silu(ab[:, :F]) * ab[:, F:]).astype(BF16)
            ybuf[slot, :rows, :] = _pack_bf16_pair(_dot(hid, w2_s[...]))

        rows_left = cnt_ref[e] - j * bm

        @pl.when(rows_left > bm // 2)
        def _():
            swiglu(bm)

        @pl.when(rows_left <= bm // 2)
        def _():
            swiglu(bm // 2)
            ybuf[slot, bm // 2:, :] = jnp.zeros((bm - bm // 2, half), I32)

        out_copy(g).start()
        return carry

    lax.fori_loop(0, nb, block, 0)

    @pl.when(e == n_exp - 1)
    def _():
        for back in range(nbuf, 0, -1):
            @pl.when(total >= back)
            def _():
                out_copy(total - back).wait()


def _experts(xs, blk0, nblk, counts, w_e1, w_e3, w_e2, after):
    n_slots, W = xs.shape
    E, D, F = w_e1.shape
    bm = EXPERT_BLOCK
    grid_spec = pltpu.PrefetchScalarGridSpec(
        num_scalar_prefetch=3,
        grid=(E,),
        in_specs=[
            pl.BlockSpec(memory_space=pl.ANY),
            pl.BlockSpec((1, D, F), lambda e, b0, nb, cnt: (e, 0, 0)),
            pl.BlockSpec((1, D, F), lambda e, b0, nb, cnt: (e, 0, 0)),
            pl.BlockSpec((1, F, D), lambda e, b0, nb, cnt: (e, 0, 0)),
            pl.BlockSpec(memory_space=pl.ANY),
        ],
        out_specs=pl.BlockSpec(memory_space=pl.ANY),
        scratch_shapes=[pltpu.VMEM((D, 2 * F), BF16), pltpu.VMEM((F, D), BF16),
                        pltpu.VMEM((EXPERT_RING, bm, W), I32), pltpu.VMEM((EXPERT_RING, bm, W), I32),
                        pltpu.SemaphoreType.DMA((EXPERT_RING,)), pltpu.SemaphoreType.DMA((EXPERT_RING,))],
    )
    return pl.pallas_call(
        _expert_kernel,
        out_shape=jax.ShapeDtypeStruct((n_slots, W), I32),
        grid_spec=grid_spec,
        compiler_params=_cparams("arbitrary"),
        name="experts",
    )(blk0, nblk, counts, xs, w_e1, w_e3, w_e2, after)


def _shared_kernel(h2_ref, ws13_ref, ws2_ref, o_ref):
    F = ws2_ref.shape[0]
    half = h2_ref.shape[1]
    lo, hi = _unpack_bf16_pair(h2_ref[...])
    ab = _dot(lo.astype(BF16), ws13_ref[:half, :]) + _dot(hi.astype(BF16), ws13_ref[half:, :])
    hid = (_silu(ab[:, :F]) * ab[:, F:]).astype(BF16)
    o_ref[...] = _dot(hid, ws2_ref[...]).astype(o_ref.dtype)


def _shared_expert(h2p, ws13, ws2):
    N, W = h2p.shape
    D = ws2.shape[1]
    tm = min(N, 512)
    full = lambda a: pl.BlockSpec(a.shape, lambda i: (0,) * a.ndim)
    return pl.pallas_call(
        _shared_kernel,
        out_shape=jax.ShapeDtypeStruct((N, D), BF16),
        grid=(N // tm,),
        in_specs=[pl.BlockSpec((tm, W), lambda i: (i, 0)), full(ws13), full(ws2)],
        out_specs=pl.BlockSpec((tm, D), lambda i: (i, 0)),
        compiler_params=_cparams("arbitrary"),
        name="shared_expert",
    )(h2p, ws13, ws2)


def _final_kernel(alpha, ms_ref, sh_ref, x1_ref, mod_ref, g_ref, b_ref, *rest):
    o_ref = rest[-1]
    gt2 = mod_ref[0, 5:6, :]
    lo, hi = _unpack_bf16_pair(ms_ref[...])
    ffn = jnp.concatenate([lo, hi], axis=1) + sh_ref[...].astype(F32)
    o_ref[...] = _layer_norm(alpha * x1_ref[...] + (1.0 + gt2) * ffn, g_ref[...], b_ref[...])


def _final_part(alpha, ms, sh, x1, mod6, ln_g, ln_b, S, tok0, row0, n_total, prev):
    D = x1.shape[1]
    N = n_total
    n_part, W = ms.shape
    tm = min(S, 512, n_part)
    assert n_part % tm == 0 and S % tm == 0 and tok0 % tm == 0 and row0 % tm == 0
    per_b = S // tm
    steps = n_part // tm
    off = tok0 // tm
    goff = row0 // tm
    tok = lambda n: pl.BlockSpec((tm, n), lambda i: (off + i, 0))
    full = lambda a: pl.BlockSpec(a.shape, lambda i: (0,) * a.ndim)
    in_specs = [pl.BlockSpec((tm, W), lambda i: (i, 0)), tok(D), tok(D),
                pl.BlockSpec((1, 6, D), lambda i: ((goff + i) // per_b, 0, 0)), full(ln_g), full(ln_b)]
    args = [ms, sh, x1, mod6, ln_g, ln_b]
    aliases = {}
    if prev is not None:
        in_specs.append(pl.BlockSpec(memory_space=pl.ANY))
        args.append(prev)
        aliases = {len(args) - 1: 0}
    return pl.pallas_call(
        functools.partial(_final_kernel, alpha),
        out_shape=jax.ShapeDtypeStruct((N, D), F32),
        grid=(steps,),
        in_specs=in_specs,
        out_specs=pl.BlockSpec((tm, D), lambda i: (goff + i, 0)),
        input_output_aliases=aliases,
        compiler_params=_cparams("arbitrary"),
        name="combine_ln2",
    )(*args)


def _moe(alpha, h2p, idx, ww, rank, cnt, x1, mod6, w_e1, w_e3, w_e2, ws13, ws2, ln_g, ln_b, S, row0, n_total, out):
    N = x1.shape[0]
    E = w_e1.shape[0]
    bm = EXPERT_BLOCK
    counts = cnt[:, 0]
    padded = (counts + bm - 1) // bm * bm
    pad_end = jnp.cumsum(padded)
    pad_start = pad_end - padded
    n_slots = N * TOP_K + E * bm
    dest = _dest_slots(idx, rank, pad_start.astype(I32))
    xs = _dispatch_rows(h2p, dest, n_slots)
    sh = _shared_expert(h2p, ws13, ws2)
    ys = _experts(xs, (pad_start // bm).astype(I32), (padded // bm).astype(I32), counts.astype(I32),
                  w_e1, w_e3, w_e2, sh)
    n_part = min(N, COMBINE_TOKENS)
    assert N % n_part == 0
    for p in range(N // n_part):
        tok0 = p * n_part
        ms = _combine_sum(ys, dest, ww, tok0, n_part)
        out = _final_part(alpha, ms, sh, x1, mod6, ln_g, ln_b, S, tok0, row0 + tok0, n_total, out)
    return out


def kernel(x, c, positions, w_ada, b_ada, w_in, w_gla_a2, b_gla_a, g_gla_norm, w_gla_o, g_cq, w_uq, g_ckv, w_ukv, w_mla_o, w_out, ln1_g, ln1_b, w_router, b_router, w_e1, w_e3, w_e2, w_s1, w_s3, w_s2, ln2_g, ln2_b):
    B, S, D = x.shape
    N = B * S
    depth = w_ada.shape[0]
    alpha = (2.0 * depth) ** 0.25
    row = lambda a: a.reshape(1, -1)
    cc, ss = _rope_tables(positions)
    x2 = x.reshape(N, D)
    for l in range(depth):
        mod6 = _modulation(c, w_ada[l], b_ada[l]).reshape(B, 6, D)
        qk, gv, gr, cq, ckv, gate_a, gate_b, tail = _in_projection(x2, mod6, _prep_w_in(w_in[l], D), S)
        wa2_p = jnp.concatenate(
            [w_gla_a2[l], jnp.zeros((LANES - GLA_GATE_RANK, w_gla_a2.shape[2]), F32)], axis=0).astype(BF16)
        y_a = _gla(qk, gv, gr, tail, wa2_p, row(b_gla_a[l]), row(g_gla_norm[l]), w_gla_o[l].astype(BF16), B, S)
        qc, kc, vv = _mla_prep(cq, ckv, tail, cc, ss, row(g_cq[l]), _prep_w_uq(w_uq[l]),
                               row(g_ckv[l]), _prep_w_ukv(w_ukv[l]))
        o_b = _mla_attention(qc, kc, vv, B, S)
        ws13 = jnp.concatenate([w_s1[l], w_s3[l]], axis=1).astype(BF16)
        ws2 = w_s2[l].astype(BF16)
        b_first = max(1, (B * MOE_FIRST_EIGHTHS) // 8) if B > 1 else B
        ranges = [(0, b_first * S)] + ([(b_first * S, (B - b_first) * S)] if B > b_first else [])
        routed =[_merge_route(alpha, x2, y_a, o_b, gate_a, gate_b, mod6, w_mla_o[l].astype(BF16),
                               w_out[l].astype(BF16), row(ln1_g[l]), row(ln1_b[l]), w_router[l].T.astype(BF16),
                               b_router[l], S, t0, n) for t0, n in ranges]
        out = None
        for (t0, n), (x1, h2p, idx, ww, rank, cnt) in zip(ranges, routed):
            out = _moe(alpha, h2p, idx, ww, rank, cnt, x1, mod6, w_e1[l], w_e3[l], w_e2[l], ws13, ws2,
                       row(ln2_g[l]), row(ln2_b[l]), S, t0, N, out)
        x2 = out
    return x2.reshape(B, S, D)
```

```python
import functools

import jax
import jax.numpy as jnp
from jax import lax
from jax.experimental import pallas as pl
from jax.experimental.pallas import tpu as pltpu
from jax.experimental.pallas import tpu_sc as plsc

CHUNK = 64
GLA_HEADS = 4
GLA_DK = 128
GLA_DV = 256
GLA_GATE_RANK = 16
GLA_GATE_TAU = 16.0
MLA_HEADS = 8
MLA_Q_RANK = 768
MLA_KV_RANK = 256
MLA_NOPE = 128
MLA_ROPE = 64
MLA_V = 128
ROPE_THETA = 10000.0
N_EXPERTS = 256
TOP_K = 8
N_GROUPS = 8
TOPK_GROUPS = 4
D_EXPERT = 256
ROUTED_SCALE = 2.5
LN_EPS = 1e-5
RMS_EPS = 1e-6
LOG2E = 1.4426950408889634

LANES = 128
VMEM_LIMIT = 56 * 1024 * 1024
EXPERT_BLOCK = 512
EXPERT_RING = 4
MOE_FIRST_EIGHTHS = 5
COMBINE_TOKENS = 8192

F32 = jnp.float32
BF16 = jnp.bfloat16
I32 = jnp.int32


def _cparams(*sem):
    return pltpu.CompilerParams(dimension_semantics=sem, vmem_limit_bytes=VMEM_LIMIT)


def _sigmoid(x):
    return 1.0 / (1.0 + jnp.exp(-x))


def _silu(x):
    return x * _sigmoid(x)


def _dot(a, b):
    return jnp.dot(a, b, preferred_element_type=F32)


def _dot_nt(a, b):
    return lax.dot_general(a, b, (((1,), (1,)), ((), ())), preferred_element_type=F32)


def _dot_tn(a, b):
    return lax.dot_general(a, b, (((0,), (0,)), ((), ())), preferred_element_type=F32)


def _pack_bf16_pair(x):
    w = x.shape[1] // 2
    u = lax.bitcast_convert_type(x.astype(BF16).astype(F32), I32)
    lo = lax.shift_right_logical(u[:, :w], jnp.int32(16))
    hi = jnp.bitwise_and(u[:, w:], jnp.int32(-65536))
    return jnp.bitwise_or(lo, hi)


def _unpack_bf16_pair(p):
    lo = lax.bitcast_convert_type(lax.shift_left(p, jnp.int32(16)), F32)
    hi = lax.bitcast_convert_type(jnp.bitwise_and(p, jnp.int32(-65536)), F32)
    return lo, hi


def _mod_kernel(c_ref, w_ref, b_ref, o_ref):
    cond = _silu(c_ref[...]).astype(BF16)
    o_ref[...] = _dot(cond, w_ref[...].astype(BF16)) + b_ref[...]


def _modulation(c, w_ada, b_ada):
    B, D = c.shape
    W = w_ada.shape[1]
    tn = D
    return pl.pallas_call(
        _mod_kernel,
        out_shape=jax.ShapeDtypeStruct((B, W), F32),
        grid=(W // tn,),
        in_specs=[
            pl.BlockSpec((B, D), lambda j: (0, 0)),
            pl.BlockSpec((D, tn), lambda j: (0, j)),
            pl.BlockSpec((1, tn), lambda j: (0, j)),
        ],
        out_specs=pl.BlockSpec((B, tn), lambda j: (0, j)),
        compiler_params=_cparams("arbitrary"),
        name="adaln_mod",
    )(c, w_ada, b_ada.reshape(1, W))


def _rope_kernel(pos_ref, f_ref, ph_ref, cc_ref, ss_ref):
    ang = pos_ref[...] * f_ref[...]
    cc_ref[...] = jnp.cos(ang)
    ss_ref[...] = jnp.sin(ang) * ph_ref[...]


def _rope_tables(positions):
    B, S = positions.shape
    N = B * S
    half = MLA_ROPE // 2
    inv_freq = ROPE_THETA ** (-jnp.arange(half, dtype=F32) * (2.0 / MLA_ROPE))
    f4 = jnp.tile(inv_freq, LANES // half).reshape(1, LANES)
    sign = jnp.tile(jnp.concatenate([-jnp.ones((half,), F32), jnp.ones((half,), F32)]), LANES // MLA_ROPE)
    pos = jnp.broadcast_to(positions.reshape(N, 1).astype(F32), (N, LANES))
    tm = min(N, 1024)
    spec = pl.BlockSpec((tm, LANES), lambda i: (i, 0))
    vec = pl.BlockSpec((1, LANES), lambda i: (0, 0))
    return pl.pallas_call(
        _rope_kernel,
        out_shape=(jax.ShapeDtypeStruct((N, LANES), F32), jax.ShapeDtypeStruct((N, LANES), F32)),
        grid=(N // tm,),
        in_specs=[spec, vec, vec],
        out_specs=(spec, spec),
        compiler_params=_cparams("arbitrary"),
        name="rope_tables",
    )(pos, f4, sign.reshape(1, LANES))


QK_W = 2 * GLA_HEADS * GLA_DK
GV_W = GLA_HEADS * GLA_DV
TAIL_W = 3 * LANES
IN_SEG = (QK_W, GV_W, GV_W, MLA_Q_RANK, MLA_KV_RANK, 1024, 1024, TAIL_W)


def _prep_w_in(w_in, D):
    s = [GLA_HEADS * GLA_DK, GLA_HEADS * GLA_DK, GV_W, GV_W, GLA_GATE_RANK, MLA_Q_RANK, MLA_KV_RANK, MLA_ROPE, D, D]
    offs = [0]
    for n in s:
        offs.append(offs[-1] + n)
    gq, gk, gv, gr, ga, cq, ckv, kr, gate_a, gate_b = [w_in[:, offs[i]:offs[i + 1]] for i in range(10)]
    half = MLA_ROPE // 2
    kr_sw = jnp.concatenate([kr[:, half:], kr[:, :half]], axis=1)
    pad = jnp.zeros((w_in.shape[0], LANES - GLA_GATE_RANK), w_in.dtype)
    return jnp.concatenate([gq, gk, gv, gr, cq, ckv, gate_a, gate_b, kr, kr, kr_sw, kr_sw, ga, pad], axis=1).astype(BF16)


def _inproj_kernel(x_ref, mod_ref, w_ref, *out_refs):
    sh1 = mod_ref[0, 0:1, :]
    sc1 = mod_ref[0, 1:2, :]
    h = (x_ref[...] * (1.0 + sc1) + sh1).astype(BF16)
    off = 0
    for ref in out_refs:
        n = ref.shape[-1]
        ref[...] = _dot(h, w_ref[:, off:off + n]).astype(ref.dtype)
        off += n


def _in_projection(x2, mod6, w_in_p, S):
    N, D = x2.shape
    W = w_in_p.shape[1]
    tm = min(S, 512)
    per_b = S // tm
    return pl.pallas_call(
        _inproj_kernel,
        out_shape=tuple(jax.ShapeDtypeStruct((N, n), BF16) for n in IN_SEG),
        grid=(N // tm,),
        in_specs=[
            pl.BlockSpec((tm, D), lambda i: (i, 0)),
            pl.BlockSpec((1, 6, D), lambda i: (i // per_b, 0, 0)),
            pl.BlockSpec((D, W), lambda i: (0, 0), pipeline_mode=pl.Buffered(1)),
        ],
        out_specs=tuple(pl.BlockSpec((tm, n), lambda i: (i, 0)) for n in IN_SEG),
        compiler_params=_cparams("arbitrary"),
        name="in_proj",
    )(x2, mod6, w_in_p)


def _gla_kernel(qk_ref, v_ref, gr_ref, tail_ref, wa2_ref, ba_ref, gn_ref, wo_ref, y_ref,
                st_ref, kd_ref, dec_ref, sall_ref, o_ref):
    t = pl.program_id(1)

    @pl.when(t == 0)
    def _():
        st_ref[...] = jnp.zeros_like(st_ref)

    ts = qk_ref.shape[0]
    nch = ts // CHUNK
    HK = GLA_HEADS * GLA_DK
    r = lax.broadcasted_iota(I32, (CHUNK, CHUNK), 0)
    c = lax.broadcasted_iota(I32, (CHUNK, CHUNK), 1)
    tri = (r >= c).astype(BF16)
    qscale = GLA_DK ** -0.5

    z = _dot(tail_ref[:, 2 * LANES:3 * LANES], wa2_ref[...]) + ba_ref[...]
    log_a = (jnp.minimum(z, 0.0) - jnp.log(1.0 + jnp.exp(-jnp.abs(z)))) * (1.0 / GLA_GATE_TAU)
    la_hi = log_a.astype(BF16)
    la_lo = (log_a - la_hi.astype(F32)).astype(BF16)
    for n in range(nch):
        rows = slice(n * CHUNK, (n + 1) * CHUNK)
        G = _dot(tri, la_hi[rows]) + _dot(tri, la_lo[rows])
        g_end = G[CHUNK - 1:CHUNK, :]
        kd_ref[rows, :] = (qk_ref[rows, HK:2 * HK].astype(F32) * jnp.exp(g_end - G)).astype(BF16)
        dec_ref[n:n + 1, :] = jnp.exp(g_end)

    for h in range(GLA_HEADS):
        ks = slice(h * GLA_DK, (h + 1) * GLA_DK)
        vs = slice(h * GLA_DV, (h + 1) * GLA_DV)
        st = st_ref[h]
        for n in range(nch):
            rows = slice(n * CHUNK, (n + 1) * CHUNK)
            st = st * dec_ref[n:n + 1, ks] + _dot_tn(v_ref[rows, vs], kd_ref[rows, ks])
            sall_ref[n * GLA_HEADS + h] = st.astype(BF16)
        st_ref[h] = st

    for n in range(nch):
        rows = slice(n * CHUNK, (n + 1) * CHUNK)
        for h in range(GLA_HEADS):
            ks = slice(h * GLA_DK, (h + 1) * GLA_DK)
            vs = slice(h * GLA_DV, (h + 1) * GLA_DV)
            qh = (qk_ref[rows, ks].astype(F32) * qscale).astype(BF16)
            o = _dot_nt(qh, sall_ref[n * GLA_HEADS + h])
            o = o * lax.rsqrt(jnp.mean(o * o, axis=-1, keepdims=True) + RMS_EPS) * gn_ref[...]
            o_ref[rows, vs] = (o * _silu(gr_ref[rows, vs].astype(F32))).astype(BF16)
    y_ref[...] = _dot(o_ref[...], wo_ref[...]).astype(y_ref.dtype)


def _gla(qk, gv, gr, tail, wa2_p, b_a, g_norm, w_o, B, S):
    N = B * S
    D = w_o.shape[1]
    ts = min(S, 512)
    per_b = S // ts
    HK = GLA_HEADS * GLA_DK
    tok = lambda n: pl.BlockSpec((ts, n), lambda b, t: (b * per_b + t, 0))
    full = lambda a: pl.BlockSpec(a.shape, lambda b, t: (0,) * a.ndim)
    return pl.pallas_call(
        _gla_kernel,
        out_shape=jax.ShapeDtypeStruct((N, D), BF16),
        grid=(B, per_b),
        in_specs=[tok(QK_W), tok(GV_W), tok(GV_W), tok(TAIL_W), full(wa2_p), full(b_a), full(g_norm), full(w_o)],
        out_specs=tok(D),
        scratch_shapes=[pltpu.VMEM((GLA_HEADS, GLA_DV, GLA_DK), F32),
                        pltpu.VMEM((ts, GLA_HEADS * GLA_DK), BF16),
                        pltpu.VMEM((ts // CHUNK, GLA_HEADS * GLA_DK), F32),
                        pltpu.VMEM((ts // CHUNK * GLA_HEADS, GLA_DV, GLA_DK), BF16),
                        pltpu.VMEM((ts, GV_W), BF16)],
        compiler_params=_cparams("arbitrary", "arbitrary"),
        name="gla",
    )(qk, gv, gr, tail, wa2_p, b_a, g_norm, w_o)


HQ = MLA_HEADS * 2 * LANES


def _prep_w_uq(w_uq):
    dh = MLA_NOPE + MLA_ROPE
    half = MLA_ROPE // 2
    nope = [w_uq[:, h * dh:h * dh + MLA_NOPE] for h in range(MLA_HEADS)]
    rope = [w_uq[:, h * dh + MLA_NOPE:(h + 1) * dh] for h in range(MLA_HEADS)]
    rope_sw = [jnp.concatenate([r[:, half:], r[:, :half]], axis=1) for r in rope]
    return jnp.concatenate(nope + rope + rope_sw, axis=1).astype(BF16)


def _prep_w_ukv(w_ukv):
    dh = MLA_NOPE + MLA_V
    kn = [w_ukv[:, h * dh:h * dh + MLA_NOPE] for h in range(MLA_HEADS)]
    vv = [w_ukv[:, h * dh + MLA_NOPE:(h + 1) * dh] for h in range(MLA_HEADS)]
    return jnp.concatenate(kn + vv, axis=1).astype(BF16)


def _rms(x, g):
    return x * lax.rsqrt(jnp.mean(x * x, axis=-1, keepdims=True) + RMS_EPS) * g


def _mla_prep_kernel(cq_ref, ckv_ref, tail_ref, cc_ref, ss_ref, gq_ref, wq_ref, gkv_ref, wkv_ref, q_ref, k_ref, v_ref):
    tm = cq_ref.shape[0]
    NP = MLA_HEADS * MLA_NOPE
    RP = MLA_HEADS * MLA_ROPE
    scale = (MLA_NOPE + MLA_ROPE) ** -0.5 * LOG2E
    cc = cc_ref[...]
    ss = ss_ref[...]
    cqn = _rms(cq_ref[...].astype(F32), gq_ref[...]).astype(BF16)
    qf = _dot(cqn, wq_ref[...]) * scale
    ckvn = _rms(ckv_ref[...].astype(F32), gkv_ref[...]).astype(BF16)
    kv = _dot(ckvn, wkv_ref[...])
    krr = (tail_ref[:, 0:LANES].astype(F32) * cc + tail_ref[:, LANES:2 * LANES].astype(F32) * ss).astype(BF16)
    lane = lax.broadcasted_iota(I32, (tm, LANES), 1)
    first = lane < MLA_ROPE
    for j in range(MLA_HEADS // 2):
        a = NP + j * LANES
        rot = qf[:, a:a + LANES] * cc + qf[:, a + RP:a + RP + LANES] * ss
        for h, keep in ((2 * j, first), (2 * j + 1, jnp.logical_not(first))):
            base = h * 2 * LANES
            q_ref[:, base:base + LANES] = qf[:, h * MLA_NOPE:(h + 1) * MLA_NOPE].astype(BF16)
            q_ref[:, base + LANES:base + 2 * LANES] = jnp.where(keep, rot, 0.0).astype(BF16)
            k_ref[:, base:base + LANES] = kv[:, h * MLA_NOPE:(h + 1) * MLA_NOPE].astype(BF16)
            k_ref[:, base + LANES:base + 2 * LANES] = krr
    v_ref[...] = kv[:, NP:].astype(BF16)


def _mla_prep(cq, ckv, tail, cc, ss, g_cq, w_uq_p, g_ckv, w_ukv_p):
    N = cq.shape[0]
    tm = min(N, 512)
    tok = lambda n: pl.BlockSpec((tm, n), lambda i: (i, 0))
    full = lambda a: pl.BlockSpec(a.shape, lambda i: (0,) * a.ndim)
    HV = MLA_HEADS * MLA_V
    return pl.pallas_call(
        _mla_prep_kernel,
        out_shape=(jax.ShapeDtypeStruct((N, HQ), BF16), jax.ShapeDtypeStruct((N, HQ), BF16),
                   jax.ShapeDtypeStruct((N, HV), BF16)),
        grid=(N // tm,),
        in_specs=[tok(MLA_Q_RANK), tok(MLA_KV_RANK), tok(TAIL_W), tok(LANES), tok(LANES),
                  full(g_cq), full(w_uq_p), full(g_ckv), full(w_ukv_p)],
        out_specs=(tok(HQ), tok(HQ), tok(HV)),
        compiler_params=_cparams("arbitrary"),
        name="mla_prep",
    )(cq, ckv, tail, cc, ss, g_cq, w_uq_p, g_ckv, w_ukv_p)


ATTN_TQ = 256


def _mla_attn_kernel(q_ref, k_ref, v_ref, o_ref, v1_ref):
    S = q_ref.shape[0]
    tq = min(S, ATTN_TQ)
    r = lax.broadcasted_iota(I32, (tq, tq), 0) // CHUNK
    c = lax.broadcasted_iota(I32, (tq, tq), 1) // CHUNK
    diag_mask = c <= r
    v1_ref[:, :MLA_V] = v_ref[...]
    v1_ref[:, MLA_V:] = jnp.ones((S, MLA_V), BF16)

    def scores(ii):
        l0 = ii * tq
        q = q_ref[l0:l0 + tq, :]
        sd = jnp.where(diag_mask, _dot_nt(q, k_ref[l0:l0 + tq, :]), -jnp.inf)
        so = _dot_nt(q, k_ref[0:l0, :]) if ii > 0 else None
        return sd, so

    def finish(ii, sd, so):
        l0 = ii * tq
        m = jnp.max(sd, axis=-1, keepdims=True)
        if so is not None:
            m = jnp.maximum(m, jnp.max(so, axis=-1, keepdims=True))
        acc = _dot(jnp.exp2((sd - m).astype(BF16)), v1_ref[l0:l0 + tq, :])
        if so is not None:
            acc = acc + _dot(jnp.exp2((so - m).astype(BF16)), v1_ref[0:l0, :])
        o_ref[l0:l0 + tq, :] = (acc[:, :MLA_V] / acc[:, MLA_V:]).astype(BF16)

    n_tiles = S // tq
    ahead = 2
    pending = [scores(ii) for ii in range(min(ahead, n_tiles))]
    for ii in range(n_tiles):
        if ii + ahead < n_tiles:
            pending.append(scores(ii + ahead))
        finish(ii, *pending.pop(0))


def _mla_attention(qc, kc, vv, B, S):
    N = B * S
    HV = MLA_HEADS * MLA_V
    return pl.pallas_call(
        _mla_attn_kernel,
        out_shape=jax.ShapeDtypeStruct((N, HV), BF16),
        grid=(B, MLA_HEADS),
        in_specs=[
            pl.BlockSpec((S, 2 * LANES), lambda b, h: (b, h)),
            pl.BlockSpec((S, 2 * LANES), lambda b, h: (b, h)),
            pl.BlockSpec((S, MLA_V), lambda b, h: (b, h)),
        ],
        out_specs=pl.BlockSpec((S, MLA_V), lambda b, h: (b, h)),
        scratch_shapes=[pltpu.VMEM((S, 2 * MLA_V), BF16)],
        compiler_params=_cparams("arbitrary", "arbitrary"),
        name="mla_attn",
    )(qc, kc, vv)


def _layer_norm(u, g, b):
    mu = jnp.mean(u, axis=-1, keepdims=True)
    d = u - mu
    var = jnp.mean(d * d, axis=-1, keepdims=True)
    return d * lax.rsqrt(var + LN_EPS) * g + b


def _merge_kernel(alpha, x_ref, ya_ref, ob_ref, ga_ref, gb_ref, mod_ref, wmo_ref, wout_ref, g_ref, b_ref, wr_ref,
                  br_ref, ws13_ref, ws2_ref, x1_ref, h2_ref, sh_ref, idx_ref, wt_ref, rank_ref, cnt_ref, lg_s, carry_s):
    i = pl.program_id(0)

    @pl.when(i == 0)
    def _():
        lg_s[...] = jnp.zeros_like(lg_s)
        carry_s[...] = jnp.zeros_like(carry_s)

    idx, wt, rank, total = _route_math(lg_s[...], br_ref[...], carry_s[...])
    total = jnp.where(i > 0, total, 0.0)
    idx_ref[...] = idx
    wt_ref[...] = wt
    rank_ref[...] = rank
    carry_s[...] = total
    cnt_ref[...] = jnp.broadcast_to(total, cnt_ref.shape).astype(I32)

    gt1 = mod_ref[0, 2:3, :]
    sh2 = mod_ref[0, 3:4, :]
    sc2 = mod_ref[0, 4:5, :]
    y = (_sigmoid(ga_ref[...].astype(F32)) * ya_ref[...].astype(F32)
         + _sigmoid(gb_ref[...].astype(F32)) * _dot(ob_ref[...], wmo_ref[...]))
    mix = _dot(y.astype(BF16), wout_ref[...])
    x1 = _layer_norm(alpha * x_ref[...] + (1.0 + gt1) * mix, g_ref[...], b_ref[...])
    x1_ref[...] = x1
    h2 = x1 * (1.0 + sc2) + sh2
    h2_ref[...] = _pack_bf16_pair(h2)
    h2b = h2.astype(BF16)
    lg_s[...] = _dot_nt(wr_ref[...], h2b)
    F = ws2_ref.shape[0]
    ab = _dot(h2b, ws13_ref[...])
    hid = (_silu(ab[:, :F]) * ab[:, F:]).astype(BF16)
    sh_ref[...] = _dot(hid, ws2_ref[...]).astype(sh_ref.dtype)


def _merge_route(alpha, x2, y_a, o_b, gate_a, gate_b, mod6, w_mla_o, w_out, ln_g, ln_b, wr_t, b_router, ws13, ws2,
                 S, tok0, Np):
    N, D = x2.shape
    E = wr_t.shape[0]
    tm = min(S, 512)
    per_b = S // tm
    assert Np % tm == 0 and tok0 % tm == 0
    n = Np // tm
    base = tok0 // tm
    cur = lambda i: jnp.minimum(i, n - 1)
    prev = lambda i: jnp.maximum(i - 1, 0)
    tok_in = lambda w: pl.BlockSpec((tm, w), lambda i: (base + cur(i), 0))
    tok = lambda w: pl.BlockSpec((tm, w), lambda i: (cur(i), 0))
    full = lambda a: pl.BlockSpec(a.shape, lambda i: (0,) * a.ndim)
    lane_blk = pl.BlockSpec((TOP_K, tm), lambda i: (0, prev(i)))
    br = b_router.reshape(E, 1).astype(F32)
    return pl.pallas_call(
        functools.partial(_merge_kernel, alpha),
        out_shape=(jax.ShapeDtypeStruct((Np, D), F32), jax.ShapeDtypeStruct((Np, D // 2), I32),
                   jax.ShapeDtypeStruct((Np, D), BF16),
                   jax.ShapeDtypeStruct((TOP_K, Np), I32), jax.ShapeDtypeStruct((TOP_K, Np), I32),
                   jax.ShapeDtypeStruct((TOP_K, Np), I32), jax.ShapeDtypeStruct((E, LANES), I32)),
        grid=(n + 1,),
        in_specs=[tok_in(D), tok_in(D), tok_in(D), tok_in(D), tok_in(D),
                  pl.BlockSpec((1, 6, D), lambda i: ((base + cur(i)) // per_b, 0, 0)),
                  full(w_mla_o), full(w_out), full(ln_g), full(ln_b), full(wr_t), full(br), full(ws13), full(ws2)],
        out_specs=(tok(D), tok(D // 2), tok(D), lane_blk, lane_blk, lane_blk,
                   pl.BlockSpec((E, LANES), lambda i: (0, 0))),
        scratch_shapes=[pltpu.VMEM((E, tm), F32), pltpu.VMEM((E, 1), F32)],
        compiler_params=_cparams("arbitrary"),
        name="merge_ln1_route",
    )(x2, y_a, o_b, gate_a, gate_b, mod6, w_mla_o, w_out, ln_g, ln_b, wr_t, br, ws13, ws2)


def _first_argmax(v, io, n):
    m = jnp.max(v, axis=0, keepdims=True)
    idx = jnp.min(jnp.where(v == m, io, n), axis=0, keepdims=True)
    return m, idx


def _route_math(lg, br, carry):
    E, T = lg.shape
    gsz = E // N_GROUPS
    neg = -jnp.inf
    s = _sigmoid(lg)
    biased = s + br
    eio = lax.broadcasted_iota(I32, (E, T), 0)
    gio = lax.broadcasted_iota(I32, (gsz, T), 0)

    gs = []
    for g in range(N_GROUPS):
        blk = biased[g * gsz:(g + 1) * gsz]
        m1, i1 = _first_argmax(blk, gio, gsz)
        m2 = jnp.max(jnp.where(gio == i1, neg, blk), axis=0, keepdims=True)
        gs.append(m1 + m2)
    cur = jnp.concatenate(gs, axis=0)
    nio = lax.broadcasted_iota(I32, (N_GROUPS, T), 0)
    gsel = jnp.zeros((N_GROUPS, T), F32)
    for _ in range(TOPK_GROUPS):
        _, gi = _first_argmax(cur, nio, N_GROUPS)
        hit = nio == gi
        gsel = jnp.where(hit, 1.0, gsel)
        cur = jnp.where(hit, neg, cur)
    emask = jnp.concatenate([jnp.broadcast_to(gsel[g:g + 1], (gsz, T)) for g in range(N_GROUPS)], axis=0) > 0.0

    cur = jnp.where(emask, biased, neg)
    idxs, ws = [], []
    sel = jnp.zeros((E, T), F32)
    for _ in range(TOP_K):
        _, ei = _first_argmax(cur, eio, E)
        hit = eio == ei
        idxs.append(ei)
        ws.append(jnp.sum(jnp.where(hit, s, 0.0), axis=0, keepdims=True))
        sel = jnp.where(hit, 1.0, sel)
        cur = jnp.where(hit, neg, cur)
    w = jnp.concatenate(ws, axis=0)
    w = w / jnp.sum(w, axis=0, keepdims=True) * ROUTED_SCALE
    wbits = lax.bitcast_convert_type(w.astype(BF16).astype(F32), I32)
    wword = jnp.bitwise_or(lax.shift_right_logical(wbits, jnp.int32(16)), jnp.bitwise_and(wbits, jnp.int32(-65536)))

    r = lax.broadcasted_iota(I32, (T, T), 0)
    c = lax.broadcasted_iota(I32, (T, T), 1)
    before = (r < c).astype(BF16)
    cnt = _dot(sel.astype(BF16), before) + carry
    ranks = [jnp.sum(jnp.where(eio == idxs[k], cnt, 0.0), axis=0, keepdims=True) for k in range(TOP_K)]
    total = cnt[:, T - 1:T] + sel[:, T - 1:T]
    return jnp.concatenate(idxs, axis=0), wword, jnp.concatenate(ranks, axis=0).astype(I32), total


def _dest_kernel(idx_ref, rank_ref, ps_ref, dest_ref):
    K, T = idx_ref.shape
    E = ps_ref.shape[0]
    eio = lax.broadcasted_iota(I32, (E, T), 0)
    ps = ps_ref[...]
    rows = [jnp.sum(jnp.where(eio == idx_ref[k:k + 1, :], ps, 0), axis=0, keepdims=True) for k in range(K)]
    dest_ref[...] = jnp.concatenate(rows, axis=0) + rank_ref[...]


def _dest_slots(idx, rank, pad_start):
    K, N = idx.shape
    E = pad_start.shape[0]
    T = min(N, 512)
    spec = pl.BlockSpec((K, T), lambda i: (0, i))
    return pl.pallas_call(
        _dest_kernel,
        out_shape=jax.ShapeDtypeStruct((K, N), I32),
        grid=(N // T,),
        in_specs=[spec, spec, pl.BlockSpec((E, 1), lambda i: (0, 0))],
        out_specs=spec,
        compiler_params=_cparams("arbitrary"),
        name="dest_slots",
    )(idx, rank, pad_start.reshape(E, 1))


SC_ROWS = 64


def _sc_workers():
    info = plsc.get_sparse_core_info()
    return info.num_cores, info.num_subcores


def _dispatch_rows(h2p, dest, n_slots):
    N, W = h2p.shape
    K = dest.shape[0]
    R = SC_ROWS
    nc, ns = _sc_workers()
    per_w = N // R // (nc * ns)
    assert per_w * R * nc * ns == N and per_w % 2 == 0, "token count must split into chunk pairs per subcore"
    mesh = plsc.VectorSubcoreMesh(core_axis_name="c", subcore_axis_name="s")

    @functools.partial(
        pl.kernel, mesh=mesh,
        out_type=jax.ShapeDtypeStruct((n_slots, W), I32),
        scratch_types=[pltpu.VMEM((K, R), I32), pltpu.VMEM((K, R), I32),
                       pltpu.VMEM((R, W), I32), pltpu.VMEM((R, W), I32),
                       pltpu.SemaphoreType.DMA, pltpu.SemaphoreType.DMA, pltpu.SemaphoreType.DMA],
    )
    def k(h_hbm, d_hbm, xs_hbm, idx0, idx1, rows0, rows1, lsem0, lsem1, ssem):
        idx, rows, lsem = (idx0, idx1), (rows0, rows1), (lsem0, lsem1)
        base = (lax.axis_index("s") * nc + lax.axis_index("c")) * per_w

        def loads(ch, b):
            tok = pl.ds(pl.multiple_of(ch * R, 8), R)
            return [pltpu.make_async_copy(d_hbm.at[kk, tok], idx[b].at[kk], lsem[b]) for kk in range(K)] + [
                pltpu.make_async_copy(h_hbm.at[tok], rows[b], lsem[b])]

        for cp in loads(base, 0):
            cp.start()
        for cp in loads(base, 0):
            cp.wait()

        @pl.loop(0, per_w, step=2)
        def _(j):
            for b in range(2):
                ch = base + j + b
                more = j + b + 1 < per_w

                @pl.when(more)
                def _():
                    for cp in loads(ch + 1, 1 - b):
                        cp.start()

                scatters = [pltpu.make_async_copy(rows[b], xs_hbm.at[idx[b].at[kk]], ssem) for kk in range(K)]
                for cp in scatters:
                    cp.start()
                for cp in scatters:
                    cp.wait()

                @pl.when(more)
                def _():
                    for cp in loads(ch + 1, 1 - b):
                        cp.wait()

    return k(h2p, dest)


COMBINE_ROWS = 8


def _combine_sum(ys, dest, ww, tok0, n_tok):
    W = ys.shape[1]
    K = dest.shape[0]
    R = COMBINE_ROWS
    L = 16
    nc, ns = _sc_workers()
    per_w = n_tok // R // (nc * ns)
    assert per_w * R * nc * ns == n_tok and per_w % 2 == 0, "token count must split into chunk pairs per subcore"
    T = per_w * R
    mesh = plsc.VectorSubcoreMesh(core_axis_name="c", subcore_axis_name="s")

    @functools.partial(
        pl.kernel, mesh=mesh, compiler_params=pltpu.CompilerParams(needs_layout_passes=False),
        out_type=jax.ShapeDtypeStruct((n_tok, W), I32),
        scratch_types=[pltpu.VMEM((K * T,), I32), pltpu.VMEM((K * T,), I32),
                       pltpu.VMEM((K, R, W), I32), pltpu.VMEM((K, R, W), I32),
                       pltpu.VMEM((R, W), I32), pltpu.VMEM((R, W), I32),
                       pltpu.SemaphoreType.DMA, pltpu.SemaphoreType.DMA,
                       pltpu.SemaphoreType.DMA, pltpu.SemaphoreType.DMA],
    )
    def k(ys_hbm, d_hbm, w_hbm, o_hbm, idx_all, w_all, buf0, buf1, out0, out1, gsem0, gsem1, osem0, osem1):
        buf, outv, gsem, osem = (buf0, buf1), (out0, out1), (gsem0, gsem1), (osem0, osem1)
        wid = lax.axis_index("s") * nc + lax.axis_index("c")
        t0 = pl.multiple_of(tok0 + wid * T, 8)
        for kk in range(K):
            pltpu.sync_copy(d_hbm.at[kk, pl.ds(t0, T)], idx_all.at[pl.ds(kk * T, T)])
            pltpu.sync_copy(w_hbm.at[kk, pl.ds(t0, T)], w_all.at[pl.ds(kk * T, T)])

        def fetch(j, b):
            return [pltpu.make_async_copy(
                ys_hbm.at[idx_all.at[pl.ds(pl.multiple_of(kk * T + j * R, 8), R)]], buf[b].at[kk], gsem[b])
                for kk in range(K)]

        def store(j, b):
            return pltpu.make_async_copy(outv[b], o_hbm.at[pl.ds(pl.multiple_of(wid * T + j * R, 8), R)], osem[b])

        for cp in fetch(0, 0):
            cp.start()

        @pl.loop(0, per_w, step=2)
        def _(j0):
            for b in range(2):
                j = j0 + b

                @pl.when(j + 1 < per_w)
                def _():
                    for cp in fetch(j + 1, 1 - b):
                        cp.start()

                for cp in fetch(j, b):
                    cp.wait()

                @pl.when(j >= 2)
                def _():
                    store(j - 2, b).wait()

                @pl.loop(0, R)
                def _(r):
                    ws = [plsc.bitcast(plsc.load_gather(w_all, [jnp.full((L,), kk * T + j * R + r, I32)]), BF16)
                          for kk in range(K)]

                    @pl.loop(0, W // L)
                    def _(q):
                        sl = pl.ds(q * L, L)
                        terms = [ws[kk] * plsc.bitcast(buf[b][kk, r, sl], BF16) for kk in range(K)]
                        while len(terms) > 1:
                            terms = [terms[i] + terms[i + 1] for i in range(0, len(terms), 2)]
                        outv[b][r, sl] = plsc.bitcast(terms[0], I32)

                store(j, b).start()

        store(per_w - 2, 0).wait()
        store(per_w - 1, 1).wait()

    return k(ys, dest, ww)


def _expert_kernel(b0_ref, nb_ref, cnt_ref, xs_hbm, w1_ref, w3_ref, w2_ref, ys_hbm,
                   w13_s, w2_s, xbuf, ybuf, sem_in, sem_out):
    e = pl.program_id(0)
    n_exp = pl.num_programs(0)
    F = w1_ref.shape[2]
    bm = xbuf.shape[1]
    half = xbuf.shape[2]
    nb = nb_ref[e]
    b0 = b0_ref[e]
    total = b0_ref[n_exp - 1] + nb_ref[n_exp - 1]

    nbuf = xbuf.shape[0]

    def in_copy(g):
        slot = g % nbuf
        return pltpu.make_async_copy(xs_hbm.at[pl.ds(g * bm, bm)], xbuf.at[slot], sem_in.at[slot])

    def out_copy(g):
        slot = g % nbuf
        return pltpu.make_async_copy(ybuf.at[slot], ys_hbm.at[pl.ds(g * bm, bm)], sem_out.at[slot])

    @pl.when(e == 0)
    def _():
        for g0 in range(nbuf - 1):
            @pl.when(g0 < total)
            def _():
                in_copy(g0).start()

    @pl.when(nb > 0)
    def _():
        w13_s[:, :F] = w1_ref[0].astype(BF16)
        w13_s[:, F:] = w3_ref[0].astype(BF16)
        w2_s[...] = w2_ref[0].astype(BF16)

    def block(j, carry):
        g = b0 + j
        slot = g % nbuf
        in_copy(g).wait()

        @pl.when(g + nbuf - 1 < total)
        def _():
            in_copy(g + nbuf - 1).start()

        @pl.when(g >= nbuf)
        def _():
            out_copy(g - nbuf).wait()

        def swiglu(rows):
            lo, hi = _unpack_bf16_pair(xbuf[slot, :rows, :])
            ab = _dot(lo.astype(BF16), w13_s[:half, :]) + _dot(hi.astype(BF16), w13_s[half:, :])
            hid = (_silu(ab[:, :F]) * ab[:, F:]).astype(BF16)
            ybuf[slot, :rows, :] = _pack_bf16_pair(_dot(hid, w2_s[...]))

        rows_left = cnt_ref[e] - j * bm

        @pl.when(rows_left > bm // 2)
        def _():
            swiglu(bm)

        @pl.when(rows_left <= bm // 2)
        def _():
            swiglu(bm // 2)
            ybuf[slot, bm // 2:, :] = jnp.zeros((bm - bm // 2, half), I32)

        out_copy(g).start()
        return carry

    lax.fori_loop(0, nb, block, 0)

    @pl.when(e == n_exp - 1)
    def _():
        for back in range(nbuf, 0, -1):
            @pl.when(total >= back)
            def _():
                out_copy(total - back).wait()


def _experts(xs, blk0, nblk, counts, w_e1, w_e3, w_e2):
    n_slots, W = xs.shape
    E, D, F = w_e1.shape
    bm = EXPERT_BLOCK
    grid_spec = pltpu.PrefetchScalarGridSpec(
        num_scalar_prefetch=3,
        grid=(E,),
        in_specs=[
            pl.BlockSpec(memory_space=pl.ANY),
            pl.BlockSpec((1, D, F), lambda e, b0, nb, cnt: (e, 0, 0)),
            pl.BlockSpec((1, D, F), lambda e, b0, nb, cnt: (e, 0, 0)),
            pl.BlockSpec((1, F, D), lambda e, b0, nb, cnt: (e, 0, 0)),
        ],
        out_specs=pl.BlockSpec(memory_space=pl.ANY),
        scratch_shapes=[pltpu.VMEM((D, 2 * F), BF16), pltpu.VMEM((F, D), BF16),
                        pltpu.VMEM((EXPERT_RING, bm, W), I32), pltpu.VMEM((EXPERT_RING, bm, W), I32),
                        pltpu.SemaphoreType.DMA((EXPERT_RING,)), pltpu.SemaphoreType.DMA((EXPERT_RING,))],
    )
    return pl.pallas_call(
        _expert_kernel,
        out_shape=jax.ShapeDtypeStruct((n_slots, W), I32),
        grid_spec=grid_spec,
        compiler_params=_cparams("arbitrary"),
        name="experts",
    )(blk0, nblk, counts, xs, w_e1, w_e3, w_e2)


def _final_kernel(alpha, ms_ref, sh_ref, x1_ref, mod_ref, g_ref, b_ref, *rest):
    o_ref = rest[-1]
    gt2 = mod_ref[0, 5:6, :]
    lo, hi = _unpack_bf16_pair(ms_ref[...])
    ffn = jnp.concatenate([lo, hi], axis=1) + sh_ref[...].astype(F32)
    o_ref[...] = _layer_norm(alpha * x1_ref[...] + (1.0 + gt2) * ffn, g_ref[...], b_ref[...])


def _final_part(alpha, ms, sh, x1, mod6, ln_g, ln_b, S, tok0, row0, n_total, prev):
    D = x1.shape[1]
    N = n_total
    n_part, W = ms.shape
    tm = min(S, 512, n_part)
    assert n_part % tm == 0 and S % tm == 0 and tok0 % tm == 0 and row0 % tm == 0
    per_b = S // tm
    steps = n_part // tm
    off = tok0 // tm
    goff = row0 // tm
    tok = lambda n: pl.BlockSpec((tm, n), lambda i: (off + i, 0))
    full = lambda a: pl.BlockSpec(a.shape, lambda i: (0,) * a.ndim)
    in_specs = [pl.BlockSpec((tm, W), lambda i: (i, 0)), tok(D), tok(D),
                pl.BlockSpec((1, 6, D), lambda i: ((goff + i) // per_b, 0, 0)), full(ln_g), full(ln_b)]
    args = [ms, sh, x1, mod6, ln_g, ln_b]
    aliases = {}
    if prev is not None:
        in_specs.append(pl.BlockSpec(memory_space=pl.ANY))
        args.append(prev)
        aliases = {len(args) - 1: 0}
    return pl.pallas_call(
        functools.partial(_final_kernel, alpha),
        out_shape=jax.ShapeDtypeStruct((N, D), F32),
        grid=(steps,),
        in_specs=in_specs,
        out_specs=pl.BlockSpec((tm, D), lambda i: (goff + i, 0)),
        input_output_aliases=aliases,
        compiler_params=_cparams("arbitrary"),
        name="combine_ln2",
    )(*args)


def _moe(alpha, h2p, sh, idx, ww, rank, cnt, x1, mod6, w_e1, w_e3, w_e2, ln_g, ln_b, S, row0, n_total, out):
    N = x1.shape[0]
    E = w_e1.shape[0]
    bm = EXPERT_BLOCK
    counts = cnt[:, 0]
    padded = (counts + bm - 1) // bm * bm
    pad_end = jnp.cumsum(padded)
    pad_start = pad_end - padded
    n_slots = N * TOP_K + E * bm
    dest = _dest_slots(idx, rank, pad_start.astype(I32))
    xs = _dispatch_rows(h2p, dest, n_slots)
    ys = _experts(xs, (pad_start // bm).astype(I32), (padded // bm).astype(I32), counts.astype(I32),
                  w_e1, w_e3, w_e2)
    n_part = min(N, COMBINE_TOKENS)
    assert N % n_part == 0
    for p in range(N // n_part):
        tok0 = p * n_part
        ms = _combine_sum(ys, dest, ww, tok0, n_part)
        out = _final_part(alpha, ms, sh, x1, mod6, ln_g, ln_b, S, tok0, row0 + tok0, n_total, out)
    return out


def kernel(x, c, positions, w_ada, b_ada, w_in, w_gla_a2, b_gla_a, g_gla_norm, w_gla_o, g_cq, w_uq, g_ckv, w_ukv, w_mla_o, w_out, ln1_g, ln1_b, w_router, b_router, w_e1, w_e3, w_e2, w_s1, w_s3, w_s2, ln2_g, ln2_b):
    B, S, D = x.shape
    N = B * S
    depth = w_ada.shape[0]
    alpha = (2.0 * depth) ** 0.25
    row = lambda a: a.reshape(1, -1)
    cc, ss = _rope_tables(positions)
    x2 = x.reshape(N, D)
    for l in range(depth):
        mod6 = _modulation(c, w_ada[l], b_ada[l]).reshape(B, 6, D)
        qk, gv, gr, cq, ckv, gate_a, gate_b, tail = _in_projection(x2, mod6, _prep_w_in(w_in[l], D), S)
        wa2_p = jnp.concatenate(
            [w_gla_a2[l], jnp.zeros((LANES - GLA_GATE_RANK, w_gla_a2.shape[2]), F32)], axis=0).astype(BF16)
        y_a = _gla(qk, gv, gr, tail, wa2_p, row(b_gla_a[l]), row(g_gla_norm[l]), w_gla_o[l].astype(BF16), B, S)
        qc, kc, vv = _mla_prep(cq, ckv, tail, cc, ss, row(g_cq[l]), _prep_w_uq(w_uq[l]),
                               row(g_ckv[l]), _prep_w_ukv(w_ukv[l]))
        o_b = _mla_attention(qc, kc, vv, B, S)
        ws13 = jnp.concatenate([w_s1[l], w_s3[l]], axis=1).astype(BF16)
        ws2 = w_s2[l].astype(BF16)
        b_first = max(1, (B * MOE_FIRST_EIGHTHS) // 8) if B > 1 else B
        ranges = [(0, b_first * S)] + ([(b_first * S, (B - b_first) * S)] if B > b_first else [])
        routed =[_merge_route(alpha, x2, y_a, o_b, gate_a, gate_b, mod6, w_mla_o[l].astype(BF16),
                               w_out[l].astype(BF16), row(ln1_g[l]), row(ln1_b[l]), w_router[l].T.astype(BF16),
                               b_router[l], ws13, ws2, S, t0, n) for t0, n in ranges]
        out = None
        for (t0, n), (x1, h2p, sh, idx, ww, rank, cnt) in zip(ranges, routed):
            out = _moe(alpha, h2p, sh, idx, ww, rank, cnt, x1, mod6, w_e1[l], w_e3[l], w_e2[l],
                       row(ln2_g[l]), row(ln2_b[l]), S, t0, N, out)
        x2 = out
    return x2.reshape(B, S, D)
```

```python
import functools

import jax
import jax.numpy as jnp
from jax import lax
from jax.experimental import pallas as pl
from jax.experimental.pallas import tpu as pltpu
from jax.experimental.pallas import tpu_sc as plsc

CHUNK = 64
GLA_HEADS = 4
GLA_DK = 128
GLA_DV = 256
GLA_GATE_RANK = 16
GLA_GATE_TAU = 16.0
MLA_HEADS = 8
MLA_Q_RANK = 768
MLA_KV_RANK = 256
MLA_NOPE = 128
MLA_ROPE = 64
MLA_V = 128
ROPE_THETA = 10000.0
N_EXPERTS = 256
TOP_K = 8
N_GROUPS = 8
TOPK_GROUPS = 4
D_EXPERT = 256
ROUTED_SCALE = 2.5
LN_EPS = 1e-5
RMS_EPS = 1e-6
LOG2E = 1.4426950408889634

LANES = 128
VMEM_LIMIT = 56 * 1024 * 1024
EXPERT_BLOCK = 512
EXPERT_RING = 4
MOE_FIRST_EIGHTHS = 5
COMBINE_TOKENS = 8192

F32 = jnp.float32
BF16 = jnp.bfloat16
I32 = jnp.int32


def _cparams(*sem):
    return pltpu.CompilerParams(dimension_semantics=sem, vmem_limit_bytes=VMEM_LIMIT)


def _sigmoid(x):
    return 1.0 / (1.0 + jnp.exp(-x))


def _silu(x):
    return x * _sigmoid(x)


def _dot(a, b):
    return jnp.dot(a, b, preferred_element_type=F32)


def _dot_nt(a, b):
    return lax.dot_general(a, b, (((1,), (1,)), ((), ())), preferred_element_type=F32)


def _dot_tn(a, b):
    return lax.dot_general(a, b, (((0,), (0,)), ((), ())), preferred_element_type=F32)


def _pack_bf16_pair(x):
    w = x.shape[1] // 2
    u = lax.bitcast_convert_type(x.astype(BF16).astype(F32), I32)
    lo = lax.shift_right_logical(u[:, :w], jnp.int32(16))
    hi = jnp.bitwise_and(u[:, w:], jnp.int32(-65536))
    return jnp.bitwise_or(lo, hi)


def _unpack_bf16_pair(p):
    lo = lax.bitcast_convert_type(lax.shift_left(p, jnp.int32(16)), F32)
    hi = lax.bitcast_convert_type(jnp.bitwise_and(p, jnp.int32(-65536)), F32)
    return lo, hi


def _mod_kernel(c_ref, w_ref, b_ref, o_ref):
    cond = _silu(c_ref[...]).astype(BF16)
    o_ref[...] = _dot(cond, w_ref[...].astype(BF16)) + b_ref[...]


def _modulation(c, w_ada, b_ada):
    B, D = c.shape
    W = w_ada.shape[1]
    tn = D
    return pl.pallas_call(
        _mod_kernel,
        out_shape=jax.ShapeDtypeStruct((B, W), F32),
        grid=(W // tn,),
        in_specs=[
            pl.BlockSpec((B, D), lambda j: (0, 0)),
            pl.BlockSpec((D, tn), lambda j: (0, j)),
            pl.BlockSpec((1, tn), lambda j: (0, j)),
        ],
        out_specs=pl.BlockSpec((B, tn), lambda j: (0, j)),
        compiler_params=_cparams("arbitrary"),
        name="adaln_mod",
    )(c, w_ada, b_ada.reshape(1, W))


def _rope_kernel(pos_ref, f_ref, ph_ref, cc_ref, ss_ref):
    ang = pos_ref[...] * f_ref[...]
    cc_ref[...] = jnp.cos(ang)
    ss_ref[...] = jnp.sin(ang) * ph_ref[...]


def _rope_tables(positions):
    B, S = positions.shape
    N = B * S
    half = MLA_ROPE // 2
    inv_freq = ROPE_THETA ** (-jnp.arange(half, dtype=F32) * (2.0 / MLA_ROPE))
    f4 = jnp.tile(inv_freq, LANES // half).reshape(1, LANES)
    sign = jnp.tile(jnp.concatenate([-jnp.ones((half,), F32), jnp.ones((half,), F32)]), LANES // MLA_ROPE)
    pos = jnp.broadcast_to(positions.reshape(N, 1).astype(F32), (N, LANES))
    tm = min(N, 1024)
    spec = pl.BlockSpec((tm, LANES), lambda i: (i, 0))
    vec = pl.BlockSpec((1, LANES), lambda i: (0, 0))
    return pl.pallas_call(
        _rope_kernel,
        out_shape=(jax.ShapeDtypeStruct((N, LANES), F32), jax.ShapeDtypeStruct((N, LANES), F32)),
        grid=(N // tm,),
        in_specs=[spec, vec, vec],
        out_specs=(spec, spec),
        compiler_params=_cparams("arbitrary"),
        name="rope_tables",
    )(pos, f4, sign.reshape(1, LANES))


QK_W = 2 * GLA_HEADS * GLA_DK
GV_W = GLA_HEADS * GLA_DV
TAIL_W = 3 * LANES
IN_SEG = (QK_W, GV_W, GV_W, MLA_Q_RANK, MLA_KV_RANK, 1024, 1024, TAIL_W)


def _prep_w_in(w_in, D):
    s = [GLA_HEADS * GLA_DK, GLA_HEADS * GLA_DK, GV_W, GV_W, GLA_GATE_RANK, MLA_Q_RANK, MLA_KV_RANK, MLA_ROPE, D, D]
    offs = [0]
    for n in s:
        offs.append(offs[-1] + n)
    gq, gk, gv, gr, ga, cq, ckv, kr, gate_a, gate_b = [w_in[:, offs[i]:offs[i + 1]] for i in range(10)]
    half = MLA_ROPE // 2
    kr_sw = jnp.concatenate([kr[:, half:], kr[:, :half]], axis=1)
    pad = jnp.zeros((w_in.shape[0], LANES - GLA_GATE_RANK), w_in.dtype)
    return jnp.concatenate([gq, gk, gv, gr, cq, ckv, gate_a, gate_b, kr, kr, kr_sw, kr_sw, ga, pad], axis=1).astype(BF16)


def _inproj_kernel(x_ref, mod_ref, w_ref, *out_refs):
    sh1 = mod_ref[0, 0:1, :]
    sc1 = mod_ref[0, 1:2, :]
    h = (x_ref[...] * (1.0 + sc1) + sh1).astype(BF16)
    off = 0
    for ref in out_refs:
        n = ref.shape[-1]
        ref[...] = _dot(h, w_ref[:, off:off + n]).astype(ref.dtype)
        off += n


def _in_projection(x2, mod6, w_in_p, S):
    N, D = x2.shape
    W = w_in_p.shape[1]
    tm = min(S, 512)
    per_b = S // tm
    return pl.pallas_call(
        _inproj_kernel,
        out_shape=tuple(jax.ShapeDtypeStruct((N, n), BF16) for n in IN_SEG),
        grid=(N // tm,),
        in_specs=[
            pl.BlockSpec((tm, D), lambda i: (i, 0)),
            pl.BlockSpec((1, 6, D), lambda i: (i // per_b, 0, 0)),
            pl.BlockSpec((D, W), lambda i: (0, 0), pipeline_mode=pl.Buffered(1)),
        ],
        out_specs=tuple(pl.BlockSpec((tm, n), lambda i: (i, 0)) for n in IN_SEG),
        compiler_params=_cparams("arbitrary"),
        name="in_proj",
    )(x2, mod6, w_in_p)


def _gla_kernel(qk_ref, v_ref, gr_ref, tail_ref, wa2_ref, ba_ref, gn_ref, wo_ref, y_ref,
                st_ref, kd_ref, dec_ref, sall_ref, o_ref):
    t = pl.program_id(1)

    @pl.when(t == 0)
    def _():
        st_ref[...] = jnp.zeros_like(st_ref)

    ts = qk_ref.shape[0]
    nch = ts // CHUNK
    HK = GLA_HEADS * GLA_DK
    r = lax.broadcasted_iota(I32, (CHUNK, CHUNK), 0)
    c = lax.broadcasted_iota(I32, (CHUNK, CHUNK), 1)
    tri = (r >= c).astype(BF16)
    qscale = GLA_DK ** -0.5

    z = _dot(tail_ref[:, 2 * LANES:3 * LANES], wa2_ref[...]) + ba_ref[...]
    log_a = (jnp.minimum(z, 0.0) - jnp.log(1.0 + jnp.exp(-jnp.abs(z)))) * (1.0 / GLA_GATE_TAU)
    la_hi = log_a.astype(BF16)
    la_lo = (log_a - la_hi.astype(F32)).astype(BF16)
    for n in range(nch):
        rows = slice(n * CHUNK, (n + 1) * CHUNK)
        G = _dot(tri, la_hi[rows]) + _dot(tri, la_lo[rows])
        g_end = G[CHUNK - 1:CHUNK, :]
        kd_ref[rows, :] = (qk_ref[rows, HK:2 * HK].astype(F32) * jnp.exp(g_end - G)).astype(BF16)
        dec_ref[n:n + 1, :] = jnp.exp(g_end)

    for h in range(GLA_HEADS):
        ks = slice(h * GLA_DK, (h + 1) * GLA_DK)
        vs = slice(h * GLA_DV, (h + 1) * GLA_DV)
        st = st_ref[h]
        for n in range(nch):
            rows = slice(n * CHUNK, (n + 1) * CHUNK)
            st = st * dec_ref[n:n + 1, ks] + _dot_tn(v_ref[rows, vs], kd_ref[rows, ks])
            sall_ref[n * GLA_HEADS + h] = st.astype(BF16)
        st_ref[h] = st

    for n in range(nch):
        rows = slice(n * CHUNK, (n + 1) * CHUNK)
        for h in range(GLA_HEADS):
            ks = slice(h * GLA_DK, (h + 1) * GLA_DK)
            vs = slice(h * GLA_DV, (h + 1) * GLA_DV)
            qh = (qk_ref[rows, ks].astype(F32) * qscale).astype(BF16)
            o = _dot_nt(qh, sall_ref[n * GLA_HEADS + h])
            o = o * lax.rsqrt(jnp.mean(o * o, axis=-1, keepdims=True) + RMS_EPS) * gn_ref[...]
            o_ref[rows, vs] = (o * _silu(gr_ref[rows, vs].astype(F32))).astype(BF16)
    y_ref[...] = _dot(o_ref[...], wo_ref[...]).astype(y_ref.dtype)


def _gla(qk, gv, gr, tail, wa2_p, b_a, g_norm, w_o, B, S):
    N = B * S
    D = w_o.shape[1]
    ts = min(S, 512)
    per_b = S // ts
    HK = GLA_HEADS * GLA_DK
    tok = lambda n: pl.BlockSpec((ts, n), lambda b, t: (b * per_b + t, 0))
    full = lambda a: pl.BlockSpec(a.shape, lambda b, t: (0,) * a.ndim)
    return pl.pallas_call(
        _gla_kernel,
        out_shape=jax.ShapeDtypeStruct((N, D), BF16),
        grid=(B, per_b),
        in_specs=[tok(QK_W), tok(GV_W), tok(GV_W), tok(TAIL_W), full(wa2_p), full(b_a), full(g_norm), full(w_o)],
        out_specs=tok(D),
        scratch_shapes=[pltpu.VMEM((GLA_HEADS, GLA_DV, GLA_DK), F32),
                        pltpu.VMEM((ts, GLA_HEADS * GLA_DK), BF16),
                        pltpu.VMEM((ts // CHUNK, GLA_HEADS * GLA_DK), F32),
                        pltpu.VMEM((ts // CHUNK * GLA_HEADS, GLA_DV, GLA_DK), BF16),
                        pltpu.VMEM((ts, GV_W), BF16)],
        compiler_params=_cparams("arbitrary", "arbitrary"),
        name="gla",
    )(qk, gv, gr, tail, wa2_p, b_a, g_norm, w_o)


HQ = MLA_HEADS * 2 * LANES


def _prep_w_uq(w_uq):
    dh = MLA_NOPE + MLA_ROPE
    half = MLA_ROPE // 2
    nope = [w_uq[:, h * dh:h * dh + MLA_NOPE] for h in range(MLA_HEADS)]
    rope = [w_uq[:, h * dh + MLA_NOPE:(h + 1) * dh] for h in range(MLA_HEADS)]
    rope_sw = [jnp.concatenate([r[:, half:], r[:, :half]], axis=1) for r in rope]
    return jnp.concatenate(nope + rope + rope_sw, axis=1).astype(BF16)


def _prep_w_ukv(w_ukv):
    dh = MLA_NOPE + MLA_V
    kn = [w_ukv[:, h * dh:h * dh + MLA_NOPE] for h in range(MLA_HEADS)]
    vv = [w_ukv[:, h * dh + MLA_NOPE:(h + 1) * dh] for h in range(MLA_HEADS)]
    return jnp.concatenate(kn + vv, axis=1).astype(BF16)


def _rms(x, g):
    return x * lax.rsqrt(jnp.mean(x * x, axis=-1, keepdims=True) + RMS_EPS) * g


def _mla_prep_kernel(cq_ref, ckv_ref, tail_ref, cc_ref, ss_ref, gq_ref, wq_ref, gkv_ref, wkv_ref, q_ref, k_ref, v_ref):
    tm = cq_ref.shape[0]
    NP = MLA_HEADS * MLA_NOPE
    RP = MLA_HEADS * MLA_ROPE
    scale = (MLA_NOPE + MLA_ROPE) ** -0.5 * LOG2E
    cc = cc_ref[...]
    ss = ss_ref[...]
    cqn = _rms(cq_ref[...].astype(F32), gq_ref[...]).astype(BF16)
    qf = _dot(cqn, wq_ref[...]) * scale
    ckvn = _rms(ckv_ref[...].astype(F32), gkv_ref[...]).astype(BF16)
    kv = _dot(ckvn, wkv_ref[...])
    krr = (tail_ref[:, 0:LANES].astype(F32) * cc + tail_ref[:, LANES:2 * LANES].astype(F32) * ss).astype(BF16)
    lane = lax.broadcasted_iota(I32, (tm, LANES), 1)
    first = lane < MLA_ROPE
    for j in range(MLA_HEADS // 2):
        a = NP + j * LANES
        rot = qf[:, a:a + LANES] * cc + qf[:, a + RP:a + RP + LANES] * ss
        for h, keep in ((2 * j, first), (2 * j + 1, jnp.logical_not(first))):
            base = h * 2 * LANES
            q_ref[:, base:base + LANES] = qf[:, h * MLA_NOPE:(h + 1) * MLA_NOPE].astype(BF16)
            q_ref[:, base + LANES:base + 2 * LANES] = jnp.where(keep, rot, 0.0).astype(BF16)
            k_ref[:, base:base + LANES] = kv[:, h * MLA_NOPE:(h + 1) * MLA_NOPE].astype(BF16)
            k_ref[:, base + LANES:base + 2 * LANES] = krr
    v_ref[...] = kv[:, NP:].astype(BF16)


def _mla_prep(cq, ckv, tail, cc, ss, g_cq, w_uq_p, g_ckv, w_ukv_p):
    N = cq.shape[0]
    tm = min(N, 512)
    tok = lambda n: pl.BlockSpec((tm, n), lambda i: (i, 0))
    full = lambda a: pl.BlockSpec(a.shape, lambda i: (0,) * a.ndim)
    HV = MLA_HEADS * MLA_V
    return pl.pallas_call(
        _mla_prep_kernel,
        out_shape=(jax.ShapeDtypeStruct((N, HQ), BF16), jax.ShapeDtypeStruct((N, HQ), BF16),
                   jax.ShapeDtypeStruct((N, HV), BF16)),
        grid=(N // tm,),
        in_specs=[tok(MLA_Q_RANK), tok(MLA_KV_RANK), tok(TAIL_W), tok(LANES), tok(LANES),
                  full(g_cq), full(w_uq_p), full(g_ckv), full(w_ukv_p)],
        out_specs=(tok(HQ), tok(HQ), tok(HV)),
        compiler_params=_cparams("arbitrary"),
        name="mla_prep",
    )(cq, ckv, tail, cc, ss, g_cq, w_uq_p, g_ckv, w_ukv_p)


ATTN_TQ = 256


def _mla_attn_kernel(q_ref, k_ref, v_ref, o_ref, v1_ref):
    S = q_ref.shape[0]
    tq = min(S, ATTN_TQ)
    r = lax.broadcasted_iota(I32, (tq, tq), 0) // CHUNK
    c = lax.broadcasted_iota(I32, (tq, tq), 1) // CHUNK
    diag_mask = c <= r
    v1_ref[:, :MLA_V] = v_ref[...]
    v1_ref[:, MLA_V:] = jnp.ones((S, MLA_V), BF16)

    def scores(ii):
        l0 = ii * tq
        q = q_ref[l0:l0 + tq, :]
        sd = jnp.where(diag_mask, _dot_nt(q, k_ref[l0:l0 + tq, :]), -jnp.inf)
        so = _dot_nt(q, k_ref[0:l0, :]) if ii > 0 else None
        return sd, so

    def finish(ii, sd, so):
        l0 = ii * tq
        m = jnp.max(sd, axis=-1, keepdims=True)
        if so is not None:
            m = jnp.maximum(m, jnp.max(so, axis=-1, keepdims=True))
        acc = _dot(jnp.exp2((sd - m).astype(BF16)), v1_ref[l0:l0 + tq, :])
        if so is not None:
            acc = acc + _dot(jnp.exp2((so - m).astype(BF16)), v1_ref[0:l0, :])
        o_ref[l0:l0 + tq, :] = (acc[:, :MLA_V] / acc[:, MLA_V:]).astype(BF16)

    n_tiles = S // tq
    ahead = 2
    pending = [scores(ii) for ii in range(min(ahead, n_tiles))]
    for ii in range(n_tiles):
        if ii + ahead < n_tiles:
            pending.append(scores(ii + ahead))
        finish(ii, *pending.pop(0))


def _mla_attention(qc, kc, vv, B, S):
    N = B * S
    HV = MLA_HEADS * MLA_V
    return pl.pallas_call(
        _mla_attn_kernel,
        out_shape=jax.ShapeDtypeStruct((N, HV), BF16),
        grid=(B, MLA_HEADS),
        in_specs=[
            pl.BlockSpec((S, 2 * LANES), lambda b, h: (b, h)),
            pl.BlockSpec((S, 2 * LANES), lambda b, h: (b, h)),
            pl.BlockSpec((S, MLA_V), lambda b, h: (b, h)),
        ],
        out_specs=pl.BlockSpec((S, MLA_V), lambda b, h: (b, h)),
        scratch_shapes=[pltpu.VMEM((S, 2 * MLA_V), BF16)],
        compiler_params=_cparams("arbitrary", "arbitrary"),
        name="mla_attn",
    )(qc, kc, vv)


def _layer_norm(u, g, b):
    mu = jnp.mean(u, axis=-1, keepdims=True)
    d = u - mu
    var = jnp.mean(d * d, axis=-1, keepdims=True)
    return d * lax.rsqrt(var + LN_EPS) * g + b


def _merge_kernel(alpha, x_ref, ya_ref, ob_ref, ga_ref, gb_ref, mod_ref, wmo_ref, wout_ref, g_ref, b_ref, wr_ref,
                  br_ref, ws13_ref, ws2_ref, x1_ref, h2_ref, sh_ref, idx_ref, wt_ref, rank_ref, cnt_ref, lg_s, carry_s):
    i = pl.program_id(0)

    @pl.when(i == 0)
    def _():
        lg_s[...] = jnp.zeros_like(lg_s)
        carry_s[...] = jnp.zeros_like(carry_s)

    idx, wt, rank, total = _route_math(lg_s[...], br_ref[...], carry_s[...])
    total = jnp.where(i > 0, total, 0.0)
    idx_ref[...] = idx
    wt_ref[...] = wt
    rank_ref[...] = rank
    carry_s[...] = total
    cnt_ref[...] = jnp.broadcast_to(total, cnt_ref.shape).astype(I32)

    gt1 = mod_ref[0, 2:3, :]
    sh2 = mod_ref[0, 3:4, :]
    sc2 = mod_ref[0, 4:5, :]
    y = (_sigmoid(ga_ref[...].astype(F32)) * ya_ref[...].astype(F32)
         + _sigmoid(gb_ref[...].astype(F32)) * _dot(ob_ref[...], wmo_ref[...]))
    mix = _dot(y.astype(BF16), wout_ref[...])
    x1 = _layer_norm(alpha * x_ref[...] + (1.0 + gt1) * mix, g_ref[...], b_ref[...])
    x1_ref[...] = x1
    h2 = x1 * (1.0 + sc2) + sh2
    h2_ref[...] = _pack_bf16_pair(h2)
    h2b = h2.astype(BF16)
    lg_s[...] = _dot_nt(wr_ref[...], h2b)
    F = ws2_ref.shape[0]
    ab = _dot(h2b, ws13_ref[...])
    hid = (_silu(ab[:, :F]) * ab[:, F:]).astype(BF16)
    sh_ref[...] = _dot(hid, ws2_ref[...]).astype(sh_ref.dtype)


def _merge_route(alpha, x2, y_a, o_b, gate_a, gate_b, mod6, w_mla_o, w_out, ln_g, ln_b, wr_t, b_router, ws13, ws2,
                 S, tok0, Np):
    N, D = x2.shape
    E = wr_t.shape[0]
    tm = min(S, 512)
    per_b = S // tm
    assert Np % tm == 0 and tok0 % tm == 0
    n = Np // tm
    base = tok0 // tm
    cur = lambda i: jnp.minimum(i, n - 1)
    prev = lambda i: jnp.maximum(i - 1, 0)
    tok_in = lambda w: pl.BlockSpec((tm, w), lambda i: (base + cur(i), 0))
    tok = lambda w: pl.BlockSpec((tm, w), lambda i: (cur(i), 0))
    full = lambda a: pl.BlockSpec(a.shape, lambda i: (0,) * a.ndim)
    lane_blk = pl.BlockSpec((TOP_K, tm), lambda i: (0, prev(i)))
    br = b_router.reshape(E, 1).astype(F32)
    return pl.pallas_call(
        functools.partial(_merge_kernel, alpha),
        out_shape=(jax.ShapeDtypeStruct((Np, D), F32), jax.ShapeDtypeStruct((Np, D // 2), I32),
                   jax.ShapeDtypeStruct((Np, D), BF16),
                   jax.ShapeDtypeStruct((TOP_K, Np), I32), jax.ShapeDtypeStruct((TOP_K, Np), I32),
                   jax.ShapeDtypeStruct((TOP_K, Np), I32), jax.ShapeDtypeStruct((E, LANES), I32)),
        grid=(n + 1,),
        in_specs=[tok_in(D), tok_in(D), tok_in(D), tok_in(D), tok_in(D),
                  pl.BlockSpec((1, 6, D), lambda i: ((base + cur(i)) // per_b, 0, 0)),
                  full(w_mla_o), full(w_out), full(ln_g), full(ln_b), full(wr_t), full(br), full(ws13), full(ws2)],
        out_specs=(tok(D), tok(D // 2), tok(D), lane_blk, lane_blk, lane_blk,
                   pl.BlockSpec((E, LANES), lambda i: (0, 0))),
        scratch_shapes=[pltpu.VMEM((E, tm), F32), pltpu.VMEM((E, 1), F32)],
        compiler_params=_cparams("arbitrary"),
        name="merge_ln1_route",
    )(x2, y_a, o_b, gate_a, gate_b, mod6, w_mla_o, w_out, ln_g, ln_b, wr_t, br, ws13, ws2)


def _first_argmax(v, io, n):
    m = jnp.max(v, axis=0, keepdims=True)
    idx = jnp.min(jnp.where(v == m, io, n), axis=0, keepdims=True)
    return m, idx


def _route_math(lg, br, carry):
    E, T = lg.shape
    gsz = E // N_GROUPS
    neg = -jnp.inf
    s = _sigmoid(lg)
    biased = s + br
    eio = lax.broadcasted_iota(I32, (E, T), 0)
    gio = lax.broadcasted_iota(I32, (gsz, T), 0)

    gs = []
    for g in range(N_GROUPS):
        blk = biased[g * gsz:(g + 1) * gsz]
        m1, i1 = _first_argmax(blk, gio, gsz)
        m2 = jnp.max(jnp.where(gio == i1, neg, blk), axis=0, keepdims=True)
        gs.append(m1 + m2)
    cur = jnp.concatenate(gs, axis=0)
    nio = lax.broadcasted_iota(I32, (N_GROUPS, T), 0)
    gsel = jnp.zeros((N_GROUPS, T), F32)
    for _ in range(TOPK_GROUPS):
        _, gi = _first_argmax(cur, nio, N_GROUPS)
        hit = nio == gi
        gsel = jnp.where(hit, 1.0, gsel)
        cur = jnp.where(hit, neg, cur)
    emask = jnp.concatenate([jnp.broadcast_to(gsel[g:g + 1], (gsz, T)) for g in range(N_GROUPS)], axis=0) > 0.0

    cur = jnp.where(emask, biased, neg)
    idxs, ws = [], []
    sel = jnp.zeros((E, T), F32)
    for _ in range(TOP_K):
        _, ei = _first_argmax(cur, eio, E)
        hit = eio == ei
        idxs.append(ei)
        ws.append(jnp.sum(jnp.where(hit, s, 0.0), axis=0, keepdims=True))
        sel = jnp.where(hit, 1.0, sel)
        cur = jnp.where(hit, neg, cur)
    w = jnp.concatenate(ws, axis=0)
    w = w / jnp.sum(w, axis=0, keepdims=True) * ROUTED_SCALE
    wbits = lax.bitcast_convert_type(w.astype(BF16).astype(F32), I32)
    wword = jnp.bitwise_or(lax.shift_right_logical(wbits, jnp.int32(16)), jnp.bitwise_and(wbits, jnp.int32(-65536)))

    r = lax.broadcasted_iota(I32, (T, T), 0)
    c = lax.broadcasted_iota(I32, (T, T), 1)
    before = (r < c).astype(BF16)
    cnt = _dot(sel.astype(BF16), before) + carry
    ranks = [jnp.sum(jnp.where(eio == idxs[k], cnt, 0.0), axis=0, keepdims=True) for k in range(TOP_K)]
    total = cnt[:, T - 1:T] + sel[:, T - 1:T]
    return jnp.concatenate(idxs, axis=0), wword, jnp.concatenate(ranks, axis=0).astype(I32), total


def _dest_kernel(idx_ref, rank_ref, ps_ref, dest_ref):
    K, T = idx_ref.shape
    E = ps_ref.shape[0]
    eio = lax.broadcasted_iota(I32, (E, T), 0)
    ps = ps_ref[...]
    rows = [jnp.sum(jnp.where(eio == idx_ref[k:k + 1, :], ps, 0), axis=0, keepdims=True) for k in range(K)]
    dest_ref[...] = jnp.concatenate(rows, axis=0) + rank_ref[...]


def _dest_slots(idx, rank, pad_start):
    K, N = idx.shape
    E = pad_start.shape[0]
    T = min(N, 512)
    spec = pl.BlockSpec((K, T), lambda i: (0, i))
    return pl.pallas_call(
        _dest_kernel,
        out_shape=jax.ShapeDtypeStruct((K, N), I32),
        grid=(N // T,),
        in_specs=[spec, spec, pl.BlockSpec((E, 1), lambda i: (0, 0))],
        out_specs=spec,
        compiler_params=_cparams("arbitrary"),
        name="dest_slots",
    )(idx, rank, pad_start.reshape(E, 1))


SC_ROWS = 64


def _sc_workers():
    info = plsc.get_sparse_core_info()
    return info.num_cores, info.num_subcores


def _dispatch_rows(h2p, dest, n_slots):
    N, W = h2p.shape
    K = dest.shape[0]
    R = SC_ROWS
    nc, ns = _sc_workers()
    per_w = N // R // (nc * ns)
    assert per_w * R * nc * ns == N and per_w % 2 == 0, "token count must split into chunk pairs per subcore"
    mesh = plsc.VectorSubcoreMesh(core_axis_name="c", subcore_axis_name="s")

    @functools.partial(
        pl.kernel, mesh=mesh,
        out_type=jax.ShapeDtypeStruct((n_slots, W), I32),
        scratch_types=[pltpu.VMEM((K, R), I32), pltpu.VMEM((K, R), I32),
                       pltpu.VMEM((R, W), I32), pltpu.VMEM((R, W), I32),
                       pltpu.SemaphoreType.DMA, pltpu.SemaphoreType.DMA, pltpu.SemaphoreType.DMA],
    )
    def k(h_hbm, d_hbm, xs_hbm, idx0, idx1, rows0, rows1, lsem0, lsem1, ssem):
        idx, rows, lsem = (idx0, idx1), (rows0, rows1), (lsem0, lsem1)
        base = (lax.axis_index("s") * nc + lax.axis_index("c")) * per_w

        def loads(ch, b):
            tok = pl.ds(pl.multiple_of(ch * R, 8), R)
            return [pltpu.make_async_copy(d_hbm.at[kk, tok], idx[b].at[kk], lsem[b]) for kk in range(K)] + [
                pltpu.make_async_copy(h_hbm.at[tok], rows[b], lsem[b])]

        for cp in loads(base, 0):
            cp.start()
        for cp in loads(base, 0):
            cp.wait()

        @pl.loop(0, per_w, step=2)
        def _(j):
            for b in range(2):
                ch = base + j + b
                more = j + b + 1 < per_w

                @pl.when(more)
                def _():
                    for cp in loads(ch + 1, 1 - b):
                        cp.start()

                scatters = [pltpu.make_async_copy(rows[b], xs_hbm.at[idx[b].at[kk]], ssem) for kk in range(K)]
                for cp in scatters:
                    cp.start()
                for cp in scatters:
                    cp.wait()

                @pl.when(more)
                def _():
                    for cp in loads(ch + 1, 1 - b):
                        cp.wait()

    return k(h2p, dest)


COMBINE_ROWS = 8


def _combine_sum(ys, dest, ww, tok0, n_tok):
    W = ys.shape[1]
    K = dest.shape[0]
    R = COMBINE_ROWS
    L = 16
    nc, ns = _sc_workers()
    per_w = n_tok // R // (nc * ns)
    assert per_w * R * nc * ns == n_tok and per_w % 2 == 0, "token count must split into chunk pairs per subcore"
    T = per_w * R
    mesh = plsc.VectorSubcoreMesh(core_axis_name="c", subcore_axis_name="s")

    @functools.partial(
        pl.kernel, mesh=mesh, compiler_params=pltpu.CompilerParams(needs_layout_passes=False),
        out_type=jax.ShapeDtypeStruct((n_tok, W), I32),
        scratch_types=[pltpu.VMEM((K * T,), I32), pltpu.VMEM((K * T,), I32),
                       pltpu.VMEM((K, R, W), I32), pltpu.VMEM((K, R, W), I32),
                       pltpu.VMEM((R, W), I32), pltpu.VMEM((R, W), I32),
                       pltpu.SemaphoreType.DMA, pltpu.SemaphoreType.DMA,
                       pltpu.SemaphoreType.DMA, pltpu.SemaphoreType.DMA],
    )
    def k(ys_hbm, d_hbm, w_hbm, o_hbm, idx_all, w_all, buf0, buf1, out0, out1, gsem0, gsem1, osem0, osem1):
        buf, outv, gsem, osem = (buf0, buf1), (out0, out1), (gsem0, gsem1), (osem0, osem1)
        wid = lax.axis_index("s") * nc + lax.axis_index("c")
        t0 = pl.multiple_of(tok0 + wid * T, 8)
        for kk in range(K):
            pltpu.sync_copy(d_hbm.at[kk, pl.ds(t0, T)], idx_all.at[pl.ds(kk * T, T)])
            pltpu.sync_copy(w_hbm.at[kk, pl.ds(t0, T)], w_all.at[pl.ds(kk * T, T)])

        def fetch(j, b):
            return [pltpu.make_async_copy(
                ys_hbm.at[idx_all.at[pl.ds(pl.multiple_of(kk * T + j * R, 8), R)]], buf[b].at[kk], gsem[b])
                for kk in range(K)]

        def store(j, b):
            return pltpu.make_async_copy(outv[b], o_hbm.at[pl.ds(pl.multiple_of(wid * T + j * R, 8), R)], osem[b])

        for cp in fetch(0, 0):
            cp.start()

        @pl.loop(0, per_w, step=2)
        def _(j0):
            for b in range(2):
                j = j0 + b

                @pl.when(j + 1 < per_w)
                def _():
                    for cp in fetch(j + 1, 1 - b):
                        cp.start()

                for cp in fetch(j, b):
                    cp.wait()

                @pl.when(j >= 2)
                def _():
                    store(j - 2, b).wait()

                @pl.loop(0, R)
                def _(r):
                    ws = [plsc.bitcast(plsc.load_gather(w_all, [jnp.full((L,), kk * T + j * R + r, I32)]), BF16)
                          for kk in range(K)]

                    @pl.loop(0, W // L)
                    def _(q):
                        sl = pl.ds(q * L, L)
                        terms = [ws[kk] * plsc.bitcast(buf[b][kk, r, sl], BF16) for kk in range(K)]
                        while len(terms) > 1:
                            terms = [terms[i] + terms[i + 1] for i in range(0, len(terms), 2)]
                        outv[b][r, sl] = plsc.bitcast(terms[0], I32)

                store(j, b).start()

        store(per_w - 2, 0).wait()
        store(per_w - 1, 1).wait()

    return k(ys, dest, ww)


def _expert_kernel(b0_ref, nb_ref, cnt_ref, xs_hbm, w1_ref, w3_ref, w2_ref, ys_hbm,
                   w13_s, w2_s, xbuf, ybuf, sem_in, sem_out):
    e = pl.program_id(0)
    n_exp = pl.num_programs(0)
    F = w1_ref.shape[2]
    bm = xbuf.shape[1]
    half = xbuf.shape[2]
    nb = nb_ref[e]
    b0 = b0_ref[e]
    total = b0_ref[n_exp - 1] + nb_ref[n_exp - 1]

    nbuf = xbuf.shape[0]

    def in_copy(g):
        slot = g % nbuf
        return pltpu.make_async_copy(xs_hbm.at[pl.ds(g * bm, bm)], xbuf.at[slot], sem_in.at[slot])

    def out_copy(g):
        slot = g % nbuf
        return pltpu.make_async_copy(ybuf.at[slot], ys_hbm.at[pl.ds(g * bm, bm)], sem_out.at[slot])

    @pl.when(e == 0)
    def _():
        for g0 in range(nbuf - 1):
            @pl.when(g0 < total)
            def _():
                in_copy(g0).start()

    @pl.when(nb > 0)
    def _():
        w13_s[:, :F] = w1_ref[0].astype(BF16)
        w13_s[:, F:] = w3_ref[0].astype(BF16)
        w2_s[...] = w2_ref[0].astype(BF16)

    def block(j, carry):
        g = b0 + j
        slot = g % nbuf
        in_copy(g).wait()

        @pl.when(g + nbuf - 1 < total)
        def _():
            in_copy(g + nbuf - 1).start()

        @pl.when(g >= nbuf)
        def _():
            out_copy(g - nbuf).wait()

        def swiglu(rows):
            lo, hi = _unpack_bf16_pair(xbuf[slot, :rows, :])
            ab = _dot(lo.astype(BF16), w13_s[:half, :]) + _dot(hi.astype(BF16), w13_s[half:, :])
            hid = (_silu(ab[:, :F]) * ab[:, F:]).astype(BF16)
            ybuf[slot, :rows, :] = _pack_bf16_pair(_dot(hid, w2_s[...]))

        rows_left = cnt_ref[e] - j * bm

        @pl.when(rows_left > bm // 2)
        def _():
            swiglu(bm)

        @pl.when(rows_left <= bm // 2)
        def _():
            swiglu(bm // 2)
            ybuf[slot, bm // 2:, :] = jnp.zeros((bm - bm // 2, half), I32)

        out_copy(g).start()
        return carry

    lax.fori_loop(0, nb, block, 0)

    @pl.when(e == n_exp - 1)
    def _():
        for back in range(nbuf, 0, -1):
            @pl.when(total >= back)
            def _():
                out_copy(total - back).wait()


def _experts(xs, blk0, nblk, counts, w_e1, w_e3, w_e2):
    n_slots, W = xs.shape
    E, D, F = w_e1.shape
    bm = EXPERT_BLOCK
    grid_spec = pltpu.PrefetchScalarGridSpec(
        num_scalar_prefetch=3,
        grid=(E,),
        in_specs=[
            pl.BlockSpec(memory_space=pl.ANY),
            pl.BlockSpec((1, D, F), lambda e, b0, nb, cnt: (e, 0, 0)),
            pl.BlockSpec((1, D, F), lambda e, b0, nb, cnt: (e, 0, 0)),
            pl.BlockSpec((1, F, D), lambda e, b0, nb, cnt: (e, 0, 0)),
        ],
        out_specs=pl.BlockSpec(memory_space=pl.ANY),
        scratch_shapes=[pltpu.VMEM((D, 2 * F), BF16), pltpu.VMEM((F, D), BF16),
                        pltpu.VMEM((EXPERT_RING, bm, W), I32), pltpu.VMEM((EXPERT_RING, bm, W), I32),
                        pltpu.SemaphoreType.DMA((EXPERT_RING,)), pltpu.SemaphoreType.DMA((EXPERT_RING,))],
    )
    return pl.pallas_call(
        _expert_kernel,
        out_shape=jax.ShapeDtypeStruct((n_slots, W), I32),
        grid_spec=grid_spec,
        compiler_params=_cparams("arbitrary"),
        name="experts",
    )(blk0, nblk, counts, xs, w_e1, w_e3, w_e2)


def _final_kernel(alpha, ms_ref, sh_ref, x1_ref, mod_ref, g_ref, b_ref, *rest):
    o_ref = rest[-1]
    gt2 = mod_ref[0, 5:6, :]
    lo, hi = _unpack_bf16_pair(ms_ref[...])
    ffn = jnp.concatenate([lo, hi], axis=1) + sh_ref[...].astype(F32)
    o_ref[...] = _layer_norm(alpha * x1_ref[...] + (1.0 + gt2) * ffn, g_ref[...], b_ref[...])


def _final_part(alpha, ms, sh, x1, mod6, ln_g, ln_b, S, tok0, row0, n_total, prev):
    D = x1.shape[1]
    N = n_total
    n_part, W = ms.shape
    tm = min(S, 512, n_part)
    assert n_part % tm == 0 and S % tm == 0 and tok0 % tm == 0 and row0 % tm == 0
    per_b = S // tm
    steps = n_part // tm
    off = tok0 // tm
    goff = row0 // tm
    tok = lambda n: pl.BlockSpec((tm, n), lambda i: (off + i, 0))
    full = lambda a: pl.BlockSpec(a.shape, lambda i: (0,) * a.ndim)
    in_specs = [pl.BlockSpec((tm, W), lambda i: (i, 0)), tok(D), tok(D),
                pl.BlockSpec((1, 6, D), lambda i: ((goff + i) // per_b, 0, 0)), full(ln_g), full(ln_b)]
    args = [ms, sh, x1, mod6, ln_g, ln_b]
    aliases = {}
    if prev is not None:
        in_specs.append(pl.BlockSpec(memory_space=pl.ANY))
        args.append(prev)
        aliases = {len(args) - 1: 0}
    return pl.pallas_call(
        functools.partial(_final_kernel, alpha),
        out_shape=jax.ShapeDtypeStruct((N, D), F32),
        grid=(steps,),
        in_specs=in_specs,
        out_specs=pl.BlockSpec((tm, D), lambda i: (goff + i, 0)),
        input_output_aliases=aliases,
        compiler_params=_cparams("arbitrary"),
        name="combine_ln2",
    )(*args)


def _moe(alpha, h2p, sh, idx, ww, rank, cnt, x1, mod6, w_e1, w_e3, w_e2, ln_g, ln_b, S, row0, n_total, out):
    N = x1.shape[0]
    E = w_e1.shape[0]
    bm = EXPERT_BLOCK
    counts = cnt[:, 0]
    padded = (counts + bm - 1) // bm * bm
    pad_end = jnp.cumsum(padded)
    pad_start = pad_end - padded
    n_slots = N * TOP_K + E * bm
    dest = _dest_slots(idx, rank, pad_start.astype(I32))
    xs = _dispatch_rows(h2p, dest, n_slots)
    ys = _experts(xs, (pad_start // bm).astype(I32), (padded // bm).astype(I32), counts.astype(I32),
                  w_e1, w_e3, w_e2)
    n_part = min(N, COMBINE_TOKENS) if row0 + N == n_total else N
    assert N % n_part == 0
    for p in range(N // n_part):
        tok0 = p * n_part
        ms = _combine_sum(ys, dest, ww, tok0, n_part)
        out = _final_part(alpha, ms, sh, x1, mod6, ln_g, ln_b, S, tok0, row0 + tok0, n_total, out)
    return out


def kernel(x, c, positions, w_ada, b_ada, w_in, w_gla_a2, b_gla_a, g_gla_norm, w_gla_o, g_cq, w_uq, g_ckv, w_ukv, w_mla_o, w_out, ln1_g, ln1_b, w_router, b_router, w_e1, w_e3, w_e2, w_s1, w_s3, w_s2, ln2_g, ln2_b):
    B, S, D = x.shape
    N = B * S
    depth = w_ada.shape[0]
    alpha = (2.0 * depth) ** 0.25
    row = lambda a: a.reshape(1, -1)
    cc, ss = _rope_tables(positions)
    x2 = x.reshape(N, D)
    for l in range(depth):
        mod6 = _modulation(c, w_ada[l], b_ada[l]).reshape(B, 6, D)
        qk, gv, gr, cq, ckv, gate_a, gate_b, tail = _in_projection(x2, mod6, _prep_w_in(w_in[l], D), S)
        wa2_p = jnp.concatenate(
            [w_gla_a2[l], jnp.zeros((LANES - GLA_GATE_RANK, w_gla_a2.shape[2]), F32)], axis=0).astype(BF16)
        y_a = _gla(qk, gv, gr, tail, wa2_p, row(b_gla_a[l]), row(g_gla_norm[l]), w_gla_o[l].astype(BF16), B, S)
        qc, kc, vv = _mla_prep(cq, ckv, tail, cc, ss, row(g_cq[l]), _prep_w_uq(w_uq[l]),
                               row(g_ckv[l]), _prep_w_ukv(w_ukv[l]))
        o_b = _mla_attention(qc, kc, vv, B, S)
        ws13 = jnp.concatenate([w_s1[l], w_s3[l]], axis=1).astype(BF16)
        ws2 = w_s2[l].astype(BF16)
        b_first = max(1, (B * MOE_FIRST_EIGHTHS) // 8) if B > 1 else B
        ranges = [(0, b_first * S)] + ([(b_first * S, (B - b_first) * S)] if B > b_first else [])
        routed =[_merge_route(alpha, x2, y_a, o_b, gate_a, gate_b, mod6, w_mla_o[l].astype(BF16),
                               w_out[l].astype(BF16), row(ln1_g[l]), row(ln1_b[l]), w_router[l].T.astype(BF16),
                               b_router[l], ws13, ws2, S, t0, n) for t0, n in ranges]
        out = None
        for (t0, n), (x1, h2p, sh, idx, ww, rank, cnt) in zip(ranges, routed):
            out = _moe(alpha, h2p, sh, idx, ww, rank, cnt, x1, mod6, w_e1[l], w_e3[l], w_e2[l],
                       row(ln2_g[l]), row(ln2_b[l]), S, t0, N, out)
        x2 = out
    return x2.reshape(B, S, D)
```

```python
import functools

import jax
import jax.numpy as jnp
from jax import lax
from jax.experimental import pallas as pl
from jax.experimental.pallas import tpu as pltpu
from jax.experimental.pallas import tpu_sc as plsc

CHUNK = 64
GLA_HEADS = 4
GLA_DK = 128
GLA_DV = 256
GLA_GATE_RANK = 16
GLA_GATE_TAU = 16.0
MLA_HEADS = 8
MLA_Q_RANK = 768
MLA_KV_RANK = 256
MLA_NOPE = 128
MLA_ROPE = 64
MLA_V = 128
ROPE_THETA = 10000.0
N_EXPERTS = 256
TOP_K = 8
N_GROUPS = 8
TOPK_GROUPS = 4
D_EXPERT = 256
ROUTED_SCALE = 2.5
LN_EPS = 1e-5
RMS_EPS = 1e-6
LOG2E = 1.4426950408889634

LANES = 128
VMEM_LIMIT = 56 * 1024 * 1024
EXPERT_BLOCK = 512
EXPERT_RING = 4
MOE_FIRST_EIGHTHS = 5
COMBINE_TOKENS = 8192

F32 = jnp.float32
BF16 = jnp.bfloat16
I32 = jnp.int32


def _cparams(*sem):
    return pltpu.CompilerParams(dimension_semantics=sem, vmem_limit_bytes=VMEM_LIMIT)


def _sigmoid(x):
    return 1.0 / (1.0 + jnp.exp(-x))


def _silu(x):
    return x * _sigmoid(x)


def _dot(a, b):
    return jnp.dot(a, b, preferred_element_type=F32)


def _dot_nt(a, b):
    return lax.dot_general(a, b, (((1,), (1,)), ((), ())), preferred_element_type=F32)


def _dot_tn(a, b):
    return lax.dot_general(a, b, (((0,), (0,)), ((), ())), preferred_element_type=F32)


def _pack_bf16_pair(x):
    w = x.shape[1] // 2
    u = lax.bitcast_convert_type(x.astype(BF16).astype(F32), I32)
    lo = lax.shift_right_logical(u[:, :w], jnp.int32(16))
    hi = jnp.bitwise_and(u[:, w:], jnp.int32(-65536))
    return jnp.bitwise_or(lo, hi)


def _unpack_bf16_pair(p):
    lo = lax.bitcast_convert_type(lax.shift_left(p, jnp.int32(16)), F32)
    hi = lax.bitcast_convert_type(jnp.bitwise_and(p, jnp.int32(-65536)), F32)
    return lo, hi


def _mod_kernel(c_ref, w_ref, b_ref, o_ref):
    cond = _silu(c_ref[...]).astype(BF16)
    o_ref[...] = _dot(cond, w_ref[...].astype(BF16)) + b_ref[...]


def _modulation(c, w_ada, b_ada):
    B, D = c.shape
    W = w_ada.shape[1]
    tn = D
    return pl.pallas_call(
        _mod_kernel,
        out_shape=jax.ShapeDtypeStruct((B, W), F32),
        grid=(W // tn,),
        in_specs=[
            pl.BlockSpec((B, D), lambda j: (0, 0)),
            pl.BlockSpec((D, tn), lambda j: (0, j)),
            pl.BlockSpec((1, tn), lambda j: (0, j)),
        ],
        out_specs=pl.BlockSpec((B, tn), lambda j: (0, j)),
        compiler_params=_cparams("arbitrary"),
        name="adaln_mod",
    )(c, w_ada, b_ada.reshape(1, W))


def _rope_kernel(pos_ref, f_ref, ph_ref, cc_ref, ss_ref):
    ang = pos_ref[...] * f_ref[...]
    cc_ref[...] = jnp.cos(ang)
    ss_ref[...] = jnp.sin(ang) * ph_ref[...]


def _rope_tables(positions):
    B, S = positions.shape
    N = B * S
    half = MLA_ROPE // 2
    inv_freq = ROPE_THETA ** (-jnp.arange(half, dtype=F32) * (2.0 / MLA_ROPE))
    f4 = jnp.tile(inv_freq, LANES // half).reshape(1, LANES)
    sign = jnp.tile(jnp.concatenate([-jnp.ones((half,), F32), jnp.ones((half,), F32)]), LANES // MLA_ROPE)
    pos = jnp.broadcast_to(positions.reshape(N, 1).astype(F32), (N, LANES))
    tm = min(N, 1024)
    spec = pl.BlockSpec((tm, LANES), lambda i: (i, 0))
    vec = pl.BlockSpec((1, LANES), lambda i: (0, 0))
    return pl.pallas_call(
        _rope_kernel,
        out_shape=(jax.ShapeDtypeStruct((N, LANES), F32), jax.ShapeDtypeStruct((N, LANES), F32)),
        grid=(N // tm,),
        in_specs=[spec, vec, vec],
        out_specs=(spec, spec),
        compiler_params=_cparams("arbitrary"),
        name="rope_tables",
    )(pos, f4, sign.reshape(1, LANES))


QK_W = 2 * GLA_HEADS * GLA_DK
GV_W = GLA_HEADS * GLA_DV
TAIL_W = 3 * LANES


def _in_segments(D):
    return (QK_W, GV_W, GV_W, MLA_Q_RANK, MLA_KV_RANK, D, D, TAIL_W)


def _prep_w_in(w_in, D):
    s = [GLA_HEADS * GLA_DK, GLA_HEADS * GLA_DK, GV_W, GV_W, GLA_GATE_RANK, MLA_Q_RANK, MLA_KV_RANK, MLA_ROPE, D, D]
    offs = [0]
    for n in s:
        offs.append(offs[-1] + n)
    gq, gk, gv, gr, ga, cq, ckv, kr, gate_a, gate_b = [w_in[:, offs[i]:offs[i + 1]] for i in range(10)]
    half = MLA_ROPE // 2
    kr_sw = jnp.concatenate([kr[:, half:], kr[:, :half]], axis=1)
    pad = jnp.zeros((w_in.shape[0], LANES - GLA_GATE_RANK), w_in.dtype)
    return jnp.concatenate([gq, gk, gv, gr, cq, ckv, gate_a, gate_b, kr, kr, kr_sw, kr_sw, ga, pad], axis=1).astype(BF16)


def _inproj_kernel(x_ref, mod_ref, w_ref, *out_refs):
    sh1 = mod_ref[0, 0:1, :]
    sc1 = mod_ref[0, 1:2, :]
    h = (x_ref[...] * (1.0 + sc1) + sh1).astype(BF16)
    off = 0
    for ref in out_refs:
        n = ref.shape[-1]
        ref[...] = _dot(h, w_ref[:, off:off + n]).astype(ref.dtype)
        off += n


def _in_projection(x2, mod6, w_in_p, S):
    N, D = x2.shape
    W = w_in_p.shape[1]
    tm = min(S, 512)
    per_b = S // tm
    segs = _in_segments(D)
    assert sum(segs) == W
    return pl.pallas_call(
        _inproj_kernel,
        out_shape=tuple(jax.ShapeDtypeStruct((N, n), BF16) for n in segs),
        grid=(N // tm,),
        in_specs=[
            pl.BlockSpec((tm, D), lambda i: (i, 0)),
            pl.BlockSpec((1, 6, D), lambda i: (i // per_b, 0, 0)),
            pl.BlockSpec((D, W), lambda i: (0, 0), pipeline_mode=pl.Buffered(1)),
        ],
        out_specs=tuple(pl.BlockSpec((tm, n), lambda i: (i, 0)) for n in segs),
        compiler_params=_cparams("arbitrary"),
        name="in_proj",
    )(x2, mod6, w_in_p)


def _gla_kernel(qk_ref, v_ref, gr_ref, tail_ref, wa2_ref, ba_ref, gn_ref, wo_ref, y_ref,
                st_ref, kd_ref, dec_ref, sall_ref, o_ref):
    t = pl.program_id(1)

    @pl.when(t == 0)
    def _():
        st_ref[...] = jnp.zeros_like(st_ref)

    ts = qk_ref.shape[0]
    nch = ts // CHUNK
    HK = GLA_HEADS * GLA_DK
    r = lax.broadcasted_iota(I32, (CHUNK, CHUNK), 0)
    c = lax.broadcasted_iota(I32, (CHUNK, CHUNK), 1)
    tri = (r >= c).astype(BF16)
    qscale = GLA_DK ** -0.5

    z = _dot(tail_ref[:, 2 * LANES:3 * LANES], wa2_ref[...]) + ba_ref[...]
    log_a = (jnp.minimum(z, 0.0) - jnp.log(1.0 + jnp.exp(-jnp.abs(z)))) * (1.0 / GLA_GATE_TAU)
    la_hi = log_a.astype(BF16)
    la_lo = (log_a - la_hi.astype(F32)).astype(BF16)
    for n in range(nch):
        rows = slice(n * CHUNK, (n + 1) * CHUNK)
        G = _dot(tri, la_hi[rows]) + _dot(tri, la_lo[rows])
        g_end = G[CHUNK - 1:CHUNK, :]
        kd_ref[rows, :] = (qk_ref[rows, HK:2 * HK].astype(F32) * jnp.exp(g_end - G)).astype(BF16)
        dec_ref[n:n + 1, :] = jnp.exp(g_end)

    for h in range(GLA_HEADS):
        ks = slice(h * GLA_DK, (h + 1) * GLA_DK)
        vs = slice(h * GLA_DV, (h + 1) * GLA_DV)
        st = st_ref[h]
        for n in range(nch):
            rows = slice(n * CHUNK, (n + 1) * CHUNK)
            st = st * dec_ref[n:n + 1, ks] + _dot_tn(v_ref[rows, vs], kd_ref[rows, ks])
            sall_ref[n * GLA_HEADS + h] = st.astype(BF16)
        st_ref[h] = st

    for n in range(nch):
        rows = slice(n * CHUNK, (n + 1) * CHUNK)
        for h in range(GLA_HEADS):
            ks = slice(h * GLA_DK, (h + 1) * GLA_DK)
            vs = slice(h * GLA_DV, (h + 1) * GLA_DV)
            qh = (qk_ref[rows, ks].astype(F32) * qscale).astype(BF16)
            o = _dot_nt(qh, sall_ref[n * GLA_HEADS + h])
            o = o * lax.rsqrt(jnp.mean(o * o, axis=-1, keepdims=True) + RMS_EPS) * gn_ref[...]
            o_ref[rows, vs] = (o * _silu(gr_ref[rows, vs].astype(F32))).astype(BF16)
    y_ref[...] = _dot(o_ref[...], wo_ref[...]).astype(y_ref.dtype)


def _gla(qk, gv, gr, tail, wa2_p, b_a, g_norm, w_o, B, S):
    N = B * S
    D = w_o.shape[1]
    ts = min(S, 512)
    per_b = S // ts
    HK = GLA_HEADS * GLA_DK
    tok = lambda n: pl.BlockSpec((ts, n), lambda b, t: (b * per_b + t, 0))
    full = lambda a: pl.BlockSpec(a.shape, lambda b, t: (0,) * a.ndim)
    return pl.pallas_call(
        _gla_kernel,
        out_shape=jax.ShapeDtypeStruct((N, D), BF16),
        grid=(B, per_b),
        in_specs=[tok(QK_W), tok(GV_W), tok(GV_W), tok(TAIL_W), full(wa2_p), full(b_a), full(g_norm), full(w_o)],
        out_specs=tok(D),
        scratch_shapes=[pltpu.VMEM((GLA_HEADS, GLA_DV, GLA_DK), F32),
                        pltpu.VMEM((ts, GLA_HEADS * GLA_DK), BF16),
                        pltpu.VMEM((ts // CHUNK, GLA_HEADS * GLA_DK), F32),
                        pltpu.VMEM((ts // CHUNK * GLA_HEADS, GLA_DV, GLA_DK), BF16),
                        pltpu.VMEM((ts, GV_W), BF16)],
        compiler_params=_cparams("arbitrary", "arbitrary"),
        name="gla",
    )(qk, gv, gr, tail, wa2_p, b_a, g_norm, w_o)


HQ = MLA_HEADS * 2 * LANES


def _prep_w_uq(w_uq):
    dh = MLA_NOPE + MLA_ROPE
    half = MLA_ROPE // 2
    nope = [w_uq[:, h * dh:h * dh + MLA_NOPE] for h in range(MLA_HEADS)]
    rope = [w_uq[:, h * dh + MLA_NOPE:(h + 1) * dh] for h in range(MLA_HEADS)]
    rope_sw = [jnp.concatenate([r[:, half:], r[:, :half]], axis=1) for r in rope]
    return jnp.concatenate(nope + rope + rope_sw, axis=1).astype(BF16)


def _prep_w_ukv(w_ukv):
    dh = MLA_NOPE + MLA_V
    kn = [w_ukv[:, h * dh:h * dh + MLA_NOPE] for h in range(MLA_HEADS)]
    vv = [w_ukv[:, h * dh + MLA_NOPE:(h + 1) * dh] for h in range(MLA_HEADS)]
    return jnp.concatenate(kn + vv, axis=1).astype(BF16)


def _rms(x, g):
    return x * lax.rsqrt(jnp.mean(x * x, axis=-1, keepdims=True) + RMS_EPS) * g


def _mla_prep_kernel(cq_ref, ckv_ref, tail_ref, cc_ref, ss_ref, gq_ref, wq_ref, gkv_ref, wkv_ref, q_ref, k_ref, v_ref):
    tm = cq_ref.shape[0]
    NP = MLA_HEADS * MLA_NOPE
    RP = MLA_HEADS * MLA_ROPE
    scale = (MLA_NOPE + MLA_ROPE) ** -0.5 * LOG2E
    cc = cc_ref[...]
    ss = ss_ref[...]
    cqn = _rms(cq_ref[...].astype(F32), gq_ref[...]).astype(BF16)
    qf = _dot(cqn, wq_ref[...]) * scale
    ckvn = _rms(ckv_ref[...].astype(F32), gkv_ref[...]).astype(BF16)
    kv = _dot(ckvn, wkv_ref[...])
    krr = (tail_ref[:, 0:LANES].astype(F32) * cc + tail_ref[:, LANES:2 * LANES].astype(F32) * ss).astype(BF16)
    lane = lax.broadcasted_iota(I32, (tm, LANES), 1)
    first = lane < MLA_ROPE
    for j in range(MLA_HEADS // 2):
        a = NP + j * LANES
        rot = qf[:, a:a + LANES] * cc + qf[:, a + RP:a + RP + LANES] * ss
        for h, keep in ((2 * j, first), (2 * j + 1, jnp.logical_not(first))):
            base = h * 2 * LANES
            q_ref[:, base:base + LANES] = qf[:, h * MLA_NOPE:(h + 1) * MLA_NOPE].astype(BF16)
            q_ref[:, base + LANES:base + 2 * LANES] = jnp.where(keep, rot, 0.0).astype(BF16)
            k_ref[:, base:base + LANES] = kv[:, h * MLA_NOPE:(h + 1) * MLA_NOPE].astype(BF16)
            k_ref[:, base + LANES:base + 2 * LANES] = krr
    v_ref[...] = kv[:, NP:].astype(BF16)


def _mla_prep(cq, ckv, tail, cc, ss, g_cq, w_uq_p, g_ckv, w_ukv_p):
    N = cq.shape[0]
    tm = min(N, 512)
    tok = lambda n: pl.BlockSpec((tm, n), lambda i: (i, 0))
    full = lambda a: pl.BlockSpec(a.shape, lambda i: (0,) * a.ndim)
    HV = MLA_HEADS * MLA_V
    return pl.pallas_call(
        _mla_prep_kernel,
        out_shape=(jax.ShapeDtypeStruct((N, HQ), BF16), jax.ShapeDtypeStruct((N, HQ), BF16),
                   jax.ShapeDtypeStruct((N, HV), BF16)),
        grid=(N // tm,),
        in_specs=[tok(MLA_Q_RANK), tok(MLA_KV_RANK), tok(TAIL_W), tok(LANES), tok(LANES),
                  full(g_cq), full(w_uq_p), full(g_ckv), full(w_ukv_p)],
        out_specs=(tok(HQ), tok(HQ), tok(HV)),
        compiler_params=_cparams("arbitrary"),
        name="mla_prep",
    )(cq, ckv, tail, cc, ss, g_cq, w_uq_p, g_ckv, w_ukv_p)


ATTN_TQ = 256


def _mla_attn_kernel(q_ref, k_ref, v_ref, o_ref, v1_ref):
    S = q_ref.shape[0]
    tq = min(S, ATTN_TQ)
    r = lax.broadcasted_iota(I32, (tq, tq), 0) // CHUNK
    c = lax.broadcasted_iota(I32, (tq, tq), 1) // CHUNK
    diag_mask = c <= r
    v1_ref[:, :MLA_V] = v_ref[...]
    v1_ref[:, MLA_V:] = jnp.ones((S, MLA_V), BF16)

    def scores(ii):
        l0 = ii * tq
        q = q_ref[l0:l0 + tq, :]
        sd = jnp.where(diag_mask, _dot_nt(q, k_ref[l0:l0 + tq, :]), -jnp.inf)
        so = _dot_nt(q, k_ref[0:l0, :]) if ii > 0 else None
        return sd, so

    def finish(ii, sd, so):
        l0 = ii * tq
        m = jnp.max(sd, axis=-1, keepdims=True)
        if so is not None:
            m = jnp.maximum(m, jnp.max(so, axis=-1, keepdims=True))
        acc = _dot(jnp.exp2((sd - m).astype(BF16)), v1_ref[l0:l0 + tq, :])
        if so is not None:
            acc = acc + _dot(jnp.exp2((so - m).astype(BF16)), v1_ref[0:l0, :])
        o_ref[l0:l0 + tq, :] = (acc[:, :MLA_V] / acc[:, MLA_V:]).astype(BF16)

    n_tiles = S // tq
    ahead = 2
    pending = [scores(ii) for ii in range(min(ahead, n_tiles))]
    for ii in range(n_tiles):
        if ii + ahead < n_tiles:
            pending.append(scores(ii + ahead))
        finish(ii, *pending.pop(0))


def _mla_attention(qc, kc, vv, B, S):
    N = B * S
    HV = MLA_HEADS * MLA_V
    return pl.pallas_call(
        _mla_attn_kernel,
        out_shape=jax.ShapeDtypeStruct((N, HV), BF16),
        grid=(B, MLA_HEADS),
        in_specs=[
            pl.BlockSpec((S, 2 * LANES), lambda b, h: (b, h)),
            pl.BlockSpec((S, 2 * LANES), lambda b, h: (b, h)),
            pl.BlockSpec((S, MLA_V), lambda b, h: (b, h)),
        ],
        out_specs=pl.BlockSpec((S, MLA_V), lambda b, h: (b, h)),
        scratch_shapes=[pltpu.VMEM((S, 2 * MLA_V), BF16)],
        compiler_params=_cparams("arbitrary", "arbitrary"),
        name="mla_attn",
    )(qc, kc, vv)


def _layer_norm(u, g, b):
    mu = jnp.mean(u, axis=-1, keepdims=True)
    d = u - mu
    var = jnp.mean(d * d, axis=-1, keepdims=True)
    return d * lax.rsqrt(var + LN_EPS) * g + b


def _merge_kernel(alpha, x_ref, ya_ref, ob_ref, ga_ref, gb_ref, mod_ref, wmo_ref, wout_ref, g_ref, b_ref, wr_ref,
                  br_ref, ws13_ref, ws2_ref, x1_ref, h2_ref, sh_ref, idx_ref, ww_ref, rank_ref, cnt_ref, lg_s, carry_s):
    i = pl.program_id(0)

    @pl.when(i == 0)
    def _():
        lg_s[...] = jnp.zeros_like(lg_s)
        carry_s[...] = jnp.zeros_like(carry_s)

    idx, ww, rank, total = _route_math(lg_s[...], br_ref[...], carry_s[...])
    total = jnp.where(i > 0, total, 0.0)
    idx_ref[...] = idx
    ww_ref[...] = ww
    rank_ref[...] = rank
    carry_s[...] = total
    cnt_ref[...] = jnp.broadcast_to(total, cnt_ref.shape).astype(I32)

    gt1 = mod_ref[0, 2:3, :]
    sh2 = mod_ref[0, 3:4, :]
    sc2 = mod_ref[0, 4:5, :]
    y = (_sigmoid(ga_ref[...].astype(F32)) * ya_ref[...].astype(F32)
         + _sigmoid(gb_ref[...].astype(F32)) * _dot(ob_ref[...], wmo_ref[...]))
    mix = _dot(y.astype(BF16), wout_ref[...])
    x1 = _layer_norm(alpha * x_ref[...] + (1.0 + gt1) * mix, g_ref[...], b_ref[...])
    x1_ref[...] = x1
    h2 = x1 * (1.0 + sc2) + sh2
    h2_ref[...] = _pack_bf16_pair(h2)
    h2b = h2.astype(BF16)
    lg_s[...] = _dot_nt(wr_ref[...], h2b)
    F = ws2_ref.shape[0]
    ab = _dot(h2b, ws13_ref[...])
    hid = (_silu(ab[:, :F]) * ab[:, F:]).astype(BF16)
    sh_ref[...] = _dot(hid, ws2_ref[...]).astype(sh_ref.dtype)


def _merge_route(alpha, x2, y_a, o_b, gate_a, gate_b, mod6, w_mla_o, w_out, ln_g, ln_b, wr_t, b_router, ws13, ws2,
                 S, tok0, Np):
    N, D = x2.shape
    E = wr_t.shape[0]
    tm = min(S, 512)
    per_b = S // tm
    assert Np % tm == 0 and tok0 % tm == 0
    n = Np // tm
    base = tok0 // tm
    cur = lambda i: jnp.minimum(i, n - 1)
    prev = lambda i: jnp.maximum(i - 1, 0)
    tok_in = lambda w: pl.BlockSpec((tm, w), lambda i: (base + cur(i), 0))
    tok = lambda w: pl.BlockSpec((tm, w), lambda i: (cur(i), 0))
    full = lambda a: pl.BlockSpec(a.shape, lambda i: (0,) * a.ndim)
    lane_blk = pl.BlockSpec((TOP_K, tm), lambda i: (0, prev(i)))
    br = b_router.reshape(E, 1).astype(F32)
    return pl.pallas_call(
        functools.partial(_merge_kernel, alpha),
        out_shape=(jax.ShapeDtypeStruct((Np, D), F32), jax.ShapeDtypeStruct((Np, D // 2), I32),
                   jax.ShapeDtypeStruct((Np, D), BF16),
                   jax.ShapeDtypeStruct((TOP_K, Np), I32), jax.ShapeDtypeStruct((TOP_K, Np), I32),
                   jax.ShapeDtypeStruct((TOP_K, Np), I32), jax.ShapeDtypeStruct((E, LANES), I32)),
        grid=(n + 1,),
        in_specs=[tok_in(D), tok_in(D), tok_in(D), tok_in(D), tok_in(D),
                  pl.BlockSpec((1, 6, D), lambda i: ((base + cur(i)) // per_b, 0, 0)),
                  full(w_mla_o), full(w_out), full(ln_g), full(ln_b), full(wr_t), full(br), full(ws13), full(ws2)],
        out_specs=(tok(D), tok(D // 2), tok(D), lane_blk, lane_blk, lane_blk,
                   pl.BlockSpec((E, LANES), lambda i: (0, 0))),
        scratch_shapes=[pltpu.VMEM((E, tm), F32), pltpu.VMEM((E, 1), F32)],
        compiler_params=_cparams("arbitrary"),
        name="merge_ln1_route",
    )(x2, y_a, o_b, gate_a, gate_b, mod6, w_mla_o, w_out, ln_g, ln_b, wr_t, br, ws13, ws2)


def _first_argmax(v, io, n):
    m = jnp.max(v, axis=0, keepdims=True)
    idx = jnp.min(jnp.where(v == m, io, n), axis=0, keepdims=True)
    return m, idx


def _route_math(lg, br, carry):
    E, T = lg.shape
    gsz = E // N_GROUPS
    neg = -jnp.inf
    s = _sigmoid(lg)
    biased = s + br
    eio = lax.broadcasted_iota(I32, (E, T), 0)
    gio = lax.broadcasted_iota(I32, (gsz, T), 0)

    gs = []
    for g in range(N_GROUPS):
        blk = biased[g * gsz:(g + 1) * gsz]
        m1, i1 = _first_argmax(blk, gio, gsz)
        m2 = jnp.max(jnp.where(gio == i1, neg, blk), axis=0, keepdims=True)
        gs.append(m1 + m2)
    cur = jnp.concatenate(gs, axis=0)
    nio = lax.broadcasted_iota(I32, (N_GROUPS, T), 0)
    gsel = jnp.zeros((N_GROUPS, T), F32)
    for _ in range(TOPK_GROUPS):
        _, gi = _first_argmax(cur, nio, N_GROUPS)
        hit = nio == gi
        gsel = jnp.where(hit, 1.0, gsel)
        cur = jnp.where(hit, neg, cur)
    emask = jnp.concatenate([jnp.broadcast_to(gsel[g:g + 1], (gsz, T)) for g in range(N_GROUPS)], axis=0) > 0.0

    cur = jnp.where(emask, biased, neg)
    idxs, ws = [], []
    sel = jnp.zeros((E, T), F32)
    for _ in range(TOP_K):
        _, ei = _first_argmax(cur, eio, E)
        hit = eio == ei
        idxs.append(ei)
        ws.append(jnp.sum(jnp.where(hit, s, 0.0), axis=0, keepdims=True))
        sel = jnp.where(hit, 1.0, sel)
        cur = jnp.where(hit, neg, cur)
    w = jnp.concatenate(ws, axis=0)
    w = w / jnp.sum(w, axis=0, keepdims=True) * ROUTED_SCALE
    wbits = lax.bitcast_convert_type(w.astype(BF16).astype(F32), I32)
    wword = jnp.bitwise_or(lax.shift_right_logical(wbits, jnp.int32(16)), jnp.bitwise_and(wbits, jnp.int32(-65536)))

    r = lax.broadcasted_iota(I32, (T, T), 0)
    c = lax.broadcasted_iota(I32, (T, T), 1)
    before = (r < c).astype(BF16)
    cnt = _dot(sel.astype(BF16), before) + carry
    ranks = [jnp.sum(jnp.where(eio == idxs[k], cnt, 0.0), axis=0, keepdims=True) for k in range(TOP_K)]
    total = cnt[:, T - 1:T] + sel[:, T - 1:T]
    return jnp.concatenate(idxs, axis=0), wword, jnp.concatenate(ranks, axis=0).astype(I32), total


def _dest_kernel(idx_ref, rank_ref, ps_ref, dest_ref):
    K, T = idx_ref.shape
    E = ps_ref.shape[0]
    eio = lax.broadcasted_iota(I32, (E, T), 0)
    ps = ps_ref[...]
    rows = [jnp.sum(jnp.where(eio == idx_ref[k:k + 1, :], ps, 0), axis=0, keepdims=True) for k in range(K)]
    dest_ref[...] = jnp.concatenate(rows, axis=0) + rank_ref[...]


def _dest_slots(idx, rank, pad_start):
    K, N = idx.shape
    E = pad_start.shape[0]
    T = min(N, 512)
    spec = pl.BlockSpec((K, T), lambda i: (0, i))
    return pl.pallas_call(
        _dest_kernel,
        out_shape=jax.ShapeDtypeStruct((K, N), I32),
        grid=(N // T,),
        in_specs=[spec, spec, pl.BlockSpec((E, 1), lambda i: (0, 0))],
        out_specs=spec,
        compiler_params=_cparams("arbitrary"),
        name="dest_slots",
    )(idx, rank, pad_start.reshape(E, 1))


SC_ROWS = 64


def _sc_workers():
    info = plsc.get_sparse_core_info()
    return info.num_cores, info.num_subcores


def _sc_lanes():
    return plsc.get_sparse_core_info().num_lanes


def _dispatch_rows(h2p, dest, n_slots):
    N, W = h2p.shape
    K = dest.shape[0]
    R = SC_ROWS
    nc, ns = _sc_workers()
    per_w = N // R // (nc * ns)
    assert per_w * R * nc * ns == N and per_w % 2 == 0, "token count must split into chunk pairs per subcore"
    mesh = plsc.VectorSubcoreMesh(core_axis_name="c", subcore_axis_name="s")

    @functools.partial(
        pl.kernel, mesh=mesh,
        out_type=jax.ShapeDtypeStruct((n_slots, W), I32),
        scratch_types=[pltpu.VMEM((K, R), I32), pltpu.VMEM((K, R), I32),
                       pltpu.VMEM((R, W), I32), pltpu.VMEM((R, W), I32),
                       pltpu.SemaphoreType.DMA, pltpu.SemaphoreType.DMA, pltpu.SemaphoreType.DMA],
    )
    def k(h_hbm, d_hbm, xs_hbm, idx0, idx1, rows0, rows1, lsem0, lsem1, ssem):
        idx, rows, lsem = (idx0, idx1), (rows0, rows1), (lsem0, lsem1)
        base = (lax.axis_index("s") * nc + lax.axis_index("c")) * per_w

        def loads(ch, b):
            tok = pl.ds(pl.multiple_of(ch * R, 8), R)
            return [pltpu.make_async_copy(d_hbm.at[kk, tok], idx[b].at[kk], lsem[b]) for kk in range(K)] + [
                pltpu.make_async_copy(h_hbm.at[tok], rows[b], lsem[b])]

        for cp in loads(base, 0):
            cp.start()
        for cp in loads(base, 0):
            cp.wait()

        @pl.loop(0, per_w, step=2)
        def _(j):
            for b in range(2):
                ch = base + j + b
                more = j + b + 1 < per_w

                @pl.when(more)
                def _():
                    for cp in loads(ch + 1, 1 - b):
                        cp.start()

                scatters = [pltpu.make_async_copy(rows[b], xs_hbm.at[idx[b].at[kk]], ssem) for kk in range(K)]
                for cp in scatters:
                    cp.start()
                for cp in scatters:
                    cp.wait()

                @pl.when(more)
                def _():
                    for cp in loads(ch + 1, 1 - b):
                        cp.wait()

    return k(h2p, dest)


COMBINE_ROWS = 8


def _combine_sum(ys, dest, ww, tok0, n_tok):
    W = ys.shape[1]
    K = dest.shape[0]
    R = COMBINE_ROWS
    L = _sc_lanes()
    nc, ns = _sc_workers()
    per_w = n_tok // R // (nc * ns)
    assert per_w * R * nc * ns == n_tok and per_w % 2 == 0, "token count must split into chunk pairs per subcore"
    T = per_w * R
    mesh = plsc.VectorSubcoreMesh(core_axis_name="c", subcore_axis_name="s")

    @functools.partial(
        pl.kernel, mesh=mesh, compiler_params=pltpu.CompilerParams(needs_layout_passes=False),
        out_type=jax.ShapeDtypeStruct((n_tok, W), I32),
        scratch_types=[pltpu.VMEM((K * T,), I32), pltpu.VMEM((K * T,), I32),
                       pltpu.VMEM((K, R, W), I32), pltpu.VMEM((K, R, W), I32),
                       pltpu.VMEM((R, W), I32), pltpu.VMEM((R, W), I32),
                       pltpu.SemaphoreType.DMA, pltpu.SemaphoreType.DMA,
                       pltpu.SemaphoreType.DMA, pltpu.SemaphoreType.DMA],
    )
    def k(ys_hbm, d_hbm, w_hbm, o_hbm, idx_all, w_all, buf0, buf1, out0, out1, gsem0, gsem1, osem0, osem1):
        buf, outv, gsem, osem = (buf0, buf1), (out0, out1), (gsem0, gsem1), (osem0, osem1)
        wid = lax.axis_index("s") * nc + lax.axis_index("c")
        t0 = pl.multiple_of(tok0 + wid * T, 8)
        for kk in range(K):
            pltpu.sync_copy(d_hbm.at[kk, pl.ds(t0, T)], idx_all.at[pl.ds(kk * T, T)])
            pltpu.sync_copy(w_hbm.at[kk, pl.ds(t0, T)], w_all.at[pl.ds(kk * T, T)])

        def fetch(j, b):
            return [pltpu.make_async_copy(
                ys_hbm.at[idx_all.at[pl.ds(pl.multiple_of(kk * T + j * R, 8), R)]], buf[b].at[kk], gsem[b])
                for kk in range(K)]

        def store(j, b):
            return pltpu.make_async_copy(outv[b], o_hbm.at[pl.ds(pl.multiple_of(wid * T + j * R, 8), R)], osem[b])

        for cp in fetch(0, 0):
            cp.start()

        @pl.loop(0, per_w, step=2)
        def _(j0):
            for b in range(2):
                j = j0 + b

                @pl.when(j + 1 < per_w)
                def _():
                    for cp in fetch(j + 1, 1 - b):
                        cp.start()

                for cp in fetch(j, b):
                    cp.wait()

                @pl.when(j >= 2)
                def _():
                    store(j - 2, b).wait()

                @pl.loop(0, R)
                def _(r):
                    ws = [plsc.bitcast(plsc.load_gather(w_all, [jnp.full((L,), kk * T + j * R + r, I32)]), BF16)
                          for kk in range(K)]

                    @pl.loop(0, W // L)
                    def _(q):
                        sl = pl.ds(q * L, L)
                        terms = [ws[kk] * plsc.bitcast(buf[b][kk, r, sl], BF16) for kk in range(K)]
                        while len(terms) > 1:
                            terms = [terms[i] + terms[i + 1] for i in range(0, len(terms), 2)]
                        outv[b][r, sl] = plsc.bitcast(terms[0], I32)

                store(j, b).start()

        store(per_w - 2, 0).wait()
        store(per_w - 1, 1).wait()

    return k(ys, dest, ww)


def _expert_kernel(b0_ref, nb_ref, cnt_ref, xs_hbm, w1_ref, w3_ref, w2_ref, ys_hbm,
                   w13_s, w2_s, xbuf, ybuf, sem_in, sem_out):
    e = pl.program_id(0)
    n_exp = pl.num_programs(0)
    F = w1_ref.shape[2]
    bm = xbuf.shape[1]
    half = xbuf.shape[2]
    nb = nb_ref[e]
    b0 = b0_ref[e]
    total = b0_ref[n_exp - 1] + nb_ref[n_exp - 1]

    nbuf = xbuf.shape[0]

    def in_copy(g):
        slot = g % nbuf
        return pltpu.make_async_copy(xs_hbm.at[pl.ds(g * bm, bm)], xbuf.at[slot], sem_in.at[slot])

    def out_copy(g):
        slot = g % nbuf
        return pltpu.make_async_copy(ybuf.at[slot], ys_hbm.at[pl.ds(g * bm, bm)], sem_out.at[slot])

    @pl.when(e == 0)
    def _():
        for g0 in range(nbuf - 1):
            @pl.when(g0 < total)
            def _():
                in_copy(g0).start()

    @pl.when(nb > 0)
    def _():
        w13_s[:, :F] = w1_ref[0].astype(BF16)
        w13_s[:, F:] = w3_ref[0].astype(BF16)
        w2_s[...] = w2_ref[0].astype(BF16)

    def block(j, carry):
        g = b0 + j
        slot = g % nbuf
        in_copy(g).wait()

        @pl.when(g + nbuf - 1 < total)
        def _():
            in_copy(g + nbuf - 1).start()

        @pl.when(g >= nbuf)
        def _():
            out_copy(g - nbuf).wait()

        def swiglu(rows):
            lo, hi = _unpack_bf16_pair(xbuf[slot, :rows, :])
            ab = _dot(lo.astype(BF16), w13_s[:half, :]) + _dot(hi.astype(BF16), w13_s[half:, :])
            hid = (_silu(ab[:, :F]) * ab[:, F:]).astype(BF16)
            ybuf[slot, :rows, :] = _pack_bf16_pair(_dot(hid, w2_s[...]))

        rows_left = cnt_ref[e] - j * bm

        @pl.when(rows_left > bm // 2)
        def _():
            swiglu(bm)

        @pl.when(rows_left <= bm // 2)
        def _():
            swiglu(bm // 2)
            ybuf[slot, bm // 2:, :] = jnp.zeros((bm - bm // 2, half), I32)

        out_copy(g).start()
        return carry

    lax.fori_loop(0, nb, block, 0)

    @pl.when(e == n_exp - 1)
    def _():
        for back in range(nbuf, 0, -1):
            @pl.when(total >= back)
            def _():
                out_copy(total - back).wait()


def _experts(xs, blk0, nblk, counts, w_e1, w_e3, w_e2):
    n_slots, W = xs.shape
    E, D, F = w_e1.shape
    bm = EXPERT_BLOCK
    grid_spec = pltpu.PrefetchScalarGridSpec(
        num_scalar_prefetch=3,
        grid=(E,),
        in_specs=[
            pl.BlockSpec(memory_space=pl.ANY),
            pl.BlockSpec((1, D, F), lambda e, b0, nb, cnt: (e, 0, 0)),
            pl.BlockSpec((1, D, F), lambda e, b0, nb, cnt: (e, 0, 0)),
            pl.BlockSpec((1, F, D), lambda e, b0, nb, cnt: (e, 0, 0)),
        ],
        out_specs=pl.BlockSpec(memory_space=pl.ANY),
        scratch_shapes=[pltpu.VMEM((D, 2 * F), BF16), pltpu.VMEM((F, D), BF16),
                        pltpu.VMEM((EXPERT_RING, bm, W), I32), pltpu.VMEM((EXPERT_RING, bm, W), I32),
                        pltpu.SemaphoreType.DMA((EXPERT_RING,)), pltpu.SemaphoreType.DMA((EXPERT_RING,))],
    )
    return pl.pallas_call(
        _expert_kernel,
        out_shape=jax.ShapeDtypeStruct((n_slots, W), I32),
        grid_spec=grid_spec,
        compiler_params=_cparams("arbitrary"),
        name="experts",
    )(blk0, nblk, counts, xs, w_e1, w_e3, w_e2)


def _final_kernel(alpha, ms_ref, sh_ref, x1_ref, mod_ref, g_ref, b_ref, *rest):
    o_ref = rest[-1]
    gt2 = mod_ref[0, 5:6, :]
    lo, hi = _unpack_bf16_pair(ms_ref[...])
    ffn = jnp.concatenate([lo, hi], axis=1) + sh_ref[...].astype(F32)
    o_ref[...] = _layer_norm(alpha * x1_ref[...] + (1.0 + gt2) * ffn, g_ref[...], b_ref[...])


def _final_part(alpha, ms, sh, x1, mod6, ln_g, ln_b, S, tok0, row0, n_total, prev):
    D = x1.shape[1]
    N = n_total
    n_part, W = ms.shape
    tm = min(S, 512, n_part)
    assert n_part % tm == 0 and S % tm == 0 and tok0 % tm == 0 and row0 % tm == 0
    per_b = S // tm
    steps = n_part // tm
    off = tok0 // tm
    goff = row0 // tm
    tok = lambda n: pl.BlockSpec((tm, n), lambda i: (off + i, 0))
    full = lambda a: pl.BlockSpec(a.shape, lambda i: (0,) * a.ndim)
    in_specs = [pl.BlockSpec((tm, W), lambda i: (i, 0)), tok(D), tok(D),
                pl.BlockSpec((1, 6, D), lambda i: ((goff + i) // per_b, 0, 0)), full(ln_g), full(ln_b)]
    args = [ms, sh, x1, mod6, ln_g, ln_b]
    aliases = {}
    if prev is not None:
        in_specs.append(pl.BlockSpec(memory_space=pl.ANY))
        args.append(prev)
        aliases = {len(args) - 1: 0}
    return pl.pallas_call(
        functools.partial(_final_kernel, alpha),
        out_shape=jax.ShapeDtypeStruct((N, D), F32),
        grid=(steps,),
        in_specs=in_specs,
        out_specs=pl.BlockSpec((tm, D), lambda i: (goff + i, 0)),
        input_output_aliases=aliases,
        compiler_params=_cparams("arbitrary"),
        name="combine_ln2",
    )(*args)


def _moe(alpha, h2p, sh, idx, ww, rank, cnt, x1, mod6, w_e1, w_e3, w_e2, ln_g, ln_b, S, row0, n_total, out):
    N = x1.shape[0]
    E = w_e1.shape[0]
    bm = EXPERT_BLOCK
    counts = cnt[:, 0]
    padded = (counts + bm - 1) // bm * bm
    pad_end = jnp.cumsum(padded)
    pad_start = pad_end - padded
    n_slots = N * TOP_K + E * bm
    dest = _dest_slots(idx, rank, pad_start.astype(I32))
    xs = _dispatch_rows(h2p, dest, n_slots)
    ys = _experts(xs, (pad_start // bm).astype(I32), (padded // bm).astype(I32), counts.astype(I32),
                  w_e1, w_e3, w_e2)
    n_part = min(N, COMBINE_TOKENS) if row0 + N == n_total else N
    assert N % n_part == 0
    for p in range(N // n_part):
        tok0 = p * n_part
        ms = _combine_sum(ys, dest, ww, tok0, n_part)
        out = _final_part(alpha, ms, sh, x1, mod6, ln_g, ln_b, S, tok0, row0 + tok0, n_total, out)
    return out


def kernel(x, c, positions, w_ada, b_ada, w_in, w_gla_a2, b_gla_a, g_gla_norm, w_gla_o, g_cq, w_uq, g_ckv, w_ukv, w_mla_o, w_out, ln1_g, ln1_b, w_router, b_router, w_e1, w_e3, w_e2, w_s1, w_s3, w_s2, ln2_g, ln2_b):
    B, S, D = x.shape
    N = B * S
    depth = w_ada.shape[0]
    alpha = (2.0 * depth) ** 0.25
    row = lambda a: a.reshape(1, -1)
    cc, ss = _rope_tables(positions)
    x2 = x.reshape(N, D)
    for l in range(depth):
        mod6 = _modulation(c, w_ada[l], b_ada[l]).reshape(B, 6, D)
        qk, gv, gr, cq, ckv, gate_a, gate_b, tail = _in_projection(x2, mod6, _prep_w_in(w_in[l], D), S)
        wa2_p = jnp.concatenate(
            [w_gla_a2[l], jnp.zeros((LANES - GLA_GATE_RANK, w_gla_a2.shape[2]), F32)], axis=0).astype(BF16)
        y_a = _gla(qk, gv, gr, tail, wa2_p, row(b_gla_a[l]), row(g_gla_norm[l]), w_gla_o[l].astype(BF16), B, S)
        qc, kc, vv = _mla_prep(cq, ckv, tail, cc, ss, row(g_cq[l]), _prep_w_uq(w_uq[l]),
                               row(g_ckv[l]), _prep_w_ukv(w_ukv[l]))
        o_b = _mla_attention(qc, kc, vv, B, S)
        ws13 = jnp.concatenate([w_s1[l], w_s3[l]], axis=1).astype(BF16)
        ws2 = w_s2[l].astype(BF16)
        b_first = max(1, (B * MOE_FIRST_EIGHTHS) // 8) if B > 1 else B
        ranges = [(0, b_first * S)] + ([(b_first * S, (B - b_first) * S)] if B > b_first else [])
        routed =[_merge_route(alpha, x2, y_a, o_b, gate_a, gate_b, mod6, w_mla_o[l].astype(BF16),
                               w_out[l].astype(BF16), row(ln1_g[l]), row(ln1_b[l]), w_router[l].T.astype(BF16),
                               b_router[l], ws13, ws2, S, t0, n) for t0, n in ranges]
        out = None
        for (t0, n), (x1, h2p, sh, idx, ww, rank, cnt) in zip(ranges, routed):
            out = _moe(alpha, h2p, sh, idx, ww, rank, cnt, x1, mod6, w_e1[l], w_e3[l], w_e2[l],
                       row(ln2_g[l]), row(ln2_b[l]), S, t0, N, out)
        x2 = out
    return x2.reshape(B, S, D)
```

```python
import functools

import jax
import jax.numpy as jnp
from jax import lax
from jax.experimental import pallas as pl
from jax.experimental.pallas import tpu as pltpu
from jax.experimental.pallas import tpu_sc as plsc

CHUNK = 64
GLA_HEADS = 4
GLA_DK = 128
GLA_DV = 256
GLA_GATE_RANK = 16
GLA_GATE_TAU = 16.0
MLA_HEADS = 8
MLA_Q_RANK = 768
MLA_KV_RANK = 256
MLA_NOPE = 128
MLA_ROPE = 64
MLA_V = 128
ROPE_THETA = 10000.0
N_EXPERTS = 256
TOP_K = 8
N_GROUPS = 8
TOPK_GROUPS = 4
D_EXPERT = 256
ROUTED_SCALE = 2.5
LN_EPS = 1e-5
RMS_EPS = 1e-6
LOG2E = 1.4426950408889634

LANES = 128
VMEM_LIMIT = 56 * 1024 * 1024
EXPERT_BLOCK = 512
EXPERT_RING = 4
MOE_FIRST_SIXTEENTHS = 11
COMBINE_TOKENS = 4096

F32 = jnp.float32
BF16 = jnp.bfloat16
I32 = jnp.int32


def _cparams(*sem):
    return pltpu.CompilerParams(dimension_semantics=sem, vmem_limit_bytes=VMEM_LIMIT)


def _sigmoid(x):
    return 1.0 / (1.0 + jnp.exp(-x))


def _silu(x):
    return x * _sigmoid(x)


def _dot(a, b):
    return jnp.dot(a, b, preferred_element_type=F32)


def _dot_nt(a, b):
    return lax.dot_general(a, b, (((1,), (1,)), ((), ())), preferred_element_type=F32)


def _dot_tn(a, b):
    return lax.dot_general(a, b, (((0,), (0,)), ((), ())), preferred_element_type=F32)


def _pack_bf16_pair(x):
    w = x.shape[1] // 2
    u = lax.bitcast_convert_type(x.astype(BF16).astype(F32), I32)
    lo = lax.shift_right_logical(u[:, :w], jnp.int32(16))
    hi = jnp.bitwise_and(u[:, w:], jnp.int32(-65536))
    return jnp.bitwise_or(lo, hi)


def _unpack_bf16_pair(p):
    lo = lax.bitcast_convert_type(lax.shift_left(p, jnp.int32(16)), F32)
    hi = lax.bitcast_convert_type(jnp.bitwise_and(p, jnp.int32(-65536)), F32)
    return lo, hi


def _mod_kernel(c_ref, w_ref, b_ref, o_ref):
    cond = _silu(c_ref[...]).astype(BF16)
    o_ref[...] = _dot(cond, w_ref[...].astype(BF16)) + b_ref[...]


def _modulation(c, w_ada, b_ada):
    B, D = c.shape
    W = w_ada.shape[1]
    tn = D
    return pl.pallas_call(
        _mod_kernel,
        out_shape=jax.ShapeDtypeStruct((B, W), F32),
        grid=(W // tn,),
        in_specs=[
            pl.BlockSpec((B, D), lambda j: (0, 0)),
            pl.BlockSpec((D, tn), lambda j: (0, j)),
            pl.BlockSpec((1, tn), lambda j: (0, j)),
        ],
        out_specs=pl.BlockSpec((B, tn), lambda j: (0, j)),
        compiler_params=_cparams("arbitrary"),
        name="adaln_mod",
    )(c, w_ada, b_ada.reshape(1, W))


def _rope_kernel(pos_ref, f_ref, ph_ref, cc_ref, ss_ref):
    ang = pos_ref[...] * f_ref[...]
    cc_ref[...] = jnp.cos(ang)
    ss_ref[...] = jnp.sin(ang) * ph_ref[...]


def _rope_tables(positions):
    B, S = positions.shape
    N = B * S
    half = MLA_ROPE // 2
    inv_freq = ROPE_THETA ** (-jnp.arange(half, dtype=F32) * (2.0 / MLA_ROPE))
    f4 = jnp.tile(inv_freq, LANES // half).reshape(1, LANES)
    sign = jnp.tile(jnp.concatenate([-jnp.ones((half,), F32), jnp.ones((half,), F32)]), LANES // MLA_ROPE)
    pos = jnp.broadcast_to(positions.reshape(N, 1).astype(F32), (N, LANES))
    tm = min(N, 1024)
    spec = pl.BlockSpec((tm, LANES), lambda i: (i, 0))
    vec = pl.BlockSpec((1, LANES), lambda i: (0, 0))
    return pl.pallas_call(
        _rope_kernel,
        out_shape=(jax.ShapeDtypeStruct((N, LANES), F32), jax.ShapeDtypeStruct((N, LANES), F32)),
        grid=(N // tm,),
        in_specs=[spec, vec, vec],
        out_specs=(spec, spec),
        compiler_params=_cparams("arbitrary"),
        name="rope_tables",
    )(pos, f4, sign.reshape(1, LANES))


QK_W = 2 * GLA_HEADS * GLA_DK
GV_W = GLA_HEADS * GLA_DV
TAIL_W = 3 * LANES


def _in_segments(D):
    return (QK_W, GV_W, GV_W, MLA_Q_RANK, MLA_KV_RANK, D, D, TAIL_W)


def _prep_w_in(w_in, D):
    s = [GLA_HEADS * GLA_DK, GLA_HEADS * GLA_DK, GV_W, GV_W, GLA_GATE_RANK, MLA_Q_RANK, MLA_KV_RANK, MLA_ROPE, D, D]
    offs = [0]
    for n in s:
        offs.append(offs[-1] + n)
    gq, gk, gv, gr, ga, cq, ckv, kr, gate_a, gate_b = [w_in[:, offs[i]:offs[i + 1]] for i in range(10)]
    half = MLA_ROPE // 2
    kr_sw = jnp.concatenate([kr[:, half:], kr[:, :half]], axis=1)
    pad = jnp.zeros((w_in.shape[0], LANES - GLA_GATE_RANK), w_in.dtype)
    return jnp.concatenate([gq, gk, gv, gr, cq, ckv, gate_a, gate_b, kr, kr, kr_sw, kr_sw, ga, pad], axis=1).astype(BF16)


def _inproj_kernel(x_ref, mod_ref, w_ref, *out_refs):
    sh1 = mod_ref[0, 0:1, :]
    sc1 = mod_ref[0, 1:2, :]
    h = (x_ref[...] * (1.0 + sc1) + sh1).astype(BF16)
    off = 0
    for ref in out_refs:
        n = ref.shape[-1]
        ref[...] = _dot(h, w_ref[:, off:off + n]).astype(ref.dtype)
        off += n


def _in_projection(x2, mod6, w_in_p, S):
    N, D = x2.shape
    W = w_in_p.shape[1]
    tm = min(S, 512)
    per_b = S // tm
    segs = _in_segments(D)
    assert sum(segs) == W
    return pl.pallas_call(
        _inproj_kernel,
        out_shape=tuple(jax.ShapeDtypeStruct((N, n), BF16) for n in segs),
        grid=(N // tm,),
        in_specs=[
            pl.BlockSpec((tm, D), lambda i: (i, 0)),
            pl.BlockSpec((1, 6, D), lambda i: (i // per_b, 0, 0)),
            pl.BlockSpec((D, W), lambda i: (0, 0), pipeline_mode=pl.Buffered(1)),
        ],
        out_specs=tuple(pl.BlockSpec((tm, n), lambda i: (i, 0)) for n in segs),
        compiler_params=_cparams("arbitrary"),
        name="in_proj",
    )(x2, mod6, w_in_p)


def _gla_kernel(qk_ref, v_ref, gr_ref, tail_ref, wa2_ref, ba_ref, gn_ref, wo_ref, y_ref,
                st_ref, kd_ref, dec_ref, sall_ref, o_ref):
    t = pl.program_id(1)

    @pl.when(t == 0)
    def _():
        st_ref[...] = jnp.zeros_like(st_ref)

    ts = qk_ref.shape[0]
    nch = ts // CHUNK
    HK = GLA_HEADS * GLA_DK
    r = lax.broadcasted_iota(I32, (CHUNK, CHUNK), 0)
    c = lax.broadcasted_iota(I32, (CHUNK, CHUNK), 1)
    tri = (r >= c).astype(BF16)
    qscale = GLA_DK ** -0.5

    z = _dot(tail_ref[:, 2 * LANES:3 * LANES], wa2_ref[...]) + ba_ref[...]
    log_a = (jnp.minimum(z, 0.0) - jnp.log(1.0 + jnp.exp(-jnp.abs(z)))) * (1.0 / GLA_GATE_TAU)
    la_hi = log_a.astype(BF16)
    la_lo = (log_a - la_hi.astype(F32)).astype(BF16)
    for n in range(nch):
        rows = slice(n * CHUNK, (n + 1) * CHUNK)
        G = _dot(tri, la_hi[rows]) + _dot(tri, la_lo[rows])
        g_end = G[CHUNK - 1:CHUNK, :]
        kd_ref[rows, :] = (qk_ref[rows, HK:2 * HK].astype(F32) * jnp.exp(g_end - G)).astype(BF16)
        dec_ref[n:n + 1, :] = jnp.exp(g_end)

    for h in range(GLA_HEADS):
        ks = slice(h * GLA_DK, (h + 1) * GLA_DK)
        vs = slice(h * GLA_DV, (h + 1) * GLA_DV)
        st = st_ref[h]
        for n in range(nch):
            rows = slice(n * CHUNK, (n + 1) * CHUNK)
            st = st * dec_ref[n:n + 1, ks] + _dot_tn(v_ref[rows, vs], kd_ref[rows, ks])
            sall_ref[n * GLA_HEADS + h] = st.astype(BF16)
        st_ref[h] = st

    for n in range(nch):
        rows = slice(n * CHUNK, (n + 1) * CHUNK)
        for h in range(GLA_HEADS):
            ks = slice(h * GLA_DK, (h + 1) * GLA_DK)
            vs = slice(h * GLA_DV, (h + 1) * GLA_DV)
            qh = (qk_ref[rows, ks].astype(F32) * qscale).astype(BF16)
            o = _dot_nt(qh, sall_ref[n * GLA_HEADS + h])
            o = o * lax.rsqrt(jnp.mean(o * o, axis=-1, keepdims=True) + RMS_EPS) * gn_ref[...]
            o_ref[rows, vs] = (o * _silu(gr_ref[rows, vs].astype(F32))).astype(BF16)
    y_ref[...] = _dot(o_ref[...], wo_ref[...]).astype(y_ref.dtype)


def _gla(qk, gv, gr, tail, wa2_p, b_a, g_norm, w_o, B, S):
    N = B * S
    D = w_o.shape[1]
    ts = min(S, 512)
    per_b = S // ts
    HK = GLA_HEADS * GLA_DK
    tok = lambda n: pl.BlockSpec((ts, n), lambda b, t: (b * per_b + t, 0))
    full = lambda a: pl.BlockSpec(a.shape, lambda b, t: (0,) * a.ndim)
    return pl.pallas_call(
        _gla_kernel,
        out_shape=jax.ShapeDtypeStruct((N, D), BF16),
        grid=(B, per_b),
        in_specs=[tok(QK_W), tok(GV_W), tok(GV_W), tok(TAIL_W), full(wa2_p), full(b_a), full(g_norm), full(w_o)],
        out_specs=tok(D),
        scratch_shapes=[pltpu.VMEM((GLA_HEADS, GLA_DV, GLA_DK), F32),
                        pltpu.VMEM((ts, GLA_HEADS * GLA_DK), BF16),
                        pltpu.VMEM((ts // CHUNK, GLA_HEADS * GLA_DK), F32),
                        pltpu.VMEM((ts // CHUNK * GLA_HEADS, GLA_DV, GLA_DK), BF16),
                        pltpu.VMEM((ts, GV_W), BF16)],
        compiler_params=_cparams("arbitrary", "arbitrary"),
        name="gla",
    )(qk, gv, gr, tail, wa2_p, b_a, g_norm, w_o)


HQ = MLA_HEADS * 2 * LANES


def _prep_w_uq(w_uq):
    dh = MLA_NOPE + MLA_ROPE
    half = MLA_ROPE // 2
    nope = [w_uq[:, h * dh:h * dh + MLA_NOPE] for h in range(MLA_HEADS)]
    rope = [w_uq[:, h * dh + MLA_NOPE:(h + 1) * dh] for h in range(MLA_HEADS)]
    rope_sw = [jnp.concatenate([r[:, half:], r[:, :half]], axis=1) for r in rope]
    return jnp.concatenate(nope + rope + rope_sw, axis=1).astype(BF16)


def _prep_w_ukv(w_ukv):
    dh = MLA_NOPE + MLA_V
    kn = [w_ukv[:, h * dh:h * dh + MLA_NOPE] for h in range(MLA_HEADS)]
    vv = [w_ukv[:, h * dh + MLA_NOPE:(h + 1) * dh] for h in range(MLA_HEADS)]
    return jnp.concatenate(kn + vv, axis=1).astype(BF16)


def _rms(x, g):
    return x * lax.rsqrt(jnp.mean(x * x, axis=-1, keepdims=True) + RMS_EPS) * g


def _mla_prep_kernel(cq_ref, ckv_ref, tail_ref, cc_ref, ss_ref, gq_ref, wq_ref, gkv_ref, wkv_ref, q_ref, k_ref, v_ref):
    tm = cq_ref.shape[0]
    NP = MLA_HEADS * MLA_NOPE
    RP = MLA_HEADS * MLA_ROPE
    scale = (MLA_NOPE + MLA_ROPE) ** -0.5 * LOG2E
    cc = cc_ref[...]
    ss = ss_ref[...]
    cqn = _rms(cq_ref[...].astype(F32), gq_ref[...]).astype(BF16)
    qf = _dot(cqn, wq_ref[...]) * scale
    ckvn = _rms(ckv_ref[...].astype(F32), gkv_ref[...]).astype(BF16)
    kv = _dot(ckvn, wkv_ref[...])
    krr = (tail_ref[:, 0:LANES].astype(F32) * cc + tail_ref[:, LANES:2 * LANES].astype(F32) * ss).astype(BF16)
    lane = lax.broadcasted_iota(I32, (tm, LANES), 1)
    first = lane < MLA_ROPE
    for j in range(MLA_HEADS // 2):
        a = NP + j * LANES
        rot = qf[:, a:a + LANES] * cc + qf[:, a + RP:a + RP + LANES] * ss
        for h, keep in ((2 * j, first), (2 * j + 1, jnp.logical_not(first))):
            base = h * 2 * LANES
            q_ref[:, base:base + LANES] = qf[:, h * MLA_NOPE:(h + 1) * MLA_NOPE].astype(BF16)
            q_ref[:, base + LANES:base + 2 * LANES] = jnp.where(keep, rot, 0.0).astype(BF16)
            k_ref[:, base:base + LANES] = kv[:, h * MLA_NOPE:(h + 1) * MLA_NOPE].astype(BF16)
            k_ref[:, base + LANES:base + 2 * LANES] = krr
    v_ref[...] = kv[:, NP:].astype(BF16)


def _mla_prep(cq, ckv, tail, cc, ss, g_cq, w_uq_p, g_ckv, w_ukv_p):
    N = cq.shape[0]
    tm = min(N, 512)
    tok = lambda n: pl.BlockSpec((tm, n), lambda i: (i, 0))
    full = lambda a: pl.BlockSpec(a.shape, lambda i: (0,) * a.ndim)
    HV = MLA_HEADS * MLA_V
    return pl.pallas_call(
        _mla_prep_kernel,
        out_shape=(jax.ShapeDtypeStruct((N, HQ), BF16), jax.ShapeDtypeStruct((N, HQ), BF16),
                   jax.ShapeDtypeStruct((N, HV), BF16)),
        grid=(N // tm,),
        in_specs=[tok(MLA_Q_RANK), tok(MLA_KV_RANK), tok(TAIL_W), tok(LANES), tok(LANES),
                  full(g_cq), full(w_uq_p), full(g_ckv), full(w_ukv_p)],
        out_specs=(tok(HQ), tok(HQ), tok(HV)),
        compiler_params=_cparams("arbitrary"),
        name="mla_prep",
    )(cq, ckv, tail, cc, ss, g_cq, w_uq_p, g_ckv, w_ukv_p)


ATTN_TQ = 256


def _mla_attn_kernel(q_ref, k_ref, v_ref, o_ref, v1_ref):
    S = q_ref.shape[0]
    tq = min(S, ATTN_TQ)
    r = lax.broadcasted_iota(I32, (tq, tq), 0) // CHUNK
    c = lax.broadcasted_iota(I32, (tq, tq), 1) // CHUNK
    diag_mask = c <= r
    v1_ref[:, :MLA_V] = v_ref[...]
    v1_ref[:, MLA_V:] = jnp.ones((S, MLA_V), BF16)

    def scores(ii):
        l0 = ii * tq
        q = q_ref[l0:l0 + tq, :]
        sd = jnp.where(diag_mask, _dot_nt(q, k_ref[l0:l0 + tq, :]), -jnp.inf)
        so = _dot_nt(q, k_ref[0:l0, :]) if ii > 0 else None
        return sd, so

    def finish(ii, sd, so):
        l0 = ii * tq
        m = jnp.max(sd, axis=-1, keepdims=True)
        if so is not None:
            m = jnp.maximum(m, jnp.max(so, axis=-1, keepdims=True))
        acc = _dot(jnp.exp2((sd - m).astype(BF16)), v1_ref[l0:l0 + tq, :])
        if so is not None:
            acc = acc + _dot(jnp.exp2((so - m).astype(BF16)), v1_ref[0:l0, :])
        o_ref[l0:l0 + tq, :] = (acc[:, :MLA_V] / acc[:, MLA_V:]).astype(BF16)

    n_tiles = S // tq
    ahead = 2
    pending = [scores(ii) for ii in range(min(ahead, n_tiles))]
    for ii in range(n_tiles):
        if ii + ahead < n_tiles:
            pending.append(scores(ii + ahead))
        finish(ii, *pending.pop(0))


def _mla_attention(qc, kc, vv, B, S):
    N = B * S
    HV = MLA_HEADS * MLA_V
    return pl.pallas_call(
        _mla_attn_kernel,
        out_shape=jax.ShapeDtypeStruct((N, HV), BF16),
        grid=(B, MLA_HEADS),
        in_specs=[
            pl.BlockSpec((S, 2 * LANES), lambda b, h: (b, h)),
            pl.BlockSpec((S, 2 * LANES), lambda b, h: (b, h)),
            pl.BlockSpec((S, MLA_V), lambda b, h: (b, h)),
        ],
        out_specs=pl.BlockSpec((S, MLA_V), lambda b, h: (b, h)),
        scratch_shapes=[pltpu.VMEM((S, 2 * MLA_V), BF16)],
        compiler_params=_cparams("arbitrary", "arbitrary"),
        name="mla_attn",
    )(qc, kc, vv)


def _layer_norm(u, g, b):
    mu = jnp.mean(u, axis=-1, keepdims=True)
    d = u - mu
    var = jnp.mean(d * d, axis=-1, keepdims=True)
    return d * lax.rsqrt(var + LN_EPS) * g + b


def _merge_kernel(alpha, x_ref, ya_ref, ob_ref, ga_ref, gb_ref, mod_ref, wmo_ref, wout_ref, g_ref, b_ref, wr_ref,
                  br_ref, ws13_ref, ws2_ref, x1_ref, h2_ref, sh_ref, idx_ref, ww_ref, rank_ref, cnt_ref, lg_s, carry_s):
    i = pl.program_id(0)

    @pl.when(i == 0)
    def _():
        lg_s[...] = jnp.zeros_like(lg_s)
        carry_s[...] = jnp.zeros_like(carry_s)

    idx, ww, rank, total = _route_math(lg_s[...], br_ref[...], carry_s[...])
    total = jnp.where(i > 0, total, 0.0)
    idx_ref[...] = idx
    ww_ref[...] = ww
    rank_ref[...] = rank
    carry_s[...] = total
    cnt_ref[...] = jnp.broadcast_to(total, cnt_ref.shape).astype(I32)

    gt1 = mod_ref[0, 2:3, :]
    sh2 = mod_ref[0, 3:4, :]
    sc2 = mod_ref[0, 4:5, :]
    y = (_sigmoid(ga_ref[...].astype(F32)) * ya_ref[...].astype(F32)
         + _sigmoid(gb_ref[...].astype(F32)) * _dot(ob_ref[...], wmo_ref[...]))
    mix = _dot(y.astype(BF16), wout_ref[...])
    x1 = _layer_norm(alpha * x_ref[...] + (1.0 + gt1) * mix, g_ref[...], b_ref[...])
    x1_ref[...] = x1
    h2 = x1 * (1.0 + sc2) + sh2
    h2_ref[...] = _pack_bf16_pair(h2)
    h2b = h2.astype(BF16)
    lg_s[...] = _dot_nt(wr_ref[...], h2b)
    F = ws2_ref.shape[0]
    ab = _dot(h2b, ws13_ref[...])
    hid = (_silu(ab[:, :F]) * ab[:, F:]).astype(BF16)
    sh_ref[...] = _dot(hid, ws2_ref[...]).astype(sh_ref.dtype)


def _merge_route(alpha, x2, y_a, o_b, gate_a, gate_b, mod6, w_mla_o, w_out, ln_g, ln_b, wr_t, b_router, ws13, ws2,
                 S, tok0, Np):
    N, D = x2.shape
    E = wr_t.shape[0]
    tm = min(S, 512)
    per_b = S // tm
    assert Np % tm == 0 and tok0 % tm == 0
    n = Np // tm
    base = tok0 // tm
    cur = lambda i: jnp.minimum(i, n - 1)
    prev = lambda i: jnp.maximum(i - 1, 0)
    tok_in = lambda w: pl.BlockSpec((tm, w), lambda i: (base + cur(i), 0))
    tok = lambda w: pl.BlockSpec((tm, w), lambda i: (cur(i), 0))
    full = lambda a: pl.BlockSpec(a.shape, lambda i: (0,) * a.ndim)
    lane_blk = pl.BlockSpec((TOP_K, tm), lambda i: (0, prev(i)))
    br = b_router.reshape(E, 1).astype(F32)
    return pl.pallas_call(
        functools.partial(_merge_kernel, alpha),
        out_shape=(jax.ShapeDtypeStruct((Np, D), F32), jax.ShapeDtypeStruct((Np, D // 2), I32),
                   jax.ShapeDtypeStruct((Np, D), BF16),
                   jax.ShapeDtypeStruct((TOP_K, Np), I32), jax.ShapeDtypeStruct((TOP_K, Np), I32),
                   jax.ShapeDtypeStruct((TOP_K, Np), I32), jax.ShapeDtypeStruct((E, LANES), I32)),
        grid=(n + 1,),
        in_specs=[tok_in(D), tok_in(D), tok_in(D), tok_in(D), tok_in(D),
                  pl.BlockSpec((1, 6, D), lambda i: ((base + cur(i)) // per_b, 0, 0)),
                  full(w_mla_o), full(w_out), full(ln_g), full(ln_b), full(wr_t), full(br), full(ws13), full(ws2)],
        out_specs=(tok(D), tok(D // 2), tok(D), lane_blk, lane_blk, lane_blk,
                   pl.BlockSpec((E, LANES), lambda i: (0, 0))),
        scratch_shapes=[pltpu.VMEM((E, tm), F32), pltpu.VMEM((E, 1), F32)],
        compiler_params=_cparams("arbitrary"),
        name="merge_ln1_route",
    )(x2, y_a, o_b, gate_a, gate_b, mod6, w_mla_o, w_out, ln_g, ln_b, wr_t, br, ws13, ws2)


def _first_argmax(v, io, n):
    m = jnp.max(v, axis=0, keepdims=True)
    idx = jnp.min(jnp.where(v == m, io, n), axis=0, keepdims=True)
    return m, idx


def _route_math(lg, br, carry):
    E, T = lg.shape
    gsz = E // N_GROUPS
    neg = -jnp.inf
    s = _sigmoid(lg)
    biased = s + br
    eio = lax.broadcasted_iota(I32, (E, T), 0)
    gio = lax.broadcasted_iota(I32, (gsz, T), 0)

    gs = []
    for g in range(N_GROUPS):
        blk = biased[g * gsz:(g + 1) * gsz]
        m1, i1 = _first_argmax(blk, gio, gsz)
        m2 = jnp.max(jnp.where(gio == i1, neg, blk), axis=0, keepdims=True)
        gs.append(m1 + m2)
    cur = jnp.concatenate(gs, axis=0)
    nio = lax.broadcasted_iota(I32, (N_GROUPS, T), 0)
    gsel = jnp.zeros((N_GROUPS, T), F32)
    for _ in range(TOPK_GROUPS):
        _, gi = _first_argmax(cur, nio, N_GROUPS)
        hit = nio == gi
        gsel = jnp.where(hit, 1.0, gsel)
        cur = jnp.where(hit, neg, cur)
    emask = jnp.concatenate([jnp.broadcast_to(gsel[g:g + 1], (gsz, T)) for g in range(N_GROUPS)], axis=0) > 0.0

    cur = jnp.where(emask, biased, neg)
    idxs, ws = [], []
    sel = jnp.zeros((E, T), F32)
    for _ in range(TOP_K):
        _, ei = _first_argmax(cur, eio, E)
        hit = eio == ei
        idxs.append(ei)
        ws.append(jnp.sum(jnp.where(hit, s, 0.0), axis=0, keepdims=True))
        sel = jnp.where(hit, 1.0, sel)
        cur = jnp.where(hit, neg, cur)
    w = jnp.concatenate(ws, axis=0)
    w = w / jnp.sum(w, axis=0, keepdims=True) * ROUTED_SCALE
    wbits = lax.bitcast_convert_type(w.astype(BF16).astype(F32), I32)
    wword = jnp.bitwise_or(lax.shift_right_logical(wbits, jnp.int32(16)), jnp.bitwise_and(wbits, jnp.int32(-65536)))

    r = lax.broadcasted_iota(I32, (T, T), 0)
    c = lax.broadcasted_iota(I32, (T, T), 1)
    before = (r < c).astype(BF16)
    cnt = _dot(sel.astype(BF16), before) + carry
    ranks = [jnp.sum(jnp.where(eio == idxs[k], cnt, 0.0), axis=0, keepdims=True) for k in range(TOP_K)]
    total = cnt[:, T - 1:T] + sel[:, T - 1:T]
    return jnp.concatenate(idxs, axis=0), wword, jnp.concatenate(ranks, axis=0).astype(I32), total


def _dest_kernel(idx_ref, rank_ref, ps_ref, dest_ref):
    K, T = idx_ref.shape
    E = ps_ref.shape[0]
    eio = lax.broadcasted_iota(I32, (E, T), 0)
    ps = ps_ref[...]
    rows = [jnp.sum(jnp.where(eio == idx_ref[k:k + 1, :], ps, 0), axis=0, keepdims=True) for k in range(K)]
    dest_ref[...] = jnp.concatenate(rows, axis=0) + rank_ref[...]


def _dest_slots(idx, rank, pad_start):
    K, N = idx.shape
    E = pad_start.shape[0]
    T = min(N, 512)
    spec = pl.BlockSpec((K, T), lambda i: (0, i))
    return pl.pallas_call(
        _dest_kernel,
        out_shape=jax.ShapeDtypeStruct((K, N), I32),
        grid=(N // T,),
        in_specs=[spec, spec, pl.BlockSpec((E, 1), lambda i: (0, 0))],
        out_specs=spec,
        compiler_params=_cparams("arbitrary"),
        name="dest_slots",
    )(idx, rank, pad_start.reshape(E, 1))


SC_ROWS = 64


def _sc_workers():
    info = plsc.get_sparse_core_info()
    return info.num_cores, info.num_subcores


def _sc_lanes():
    return plsc.get_sparse_core_info().num_lanes


def _dispatch_rows(h2p, dest, n_slots):
    N, W = h2p.shape
    K = dest.shape[0]
    R = SC_ROWS
    nc, ns = _sc_workers()
    per_w = N // R // (nc * ns)
    assert per_w * R * nc * ns == N and per_w % 2 == 0, "token count must split into chunk pairs per subcore"
    mesh = plsc.VectorSubcoreMesh(core_axis_name="c", subcore_axis_name="s")

    @functools.partial(
        pl.kernel, mesh=mesh,
        out_type=jax.ShapeDtypeStruct((n_slots, W), I32),
        scratch_types=[pltpu.VMEM((K, R), I32), pltpu.VMEM((K, R), I32),
                       pltpu.VMEM((R, W), I32), pltpu.VMEM((R, W), I32),
                       pltpu.SemaphoreType.DMA, pltpu.SemaphoreType.DMA, pltpu.SemaphoreType.DMA],
    )
    def k(h_hbm, d_hbm, xs_hbm, idx0, idx1, rows0, rows1, lsem0, lsem1, ssem):
        idx, rows, lsem = (idx0, idx1), (rows0, rows1), (lsem0, lsem1)
        base = (lax.axis_index("s") * nc + lax.axis_index("c")) * per_w

        def loads(ch, b):
            tok = pl.ds(pl.multiple_of(ch * R, 8), R)
            return [pltpu.make_async_copy(d_hbm.at[kk, tok], idx[b].at[kk], lsem[b]) for kk in range(K)] + [
                pltpu.make_async_copy(h_hbm.at[tok], rows[b], lsem[b])]

        for cp in loads(base, 0):
            cp.start()
        for cp in loads(base, 0):
            cp.wait()

        @pl.loop(0, per_w, step=2)
        def _(j):
            for b in range(2):
                ch = base + j + b
                more = j + b + 1 < per_w

                @pl.when(more)
                def _():
                    for cp in loads(ch + 1, 1 - b):
                        cp.start()

                scatters = [pltpu.make_async_copy(rows[b], xs_hbm.at[idx[b].at[kk]], ssem) for kk in range(K)]
                for cp in scatters:
                    cp.start()
                for cp in scatters:
                    cp.wait()

                @pl.when(more)
                def _():
                    for cp in loads(ch + 1, 1 - b):
                        cp.wait()

    return k(h2p, dest)


COMBINE_ROWS = 8


def _combine_sum(ys, dest, ww, tok0, n_tok):
    W = ys.shape[1]
    K = dest.shape[0]
    R = COMBINE_ROWS
    L = _sc_lanes()
    nc, ns = _sc_workers()
    per_w = n_tok // R // (nc * ns)
    assert per_w * R * nc * ns == n_tok and per_w % 2 == 0, "token count must split into chunk pairs per subcore"
    T = per_w * R
    mesh = plsc.VectorSubcoreMesh(core_axis_name="c", subcore_axis_name="s")

    @functools.partial(
        pl.kernel, mesh=mesh, compiler_params=pltpu.CompilerParams(needs_layout_passes=False),
        out_type=jax.ShapeDtypeStruct((n_tok, W), I32),
        scratch_types=[pltpu.VMEM((K * T,), I32), pltpu.VMEM((K * T,), I32),
                       pltpu.VMEM((K, R, W), I32), pltpu.VMEM((K, R, W), I32),
                       pltpu.VMEM((R, W), I32), pltpu.VMEM((R, W), I32),
                       pltpu.SemaphoreType.DMA, pltpu.SemaphoreType.DMA,
                       pltpu.SemaphoreType.DMA, pltpu.SemaphoreType.DMA],
    )
    def k(ys_hbm, d_hbm, w_hbm, o_hbm, idx_all, w_all, buf0, buf1, out0, out1, gsem0, gsem1, osem0, osem1):
        buf, outv, gsem, osem = (buf0, buf1), (out0, out1), (gsem0, gsem1), (osem0, osem1)
        wid = lax.axis_index("s") * nc + lax.axis_index("c")
        t0 = pl.multiple_of(tok0 + wid * T, 8)
        for kk in range(K):
            pltpu.sync_copy(d_hbm.at[kk, pl.ds(t0, T)], idx_all.at[pl.ds(kk * T, T)])
            pltpu.sync_copy(w_hbm.at[kk, pl.ds(t0, T)], w_all.at[pl.ds(kk * T, T)])

        def fetch(j, b):
            return [pltpu.make_async_copy(
                ys_hbm.at[idx_all.at[pl.ds(pl.multiple_of(kk * T + j * R, 8), R)]], buf[b].at[kk], gsem[b])
                for kk in range(K)]

        def store(j, b):
            return pltpu.make_async_copy(outv[b], o_hbm.at[pl.ds(pl.multiple_of(wid * T + j * R, 8), R)], osem[b])

        for cp in fetch(0, 0):
            cp.start()

        @pl.loop(0, per_w, step=2)
        def _(j0):
            for b in range(2):
                j = j0 + b

                @pl.when(j + 1 < per_w)
                def _():
                    for cp in fetch(j + 1, 1 - b):
                        cp.start()

                for cp in fetch(j, b):
                    cp.wait()

                @pl.when(j >= 2)
                def _():
                    store(j - 2, b).wait()

                @pl.loop(0, R)
                def _(r):
                    ws = [plsc.bitcast(plsc.load_gather(w_all, [jnp.full((L,), kk * T + j * R + r, I32)]), BF16)
                          for kk in range(K)]

                    @pl.loop(0, W // L)
                    def _(q):
                        sl = pl.ds(q * L, L)
                        terms = [ws[kk] * plsc.bitcast(buf[b][kk, r, sl], BF16) for kk in range(K)]
                        while len(terms) > 1:
                            terms = [terms[i] + terms[i + 1] for i in range(0, len(terms), 2)]
                        outv[b][r, sl] = plsc.bitcast(terms[0], I32)

                store(j, b).start()

        store(per_w - 2, 0).wait()
        store(per_w - 1, 1).wait()

    return k(ys, dest, ww)


def _expert_kernel(b0_ref, nb_ref, cnt_ref, xs_hbm, w1_ref, w3_ref, w2_ref, ys_hbm,
                   w13_s, w2_s, xbuf, ybuf, sem_in, sem_out):
    e = pl.program_id(0)
    n_exp = pl.num_programs(0)
    F = w1_ref.shape[2]
    bm = xbuf.shape[1]
    half = xbuf.shape[2]
    nb = nb_ref[e]
    b0 = b0_ref[e]
    total = b0_ref[n_exp - 1] + nb_ref[n_exp - 1]

    nbuf = xbuf.shape[0]

    def in_copy(g):
        slot = g % nbuf
        return pltpu.make_async_copy(xs_hbm.at[pl.ds(g * bm, bm)], xbuf.at[slot], sem_in.at[slot])

    def out_copy(g):
        slot = g % nbuf
        return pltpu.make_async_copy(ybuf.at[slot], ys_hbm.at[pl.ds(g * bm, bm)], sem_out.at[slot])

    @pl.when(e == 0)
    def _():
        for g0 in range(nbuf - 1):
            @pl.when(g0 < total)
            def _():
                in_copy(g0).start()

    @pl.when(nb > 0)
    def _():
        w13_s[:, :F] = w1_ref[0].astype(BF16)
        w13_s[:, F:] = w3_ref[0].astype(BF16)
        w2_s[...] = w2_ref[0].astype(BF16)

    def block(j, carry):
        g = b0 + j
        slot = g % nbuf
        in_copy(g).wait()

        @pl.when(g + nbuf - 1 < total)
        def _():
            in_copy(g + nbuf - 1).start()

        @pl.when(g >= nbuf)
        def _():
            out_copy(g - nbuf).wait()

        def swiglu(rows):
            lo, hi = _unpack_bf16_pair(xbuf[slot, :rows, :])
            ab = _dot(lo.astype(BF16), w13_s[:half, :]) + _dot(hi.astype(BF16), w13_s[half:, :])
            hid = (_silu(ab[:, :F]) * ab[:, F:]).astype(BF16)
            ybuf[slot, :rows, :] = _pack_bf16_pair(_dot(hid, w2_s[...]))

        rows_left = cnt_ref[e] - j * bm

        @pl.when(rows_left > bm // 2)
        def _():
            swiglu(bm)

        @pl.when(rows_left <= bm // 2)
        def _():
            swiglu(bm // 2)
            ybuf[slot, bm // 2:, :] = jnp.zeros((bm - bm // 2, half), I32)

        out_copy(g).start()
        return carry

    lax.fori_loop(0, nb, block, 0)

    @pl.when(e == n_exp - 1)
    def _():
        for back in range(nbuf, 0, -1):
            @pl.when(total >= back)
            def _():
                out_copy(total - back).wait()


def _experts(xs, blk0, nblk, counts, w_e1, w_e3, w_e2):
    n_slots, W = xs.shape
    E, D, F = w_e1.shape
    bm = EXPERT_BLOCK
    grid_spec = pltpu.PrefetchScalarGridSpec(
        num_scalar_prefetch=3,
        grid=(E,),
        in_specs=[
            pl.BlockSpec(memory_space=pl.ANY),
            pl.BlockSpec((1, D, F), lambda e, b0, nb, cnt: (e, 0, 0)),
            pl.BlockSpec((1, D, F), lambda e, b0, nb, cnt: (e, 0, 0)),
            pl.BlockSpec((1, F, D), lambda e, b0, nb, cnt: (e, 0, 0)),
        ],
        out_specs=pl.BlockSpec(memory_space=pl.ANY),
        scratch_shapes=[pltpu.VMEM((D, 2 * F), BF16), pltpu.VMEM((F, D), BF16),
                        pltpu.VMEM((EXPERT_RING, bm, W), I32), pltpu.VMEM((EXPERT_RING, bm, W), I32),
                        pltpu.SemaphoreType.DMA((EXPERT_RING,)), pltpu.SemaphoreType.DMA((EXPERT_RING,))],
    )
    return pl.pallas_call(
        _expert_kernel,
        out_shape=jax.ShapeDtypeStruct((n_slots, W), I32),
        grid_spec=grid_spec,
        compiler_params=_cparams("arbitrary"),
        name="experts",
    )(blk0, nblk, counts, xs, w_e1, w_e3, w_e2)


def _final_kernel(alpha, ms_ref, sh_ref, x1_ref, mod_ref, g_ref, b_ref, *rest):
    o_ref = rest[-1]
    gt2 = mod_ref[0, 5:6, :]
    lo, hi = _unpack_bf16_pair(ms_ref[...])
    ffn = jnp.concatenate([lo, hi], axis=1) + sh_ref[...].astype(F32)
    o_ref[...] = _layer_norm(alpha * x1_ref[...] + (1.0 + gt2) * ffn, g_ref[...], b_ref[...])


def _final_part(alpha, ms, sh, x1, mod6, ln_g, ln_b, S, tok0, row0, n_total, prev):
    D = x1.shape[1]
    N = n_total
    n_part, W = ms.shape
    tm = min(S, 512, n_part)
    assert n_part % tm == 0 and S % tm == 0 and tok0 % tm == 0 and row0 % tm == 0
    per_b = S // tm
    steps = n_part // tm
    off = tok0 // tm
    goff = row0 // tm
    tok = lambda n: pl.BlockSpec((tm, n), lambda i: (off + i, 0))
    full = lambda a: pl.BlockSpec(a.shape, lambda i: (0,) * a.ndim)
    in_specs = [pl.BlockSpec((tm, W), lambda i: (i, 0)), tok(D), tok(D),
                pl.BlockSpec((1, 6, D), lambda i: ((goff + i) // per_b, 0, 0)), full(ln_g), full(ln_b)]
    args = [ms, sh, x1, mod6, ln_g, ln_b]
    aliases = {}
    if prev is not None:
        in_specs.append(pl.BlockSpec(memory_space=pl.ANY))
        args.append(prev)
        aliases = {len(args) - 1: 0}
    return pl.pallas_call(
        functools.partial(_final_kernel, alpha),
        out_shape=jax.ShapeDtypeStruct((N, D), F32),
        grid=(steps,),
        in_specs=in_specs,
        out_specs=pl.BlockSpec((tm, D), lambda i: (goff + i, 0)),
        input_output_aliases=aliases,
        compiler_params=_cparams("arbitrary"),
        name="combine_ln2",
    )(*args)


def _moe(alpha, h2p, sh, idx, ww, rank, cnt, x1, mod6, w_e1, w_e3, w_e2, ln_g, ln_b, S, row0, n_total, out):
    N = x1.shape[0]
    E = w_e1.shape[0]
    bm = EXPERT_BLOCK
    counts = cnt[:, 0]
    padded = (counts + bm - 1) // bm * bm
    pad_end = jnp.cumsum(padded)
    pad_start = pad_end - padded
    n_slots = N * TOP_K + E * bm
    dest = _dest_slots(idx, rank, pad_start.astype(I32))
    xs = _dispatch_rows(h2p, dest, n_slots)
    ys = _experts(xs, (pad_start // bm).astype(I32), (padded // bm).astype(I32), counts.astype(I32),
                  w_e1, w_e3, w_e2)
    n_part = min(N, COMBINE_TOKENS) if row0 + N == n_total else N
    assert N % n_part == 0
    for p in range(N // n_part):
        tok0 = p * n_part
        ms = _combine_sum(ys, dest, ww, tok0, n_part)
        out = _final_part(alpha, ms, sh, x1, mod6, ln_g, ln_b, S, tok0, row0 + tok0, n_total, out)
    return out


def kernel(x, c, positions, w_ada, b_ada, w_in, w_gla_a2, b_gla_a, g_gla_norm, w_gla_o, g_cq, w_uq, g_ckv, w_ukv, w_mla_o, w_out, ln1_g, ln1_b, w_router, b_router, w_e1, w_e3, w_e2, w_s1, w_s3, w_s2, ln2_g, ln2_b):
    B, S, D = x.shape
    N = B * S
    depth = w_ada.shape[0]
    alpha = (2.0 * depth) ** 0.25
    row = lambda a: a.reshape(1, -1)
    cc, ss = _rope_tables(positions)
    x2 = x.reshape(N, D)
    for l in range(depth):
        mod6 = _modulation(c, w_ada[l], b_ada[l]).reshape(B, 6, D)
        qk, gv, gr, cq, ckv, gate_a, gate_b, tail = _in_projection(x2, mod6, _prep_w_in(w_in[l], D), S)
        wa2_p = jnp.concatenate(
            [w_gla_a2[l], jnp.zeros((LANES - GLA_GATE_RANK, w_gla_a2.shape[2]), F32)], axis=0).astype(BF16)
        y_a = _gla(qk, gv, gr, tail, wa2_p, row(b_gla_a[l]), row(g_gla_norm[l]), w_gla_o[l].astype(BF16), B, S)
        qc, kc, vv = _mla_prep(cq, ckv, tail, cc, ss, row(g_cq[l]), _prep_w_uq(w_uq[l]),
                               row(g_ckv[l]), _prep_w_ukv(w_ukv[l]))
        o_b = _mla_attention(qc, kc, vv, B, S)
        ws13 = jnp.concatenate([w_s1[l], w_s3[l]], axis=1).astype(BF16)
        ws2 = w_s2[l].astype(BF16)
        b_first = max(1, (B * MOE_FIRST_SIXTEENTHS) // 16) if B > 1 else B
        ranges = [(0, b_first * S)] + ([(b_first * S, (B - b_first) * S)] if B > b_first else [])
        routed =[_merge_route(alpha, x2, y_a, o_b, gate_a, gate_b, mod6, w_mla_o[l].astype(BF16),
                               w_out[l].astype(BF16), row(ln1_g[l]), row(ln1_b[l]), w_router[l].T.astype(BF16),
                               b_router[l], ws13, ws2, S, t0, n) for t0, n in ranges]
        out = None
        for (t0, n), (x1, h2p, sh, idx, ww, rank, cnt) in zip(ranges, routed):
            out = _moe(alpha, h2p, sh, idx, ww, rank, cnt, x1, mod6, w_e1[l], w_e3[l], w_e2[l],
                       row(ln2_g[l]), row(ln2_b[l]), S, t0, N, out)
        x2 = out
    return x2.reshape(B, S, D)
```

```python
import functools

import jax
import jax.numpy as jnp
from jax import lax
from jax.experimental import pallas as pl
from jax.experimental.pallas import tpu as pltpu
from jax.experimental.pallas import tpu_sc as plsc

CHUNK = 64
GLA_HEADS = 4
GLA_DK = 128
GLA_DV = 256
GLA_GATE_RANK = 16
GLA_GATE_TAU = 16.0
MLA_HEADS = 8
MLA_Q_RANK = 768
MLA_KV_RANK = 256
MLA_NOPE = 128
MLA_ROPE = 64
MLA_V = 128
ROPE_THETA = 10000.0
N_EXPERTS = 256
TOP_K = 8
N_GROUPS = 8
TOPK_GROUPS = 4
D_EXPERT = 256
ROUTED_SCALE = 2.5
LN_EPS = 1e-5
RMS_EPS = 1e-6
LOG2E = 1.4426950408889634

LANES = 128
VMEM_LIMIT = 56 * 1024 * 1024
EXPERT_BLOCK = 512
EXPERT_RING = 4
ROUTE_TOKENS = 128
MOE_FIRST_EIGHTHS = 5
COMBINE_TOKENS = 8192

F32 = jnp.float32
BF16 = jnp.bfloat16
I32 = jnp.int32


def _cparams(*sem):
    return pltpu.CompilerParams(dimension_semantics=sem, vmem_limit_bytes=VMEM_LIMIT)


def _sigmoid(x):
    return 1.0 / (1.0 + jnp.exp(-x))


def _silu(x):
    return x * _sigmoid(x)


def _dot(a, b):
    return jnp.dot(a, b, preferred_element_type=F32)


def _dot_nt(a, b):
    return lax.dot_general(a, b, (((1,), (1,)), ((), ())), preferred_element_type=F32)


def _dot_tn(a, b):
    return lax.dot_general(a, b, (((0,), (0,)), ((), ())), preferred_element_type=F32)


def _pack_bf16_pair(x):
    w = x.shape[1] // 2
    u = lax.bitcast_convert_type(x.astype(BF16).astype(F32), I32)
    lo = lax.shift_right_logical(u[:, :w], jnp.int32(16))
    hi = jnp.bitwise_and(u[:, w:], jnp.int32(-65536))
    return jnp.bitwise_or(lo, hi)


def _unpack_bf16_pair(p):
    lo = lax.bitcast_convert_type(lax.shift_left(p, jnp.int32(16)), F32)
    hi = lax.bitcast_convert_type(jnp.bitwise_and(p, jnp.int32(-65536)), F32)
    return lo, hi


def _mod_kernel(c_ref, w_ref, b_ref, o_ref):
    cond = _silu(c_ref[...]).astype(BF16)
    o_ref[...] = _dot(cond, w_ref[...].astype(BF16)) + b_ref[...]


def _modulation(c, w_ada, b_ada):
    B, D = c.shape
    W = w_ada.shape[1]
    tn = D
    return pl.pallas_call(
        _mod_kernel,
        out_shape=jax.ShapeDtypeStruct((B, W), F32),
        grid=(W // tn,),
        in_specs=[
            pl.BlockSpec((B, D), lambda j: (0, 0)),
            pl.BlockSpec((D, tn), lambda j: (0, j)),
            pl.BlockSpec((1, tn), lambda j: (0, j)),
        ],
        out_specs=pl.BlockSpec((B, tn), lambda j: (0, j)),
        compiler_params=_cparams("arbitrary"),
        name="adaln_mod",
    )(c, w_ada, b_ada.reshape(1, W))


def _rope_kernel(pos_ref, f_ref, ph_ref, cc_ref, ss_ref):
    ang = pos_ref[...] * f_ref[...]
    cc_ref[...] = jnp.cos(ang)
    ss_ref[...] = jnp.sin(ang) * ph_ref[...]


def _rope_tables(positions):
    B, S = positions.shape
    N = B * S
    half = MLA_ROPE // 2
    inv_freq = ROPE_THETA ** (-jnp.arange(half, dtype=F32) * (2.0 / MLA_ROPE))
    f4 = jnp.tile(inv_freq, LANES // half).reshape(1, LANES)
    sign = jnp.tile(jnp.concatenate([-jnp.ones((half,), F32), jnp.ones((half,), F32)]), LANES // MLA_ROPE)
    pos = jnp.broadcast_to(positions.reshape(N, 1).astype(F32), (N, LANES))
    tm = min(N, 1024)
    spec = pl.BlockSpec((tm, LANES), lambda i: (i, 0))
    vec = pl.BlockSpec((1, LANES), lambda i: (0, 0))
    return pl.pallas_call(
        _rope_kernel,
        out_shape=(jax.ShapeDtypeStruct((N, LANES), F32), jax.ShapeDtypeStruct((N, LANES), F32)),
        grid=(N // tm,),
        in_specs=[spec, vec, vec],
        out_specs=(spec, spec),
        compiler_params=_cparams("arbitrary"),
        name="rope_tables",
    )(pos, f4, sign.reshape(1, LANES))


QK_W = 2 * GLA_HEADS * GLA_DK
GV_W = GLA_HEADS * GLA_DV
TAIL_W = 3 * LANES


def _in_segments(D):
    return (QK_W, GV_W, GV_W, MLA_Q_RANK, MLA_KV_RANK, D, D, TAIL_W)


def _prep_w_in(w_in, D):
    s = [GLA_HEADS * GLA_DK, GLA_HEADS * GLA_DK, GV_W, GV_W, GLA_GATE_RANK, MLA_Q_RANK, MLA_KV_RANK, MLA_ROPE, D, D]
    offs = [0]
    for n in s:
        offs.append(offs[-1] + n)
    gq, gk, gv, gr, ga, cq, ckv, kr, gate_a, gate_b = [w_in[:, offs[i]:offs[i + 1]] for i in range(10)]
    half = MLA_ROPE // 2
    kr_sw = jnp.concatenate([kr[:, half:], kr[:, :half]], axis=1)
    pad = jnp.zeros((w_in.shape[0], LANES - GLA_GATE_RANK), w_in.dtype)
    return jnp.concatenate([gq, gk, gv, gr, cq, ckv, gate_a, gate_b, kr, kr, kr_sw, kr_sw, ga, pad], axis=1).astype(BF16)


def _inproj_kernel(x_ref, mod_ref, w_ref, *out_refs):
    sh1 = mod_ref[0, 0:1, :]
    sc1 = mod_ref[0, 1:2, :]
    h = (x_ref[...] * (1.0 + sc1) + sh1).astype(BF16)
    off = 0
    for ref in out_refs:
        n = ref.shape[-1]
        ref[...] = _dot(h, w_ref[:, off:off + n]).astype(ref.dtype)
        off += n


def _in_projection(x2, mod6, w_in_p, S):
    N, D = x2.shape
    W = w_in_p.shape[1]
    tm = min(S, 512)
    per_b = S // tm
    segs = _in_segments(D)
    assert sum(segs) == W
    return pl.pallas_call(
        _inproj_kernel,
        out_shape=tuple(jax.ShapeDtypeStruct((N, n), BF16) for n in segs),
        grid=(N // tm,),
        in_specs=[
            pl.BlockSpec((tm, D), lambda i: (i, 0)),
            pl.BlockSpec((1, 6, D), lambda i: (i // per_b, 0, 0)),
            pl.BlockSpec((D, W), lambda i: (0, 0), pipeline_mode=pl.Buffered(1)),
        ],
        out_specs=tuple(pl.BlockSpec((tm, n), lambda i: (i, 0)) for n in segs),
        compiler_params=_cparams("arbitrary"),
        name="in_proj",
    )(x2, mod6, w_in_p)


def _gla_kernel(qk_ref, v_ref, gr_ref, tail_ref, wa2_ref, ba_ref, gn_ref, wo_ref, y_ref,
                st_ref, kd_ref, dec_ref, sall_ref, o_ref):
    t = pl.program_id(1)

    @pl.when(t == 0)
    def _():
        st_ref[...] = jnp.zeros_like(st_ref)

    ts = qk_ref.shape[0]
    nch = ts // CHUNK
    HK = GLA_HEADS * GLA_DK
    r = lax.broadcasted_iota(I32, (CHUNK, CHUNK), 0)
    c = lax.broadcasted_iota(I32, (CHUNK, CHUNK), 1)
    tri = (r >= c).astype(BF16)
    qscale = GLA_DK ** -0.5

    z = _dot(tail_ref[:, 2 * LANES:3 * LANES], wa2_ref[...]) + ba_ref[...]
    log_a = (jnp.minimum(z, 0.0) - jnp.log(1.0 + jnp.exp(-jnp.abs(z)))) * (1.0 / GLA_GATE_TAU)
    la_hi = log_a.astype(BF16)
    la_lo = (log_a - la_hi.astype(F32)).astype(BF16)
    for n in range(nch):
        rows = slice(n * CHUNK, (n + 1) * CHUNK)
        G = _dot(tri, la_hi[rows]) + _dot(tri, la_lo[rows])
        g_end = G[CHUNK - 1:CHUNK, :]
        kd_ref[rows, :] = (qk_ref[rows, HK:2 * HK].astype(F32) * jnp.exp(g_end - G)).astype(BF16)
        dec_ref[n:n + 1, :] = jnp.exp(g_end)

    for h in range(GLA_HEADS):
        ks = slice(h * GLA_DK, (h + 1) * GLA_DK)
        vs = slice(h * GLA_DV, (h + 1) * GLA_DV)
        st = st_ref[h]
        for n in range(nch):
            rows = slice(n * CHUNK, (n + 1) * CHUNK)
            st = st * dec_ref[n:n + 1, ks] + _dot_tn(v_ref[rows, vs], kd_ref[rows, ks])
            sall_ref[n * GLA_HEADS + h] = st.astype(BF16)
        st_ref[h] = st

    for n in range(nch):
        rows = slice(n * CHUNK, (n + 1) * CHUNK)
        for h in range(GLA_HEADS):
            ks = slice(h * GLA_DK, (h + 1) * GLA_DK)
            vs = slice(h * GLA_DV, (h + 1) * GLA_DV)
            qh = (qk_ref[rows, ks].astype(F32) * qscale).astype(BF16)
            o = _dot_nt(qh, sall_ref[n * GLA_HEADS + h])
            o = o * lax.rsqrt(jnp.mean(o * o, axis=-1, keepdims=True) + RMS_EPS) * gn_ref[...]
            o_ref[rows, vs] = (o * _silu(gr_ref[rows, vs].astype(F32))).astype(BF16)
    y_ref[...] = _dot(o_ref[...], wo_ref[...]).astype(y_ref.dtype)


def _gla(qk, gv, gr, tail, wa2_p, b_a, g_norm, w_o, B, S):
    N = B * S
    D = w_o.shape[1]
    ts = min(S, 512)
    per_b = S // ts
    HK = GLA_HEADS * GLA_DK
    tok = lambda n: pl.BlockSpec((ts, n), lambda b, t: (b * per_b + t, 0))
    full = lambda a: pl.BlockSpec(a.shape, lambda b, t: (0,) * a.ndim)
    return pl.pallas_call(
        _gla_kernel,
        out_shape=jax.ShapeDtypeStruct((N, D), BF16),
        grid=(B, per_b),
        in_specs=[tok(QK_W), tok(GV_W), tok(GV_W), tok(TAIL_W), full(wa2_p), full(b_a), full(g_norm), full(w_o)],
        out_specs=tok(D),
        scratch_shapes=[pltpu.VMEM((GLA_HEADS, GLA_DV, GLA_DK), F32),
                        pltpu.VMEM((ts, GLA_HEADS * GLA_DK), BF16),
                        pltpu.VMEM((ts // CHUNK, GLA_HEADS * GLA_DK), F32),
                        pltpu.VMEM((ts // CHUNK * GLA_HEADS, GLA_DV, GLA_DK), BF16),
                        pltpu.VMEM((ts, GV_W), BF16)],
        compiler_params=_cparams("arbitrary", "arbitrary"),
        name="gla",
    )(qk, gv, gr, tail, wa2_p, b_a, g_norm, w_o)


HQ = MLA_HEADS * 2 * LANES


def _prep_w_uq(w_uq):
    dh = MLA_NOPE + MLA_ROPE
    half = MLA_ROPE // 2
    nope = [w_uq[:, h * dh:h * dh + MLA_NOPE] for h in range(MLA_HEADS)]
    rope = [w_uq[:, h * dh + MLA_NOPE:(h + 1) * dh] for h in range(MLA_HEADS)]
    rope_sw = [jnp.concatenate([r[:, half:], r[:, :half]], axis=1) for r in rope]
    return jnp.concatenate(nope + rope + rope_sw, axis=1).astype(BF16)


def _prep_w_ukv(w_ukv):
    dh = MLA_NOPE + MLA_V
    kn = [w_ukv[:, h * dh:h * dh + MLA_NOPE] for h in range(MLA_HEADS)]
    vv = [w_ukv[:, h * dh + MLA_NOPE:(h + 1) * dh] for h in range(MLA_HEADS)]
    return jnp.concatenate(kn + vv, axis=1).astype(BF16)


def _rms(x, g):
    return x * lax.rsqrt(jnp.mean(x * x, axis=-1, keepdims=True) + RMS_EPS) * g


def _mla_prep_kernel(cq_ref, ckv_ref, tail_ref, cc_ref, ss_ref, gq_ref, wq_ref, gkv_ref, wkv_ref, q_ref, k_ref, v_ref):
    tm = cq_ref.shape[0]
    NP = MLA_HEADS * MLA_NOPE
    RP = MLA_HEADS * MLA_ROPE
    scale = (MLA_NOPE + MLA_ROPE) ** -0.5 * LOG2E
    cc = cc_ref[...]
    ss = ss_ref[...]
    cqn = _rms(cq_ref[...].astype(F32), gq_ref[...]).astype(BF16)
    qf = _dot(cqn, wq_ref[...]) * scale
    ckvn = _rms(ckv_ref[...].astype(F32), gkv_ref[...]).astype(BF16)
    kv = _dot(ckvn, wkv_ref[...])
    krr = (tail_ref[:, 0:LANES].astype(F32) * cc + tail_ref[:, LANES:2 * LANES].astype(F32) * ss).astype(BF16)
    lane = lax.broadcasted_iota(I32, (tm, LANES), 1)
    first = lane < MLA_ROPE
    for j in range(MLA_HEADS // 2):
        a = NP + j * LANES
        rot = qf[:, a:a + LANES] * cc + qf[:, a + RP:a + RP + LANES] * ss
        for h, keep in ((2 * j, first), (2 * j + 1, jnp.logical_not(first))):
            base = h * 2 * LANES
            q_ref[:, base:base + LANES] = qf[:, h * MLA_NOPE:(h + 1) * MLA_NOPE].astype(BF16)
            q_ref[:, base + LANES:base + 2 * LANES] = jnp.where(keep, rot, 0.0).astype(BF16)
            k_ref[:, base:base + LANES] = kv[:, h * MLA_NOPE:(h + 1) * MLA_NOPE].astype(BF16)
            k_ref[:, base + LANES:base + 2 * LANES] = krr
    v_ref[...] = kv[:, NP:].astype(BF16)


def _mla_prep(cq, ckv, tail, cc, ss, g_cq, w_uq_p, g_ckv, w_ukv_p):
    N = cq.shape[0]
    tm = min(N, 512)
    tok = lambda n: pl.BlockSpec((tm, n), lambda i: (i, 0))
    full = lambda a: pl.BlockSpec(a.shape, lambda i: (0,) * a.ndim)
    HV = MLA_HEADS * MLA_V
    return pl.pallas_call(
        _mla_prep_kernel,
        out_shape=(jax.ShapeDtypeStruct((N, HQ), BF16), jax.ShapeDtypeStruct((N, HQ), BF16),
                   jax.ShapeDtypeStruct((N, HV), BF16)),
        grid=(N // tm,),
        in_specs=[tok(MLA_Q_RANK), tok(MLA_KV_RANK), tok(TAIL_W), tok(LANES), tok(LANES),
                  full(g_cq), full(w_uq_p), full(g_ckv), full(w_ukv_p)],
        out_specs=(tok(HQ), tok(HQ), tok(HV)),
        compiler_params=_cparams("arbitrary"),
        name="mla_prep",
    )(cq, ckv, tail, cc, ss, g_cq, w_uq_p, g_ckv, w_ukv_p)


ATTN_TQ = 256


def _mla_attn_kernel(q_ref, k_ref, v_ref, o_ref, v1_ref):
    S = q_ref.shape[0]
    tq = min(S, ATTN_TQ)
    r = lax.broadcasted_iota(I32, (tq, tq), 0) // CHUNK
    c = lax.broadcasted_iota(I32, (tq, tq), 1) // CHUNK
    diag_mask = c <= r
    v1_ref[:, :MLA_V] = v_ref[...]
    v1_ref[:, MLA_V:] = jnp.ones((S, MLA_V), BF16)

    def scores(ii):
        l0 = ii * tq
        q = q_ref[l0:l0 + tq, :]
        sd = jnp.where(diag_mask, _dot_nt(q, k_ref[l0:l0 + tq, :]), -jnp.inf)
        so = _dot_nt(q, k_ref[0:l0, :]) if ii > 0 else None
        return sd, so

    def finish(ii, sd, so):
        l0 = ii * tq
        m = jnp.max(sd, axis=-1, keepdims=True)
        if so is not None:
            m = jnp.maximum(m, jnp.max(so, axis=-1, keepdims=True))
        acc = _dot(jnp.exp2((sd - m).astype(BF16)), v1_ref[l0:l0 + tq, :])
        if so is not None:
            acc = acc + _dot(jnp.exp2((so - m).astype(BF16)), v1_ref[0:l0, :])
        o_ref[l0:l0 + tq, :] = (acc[:, :MLA_V] / acc[:, MLA_V:]).astype(BF16)

    n_tiles = S // tq
    ahead = 2
    pending = [scores(ii) for ii in range(min(ahead, n_tiles))]
    for ii in range(n_tiles):
        if ii + ahead < n_tiles:
            pending.append(scores(ii + ahead))
        finish(ii, *pending.pop(0))


def _mla_attention(qc, kc, vv, B, S):
    N = B * S
    HV = MLA_HEADS * MLA_V
    return pl.pallas_call(
        _mla_attn_kernel,
        out_shape=jax.ShapeDtypeStruct((N, HV), BF16),
        grid=(B, MLA_HEADS),
        in_specs=[
            pl.BlockSpec((S, 2 * LANES), lambda b, h: (b, h)),
            pl.BlockSpec((S, 2 * LANES), lambda b, h: (b, h)),
            pl.BlockSpec((S, MLA_V), lambda b, h: (b, h)),
        ],
        out_specs=pl.BlockSpec((S, MLA_V), lambda b, h: (b, h)),
        scratch_shapes=[pltpu.VMEM((S, 2 * MLA_V), BF16)],
        compiler_params=_cparams("arbitrary", "arbitrary"),
        name="mla_attn",
    )(qc, kc, vv)


def _layer_norm(u, g, b):
    mu = jnp.mean(u, axis=-1, keepdims=True)
    d = u - mu
    var = jnp.mean(d * d, axis=-1, keepdims=True)
    return d * lax.rsqrt(var + LN_EPS) * g + b


def _merge_kernel(alpha, x_ref, ya_ref, ob_ref, ga_ref, gb_ref, mod_ref, wmo_ref, wout_ref, g_ref, b_ref, wr_ref,
                  br_ref, ws13_ref, ws2_ref, x1_ref, h2_ref, sh_ref, idx_ref, ww_ref, rank_ref, cnt_ref, lg_s, carry_s):
    i = pl.program_id(0)

    @pl.when(i == 0)
    def _():
        lg_s[...] = jnp.zeros_like(lg_s)
        carry_s[...] = jnp.zeros_like(carry_s)

    parts, total = [], carry_s[...]
    for c0 in range(0, lg_s.shape[1], ROUTE_TOKENS):
        *res, total = _route_math(lg_s[:, c0:c0 + ROUTE_TOKENS], br_ref[...], total)
        parts.append(res)
    idx, ww, rank = [jnp.concatenate(col, axis=1) for col in zip(*parts)]
    total = jnp.where(i > 0, total, 0.0)
    idx_ref[...] = idx
    ww_ref[...] = ww
    rank_ref[...] = rank
    carry_s[...] = total
    cnt_ref[...] = jnp.broadcast_to(total, cnt_ref.shape).astype(I32)

    gt1 = mod_ref[0, 2:3, :]
    sh2 = mod_ref[0, 3:4, :]
    sc2 = mod_ref[0, 4:5, :]
    y = (_sigmoid(ga_ref[...].astype(F32)) * ya_ref[...].astype(F32)
         + _sigmoid(gb_ref[...].astype(F32)) * _dot(ob_ref[...], wmo_ref[...]))
    mix = _dot(y.astype(BF16), wout_ref[...])
    x1 = _layer_norm(alpha * x_ref[...] + (1.0 + gt1) * mix, g_ref[...], b_ref[...])
    x1_ref[...] = x1
    h2 = x1 * (1.0 + sc2) + sh2
    h2_ref[...] = _pack_bf16_pair(h2)
    h2b = h2.astype(BF16)
    lg_s[...] = _dot_nt(wr_ref[...], h2b)
    F = ws2_ref.shape[0]
    ab = _dot(h2b, ws13_ref[...])
    hid = (_silu(ab[:, :F]) * ab[:, F:]).astype(BF16)
    sh_ref[...] = _dot(hid, ws2_ref[...]).astype(sh_ref.dtype)


def _merge_route(alpha, x2, y_a, o_b, gate_a, gate_b, mod6, w_mla_o, w_out, ln_g, ln_b, wr_t, b_router, ws13, ws2,
                 S, tok0, Np):
    N, D = x2.shape
    E = wr_t.shape[0]
    tm = min(S, 512)
    per_b = S // tm
    assert Np % tm == 0 and tok0 % tm == 0
    n = Np // tm
    base = tok0 // tm
    cur = lambda i: jnp.minimum(i, n - 1)
    prev = lambda i: jnp.maximum(i - 1, 0)
    tok_in = lambda w: pl.BlockSpec((tm, w), lambda i: (base + cur(i), 0))
    tok = lambda w: pl.BlockSpec((tm, w), lambda i: (cur(i), 0))
    full = lambda a: pl.BlockSpec(a.shape, lambda i: (0,) * a.ndim)
    lane_blk = pl.BlockSpec((TOP_K, tm), lambda i: (0, prev(i)))
    br = b_router.reshape(E, 1).astype(F32)
    return pl.pallas_call(
        functools.partial(_merge_kernel, alpha),
        out_shape=(jax.ShapeDtypeStruct((Np, D), F32), jax.ShapeDtypeStruct((Np, D // 2), I32),
                   jax.ShapeDtypeStruct((Np, D), BF16),
                   jax.ShapeDtypeStruct((TOP_K, Np), I32), jax.ShapeDtypeStruct((TOP_K, Np), I32),
                   jax.ShapeDtypeStruct((TOP_K, Np), I32), jax.ShapeDtypeStruct((E, LANES), I32)),
        grid=(n + 1,),
        in_specs=[tok_in(D), tok_in(D), tok_in(D), tok_in(D), tok_in(D),
                  pl.BlockSpec((1, 6, D), lambda i: ((base + cur(i)) // per_b, 0, 0)),
                  full(w_mla_o), full(w_out), full(ln_g), full(ln_b), full(wr_t), full(br), full(ws13), full(ws2)],
        out_specs=(tok(D), tok(D // 2), tok(D), lane_blk, lane_blk, lane_blk,
                   pl.BlockSpec((E, LANES), lambda i: (0, 0))),
        scratch_shapes=[pltpu.VMEM((E, tm), F32), pltpu.VMEM((E, 1), F32)],
        compiler_params=_cparams("arbitrary"),
        name="merge_ln1_route",
    )(x2, y_a, o_b, gate_a, gate_b, mod6, w_mla_o, w_out, ln_g, ln_b, wr_t, br, ws13, ws2)


def _first_argmax(v, io, n):
    m = jnp.max(v, axis=0, keepdims=True)
    idx = jnp.min(jnp.where(v == m, io, n), axis=0, keepdims=True)
    return m, idx


def _route_math(lg, br, carry):
    E, T = lg.shape
    gsz = E // N_GROUPS
    neg = -jnp.inf
    s = _sigmoid(lg)
    biased = s + br
    eio = lax.broadcasted_iota(I32, (E, T), 0)
    gio = lax.broadcasted_iota(I32, (gsz, T), 0)

    gs = []
    for g in range(N_GROUPS):
        blk = biased[g * gsz:(g + 1) * gsz]
        m1, i1 = _first_argmax(blk, gio, gsz)
        m2 = jnp.max(jnp.where(gio == i1, neg, blk), axis=0, keepdims=True)
        gs.append(m1 + m2)
    cur = jnp.concatenate(gs, axis=0)
    nio = lax.broadcasted_iota(I32, (N_GROUPS, T), 0)
    gsel = jnp.zeros((N_GROUPS, T), F32)
    for _ in range(TOPK_GROUPS):
        _, gi = _first_argmax(cur, nio, N_GROUPS)
        hit = nio == gi
        gsel = jnp.where(hit, 1.0, gsel)
        cur = jnp.where(hit, neg, cur)
    emask = jnp.concatenate([jnp.broadcast_to(gsel[g:g + 1], (gsz, T)) for g in range(N_GROUPS)], axis=0) > 0.0

    cur = jnp.where(emask, biased, neg)
    idxs, ws = [], []
    sel = jnp.zeros((E, T), F32)
    for _ in range(TOP_K):
        _, ei = _first_argmax(cur, eio, E)
        hit = eio == ei
        idxs.append(ei)
        ws.append(jnp.sum(jnp.where(hit, s, 0.0), axis=0, keepdims=True))
        sel = jnp.where(hit, 1.0, sel)
        cur = jnp.where(hit, neg, cur)
    w = jnp.concatenate(ws, axis=0)
    w = w / jnp.sum(w, axis=0, keepdims=True) * ROUTED_SCALE
    wbits = lax.bitcast_convert_type(w.astype(BF16).astype(F32), I32)
    wword = jnp.bitwise_or(lax.shift_right_logical(wbits, jnp.int32(16)), jnp.bitwise_and(wbits, jnp.int32(-65536)))

    r = lax.broadcasted_iota(I32, (T, T), 0)
    c = lax.broadcasted_iota(I32, (T, T), 1)
    before = (r < c).astype(BF16)
    cnt = _dot(sel.astype(BF16), before) + carry
    ranks = [jnp.sum(jnp.where(eio == idxs[k], cnt, 0.0), axis=0, keepdims=True) for k in range(TOP_K)]
    total = cnt[:, T - 1:T] + sel[:, T - 1:T]
    return jnp.concatenate(idxs, axis=0), wword, jnp.concatenate(ranks, axis=0).astype(I32), total


def _dest_kernel(idx_ref, rank_ref, ps_ref, dest_ref):
    K, T = idx_ref.shape
    E = ps_ref.shape[0]
    eio = lax.broadcasted_iota(I32, (E, T), 0)
    ps = ps_ref[...]
    rows = [jnp.sum(jnp.where(eio == idx_ref[k:k + 1, :], ps, 0), axis=0, keepdims=True) for k in range(K)]
    dest_ref[...] = jnp.concatenate(rows, axis=0) + rank_ref[...]


def _dest_slots(idx, rank, pad_start):
    K, N = idx.shape
    E = pad_start.shape[0]
    T = min(N, 512)
    spec = pl.BlockSpec((K, T), lambda i: (0, i))
    return pl.pallas_call(
        _dest_kernel,
        out_shape=jax.ShapeDtypeStruct((K, N), I32),
        grid=(N // T,),
        in_specs=[spec, spec, pl.BlockSpec((E, 1), lambda i: (0, 0))],
        out_specs=spec,
        compiler_params=_cparams("arbitrary"),
        name="dest_slots",
    )(idx, rank, pad_start.reshape(E, 1))


SC_ROWS = 64


def _sc_workers():
    info = plsc.get_sparse_core_info()
    return info.num_cores, info.num_subcores


def _sc_lanes():
    return plsc.get_sparse_core_info().num_lanes


def _dispatch_rows(h2p, dest, n_slots):
    N, W = h2p.shape
    K = dest.shape[0]
    R = SC_ROWS
    nc, ns = _sc_workers()
    per_w = N // R // (nc * ns)
    assert per_w * R * nc * ns == N and per_w % 2 == 0, "token count must split into chunk pairs per subcore"
    mesh = plsc.VectorSubcoreMesh(core_axis_name="c", subcore_axis_name="s")

    @functools.partial(
        pl.kernel, mesh=mesh,
        out_type=jax.ShapeDtypeStruct((n_slots, W), I32),
        scratch_types=[pltpu.VMEM((K, R), I32), pltpu.VMEM((K, R), I32),
                       pltpu.VMEM((R, W), I32), pltpu.VMEM((R, W), I32),
                       pltpu.SemaphoreType.DMA, pltpu.SemaphoreType.DMA, pltpu.SemaphoreType.DMA],
    )
    def k(h_hbm, d_hbm, xs_hbm, idx0, idx1, rows0, rows1, lsem0, lsem1, ssem):
        idx, rows, lsem = (idx0, idx1), (rows0, rows1), (lsem0, lsem1)
        base = (lax.axis_index("s") * nc + lax.axis_index("c")) * per_w

        def loads(ch, b):
            tok = pl.ds(pl.multiple_of(ch * R, 8), R)
            return [pltpu.make_async_copy(d_hbm.at[kk, tok], idx[b].at[kk], lsem[b]) for kk in range(K)] + [
                pltpu.make_async_copy(h_hbm.at[tok], rows[b], lsem[b])]

        for cp in loads(base, 0):
            cp.start()
        for cp in loads(base, 0):
            cp.wait()

        @pl.loop(0, per_w, step=2)
        def _(j):
            for b in range(2):
                ch = base + j + b
                more = j + b + 1 < per_w

                @pl.when(more)
                def _():
                    for cp in loads(ch + 1, 1 - b):
                        cp.start()

                scatters = [pltpu.make_async_copy(rows[b], xs_hbm.at[idx[b].at[kk]], ssem) for kk in range(K)]
                for cp in scatters:
                    cp.start()
                for cp in scatters:
                    cp.wait()

                @pl.when(more)
                def _():
                    for cp in loads(ch + 1, 1 - b):
                        cp.wait()

    return k(h2p, dest)


COMBINE_ROWS = 8


def _combine_sum(ys, dest, ww, tok0, n_tok):
    W = ys.shape[1]
    K = dest.shape[0]
    R = COMBINE_ROWS
    L = _sc_lanes()
    nc, ns = _sc_workers()
    per_w = n_tok // R // (nc * ns)
    assert per_w * R * nc * ns == n_tok and per_w % 2 == 0, "token count must split into chunk pairs per subcore"
    T = per_w * R
    mesh = plsc.VectorSubcoreMesh(core_axis_name="c", subcore_axis_name="s")

    @functools.partial(
        pl.kernel, mesh=mesh, compiler_params=pltpu.CompilerParams(needs_layout_passes=False),
        out_type=jax.ShapeDtypeStruct((n_tok, W), I32),
        scratch_types=[pltpu.VMEM((K * T,), I32), pltpu.VMEM((K * T,), I32),
                       pltpu.VMEM((K, R, W), I32), pltpu.VMEM((K, R, W), I32),
                       pltpu.VMEM((R, W), I32), pltpu.VMEM((R, W), I32),
                       pltpu.SemaphoreType.DMA, pltpu.SemaphoreType.DMA,
                       pltpu.SemaphoreType.DMA, pltpu.SemaphoreType.DMA],
    )
    def k(ys_hbm, d_hbm, w_hbm, o_hbm, idx_all, w_all, buf0, buf1, out0, out1, gsem0, gsem1, osem0, osem1):
        buf, outv, gsem, osem = (buf0, buf1), (out0, out1), (gsem0, gsem1), (osem0, osem1)
        wid = lax.axis_index("s") * nc + lax.axis_index("c")
        t0 = pl.multiple_of(tok0 + wid * T, 8)
        for kk in range(K):
            pltpu.sync_copy(d_hbm.at[kk, pl.ds(t0, T)], idx_all.at[pl.ds(kk * T, T)])
            pltpu.sync_copy(w_hbm.at[kk, pl.ds(t0, T)], w_all.at[pl.ds(kk * T, T)])

        def fetch(j, b):
            return [pltpu.make_async_copy(
                ys_hbm.at[idx_all.at[pl.ds(pl.multiple_of(kk * T + j * R, 8), R)]], buf[b].at[kk], gsem[b])
                for kk in range(K)]

        def store(j, b):
            return pltpu.make_async_copy(outv[b], o_hbm.at[pl.ds(pl.multiple_of(wid * T + j * R, 8), R)], osem[b])

        for cp in fetch(0, 0):
            cp.start()

        @pl.loop(0, per_w, step=2)
        def _(j0):
            for b in range(2):
                j = j0 + b

                @pl.when(j + 1 < per_w)
                def _():
                    for cp in fetch(j + 1, 1 - b):
                        cp.start()

                for cp in fetch(j, b):
                    cp.wait()

                @pl.when(j >= 2)
                def _():
                    store(j - 2, b).wait()

                @pl.loop(0, R)
                def _(r):
                    ws = [plsc.bitcast(plsc.load_gather(w_all, [jnp.full((L,), kk * T + j * R + r, I32)]), BF16)
                          for kk in range(K)]

                    @pl.loop(0, W // L)
                    def _(q):
                        sl = pl.ds(q * L, L)
                        terms = [ws[kk] * plsc.bitcast(buf[b][kk, r, sl], BF16) for kk in range(K)]
                        while len(terms) > 1:
                            terms = [terms[i] + terms[i + 1] for i in range(0, len(terms), 2)]
                        outv[b][r, sl] = plsc.bitcast(terms[0], I32)

                store(j, b).start()

        store(per_w - 2, 0).wait()
        store(per_w - 1, 1).wait()

    return k(ys, dest, ww)


def _expert_kernel(b0_ref, nb_ref, cnt_ref, xs_hbm, w1_ref, w3_ref, w2_ref, ys_hbm,
                   w13_s, w2_s, xbuf, ybuf, sem_in, sem_out):
    e = pl.program_id(0)
    n_exp = pl.num_programs(0)
    F = w1_ref.shape[2]
    bm = xbuf.shape[1]
    half = xbuf.shape[2]
    nb = nb_ref[e]
    b0 = b0_ref[e]
    total = b0_ref[n_exp - 1] + nb_ref[n_exp - 1]

    nbuf = xbuf.shape[0]

    def in_copy(g):
        slot = g % nbuf
        return pltpu.make_async_copy(xs_hbm.at[pl.ds(g * bm, bm)], xbuf.at[slot], sem_in.at[slot])

    def out_copy(g):
        slot = g % nbuf
        return pltpu.make_async_copy(ybuf.at[slot], ys_hbm.at[pl.ds(g * bm, bm)], sem_out.at[slot])

    @pl.when(e == 0)
    def _():
        for g0 in range(nbuf - 1):
            @pl.when(g0 < total)
            def _():
                in_copy(g0).start()

    @pl.when(nb > 0)
    def _():
        w13_s[:, :F] = w1_ref[0].astype(BF16)
        w13_s[:, F:] = w3_ref[0].astype(BF16)
        w2_s[...] = w2_ref[0].astype(BF16)

    def block(j, carry):
        g = b0 + j
        slot = g % nbuf
        in_copy(g).wait()

        @pl.when(g + nbuf - 1 < total)
        def _():
            in_copy(g + nbuf - 1).start()

        @pl.when(g >= nbuf)
        def _():
            out_copy(g - nbuf).wait()

        def swiglu(rows):
            lo, hi = _unpack_bf16_pair(xbuf[slot, :rows, :])
            ab = _dot(lo.astype(BF16), w13_s[:half, :]) + _dot(hi.astype(BF16), w13_s[half:, :])
            hid = (_silu(ab[:, :F]) * ab[:, F:]).astype(BF16)
            ybuf[slot, :rows, :] = _pack_bf16_pair(_dot(hid, w2_s[...]))

        rows_left = cnt_ref[e] - j * bm

        @pl.when(rows_left > bm // 2)
        def _():
            swiglu(bm)

        @pl.when(rows_left <= bm // 2)
        def _():
            swiglu(bm // 2)
            ybuf[slot, bm // 2:, :] = jnp.zeros((bm - bm // 2, half), I32)

        out_copy(g).start()
        return carry

    lax.fori_loop(0, nb, block, 0)

    @pl.when(e == n_exp - 1)
    def _():
        for back in range(nbuf, 0, -1):
            @pl.when(total >= back)
            def _():
                out_copy(total - back).wait()


def _experts(xs, blk0, nblk, counts, w_e1, w_e3, w_e2):
    n_slots, W = xs.shape
    E, D, F = w_e1.shape
    bm = EXPERT_BLOCK
    grid_spec = pltpu.PrefetchScalarGridSpec(
        num_scalar_prefetch=3,
        grid=(E,),
        in_specs=[
            pl.BlockSpec(memory_space=pl.ANY),
            pl.BlockSpec((1, D, F), lambda e, b0, nb, cnt: (e, 0, 0)),
            pl.BlockSpec((1, D, F), lambda e, b0, nb, cnt: (e, 0, 0)),
            pl.BlockSpec((1, F, D), lambda e, b0, nb, cnt: (e, 0, 0)),
        ],
        out_specs=pl.BlockSpec(memory_space=pl.ANY),
        scratch_shapes=[pltpu.VMEM((D, 2 * F), BF16), pltpu.VMEM((F, D), BF16),
                        pltpu.VMEM((EXPERT_RING, bm, W), I32), pltpu.VMEM((EXPERT_RING, bm, W), I32),
                        pltpu.SemaphoreType.DMA((EXPERT_RING,)), pltpu.SemaphoreType.DMA((EXPERT_RING,))],
    )
    return pl.pallas_call(
        _expert_kernel,
        out_shape=jax.ShapeDtypeStruct((n_slots, W), I32),
        grid_spec=grid_spec,
        compiler_params=_cparams("arbitrary"),
        name="experts",
    )(blk0, nblk, counts, xs, w_e1, w_e3, w_e2)


def _final_kernel(alpha, ms_ref, sh_ref, x1_ref, mod_ref, g_ref, b_ref, *rest):
    o_ref = rest[-1]
    gt2 = mod_ref[0, 5:6, :]
    lo, hi = _unpack_bf16_pair(ms_ref[...])
    ffn = jnp.concatenate([lo, hi], axis=1) + sh_ref[...].astype(F32)
    o_ref[...] = _layer_norm(alpha * x1_ref[...] + (1.0 + gt2) * ffn, g_ref[...], b_ref[...])


def _final_part(alpha, ms, sh, x1, mod6, ln_g, ln_b, S, tok0, row0, n_total, prev):
    D = x1.shape[1]
    N = n_total
    n_part, W = ms.shape
    tm = min(S, 512, n_part)
    assert n_part % tm == 0 and S % tm == 0 and tok0 % tm == 0 and row0 % tm == 0
    per_b = S // tm
    steps = n_part // tm
    off = tok0 // tm
    goff = row0 // tm
    tok = lambda n: pl.BlockSpec((tm, n), lambda i: (off + i, 0))
    full = lambda a: pl.BlockSpec(a.shape, lambda i: (0,) * a.ndim)
    in_specs = [pl.BlockSpec((tm, W), lambda i: (i, 0)), tok(D), tok(D),
                pl.BlockSpec((1, 6, D), lambda i: ((goff + i) // per_b, 0, 0)), full(ln_g), full(ln_b)]
    args = [ms, sh, x1, mod6, ln_g, ln_b]
    aliases = {}
    if prev is not None:
        in_specs.append(pl.BlockSpec(memory_space=pl.ANY))
        args.append(prev)
        aliases = {len(args) - 1: 0}
    return pl.pallas_call(
        functools.partial(_final_kernel, alpha),
        out_shape=jax.ShapeDtypeStruct((N, D), F32),
        grid=(steps,),
        in_specs=in_specs,
        out_specs=pl.BlockSpec((tm, D), lambda i: (goff + i, 0)),
        input_output_aliases=aliases,
        compiler_params=_cparams("arbitrary"),
        name="combine_ln2",
    )(*args)


def _moe(alpha, h2p, sh, idx, ww, rank, cnt, x1, mod6, w_e1, w_e3, w_e2, ln_g, ln_b, S, row0, n_total, out):
    N = x1.shape[0]
    E = w_e1.shape[0]
    bm = EXPERT_BLOCK
    counts = cnt[:, 0]
    padded = (counts + bm - 1) // bm * bm
    pad_end = jnp.cumsum(padded)
    pad_start = pad_end - padded
    n_slots = N * TOP_K + E * bm
    dest = _dest_slots(idx, rank, pad_start.astype(I32))
    xs = _dispatch_rows(h2p, dest, n_slots)
    ys = _experts(xs, (pad_start // bm).astype(I32), (padded // bm).astype(I32), counts.astype(I32),
                  w_e1, w_e3, w_e2)
    n_part = min(N, COMBINE_TOKENS) if row0 + N == n_total else N
    assert N % n_part == 0
    for p in range(N // n_part):
        tok0 = p * n_part
        ms = _combine_sum(ys, dest, ww, tok0, n_part)
        out = _final_part(alpha, ms, sh, x1, mod6, ln_g, ln_b, S, tok0, row0 + tok0, n_total, out)
    return out


def kernel(x, c, positions, w_ada, b_ada, w_in, w_gla_a2, b_gla_a, g_gla_norm, w_gla_o, g_cq, w_uq, g_ckv, w_ukv, w_mla_o, w_out, ln1_g, ln1_b, w_router, b_router, w_e1, w_e3, w_e2, w_s1, w_s3, w_s2, ln2_g, ln2_b):
    B, S, D = x.shape
    N = B * S
    depth = w_ada.shape[0]
    alpha = (2.0 * depth) ** 0.25
    row = lambda a: a.reshape(1, -1)
    cc, ss = _rope_tables(positions)
    x2 = x.reshape(N, D)
    for l in range(depth):
        mod6 = _modulation(c, w_ada[l], b_ada[l]).reshape(B, 6, D)
        qk, gv, gr, cq, ckv, gate_a, gate_b, tail = _in_projection(x2, mod6, _prep_w_in(w_in[l], D), S)
        wa2_p = jnp.concatenate(
            [w_gla_a2[l], jnp.zeros((LANES - GLA_GATE_RANK, w_gla_a2.shape[2]), F32)], axis=0).astype(BF16)
        y_a = _gla(qk, gv, gr, tail, wa2_p, row(b_gla_a[l]), row(g_gla_norm[l]), w_gla_o[l].astype(BF16), B, S)
        qc, kc, vv = _mla_prep(cq, ckv, tail, cc, ss, row(g_cq[l]), _prep_w_uq(w_uq[l]),
                               row(g_ckv[l]), _prep_w_ukv(w_ukv[l]))
        o_b = _mla_attention(qc, kc, vv, B, S)
        ws13 = jnp.concatenate([w_s1[l], w_s3[l]], axis=1).astype(BF16)
        ws2 = w_s2[l].astype(BF16)
        b_first = max(1, (B * MOE_FIRST_EIGHTHS) // 8) if B > 1 else B
        ranges = [(0, b_first * S)] + ([(b_first * S, (B - b_first) * S)] if B > b_first else [])
        routed =[_merge_route(alpha, x2, y_a, o_b, gate_a, gate_b, mod6, w_mla_o[l].astype(BF16),
                               w_out[l].astype(BF16), row(ln1_g[l]), row(ln1_b[l]), w_router[l].T.astype(BF16),
                               b_router[l], ws13, ws2, S, t0, n) for t0, n in ranges]
        out = None
        for (t0, n), (x1, h2p, sh, idx, ww, rank, cnt) in zip(ranges, routed):
            out = _moe(alpha, h2p, sh, idx, ww, rank, cnt, x1, mod6, w_e1[l], w_e3[l], w_e2[l],
                       row(ln2_g[l]), row(ln2_b[l]), S, t0, N, out)
        x2 = out
    return x2.reshape(B, S, D)
```

```python
import functools

import jax
import jax.numpy as jnp
from jax import lax
from jax.experimental import pallas as pl
from jax.experimental.pallas import tpu as pltpu
from jax.experimental.pallas import tpu_sc as plsc

CHUNK = 64
GLA_HEADS = 4
GLA_DK = 128
GLA_DV = 256
GLA_GATE_RANK = 16
GLA_GATE_TAU = 16.0
MLA_HEADS = 8
MLA_Q_RANK = 768
MLA_KV_RANK = 256
MLA_NOPE = 128
MLA_ROPE = 64
MLA_V = 128
ROPE_THETA = 10000.0
N_EXPERTS = 256
TOP_K = 8
N_GROUPS = 8
TOPK_GROUPS = 4
D_EXPERT = 256
ROUTED_SCALE = 2.5
LN_EPS = 1e-5
RMS_EPS = 1e-6
LOG2E = 1.4426950408889634

LANES = 128
VMEM_LIMIT = 56 * 1024 * 1024
EXPERT_BLOCK = 512
EXPERT_RING = 4
ROUTE_TOKENS = 128
MOE_FIRST_EIGHTHS = 5
COMBINE_TOKENS = 8192

F32 = jnp.float32
BF16 = jnp.bfloat16
I32 = jnp.int32


def _cparams(*sem):
    return pltpu.CompilerParams(dimension_semantics=sem, vmem_limit_bytes=VMEM_LIMIT)


def _sigmoid(x):
    return 1.0 / (1.0 + jnp.exp(-x))


def _silu(x):
    return x * _sigmoid(x)


def _dot(a, b):
    return jnp.dot(a, b, preferred_element_type=F32)


def _dot_nt(a, b):
    return lax.dot_general(a, b, (((1,), (1,)), ((), ())), preferred_element_type=F32)


def _dot_tn(a, b):
    return lax.dot_general(a, b, (((0,), (0,)), ((), ())), preferred_element_type=F32)


def _pack_bf16_pair(x):
    w = x.shape[1] // 2
    u = lax.bitcast_convert_type(x.astype(BF16).astype(F32), I32)
    lo = lax.shift_right_logical(u[:, :w], jnp.int32(16))
    hi = jnp.bitwise_and(u[:, w:], jnp.int32(-65536))
    return jnp.bitwise_or(lo, hi)


def _unpack_bf16_pair(p):
    lo = lax.bitcast_convert_type(lax.shift_left(p, jnp.int32(16)), F32)
    hi = lax.bitcast_convert_type(jnp.bitwise_and(p, jnp.int32(-65536)), F32)
    return lo, hi


def _mod_kernel(c_ref, w_ref, b_ref, o_ref):
    cond = _silu(c_ref[...]).astype(BF16)
    o_ref[...] = _dot(cond, w_ref[...].astype(BF16)) + b_ref[...]


def _modulation(c, w_ada, b_ada):
    B, D = c.shape
    W = w_ada.shape[1]
    tn = D
    return pl.pallas_call(
        _mod_kernel,
        out_shape=jax.ShapeDtypeStruct((B, W), F32),
        grid=(W // tn,),
        in_specs=[
            pl.BlockSpec((B, D), lambda j: (0, 0)),
            pl.BlockSpec((D, tn), lambda j: (0, j)),
            pl.BlockSpec((1, tn), lambda j: (0, j)),
        ],
        out_specs=pl.BlockSpec((B, tn), lambda j: (0, j)),
        compiler_params=_cparams("arbitrary"),
        name="adaln_mod",
    )(c, w_ada, b_ada.reshape(1, W))


def _rope_kernel(pos_ref, f_ref, ph_ref, cc_ref, ss_ref):
    ang = pos_ref[...] * f_ref[...]
    cc_ref[...] = jnp.cos(ang)
    ss_ref[...] = jnp.sin(ang) * ph_ref[...]


def _rope_tables(positions):
    B, S = positions.shape
    N = B * S
    half = MLA_ROPE // 2
    inv_freq = ROPE_THETA ** (-jnp.arange(half, dtype=F32) * (2.0 / MLA_ROPE))
    f4 = jnp.tile(inv_freq, LANES // half).reshape(1, LANES)
    sign = jnp.tile(jnp.concatenate([-jnp.ones((half,), F32), jnp.ones((half,), F32)]), LANES // MLA_ROPE)
    pos = jnp.broadcast_to(positions.reshape(N, 1).astype(F32), (N, LANES))
    tm = min(N, 1024)
    spec = pl.BlockSpec((tm, LANES), lambda i: (i, 0))
    vec = pl.BlockSpec((1, LANES), lambda i: (0, 0))
    return pl.pallas_call(
        _rope_kernel,
        out_shape=(jax.ShapeDtypeStruct((N, LANES), F32), jax.ShapeDtypeStruct((N, LANES), F32)),
        grid=(N // tm,),
        in_specs=[spec, vec, vec],
        out_specs=(spec, spec),
        compiler_params=_cparams("arbitrary"),
        name="rope_tables",
    )(pos, f4, sign.reshape(1, LANES))


QK_W = 2 * GLA_HEADS * GLA_DK
GV_W = GLA_HEADS * GLA_DV
TAIL_W = 3 * LANES


def _in_segments(D):
    return (QK_W, GV_W, GV_W, MLA_Q_RANK, MLA_KV_RANK, D, D, TAIL_W)


def _prep_w_in(w_in, D):
    s = [GLA_HEADS * GLA_DK, GLA_HEADS * GLA_DK, GV_W, GV_W, GLA_GATE_RANK, MLA_Q_RANK, MLA_KV_RANK, MLA_ROPE, D, D]
    offs = [0]
    for n in s:
        offs.append(offs[-1] + n)
    gq, gk, gv, gr, ga, cq, ckv, kr, gate_a, gate_b = [w_in[:, offs[i]:offs[i + 1]] for i in range(10)]
    half = MLA_ROPE // 2
    kr_sw = jnp.concatenate([kr[:, half:], kr[:, :half]], axis=1)
    pad = jnp.zeros((w_in.shape[0], LANES - GLA_GATE_RANK), w_in.dtype)
    return jnp.concatenate([gq, gk, gv, gr, cq, ckv, gate_a, gate_b, kr, kr, kr_sw, kr_sw, ga, pad], axis=1).astype(BF16)


def _inproj_kernel(x_ref, mod_ref, w_ref, *out_refs):
    sh1 = mod_ref[0, 0:1, :]
    sc1 = mod_ref[0, 1:2, :]
    h = (x_ref[...] * (1.0 + sc1) + sh1).astype(BF16)
    off = 0
    for ref in out_refs:
        n = ref.shape[-1]
        ref[...] = _dot(h, w_ref[:, off:off + n]).astype(ref.dtype)
        off += n


def _in_projection(x2, mod6, w_in_p, S):
    N, D = x2.shape
    W = w_in_p.shape[1]
    tm = min(S, 512)
    per_b = S // tm
    segs = _in_segments(D)
    assert sum(segs) == W
    return pl.pallas_call(
        _inproj_kernel,
        out_shape=tuple(jax.ShapeDtypeStruct((N, n), BF16) for n in segs),
        grid=(N // tm,),
        in_specs=[
            pl.BlockSpec((tm, D), lambda i: (i, 0)),
            pl.BlockSpec((1, 6, D), lambda i: (i // per_b, 0, 0)),
            pl.BlockSpec((D, W), lambda i: (0, 0), pipeline_mode=pl.Buffered(1)),
        ],
        out_specs=tuple(pl.BlockSpec((tm, n), lambda i: (i, 0)) for n in segs),
        compiler_params=_cparams("arbitrary"),
        name="in_proj",
    )(x2, mod6, w_in_p)


def _gla_kernel(qk_ref, v_ref, gr_ref, tail_ref, wa2_ref, ba_ref, gn_ref, wo_ref, y_ref,
                st_ref, kd_ref, dec_ref, sall_ref, o_ref):
    t = pl.program_id(1)

    @pl.when(t == 0)
    def _():
        st_ref[...] = jnp.zeros_like(st_ref)

    ts = qk_ref.shape[0]
    nch = ts // CHUNK
    HK = GLA_HEADS * GLA_DK
    r = lax.broadcasted_iota(I32, (CHUNK, CHUNK), 0)
    c = lax.broadcasted_iota(I32, (CHUNK, CHUNK), 1)
    tri = (r >= c).astype(BF16)
    qscale = GLA_DK ** -0.5

    z = _dot(tail_ref[:, 2 * LANES:3 * LANES], wa2_ref[...]) + ba_ref[...]
    log_a = (jnp.minimum(z, 0.0) - jnp.log(1.0 + jnp.exp(-jnp.abs(z)))) * (1.0 / GLA_GATE_TAU)
    la_hi = log_a.astype(BF16)
    la_lo = (log_a - la_hi.astype(F32)).astype(BF16)
    for n in range(nch):
        rows = slice(n * CHUNK, (n + 1) * CHUNK)
        G = _dot(tri, la_hi[rows]) + _dot(tri, la_lo[rows])
        g_end = G[CHUNK - 1:CHUNK, :]
        kd_ref[rows, :] = (qk_ref[rows, HK:2 * HK].astype(F32) * jnp.exp(g_end - G)).astype(BF16)
        dec_ref[n:n + 1, :] = jnp.exp(g_end)

    for h in range(GLA_HEADS):
        ks = slice(h * GLA_DK, (h + 1) * GLA_DK)
        vs = slice(h * GLA_DV, (h + 1) * GLA_DV)
        st = st_ref[h]
        for n in range(nch):
            rows = slice(n * CHUNK, (n + 1) * CHUNK)
            st = st * dec_ref[n:n + 1, ks] + _dot_tn(v_ref[rows, vs], kd_ref[rows, ks])
            sall_ref[n * GLA_HEADS + h] = st.astype(BF16)
        st_ref[h] = st

    for n in range(nch):
        rows = slice(n * CHUNK, (n + 1) * CHUNK)
        for h in range(GLA_HEADS):
            ks = slice(h * GLA_DK, (h + 1) * GLA_DK)
            vs = slice(h * GLA_DV, (h + 1) * GLA_DV)
            qh = (qk_ref[rows, ks].astype(F32) * qscale).astype(BF16)
            o = _dot_nt(qh, sall_ref[n * GLA_HEADS + h])
            o = o * lax.rsqrt(jnp.mean(o * o, axis=-1, keepdims=True) + RMS_EPS) * gn_ref[...]
            o_ref[rows, vs] = (o * _silu(gr_ref[rows, vs].astype(F32))).astype(BF16)
    y_ref[...] = _dot(o_ref[...], wo_ref[...]).astype(y_ref.dtype)


def _gla(qk, gv, gr, tail, wa2_p, b_a, g_norm, w_o, B, S):
    N = B * S
    D = w_o.shape[1]
    ts = min(S, 512)
    per_b = S // ts
    HK = GLA_HEADS * GLA_DK
    tok = lambda n: pl.BlockSpec((ts, n), lambda b, t: (b * per_b + t, 0))
    full = lambda a: pl.BlockSpec(a.shape, lambda b, t: (0,) * a.ndim)
    return pl.pallas_call(
        _gla_kernel,
        out_shape=jax.ShapeDtypeStruct((N, D), BF16),
        grid=(B, per_b),
        in_specs=[tok(QK_W), tok(GV_W), tok(GV_W), tok(TAIL_W), full(wa2_p), full(b_a), full(g_norm), full(w_o)],
        out_specs=tok(D),
        scratch_shapes=[pltpu.VMEM((GLA_HEADS, GLA_DV, GLA_DK), F32),
                        pltpu.VMEM((ts, GLA_HEADS * GLA_DK), BF16),
                        pltpu.VMEM((ts // CHUNK, GLA_HEADS * GLA_DK), F32),
                        pltpu.VMEM((ts // CHUNK * GLA_HEADS, GLA_DV, GLA_DK), BF16),
                        pltpu.VMEM((ts, GV_W), BF16)],
        compiler_params=_cparams("arbitrary", "arbitrary"),
        name="gla",
    )(qk, gv, gr, tail, wa2_p, b_a, g_norm, w_o)


HQ = MLA_HEADS * 2 * LANES


def _prep_w_uq(w_uq):
    dh = MLA_NOPE + MLA_ROPE
    half = MLA_ROPE // 2
    nope = [w_uq[:, h * dh:h * dh + MLA_NOPE] for h in range(MLA_HEADS)]
    rope = [w_uq[:, h * dh + MLA_NOPE:(h + 1) * dh] for h in range(MLA_HEADS)]
    rope_sw = [jnp.concatenate([r[:, half:], r[:, :half]], axis=1) for r in rope]
    return jnp.concatenate(nope + rope + rope_sw, axis=1).astype(BF16)


def _prep_w_ukv(w_ukv):
    dh = MLA_NOPE + MLA_V
    kn = [w_ukv[:, h * dh:h * dh + MLA_NOPE] for h in range(MLA_HEADS)]
    vv = [w_ukv[:, h * dh + MLA_NOPE:(h + 1) * dh] for h in range(MLA_HEADS)]
    return jnp.concatenate(kn + vv, axis=1).astype(BF16)


def _rms(x, g):
    return x * lax.rsqrt(jnp.mean(x * x, axis=-1, keepdims=True) + RMS_EPS) * g


def _mla_prep_kernel(cq_ref, ckv_ref, tail_ref, cc_ref, ss_ref, gq_ref, wq_ref, gkv_ref, wkv_ref, q_ref, k_ref, v_ref):
    tm = cq_ref.shape[0]
    NP = MLA_HEADS * MLA_NOPE
    RP = MLA_HEADS * MLA_ROPE
    scale = (MLA_NOPE + MLA_ROPE) ** -0.5 * LOG2E
    cc = cc_ref[...]
    ss = ss_ref[...]
    cqn = _rms(cq_ref[...].astype(F32), gq_ref[...]).astype(BF16)
    qf = _dot(cqn, wq_ref[...]) * scale
    ckvn = _rms(ckv_ref[...].astype(F32), gkv_ref[...]).astype(BF16)
    kv = _dot(ckvn, wkv_ref[...])
    krr = (tail_ref[:, 0:LANES].astype(F32) * cc + tail_ref[:, LANES:2 * LANES].astype(F32) * ss).astype(BF16)
    lane = lax.broadcasted_iota(I32, (tm, LANES), 1)
    first = lane < MLA_ROPE
    for j in range(MLA_HEADS // 2):
        a = NP + j * LANES
        rot = qf[:, a:a + LANES] * cc + qf[:, a + RP:a + RP + LANES] * ss
        for h, keep in ((2 * j, first), (2 * j + 1, jnp.logical_not(first))):
            base = h * 2 * LANES
            q_ref[:, base:base + LANES] = qf[:, h * MLA_NOPE:(h + 1) * MLA_NOPE].astype(BF16)
            q_ref[:, base + LANES:base + 2 * LANES] = jnp.where(keep, rot, 0.0).astype(BF16)
            k_ref[:, base:base + LANES] = kv[:, h * MLA_NOPE:(h + 1) * MLA_NOPE].astype(BF16)
            k_ref[:, base + LANES:base + 2 * LANES] = krr
    v_ref[...] = kv[:, NP:].astype(BF16)


def _mla_prep(cq, ckv, tail, cc, ss, g_cq, w_uq_p, g_ckv, w_ukv_p):
    N = cq.shape[0]
    tm = min(N, 512)
    tok = lambda n: pl.BlockSpec((tm, n), lambda i: (i, 0))
    full = lambda a: pl.BlockSpec(a.shape, lambda i: (0,) * a.ndim)
    HV = MLA_HEADS * MLA_V
    return pl.pallas_call(
        _mla_prep_kernel,
        out_shape=(jax.ShapeDtypeStruct((N, HQ), BF16), jax.ShapeDtypeStruct((N, HQ), BF16),
                   jax.ShapeDtypeStruct((N, HV), BF16)),
        grid=(N // tm,),
        in_specs=[tok(MLA_Q_RANK), tok(MLA_KV_RANK), tok(TAIL_W), tok(LANES), tok(LANES),
                  full(g_cq), full(w_uq_p), full(g_ckv), full(w_ukv_p)],
        out_specs=(tok(HQ), tok(HQ), tok(HV)),
        compiler_params=_cparams("arbitrary"),
        name="mla_prep",
    )(cq, ckv, tail, cc, ss, g_cq, w_uq_p, g_ckv, w_ukv_p)


ATTN_TQ = 256


def _mla_attn_kernel(q_ref, k_ref, v_ref, o_ref, v1_ref):
    S = q_ref.shape[0]
    tq = min(S, ATTN_TQ)
    r = lax.broadcasted_iota(I32, (tq, tq), 0) // CHUNK
    c = lax.broadcasted_iota(I32, (tq, tq), 1) // CHUNK
    diag_mask = c <= r
    v1_ref[:, :MLA_V] = v_ref[...]
    v1_ref[:, MLA_V:] = jnp.ones((S, MLA_V), BF16)

    def scores(ii):
        l0 = ii * tq
        q = q_ref[l0:l0 + tq, :]
        sd = jnp.where(diag_mask, _dot_nt(q, k_ref[l0:l0 + tq, :]), -jnp.inf)
        so = _dot_nt(q, k_ref[0:l0, :]) if ii > 0 else None
        return sd, so

    def finish(ii, sd, so):
        l0 = ii * tq
        m = jnp.max(sd, axis=-1, keepdims=True)
        if so is not None:
            m = jnp.maximum(m, jnp.max(so, axis=-1, keepdims=True))
        acc = _dot(jnp.exp2((sd - m).astype(BF16)), v1_ref[l0:l0 + tq, :])
        if so is not None:
            acc = acc + _dot(jnp.exp2((so - m).astype(BF16)), v1_ref[0:l0, :])
        o_ref[l0:l0 + tq, :] = (acc[:, :MLA_V] / acc[:, MLA_V:]).astype(BF16)

    n_tiles = S // tq
    ahead = 3
    pending = [scores(ii) for ii in range(min(ahead, n_tiles))]
    for ii in range(n_tiles):
        if ii + ahead < n_tiles:
            pending.append(scores(ii + ahead))
        finish(ii, *pending.pop(0))


def _mla_attention(qc, kc, vv, B, S):
    N = B * S
    HV = MLA_HEADS * MLA_V
    return pl.pallas_call(
        _mla_attn_kernel,
        out_shape=jax.ShapeDtypeStruct((N, HV), BF16),
        grid=(B, MLA_HEADS),
        in_specs=[
            pl.BlockSpec((S, 2 * LANES), lambda b, h: (b, h)),
            pl.BlockSpec((S, 2 * LANES), lambda b, h: (b, h)),
            pl.BlockSpec((S, MLA_V), lambda b, h: (b, h)),
        ],
        out_specs=pl.BlockSpec((S, MLA_V), lambda b, h: (b, h)),
        scratch_shapes=[pltpu.VMEM((S, 2 * MLA_V), BF16)],
        compiler_params=_cparams("arbitrary", "arbitrary"),
        name="mla_attn",
    )(qc, kc, vv)


def _layer_norm(u, g, b):
    mu = jnp.mean(u, axis=-1, keepdims=True)
    d = u - mu
    var = jnp.mean(d * d, axis=-1, keepdims=True)
    return d * lax.rsqrt(var + LN_EPS) * g + b


def _merge_kernel(alpha, x_ref, ya_ref, ob_ref, ga_ref, gb_ref, mod_ref, wmo_ref, wout_ref, g_ref, b_ref, wr_ref,
                  br_ref, ws13_ref, ws2_ref, x1_ref, h2_ref, sh_ref, idx_ref, ww_ref, rank_ref, cnt_ref, lg_s, carry_s):
    i = pl.program_id(0)

    @pl.when(i == 0)
    def _():
        lg_s[...] = jnp.zeros_like(lg_s)
        carry_s[...] = jnp.zeros_like(carry_s)

    parts, total = [], carry_s[...]
    for c0 in range(0, lg_s.shape[1], ROUTE_TOKENS):
        *res, total = _route_math(lg_s[:, c0:c0 + ROUTE_TOKENS], br_ref[...], total)
        parts.append(res)
    idx, ww, rank = [jnp.concatenate(col, axis=1) for col in zip(*parts)]
    total = jnp.where(i > 0, total, 0.0)
    idx_ref[...] = idx
    ww_ref[...] = ww
    rank_ref[...] = rank
    carry_s[...] = total
    cnt_ref[...] = jnp.broadcast_to(total, cnt_ref.shape).astype(I32)

    gt1 = mod_ref[0, 2:3, :]
    sh2 = mod_ref[0, 3:4, :]
    sc2 = mod_ref[0, 4:5, :]
    y = (_sigmoid(ga_ref[...].astype(F32)) * ya_ref[...].astype(F32)
         + _sigmoid(gb_ref[...].astype(F32)) * _dot(ob_ref[...], wmo_ref[...]))
    mix = _dot(y.astype(BF16), wout_ref[...])
    x1 = _layer_norm(alpha * x_ref[...] + (1.0 + gt1) * mix, g_ref[...], b_ref[...])
    x1_ref[...] = x1
    h2 = x1 * (1.0 + sc2) + sh2
    h2_ref[...] = _pack_bf16_pair(h2)
    h2b = h2.astype(BF16)
    lg_s[...] = _dot_nt(wr_ref[...], h2b)
    F = ws2_ref.shape[0]
    ab = _dot(h2b, ws13_ref[...])
    hid = (_silu(ab[:, :F]) * ab[:, F:]).astype(BF16)
    sh_ref[...] = _dot(hid, ws2_ref[...]).astype(sh_ref.dtype)


def _merge_route(alpha, x2, y_a, o_b, gate_a, gate_b, mod6, w_mla_o, w_out, ln_g, ln_b, wr_t, b_router, ws13, ws2,
                 S, tok0, Np):
    N, D = x2.shape
    E = wr_t.shape[0]
    tm = min(S, 512)
    per_b = S // tm
    assert Np % tm == 0 and tok0 % tm == 0
    n = Np // tm
    base = tok0 // tm
    cur = lambda i: jnp.minimum(i, n - 1)
    prev = lambda i: jnp.maximum(i - 1, 0)
    tok_in = lambda w: pl.BlockSpec((tm, w), lambda i: (base + cur(i), 0))
    tok = lambda w: pl.BlockSpec((tm, w), lambda i: (cur(i), 0))
    full = lambda a: pl.BlockSpec(a.shape, lambda i: (0,) * a.ndim)
    lane_blk = pl.BlockSpec((TOP_K, tm), lambda i: (0, prev(i)))
    br = b_router.reshape(E, 1).astype(F32)
    return pl.pallas_call(
        functools.partial(_merge_kernel, alpha),
        out_shape=(jax.ShapeDtypeStruct((Np, D), F32), jax.ShapeDtypeStruct((Np, D // 2), I32),
                   jax.ShapeDtypeStruct((Np, D), BF16),
                   jax.ShapeDtypeStruct((TOP_K, Np), I32), jax.ShapeDtypeStruct((TOP_K, Np), I32),
                   jax.ShapeDtypeStruct((TOP_K, Np), I32), jax.ShapeDtypeStruct((E, LANES), I32)),
        grid=(n + 1,),
        in_specs=[tok_in(D), tok_in(D), tok_in(D), tok_in(D), tok_in(D),
                  pl.BlockSpec((1, 6, D), lambda i: ((base + cur(i)) // per_b, 0, 0)),
                  full(w_mla_o), full(w_out), full(ln_g), full(ln_b), full(wr_t), full(br), full(ws13), full(ws2)],
        out_specs=(tok(D), tok(D // 2), tok(D), lane_blk, lane_blk, lane_blk,
                   pl.BlockSpec((E, LANES), lambda i: (0, 0))),
        scratch_shapes=[pltpu.VMEM((E, tm), F32), pltpu.VMEM((E, 1), F32)],
        compiler_params=_cparams("arbitrary"),
        name="merge_ln1_route",
    )(x2, y_a, o_b, gate_a, gate_b, mod6, w_mla_o, w_out, ln_g, ln_b, wr_t, br, ws13, ws2)


def _first_argmax(v, io, n):
    m = jnp.max(v, axis=0, keepdims=True)
    idx = jnp.min(jnp.where(v == m, io, n), axis=0, keepdims=True)
    return m, idx


def _route_math(lg, br, carry):
    E, T = lg.shape
    gsz = E // N_GROUPS
    neg = -jnp.inf
    s = _sigmoid(lg)
    biased = s + br
    eio = lax.broadcasted_iota(I32, (E, T), 0)
    gio = lax.broadcasted_iota(I32, (gsz, T), 0)

    gs = []
    for g in range(N_GROUPS):
        blk = biased[g * gsz:(g + 1) * gsz]
        m1, i1 = _first_argmax(blk, gio, gsz)
        m2 = jnp.max(jnp.where(gio == i1, neg, blk), axis=0, keepdims=True)
        gs.append(m1 + m2)
    cur = jnp.concatenate(gs, axis=0)
    nio = lax.broadcasted_iota(I32, (N_GROUPS, T), 0)
    gsel = jnp.zeros((N_GROUPS, T), F32)
    for _ in range(TOPK_GROUPS):
        _, gi = _first_argmax(cur, nio, N_GROUPS)
        hit = nio == gi
        gsel = jnp.where(hit, 1.0, gsel)
        cur = jnp.where(hit, neg, cur)
    emask = jnp.concatenate([jnp.broadcast_to(gsel[g:g + 1], (gsz, T)) for g in range(N_GROUPS)], axis=0) > 0.0

    cur = jnp.where(emask, biased, neg)
    idxs, ws = [], []
    sel = jnp.zeros((E, T), F32)
    for _ in range(TOP_K):
        _, ei = _first_argmax(cur, eio, E)
        hit = eio == ei
        idxs.append(ei)
        ws.append(jnp.sum(jnp.where(hit, s, 0.0), axis=0, keepdims=True))
        sel = jnp.where(hit, 1.0, sel)
        cur = jnp.where(hit, neg, cur)
    w = jnp.concatenate(ws, axis=0)
    w = w / jnp.sum(w, axis=0, keepdims=True) * ROUTED_SCALE
    wbits = lax.bitcast_convert_type(w.astype(BF16).astype(F32), I32)
    wword = jnp.bitwise_or(lax.shift_right_logical(wbits, jnp.int32(16)), jnp.bitwise_and(wbits, jnp.int32(-65536)))

    r = lax.broadcasted_iota(I32, (T, T), 0)
    c = lax.broadcasted_iota(I32, (T, T), 1)
    before = (r < c).astype(BF16)
    cnt = _dot(sel.astype(BF16), before) + carry
    ranks = [jnp.sum(jnp.where(eio == idxs[k], cnt, 0.0), axis=0, keepdims=True) for k in range(TOP_K)]
    total = cnt[:, T - 1:T] + sel[:, T - 1:T]
    return jnp.concatenate(idxs, axis=0), wword, jnp.concatenate(ranks, axis=0).astype(I32), total


def _dest_kernel(idx_ref, rank_ref, ps_ref, dest_ref):
    K, T = idx_ref.shape
    E = ps_ref.shape[0]
    eio = lax.broadcasted_iota(I32, (E, T), 0)
    ps = ps_ref[...]
    rows = [jnp.sum(jnp.where(eio == idx_ref[k:k + 1, :], ps, 0), axis=0, keepdims=True) for k in range(K)]
    dest_ref[...] = jnp.concatenate(rows, axis=0) + rank_ref[...]


def _dest_slots(idx, rank, pad_start):
    K, N = idx.shape
    E = pad_start.shape[0]
    T = min(N, 512)
    spec = pl.BlockSpec((K, T), lambda i: (0, i))
    return pl.pallas_call(
        _dest_kernel,
        out_shape=jax.ShapeDtypeStruct((K, N), I32),
        grid=(N // T,),
        in_specs=[spec, spec, pl.BlockSpec((E, 1), lambda i: (0, 0))],
        out_specs=spec,
        compiler_params=_cparams("arbitrary"),
        name="dest_slots",
    )(idx, rank, pad_start.reshape(E, 1))


SC_ROWS = 64


def _sc_workers():
    info = plsc.get_sparse_core_info()
    return info.num_cores, info.num_subcores


def _sc_lanes():
    return plsc.get_sparse_core_info().num_lanes


def _dispatch_rows(h2p, dest, n_slots):
    N, W = h2p.shape
    K = dest.shape[0]
    R = SC_ROWS
    nc, ns = _sc_workers()
    per_w = N // R // (nc * ns)
    assert per_w * R * nc * ns == N and per_w % 2 == 0, "token count must split into chunk pairs per subcore"
    mesh = plsc.VectorSubcoreMesh(core_axis_name="c", subcore_axis_name="s")

    @functools.partial(
        pl.kernel, mesh=mesh,
        out_type=jax.ShapeDtypeStruct((n_slots, W), I32),
        scratch_types=[pltpu.VMEM((K, R), I32), pltpu.VMEM((K, R), I32),
                       pltpu.VMEM((R, W), I32), pltpu.VMEM((R, W), I32),
                       pltpu.SemaphoreType.DMA, pltpu.SemaphoreType.DMA, pltpu.SemaphoreType.DMA],
    )
    def k(h_hbm, d_hbm, xs_hbm, idx0, idx1, rows0, rows1, lsem0, lsem1, ssem):
        idx, rows, lsem = (idx0, idx1), (rows0, rows1), (lsem0, lsem1)
        base = (lax.axis_index("s") * nc + lax.axis_index("c")) * per_w

        def loads(ch, b):
            tok = pl.ds(pl.multiple_of(ch * R, 8), R)
            return [pltpu.make_async_copy(d_hbm.at[kk, tok], idx[b].at[kk], lsem[b]) for kk in range(K)] + [
                pltpu.make_async_copy(h_hbm.at[tok], rows[b], lsem[b])]

        for cp in loads(base, 0):
            cp.start()
        for cp in loads(base, 0):
            cp.wait()

        @pl.loop(0, per_w, step=2)
        def _(j):
            for b in range(2):
                ch = base + j + b
                more = j + b + 1 < per_w

                @pl.when(more)
                def _():
                    for cp in loads(ch + 1, 1 - b):
                        cp.start()

                scatters = [pltpu.make_async_copy(rows[b], xs_hbm.at[idx[b].at[kk]], ssem) for kk in range(K)]
                for cp in scatters:
                    cp.start()
                for cp in scatters:
                    cp.wait()

                @pl.when(more)
                def _():
                    for cp in loads(ch + 1, 1 - b):
                        cp.wait()

    return k(h2p, dest)


COMBINE_ROWS = 8


def _combine_sum(ys, dest, ww, tok0, n_tok):
    W = ys.shape[1]
    K = dest.shape[0]
    R = COMBINE_ROWS
    L = _sc_lanes()
    nc, ns = _sc_workers()
    per_w = n_tok // R // (nc * ns)
    assert per_w * R * nc * ns == n_tok and per_w % 2 == 0, "token count must split into chunk pairs per subcore"
    T = per_w * R
    mesh = plsc.VectorSubcoreMesh(core_axis_name="c", subcore_axis_name="s")

    @functools.partial(
        pl.kernel, mesh=mesh, compiler_params=pltpu.CompilerParams(needs_layout_passes=False),
        out_type=jax.ShapeDtypeStruct((n_tok, W), I32),
        scratch_types=[pltpu.VMEM((K * T,), I32), pltpu.VMEM((K * T,), I32),
                       pltpu.VMEM((K, R, W), I32), pltpu.VMEM((K, R, W), I32),
                       pltpu.VMEM((R, W), I32), pltpu.VMEM((R, W), I32),
                       pltpu.SemaphoreType.DMA, pltpu.SemaphoreType.DMA,
                       pltpu.SemaphoreType.DMA, pltpu.SemaphoreType.DMA],
    )
    def k(ys_hbm, d_hbm, w_hbm, o_hbm, idx_all, w_all, buf0, buf1, out0, out1, gsem0, gsem1, osem0, osem1):
        buf, outv, gsem, osem = (buf0, buf1), (out0, out1), (gsem0, gsem1), (osem0, osem1)
        wid = lax.axis_index("s") * nc + lax.axis_index("c")
        t0 = pl.multiple_of(tok0 + wid * T, 8)
        for kk in range(K):
            pltpu.sync_copy(d_hbm.at[kk, pl.ds(t0, T)], idx_all.at[pl.ds(kk * T, T)])
            pltpu.sync_copy(w_hbm.at[kk, pl.ds(t0, T)], w_all.at[pl.ds(kk * T, T)])

        def fetch(j, b):
            return [pltpu.make_async_copy(
                ys_hbm.at[idx_all.at[pl.ds(pl.multiple_of(kk * T + j * R, 8), R)]], buf[b].at[kk], gsem[b])
                for kk in range(K)]

        def store(j, b):
            return pltpu.make_async_copy(outv[b], o_hbm.at[pl.ds(pl.multiple_of(wid * T + j * R, 8), R)], osem[b])

        for cp in fetch(0, 0):
            cp.start()

        @pl.loop(0, per_w, step=2)
        def _(j0):
            for b in range(2):
                j = j0 + b

                @pl.when(j + 1 < per_w)
                def _():
                    for cp in fetch(j + 1, 1 - b):
                        cp.start()

                for cp in fetch(j, b):
                    cp.wait()

                @pl.when(j >= 2)
                def _():
                    store(j - 2, b).wait()

                @pl.loop(0, R)
                def _(r):
                    ws = [plsc.bitcast(plsc.load_gather(w_all, [jnp.full((L,), kk * T + j * R + r, I32)]), BF16)
                          for kk in range(K)]

                    @pl.loop(0, W // L)
                    def _(q):
                        sl = pl.ds(q * L, L)
                        terms = [ws[kk] * plsc.bitcast(buf[b][kk, r, sl], BF16) for kk in range(K)]
                        while len(terms) > 1:
                            terms = [terms[i] + terms[i + 1] for i in range(0, len(terms), 2)]
                        outv[b][r, sl] = plsc.bitcast(terms[0], I32)

                store(j, b).start()

        store(per_w - 2, 0).wait()
        store(per_w - 1, 1).wait()

    return k(ys, dest, ww)


def _expert_kernel(b0_ref, nb_ref, cnt_ref, xs_hbm, w1_ref, w3_ref, w2_ref, ys_hbm,
                   w13_s, w2_s, xbuf, ybuf, sem_in, sem_out):
    e = pl.program_id(0)
    n_exp = pl.num_programs(0)
    F = w1_ref.shape[2]
    bm = xbuf.shape[1]
    half = xbuf.shape[2]
    nb = nb_ref[e]
    b0 = b0_ref[e]
    total = b0_ref[n_exp - 1] + nb_ref[n_exp - 1]

    nbuf = xbuf.shape[0]

    def in_copy(g):
        slot = g % nbuf
        return pltpu.make_async_copy(xs_hbm.at[pl.ds(g * bm, bm)], xbuf.at[slot], sem_in.at[slot])

    def out_copy(g):
        slot = g % nbuf
        return pltpu.make_async_copy(ybuf.at[slot], ys_hbm.at[pl.ds(g * bm, bm)], sem_out.at[slot])

    @pl.when(e == 0)
    def _():
        for g0 in range(nbuf - 1):
            @pl.when(g0 < total)
            def _():
                in_copy(g0).start()

    @pl.when(nb > 0)
    def _():
        w13_s[:, :F] = w1_ref[0].astype(BF16)
        w13_s[:, F:] = w3_ref[0].astype(BF16)
        w2_s[...] = w2_ref[0].astype(BF16)

    def block(j, carry):
        g = b0 + j
        slot = g % nbuf
        in_copy(g).wait()

        @pl.when(g + nbuf - 1 < total)
        def _():
            in_copy(g + nbuf - 1).start()

        @pl.when(g >= nbuf)
        def _():
            out_copy(g - nbuf).wait()

        def swiglu(rows):
            lo, hi = _unpack_bf16_pair(xbuf[slot, :rows, :])
            ab = _dot(lo.astype(BF16), w13_s[:half, :]) + _dot(hi.astype(BF16), w13_s[half:, :])
            hid = (_silu(ab[:, :F]) * ab[:, F:]).astype(BF16)
            ybuf[slot, :rows, :] = _pack_bf16_pair(_dot(hid, w2_s[...]))

        rows_left = cnt_ref[e] - j * bm

        @pl.when(rows_left > bm // 2)
        def _():
            swiglu(bm)

        @pl.when(rows_left <= bm // 2)
        def _():
            swiglu(bm // 2)
            ybuf[slot, bm // 2:, :] = jnp.zeros((bm - bm // 2, half), I32)

        out_copy(g).start()
        return carry

    lax.fori_loop(0, nb, block, 0)

    @pl.when(e == n_exp - 1)
    def _():
        for back in range(nbuf, 0, -1):
            @pl.when(total >= back)
            def _():
                out_copy(total - back).wait()


def _experts(xs, blk0, nblk, counts, w_e1, w_e3, w_e2):
    n_slots, W = xs.shape
    E, D, F = w_e1.shape
    bm = EXPERT_BLOCK
    grid_spec = pltpu.PrefetchScalarGridSpec(
        num_scalar_prefetch=3,
        grid=(E,),
        in_specs=[
            pl.BlockSpec(memory_space=pl.ANY),
            pl.BlockSpec((1, D, F), lambda e, b0, nb, cnt: (e, 0, 0)),
            pl.BlockSpec((1, D, F), lambda e, b0, nb, cnt: (e, 0, 0)),
            pl.BlockSpec((1, F, D), lambda e, b0, nb, cnt: (e, 0, 0)),
        ],
        out_specs=pl.BlockSpec(memory_space=pl.ANY),
        scratch_shapes=[pltpu.VMEM((D, 2 * F), BF16), pltpu.VMEM((F, D), BF16),
                        pltpu.VMEM((EXPERT_RING, bm, W), I32), pltpu.VMEM((EXPERT_RING, bm, W), I32),
                        pltpu.SemaphoreType.DMA((EXPERT_RING,)), pltpu.SemaphoreType.DMA((EXPERT_RING,))],
    )
    return pl.pallas_call(
        _expert_kernel,
        out_shape=jax.ShapeDtypeStruct((n_slots, W), I32),
        grid_spec=grid_spec,
        compiler_params=_cparams("arbitrary"),
        name="experts",
    )(blk0, nblk, counts, xs, w_e1, w_e3, w_e2)


def _final_kernel(alpha, ms_ref, sh_ref, x1_ref, mod_ref, g_ref, b_ref, *rest):
    o_ref = rest[-1]
    gt2 = mod_ref[0, 5:6, :]
    lo, hi = _unpack_bf16_pair(ms_ref[...])
    ffn = jnp.concatenate([lo, hi], axis=1) + sh_ref[...].astype(F32)
    o_ref[...] = _layer_norm(alpha * x1_ref[...] + (1.0 + gt2) * ffn, g_ref[...], b_ref[...])


def _final_part(alpha, ms, sh, x1, mod6, ln_g, ln_b, S, tok0, row0, n_total, prev):
    D = x1.shape[1]
    N = n_total
    n_part, W = ms.shape
    tm = min(S, 512, n_part)
    assert n_part % tm == 0 and S % tm == 0 and tok0 % tm == 0 and row0 % tm == 0
    per_b = S // tm
    steps = n_part // tm
    off = tok0 // tm
    goff = row0 // tm
    tok = lambda n: pl.BlockSpec((tm, n), lambda i: (off + i, 0))
    full = lambda a: pl.BlockSpec(a.shape, lambda i: (0,) * a.ndim)
    in_specs = [pl.BlockSpec((tm, W), lambda i: (i, 0)), tok(D), tok(D),
                pl.BlockSpec((1, 6, D), lambda i: ((goff + i) // per_b, 0, 0)), full(ln_g), full(ln_b)]
    args = [ms, sh, x1, mod6, ln_g, ln_b]
    aliases = {}
    if prev is not None:
        in_specs.append(pl.BlockSpec(memory_space=pl.ANY))
        args.append(prev)
        aliases = {len(args) - 1: 0}
    return pl.pallas_call(
        functools.partial(_final_kernel, alpha),
        out_shape=jax.ShapeDtypeStruct((N, D), F32),
        grid=(steps,),
        in_specs=in_specs,
        out_specs=pl.BlockSpec((tm, D), lambda i: (goff + i, 0)),
        input_output_aliases=aliases,
        compiler_params=_cparams("arbitrary"),
        name="combine_ln2",
    )(*args)


def _moe(alpha, h2p, sh, idx, ww, rank, cnt, x1, mod6, w_e1, w_e3, w_e2, ln_g, ln_b, S, row0, n_total, out):
    N = x1.shape[0]
    E = w_e1.shape[0]
    bm = EXPERT_BLOCK
    counts = cnt[:, 0]
    padded = (counts + bm - 1) // bm * bm
    pad_end = jnp.cumsum(padded)
    pad_start = pad_end - padded
    n_slots = N * TOP_K + E * bm
    dest = _dest_slots(idx, rank, pad_start.astype(I32))
    xs = _dispatch_rows(h2p, dest, n_slots)
    ys = _experts(xs, (pad_start // bm).astype(I32), (padded // bm).astype(I32), counts.astype(I32),
                  w_e1, w_e3, w_e2)
    n_part = min(N, COMBINE_TOKENS) if row0 + N == n_total else N
    assert N % n_part == 0
    for p in range(N // n_part):
        tok0 = p * n_part
        ms = _combine_sum(ys, dest, ww, tok0, n_part)
        out = _final_part(alpha, ms, sh, x1, mod6, ln_g, ln_b, S, tok0, row0 + tok0, n_total, out)
    return out


def kernel(x, c, positions, w_ada, b_ada, w_in, w_gla_a2, b_gla_a, g_gla_norm, w_gla_o, g_cq, w_uq, g_ckv, w_ukv, w_mla_o, w_out, ln1_g, ln1_b, w_router, b_router, w_e1, w_e3, w_e2, w_s1, w_s3, w_s2, ln2_g, ln2_b):
    B, S, D = x.shape
    N = B * S
    depth = w_ada.shape[0]
    alpha = (2.0 * depth) ** 0.25
    row = lambda a: a.reshape(1, -1)
    cc, ss = _rope_tables(positions)
    x2 = x.reshape(N, D)
    for l in range(depth):
        mod6 = _modulation(c, w_ada[l], b_ada[l]).reshape(B, 6, D)
        qk, gv, gr, cq, ckv, gate_a, gate_b, tail = _in_projection(x2, mod6, _prep_w_in(w_in[l], D), S)
        wa2_p = jnp.concatenate(
            [w_gla_a2[l], jnp.zeros((LANES - GLA_GATE_RANK, w_gla_a2.shape[2]), F32)], axis=0).astype(BF16)
        y_a = _gla(qk, gv, gr, tail, wa2_p, row(b_gla_a[l]), row(g_gla_norm[l]), w_gla_o[l].astype(BF16), B, S)
        qc, kc, vv = _mla_prep(cq, ckv, tail, cc, ss, row(g_cq[l]), _prep_w_uq(w_uq[l]),
                               row(g_ckv[l]), _prep_w_ukv(w_ukv[l]))
        o_b = _mla_attention(qc, kc, vv, B, S)
        ws13 = jnp.concatenate([w_s1[l], w_s3[l]], axis=1).astype(BF16)
        ws2 = w_s2[l].astype(BF16)
        b_first = max(1, (B * MOE_FIRST_EIGHTHS) // 8) if B > 1 else B
        ranges = [(0, b_first * S)] + ([(b_first * S, (B - b_first) * S)] if B > b_first else [])
        routed =[_merge_route(alpha, x2, y_a, o_b, gate_a, gate_b, mod6, w_mla_o[l].astype(BF16),
                               w_out[l].astype(BF16), row(ln1_g[l]), row(ln1_b[l]), w_router[l].T.astype(BF16),
                               b_router[l], ws13, ws2, S, t0, n) for t0, n in ranges]
        out = None
        for (t0, n), (x1, h2p, sh, idx, ww, rank, cnt) in zip(ranges, routed):
            out = _moe(alpha, h2p, sh, idx, ww, rank, cnt, x1, mod6, w_e1[l], w_e3[l], w_e2[l],
                       row(ln2_g[l]), row(ln2_b[l]), S, t0, N, out)
        x2 = out
    return x2.reshape(B, S, D)
```

```python
import functools

import jax
import jax.numpy as jnp
from jax import lax
from jax.experimental import pallas as pl
from jax.experimental.pallas import tpu as pltpu
from jax.experimental.pallas import tpu_sc as plsc

CHUNK = 64
GLA_HEADS = 4
GLA_DK = 128
GLA_DV = 256
GLA_GATE_RANK = 16
GLA_GATE_TAU = 16.0
MLA_HEADS = 8
MLA_Q_RANK = 768
MLA_KV_RANK = 256
MLA_NOPE = 128
MLA_ROPE = 64
MLA_V = 128
ROPE_THETA = 10000.0
N_EXPERTS = 256
TOP_K = 8
N_GROUPS = 8
TOPK_GROUPS = 4
D_EXPERT = 256
ROUTED_SCALE = 2.5
LN_EPS = 1e-5
RMS_EPS = 1e-6
LOG2E = 1.4426950408889634

LANES = 128
VMEM_LIMIT = 56 * 1024 * 1024
EXPERT_BLOCK = 512
EXPERT_RING = 6
ROUTE_TOKENS = 128
MOE_FIRST_EIGHTHS = 5
COMBINE_TOKENS = 4096

F32 = jnp.float32
BF16 = jnp.bfloat16
I32 = jnp.int32


def _cparams(*sem):
    return pltpu.CompilerParams(dimension_semantics=sem, vmem_limit_bytes=VMEM_LIMIT)


def _sigmoid(x):
    return 1.0 / (1.0 + jnp.exp(-x))


def _silu(x):
    return x * _sigmoid(x)


def _dot(a, b):
    return jnp.dot(a, b, preferred_element_type=F32)


def _dot_nt(a, b):
    return lax.dot_general(a, b, (((1,), (1,)), ((), ())), preferred_element_type=F32)


def _dot_tn(a, b):
    return lax.dot_general(a, b, (((0,), (0,)), ((), ())), preferred_element_type=F32)


def _pack_bf16_pair(x):
    w = x.shape[1] // 2
    u = lax.bitcast_convert_type(x.astype(BF16).astype(F32), I32)
    lo = lax.shift_right_logical(u[:, :w], jnp.int32(16))
    hi = jnp.bitwise_and(u[:, w:], jnp.int32(-65536))
    return jnp.bitwise_or(lo, hi)


def _unpack_bf16_pair(p):
    lo = lax.bitcast_convert_type(lax.shift_left(p, jnp.int32(16)), F32)
    hi = lax.bitcast_convert_type(jnp.bitwise_and(p, jnp.int32(-65536)), F32)
    return lo, hi


def _mod_kernel(c_ref, w_ref, b_ref, o_ref):
    cond = _silu(c_ref[...]).astype(BF16)
    o_ref[...] = _dot(cond, w_ref[...].astype(BF16)) + b_ref[...]


def _modulation(c, w_ada, b_ada):
    B, D = c.shape
    W = w_ada.shape[1]
    tn = D
    return pl.pallas_call(
        _mod_kernel,
        out_shape=jax.ShapeDtypeStruct((B, W), F32),
        grid=(W // tn,),
        in_specs=[
            pl.BlockSpec((B, D), lambda j: (0, 0)),
            pl.BlockSpec((D, tn), lambda j: (0, j)),
            pl.BlockSpec((1, tn), lambda j: (0, j)),
        ],
        out_specs=pl.BlockSpec((B, tn), lambda j: (0, j)),
        compiler_params=_cparams("arbitrary"),
        name="adaln_mod",
    )(c, w_ada, b_ada.reshape(1, W))


def _rope_kernel(pos_ref, f_ref, ph_ref, cc_ref, ss_ref):
    ang = pos_ref[...] * f_ref[...]
    cc_ref[...] = jnp.cos(ang)
    ss_ref[...] = jnp.sin(ang) * ph_ref[...]


def _rope_tables(positions):
    B, S = positions.shape
    N = B * S
    half = MLA_ROPE // 2
    inv_freq = ROPE_THETA ** (-jnp.arange(half, dtype=F32) * (2.0 / MLA_ROPE))
    f4 = jnp.tile(inv_freq, LANES // half).reshape(1, LANES)
    sign = jnp.tile(jnp.concatenate([-jnp.ones((half,), F32), jnp.ones((half,), F32)]), LANES // MLA_ROPE)
    pos = jnp.broadcast_to(positions.reshape(N, 1).astype(F32), (N, LANES))
    tm = min(N, 1024)
    spec = pl.BlockSpec((tm, LANES), lambda i: (i, 0))
    vec = pl.BlockSpec((1, LANES), lambda i: (0, 0))
    return pl.pallas_call(
        _rope_kernel,
        out_shape=(jax.ShapeDtypeStruct((N, LANES), F32), jax.ShapeDtypeStruct((N, LANES), F32)),
        grid=(N // tm,),
        in_specs=[spec, vec, vec],
        out_specs=(spec, spec),
        compiler_params=_cparams("arbitrary"),
        name="rope_tables",
    )(pos, f4, sign.reshape(1, LANES))


QK_W = 2 * GLA_HEADS * GLA_DK
GV_W = GLA_HEADS * GLA_DV
TAIL_W = 3 * LANES


def _in_segments(D):
    return (QK_W, GV_W, GV_W, MLA_Q_RANK, MLA_KV_RANK, D, D, TAIL_W)


def _prep_w_in(w_in, D):
    s = [GLA_HEADS * GLA_DK, GLA_HEADS * GLA_DK, GV_W, GV_W, GLA_GATE_RANK, MLA_Q_RANK, MLA_KV_RANK, MLA_ROPE, D, D]
    offs = [0]
    for n in s:
        offs.append(offs[-1] + n)
    gq, gk, gv, gr, ga, cq, ckv, kr, gate_a, gate_b = [w_in[:, offs[i]:offs[i + 1]] for i in range(10)]
    half = MLA_ROPE // 2
    kr_sw = jnp.concatenate([kr[:, half:], kr[:, :half]], axis=1)
    pad = jnp.zeros((w_in.shape[0], LANES - GLA_GATE_RANK), w_in.dtype)
    return jnp.concatenate([gq, gk, gv, gr, cq, ckv, gate_a, gate_b, kr, kr, kr_sw, kr_sw, ga, pad], axis=1).astype(BF16)


def _inproj_kernel(x_ref, mod_ref, w_ref, *out_refs):
    sh1 = mod_ref[0, 0:1, :]
    sc1 = mod_ref[0, 1:2, :]
    h = (x_ref[...] * (1.0 + sc1) + sh1).astype(BF16)
    off = 0
    for ref in out_refs:
        n = ref.shape[-1]
        ref[...] = _dot(h, w_ref[:, off:off + n]).astype(ref.dtype)
        off += n


def _in_projection(x2, mod6, w_in_p, S):
    N, D = x2.shape
    W = w_in_p.shape[1]
    tm = min(S, 512)
    per_b = S // tm
    segs = _in_segments(D)
    assert sum(segs) == W
    return pl.pallas_call(
        _inproj_kernel,
        out_shape=tuple(jax.ShapeDtypeStruct((N, n), BF16) for n in segs),
        grid=(N // tm,),
        in_specs=[
            pl.BlockSpec((tm, D), lambda i: (i, 0)),
            pl.BlockSpec((1, 6, D), lambda i: (i // per_b, 0, 0)),
            pl.BlockSpec((D, W), lambda i: (0, 0), pipeline_mode=pl.Buffered(1)),
        ],
        out_specs=tuple(pl.BlockSpec((tm, n), lambda i: (i, 0)) for n in segs),
        compiler_params=_cparams("arbitrary"),
        name="in_proj",
    )(x2, mod6, w_in_p)


def _gla_kernel(qk_ref, v_ref, gr_ref, tail_ref, wa2_ref, ba_ref, gn_ref, wo_ref, y_ref,
                st_ref, kd_ref, dec_ref, sall_ref, o_ref):
    t = pl.program_id(1)

    @pl.when(t == 0)
    def _():
        st_ref[...] = jnp.zeros_like(st_ref)

    ts = qk_ref.shape[0]
    nch = ts // CHUNK
    HK = GLA_HEADS * GLA_DK
    r = lax.broadcasted_iota(I32, (CHUNK, CHUNK), 0)
    c = lax.broadcasted_iota(I32, (CHUNK, CHUNK), 1)
    tri = (r >= c).astype(BF16)
    qscale = GLA_DK ** -0.5

    z = _dot(tail_ref[:, 2 * LANES:3 * LANES], wa2_ref[...]) + ba_ref[...]
    log_a = (jnp.minimum(z, 0.0) - jnp.log(1.0 + jnp.exp(-jnp.abs(z)))) * (1.0 / GLA_GATE_TAU)
    la_hi = log_a.astype(BF16)
    la_lo = (log_a - la_hi.astype(F32)).astype(BF16)
    for n in range(nch):
        rows = slice(n * CHUNK, (n + 1) * CHUNK)
        G = _dot(tri, la_hi[rows]) + _dot(tri, la_lo[rows])
        g_end = G[CHUNK - 1:CHUNK, :]
        kd_ref[rows, :] = (qk_ref[rows, HK:2 * HK].astype(F32) * jnp.exp(g_end - G)).astype(BF16)
        dec_ref[n:n + 1, :] = jnp.exp(g_end)

    for h in range(GLA_HEADS):
        ks = slice(h * GLA_DK, (h + 1) * GLA_DK)
        vs = slice(h * GLA_DV, (h + 1) * GLA_DV)
        st = st_ref[h]
        for n in range(nch):
            rows = slice(n * CHUNK, (n + 1) * CHUNK)
            st = st * dec_ref[n:n + 1, ks] + _dot_tn(v_ref[rows, vs], kd_ref[rows, ks])
            sall_ref[n * GLA_HEADS + h] = st.astype(BF16)
        st_ref[h] = st

    for n in range(nch):
        rows = slice(n * CHUNK, (n + 1) * CHUNK)
        for h in range(GLA_HEADS):
            ks = slice(h * GLA_DK, (h + 1) * GLA_DK)
            vs = slice(h * GLA_DV, (h + 1) * GLA_DV)
            qh = (qk_ref[rows, ks].astype(F32) * qscale).astype(BF16)
            o = _dot_nt(qh, sall_ref[n * GLA_HEADS + h])
            o = o * lax.rsqrt(jnp.mean(o * o, axis=-1, keepdims=True) + RMS_EPS) * gn_ref[...]
            o_ref[rows, vs] = (o * _silu(gr_ref[rows, vs].astype(F32))).astype(BF16)
    y_ref[...] = _dot(o_ref[...], wo_ref[...]).astype(y_ref.dtype)


def _gla(qk, gv, gr, tail, wa2_p, b_a, g_norm, w_o, B, S):
    N = B * S
    D = w_o.shape[1]
    ts = min(S, 512)
    per_b = S // ts
    HK = GLA_HEADS * GLA_DK
    tok = lambda n: pl.BlockSpec((ts, n), lambda b, t: (b * per_b + t, 0))
    full = lambda a: pl.BlockSpec(a.shape, lambda b, t: (0,) * a.ndim)
    return pl.pallas_call(
        _gla_kernel,
        out_shape=jax.ShapeDtypeStruct((N, D), BF16),
        grid=(B, per_b),
        in_specs=[tok(QK_W), tok(GV_W), tok(GV_W), tok(TAIL_W), full(wa2_p), full(b_a), full(g_norm), full(w_o)],
        out_specs=tok(D),
        scratch_shapes=[pltpu.VMEM((GLA_HEADS, GLA_DV, GLA_DK), F32),
                        pltpu.VMEM((ts, GLA_HEADS * GLA_DK), BF16),
                        pltpu.VMEM((ts // CHUNK, GLA_HEADS * GLA_DK), F32),
                        pltpu.VMEM((ts // CHUNK * GLA_HEADS, GLA_DV, GLA_DK), BF16),
                        pltpu.VMEM((ts, GV_W), BF16)],
        compiler_params=_cparams("arbitrary", "arbitrary"),
        name="gla",
    )(qk, gv, gr, tail, wa2_p, b_a, g_norm, w_o)


HQ = MLA_HEADS * 2 * LANES


def _prep_w_uq(w_uq):
    dh = MLA_NOPE + MLA_ROPE
    half = MLA_ROPE // 2
    nope = [w_uq[:, h * dh:h * dh + MLA_NOPE] for h in range(MLA_HEADS)]
    rope = [w_uq[:, h * dh + MLA_NOPE:(h + 1) * dh] for h in range(MLA_HEADS)]
    rope_sw = [jnp.concatenate([r[:, half:], r[:, :half]], axis=1) for r in rope]
    return jnp.concatenate(nope + rope + rope_sw, axis=1).astype(BF16)


def _prep_w_ukv(w_ukv):
    dh = MLA_NOPE + MLA_V
    kn = [w_ukv[:, h * dh:h * dh + MLA_NOPE] for h in range(MLA_HEADS)]
    vv = [w_ukv[:, h * dh + MLA_NOPE:(h + 1) * dh] for h in range(MLA_HEADS)]
    return jnp.concatenate(kn + vv, axis=1).astype(BF16)


def _rms(x, g):
    return x * lax.rsqrt(jnp.mean(x * x, axis=-1, keepdims=True) + RMS_EPS) * g


def _mla_prep_kernel(cq_ref, ckv_ref, tail_ref, cc_ref, ss_ref, gq_ref, wq_ref, gkv_ref, wkv_ref, q_ref, k_ref, v_ref):
    tm = cq_ref.shape[0]
    NP = MLA_HEADS * MLA_NOPE
    RP = MLA_HEADS * MLA_ROPE
    scale = (MLA_NOPE + MLA_ROPE) ** -0.5 * LOG2E
    cc = cc_ref[...]
    ss = ss_ref[...]
    cqn = _rms(cq_ref[...].astype(F32), gq_ref[...]).astype(BF16)
    qf = _dot(cqn, wq_ref[...]) * scale
    ckvn = _rms(ckv_ref[...].astype(F32), gkv_ref[...]).astype(BF16)
    kv = _dot(ckvn, wkv_ref[...])
    krr = (tail_ref[:, 0:LANES].astype(F32) * cc + tail_ref[:, LANES:2 * LANES].astype(F32) * ss).astype(BF16)
    lane = lax.broadcasted_iota(I32, (tm, LANES), 1)
    first = lane < MLA_ROPE
    for j in range(MLA_HEADS // 2):
        a = NP + j * LANES
        rot = qf[:, a:a + LANES] * cc + qf[:, a + RP:a + RP + LANES] * ss
        for h, keep in ((2 * j, first), (2 * j + 1, jnp.logical_not(first))):
            base = h * 2 * LANES
            q_ref[:, base:base + LANES] = qf[:, h * MLA_NOPE:(h + 1) * MLA_NOPE].astype(BF16)
            q_ref[:, base + LANES:base + 2 * LANES] = jnp.where(keep, rot, 0.0).astype(BF16)
            k_ref[:, base:base + LANES] = kv[:, h * MLA_NOPE:(h + 1) * MLA_NOPE].astype(BF16)
            k_ref[:, base + LANES:base + 2 * LANES] = krr
    v_ref[...] = kv[:, NP:].astype(BF16)


def _mla_prep(cq, ckv, tail, cc, ss, g_cq, w_uq_p, g_ckv, w_ukv_p):
    N = cq.shape[0]
    tm = min(N, 512)
    tok = lambda n: pl.BlockSpec((tm, n), lambda i: (i, 0))
    full = lambda a: pl.BlockSpec(a.shape, lambda i: (0,) * a.ndim)
    HV = MLA_HEADS * MLA_V
    return pl.pallas_call(
        _mla_prep_kernel,
        out_shape=(jax.ShapeDtypeStruct((N, HQ), BF16), jax.ShapeDtypeStruct((N, HQ), BF16),
                   jax.ShapeDtypeStruct((N, HV), BF16)),
        grid=(N // tm,),
        in_specs=[tok(MLA_Q_RANK), tok(MLA_KV_RANK), tok(TAIL_W), tok(LANES), tok(LANES),
                  full(g_cq), full(w_uq_p), full(g_ckv), full(w_ukv_p)],
        out_specs=(tok(HQ), tok(HQ), tok(HV)),
        compiler_params=_cparams("arbitrary"),
        name="mla_prep",
    )(cq, ckv, tail, cc, ss, g_cq, w_uq_p, g_ckv, w_ukv_p)


ATTN_TQ = 256


def _mla_attn_kernel(q_ref, k_ref, v_ref, o_ref, v1_ref):
    S = q_ref.shape[0]
    tq = min(S, ATTN_TQ)
    r = lax.broadcasted_iota(I32, (tq, tq), 0) // CHUNK
    c = lax.broadcasted_iota(I32, (tq, tq), 1) // CHUNK
    diag_mask = c <= r
    v1_ref[:, :MLA_V] = v_ref[...]
    v1_ref[:, MLA_V:] = jnp.ones((S, MLA_V), BF16)

    def scores(ii):
        l0 = ii * tq
        q = q_ref[l0:l0 + tq, :]
        sd = jnp.where(diag_mask, _dot_nt(q, k_ref[l0:l0 + tq, :]), -jnp.inf)
        so = _dot_nt(q, k_ref[0:l0, :]) if ii > 0 else None
        return sd, so

    def finish(ii, sd, so):
        l0 = ii * tq
        m = jnp.max(sd, axis=-1, keepdims=True)
        if so is not None:
            m = jnp.maximum(m, jnp.max(so, axis=-1, keepdims=True))
        acc = _dot(jnp.exp2((sd - m).astype(BF16)), v1_ref[l0:l0 + tq, :])
        if so is not None:
            acc = acc + _dot(jnp.exp2((so - m).astype(BF16)), v1_ref[0:l0, :])
        o_ref[l0:l0 + tq, :] = (acc[:, :MLA_V] / acc[:, MLA_V:]).astype(BF16)

    n_tiles = S // tq
    ahead = 2
    pending = [scores(ii) for ii in range(min(ahead, n_tiles))]
    for ii in range(n_tiles):
        if ii + ahead < n_tiles:
            pending.append(scores(ii + ahead))
        finish(ii, *pending.pop(0))


def _mla_attention(qc, kc, vv, B, S):
    N = B * S
    HV = MLA_HEADS * MLA_V
    return pl.pallas_call(
        _mla_attn_kernel,
        out_shape=jax.ShapeDtypeStruct((N, HV), BF16),
        grid=(B, MLA_HEADS),
        in_specs=[
            pl.BlockSpec((S, 2 * LANES), lambda b, h: (b, h)),
            pl.BlockSpec((S, 2 * LANES), lambda b, h: (b, h)),
            pl.BlockSpec((S, MLA_V), lambda b, h: (b, h)),
        ],
        out_specs=pl.BlockSpec((S, MLA_V), lambda b, h: (b, h)),
        scratch_shapes=[pltpu.VMEM((S, 2 * MLA_V), BF16)],
        compiler_params=_cparams("arbitrary", "arbitrary"),
        name="mla_attn",
    )(qc, kc, vv)


def _layer_norm(u, g, b):
    mu = jnp.mean(u, axis=-1, keepdims=True)
    d = u - mu
    var = jnp.mean(d * d, axis=-1, keepdims=True)
    return d * lax.rsqrt(var + LN_EPS) * g + b


def _merge_kernel(alpha, x_ref, ya_ref, ob_ref, ga_ref, gb_ref, mod_ref, wmo_ref, wout_ref, g_ref, b_ref, wr_ref,
                  br_ref, ws13_ref, ws2_ref, x1_ref, h2_ref, sh_ref, idx_ref, ww_ref, rank_ref, cnt_ref, lg_s, carry_s):
    i = pl.program_id(0)

    @pl.when(i == 0)
    def _():
        lg_s[...] = jnp.zeros_like(lg_s)
        carry_s[...] = jnp.zeros_like(carry_s)

    parts, total = [], carry_s[...]
    for c0 in range(0, lg_s.shape[1], ROUTE_TOKENS):
        *res, total = _route_math(lg_s[:, c0:c0 + ROUTE_TOKENS], br_ref[...], total)
        parts.append(res)
    idx, ww, rank = [jnp.concatenate(col, axis=1) for col in zip(*parts)]
    total = jnp.where(i > 0, total, 0.0)
    idx_ref[...] = idx
    ww_ref[...] = ww
    rank_ref[...] = rank
    carry_s[...] = total
    cnt_ref[...] = jnp.broadcast_to(total, cnt_ref.shape).astype(I32)

    gt1 = mod_ref[0, 2:3, :]
    sh2 = mod_ref[0, 3:4, :]
    sc2 = mod_ref[0, 4:5, :]
    y = (_sigmoid(ga_ref[...].astype(F32)) * ya_ref[...].astype(F32)
         + _sigmoid(gb_ref[...].astype(F32)) * _dot(ob_ref[...], wmo_ref[...]))
    mix = _dot(y.astype(BF16), wout_ref[...])
    x1 = _layer_norm(alpha * x_ref[...] + (1.0 + gt1) * mix, g_ref[...], b_ref[...])
    x1_ref[...] = x1
    h2 = x1 * (1.0 + sc2) + sh2
    h2_ref[...] = _pack_bf16_pair(h2)
    h2b = h2.astype(BF16)
    lg_s[...] = _dot_nt(wr_ref[...], h2b)
    F = ws2_ref.shape[0]
    ab = _dot(h2b, ws13_ref[...])
    hid = (_silu(ab[:, :F]) * ab[:, F:]).astype(BF16)
    sh_ref[...] = _dot(hid, ws2_ref[...]).astype(sh_ref.dtype)


def _merge_route(alpha, x2, y_a, o_b, gate_a, gate_b, mod6, w_mla_o, w_out, ln_g, ln_b, wr_t, b_router, ws13, ws2,
                 S, tok0, Np):
    N, D = x2.shape
    E = wr_t.shape[0]
    tm = min(S, 512)
    per_b = S // tm
    assert Np % tm == 0 and tok0 % tm == 0
    n = Np // tm
    base = tok0 // tm
    cur = lambda i: jnp.minimum(i, n - 1)
    prev = lambda i: jnp.maximum(i - 1, 0)
    tok_in = lambda w: pl.BlockSpec((tm, w), lambda i: (base + cur(i), 0))
    tok = lambda w: pl.BlockSpec((tm, w), lambda i: (cur(i), 0))
    full = lambda a: pl.BlockSpec(a.shape, lambda i: (0,) * a.ndim)
    lane_blk = pl.BlockSpec((TOP_K, tm), lambda i: (0, prev(i)))
    br = b_router.reshape(E, 1).astype(F32)
    return pl.pallas_call(
        functools.partial(_merge_kernel, alpha),
        out_shape=(jax.ShapeDtypeStruct((Np, D), F32), jax.ShapeDtypeStruct((Np, D // 2), I32),
                   jax.ShapeDtypeStruct((Np, D), BF16),
                   jax.ShapeDtypeStruct((TOP_K, Np), I32), jax.ShapeDtypeStruct((TOP_K, Np), I32),
                   jax.ShapeDtypeStruct((TOP_K, Np), I32), jax.ShapeDtypeStruct((E, LANES), I32)),
        grid=(n + 1,),
        in_specs=[tok_in(D), tok_in(D), tok_in(D), tok_in(D), tok_in(D),
                  pl.BlockSpec((1, 6, D), lambda i: ((base + cur(i)) // per_b, 0, 0)),
                  full(w_mla_o), full(w_out), full(ln_g), full(ln_b), full(wr_t), full(br), full(ws13), full(ws2)],
        out_specs=(tok(D), tok(D // 2), tok(D), lane_blk, lane_blk, lane_blk,
                   pl.BlockSpec((E, LANES), lambda i: (0, 0))),
        scratch_shapes=[pltpu.VMEM((E, tm), F32), pltpu.VMEM((E, 1), F32)],
        compiler_params=_cparams("arbitrary"),
        name="merge_ln1_route",
    )(x2, y_a, o_b, gate_a, gate_b, mod6, w_mla_o, w_out, ln_g, ln_b, wr_t, br, ws13, ws2)


def _first_argmax(v, io, n):
    m = jnp.max(v, axis=0, keepdims=True)
    idx = jnp.min(jnp.where(v == m, io, n), axis=0, keepdims=True)
    return m, idx


def _route_math(lg, br, carry):
    E, T = lg.shape
    gsz = E // N_GROUPS
    neg = -jnp.inf
    s = _sigmoid(lg)
    biased = s + br
    eio = lax.broadcasted_iota(I32, (E, T), 0)
    gio = lax.broadcasted_iota(I32, (gsz, T), 0)

    gs = []
    for g in range(N_GROUPS):
        blk = biased[g * gsz:(g + 1) * gsz]
        m1, i1 = _first_argmax(blk, gio, gsz)
        m2 = jnp.max(jnp.where(gio == i1, neg, blk), axis=0, keepdims=True)
        gs.append(m1 + m2)
    cur = jnp.concatenate(gs, axis=0)
    nio = lax.broadcasted_iota(I32, (N_GROUPS, T), 0)
    gsel = jnp.zeros((N_GROUPS, T), F32)
    for _ in range(TOPK_GROUPS):
        _, gi = _first_argmax(cur, nio, N_GROUPS)
        hit = nio == gi
        gsel = jnp.where(hit, 1.0, gsel)
        cur = jnp.where(hit, neg, cur)
    emask = jnp.concatenate([jnp.broadcast_to(gsel[g:g + 1], (gsz, T)) for g in range(N_GROUPS)], axis=0) > 0.0

    cur = jnp.where(emask, biased, neg)
    idxs, ws = [], []
    sel = jnp.zeros((E, T), F32)
    for _ in range(TOP_K):
        _, ei = _first_argmax(cur, eio, E)
        hit = eio == ei
        idxs.append(ei)
        ws.append(jnp.sum(jnp.where(hit, s, 0.0), axis=0, keepdims=True))
        sel = jnp.where(hit, 1.0, sel)
        cur = jnp.where(hit, neg, cur)
    w = jnp.concatenate(ws, axis=0)
    w = w / jnp.sum(w, axis=0, keepdims=True) * ROUTED_SCALE
    wbits = lax.bitcast_convert_type(w.astype(BF16).astype(F32), I32)
    wword = jnp.bitwise_or(lax.shift_right_logical(wbits, jnp.int32(16)), jnp.bitwise_and(wbits, jnp.int32(-65536)))

    r = lax.broadcasted_iota(I32, (T, T), 0)
    c = lax.broadcasted_iota(I32, (T, T), 1)
    before = (r < c).astype(BF16)
    cnt = _dot(sel.astype(BF16), before) + carry
    ranks = [jnp.sum(jnp.where(eio == idxs[k], cnt, 0.0), axis=0, keepdims=True) for k in range(TOP_K)]
    total = cnt[:, T - 1:T] + sel[:, T - 1:T]
    return jnp.concatenate(idxs, axis=0), wword, jnp.concatenate(ranks, axis=0).astype(I32), total


def _dest_kernel(idx_ref, rank_ref, ps_ref, dest_ref):
    K, T = idx_ref.shape
    E = ps_ref.shape[0]
    eio = lax.broadcasted_iota(I32, (E, T), 0)
    ps = ps_ref[...]
    rows = [jnp.sum(jnp.where(eio == idx_ref[k:k + 1, :], ps, 0), axis=0, keepdims=True) for k in range(K)]
    dest_ref[...] = jnp.concatenate(rows, axis=0) + rank_ref[...]


def _dest_slots(idx, rank, pad_start):
    K, N = idx.shape
    E = pad_start.shape[0]
    T = min(N, 512)
    spec = pl.BlockSpec((K, T), lambda i: (0, i))
    return pl.pallas_call(
        _dest_kernel,
        out_shape=jax.ShapeDtypeStruct((K, N), I32),
        grid=(N // T,),
        in_specs=[spec, spec, pl.BlockSpec((E, 1), lambda i: (0, 0))],
        out_specs=spec,
        compiler_params=_cparams("arbitrary"),
        name="dest_slots",
    )(idx, rank, pad_start.reshape(E, 1))


SC_ROWS = 64


def _sc_workers():
    info = plsc.get_sparse_core_info()
    return info.num_cores, info.num_subcores


def _sc_lanes():
    return plsc.get_sparse_core_info().num_lanes


def _dispatch_rows(h2p, dest, n_slots):
    N, W = h2p.shape
    K = dest.shape[0]
    R = SC_ROWS
    nc, ns = _sc_workers()
    per_w = N // R // (nc * ns)
    assert per_w * R * nc * ns == N and per_w % 2 == 0, "token count must split into chunk pairs per subcore"
    mesh = plsc.VectorSubcoreMesh(core_axis_name="c", subcore_axis_name="s")

    @functools.partial(
        pl.kernel, mesh=mesh,
        out_type=jax.ShapeDtypeStruct((n_slots, W), I32),
        scratch_types=[pltpu.VMEM((K, R), I32), pltpu.VMEM((K, R), I32),
                       pltpu.VMEM((R, W), I32), pltpu.VMEM((R, W), I32),
                       pltpu.SemaphoreType.DMA, pltpu.SemaphoreType.DMA, pltpu.SemaphoreType.DMA],
    )
    def k(h_hbm, d_hbm, xs_hbm, idx0, idx1, rows0, rows1, lsem0, lsem1, ssem):
        idx, rows, lsem = (idx0, idx1), (rows0, rows1), (lsem0, lsem1)
        base = (lax.axis_index("s") * nc + lax.axis_index("c")) * per_w

        def loads(ch, b):
            tok = pl.ds(pl.multiple_of(ch * R, 8), R)
            return [pltpu.make_async_copy(d_hbm.at[kk, tok], idx[b].at[kk], lsem[b]) for kk in range(K)] + [
                pltpu.make_async_copy(h_hbm.at[tok], rows[b], lsem[b])]

        for cp in loads(base, 0):
            cp.start()
        for cp in loads(base, 0):
            cp.wait()

        @pl.loop(0, per_w, step=2)
        def _(j):
            for b in range(2):
                ch = base + j + b
                more = j + b + 1 < per_w

                @pl.when(more)
                def _():
                    for cp in loads(ch + 1, 1 - b):
                        cp.start()

                scatters = [pltpu.make_async_copy(rows[b], xs_hbm.at[idx[b].at[kk]], ssem) for kk in range(K)]
                for cp in scatters:
                    cp.start()
                for cp in scatters:
                    cp.wait()

                @pl.when(more)
                def _():
                    for cp in loads(ch + 1, 1 - b):
                        cp.wait()

    return k(h2p, dest)


COMBINE_ROWS = 8


def _combine_sum(ys, dest, ww, tok0, n_tok):
    W = ys.shape[1]
    K = dest.shape[0]
    R = COMBINE_ROWS
    L = _sc_lanes()
    nc, ns = _sc_workers()
    per_w = n_tok // R // (nc * ns)
    assert per_w * R * nc * ns == n_tok and per_w % 2 == 0, "token count must split into chunk pairs per subcore"
    T = per_w * R
    mesh = plsc.VectorSubcoreMesh(core_axis_name="c", subcore_axis_name="s")

    @functools.partial(
        pl.kernel, mesh=mesh, compiler_params=pltpu.CompilerParams(needs_layout_passes=False),
        out_type=jax.ShapeDtypeStruct((n_tok, W), I32),
        scratch_types=[pltpu.VMEM((K * T,), I32), pltpu.VMEM((K * T,), I32),
                       pltpu.VMEM((K, R, W), I32), pltpu.VMEM((K, R, W), I32),
                       pltpu.VMEM((R, W), I32), pltpu.VMEM((R, W), I32),
                       pltpu.SemaphoreType.DMA, pltpu.SemaphoreType.DMA,
                       pltpu.SemaphoreType.DMA, pltpu.SemaphoreType.DMA],
    )
    def k(ys_hbm, d_hbm, w_hbm, o_hbm, idx_all, w_all, buf0, buf1, out0, out1, gsem0, gsem1, osem0, osem1):
        buf, outv, gsem, osem = (buf0, buf1), (out0, out1), (gsem0, gsem1), (osem0, osem1)
        wid = lax.axis_index("s") * nc + lax.axis_index("c")
        t0 = pl.multiple_of(tok0 + wid * T, 8)
        for kk in range(K):
            pltpu.sync_copy(d_hbm.at[kk, pl.ds(t0, T)], idx_all.at[pl.ds(kk * T, T)])
            pltpu.sync_copy(w_hbm.at[kk, pl.ds(t0, T)], w_all.at[pl.ds(kk * T, T)])

        def fetch(j, b):
            return [pltpu.make_async_copy(
                ys_hbm.at[idx_all.at[pl.ds(pl.multiple_of(kk * T + j * R, 8), R)]], buf[b].at[kk], gsem[b])
                for kk in range(K)]

        def store(j, b):
            return pltpu.make_async_copy(outv[b], o_hbm.at[pl.ds(pl.multiple_of(wid * T + j * R, 8), R)], osem[b])

        for cp in fetch(0, 0):
            cp.start()

        @pl.loop(0, per_w, step=2)
        def _(j0):
            for b in range(2):
                j = j0 + b

                @pl.when(j + 1 < per_w)
                def _():
                    for cp in fetch(j + 1, 1 - b):
                        cp.start()

                for cp in fetch(j, b):
                    cp.wait()

                @pl.when(j >= 2)
                def _():
                    store(j - 2, b).wait()

                @pl.loop(0, R)
                def _(r):
                    ws = [plsc.bitcast(plsc.load_gather(w_all, [jnp.full((L,), kk * T + j * R + r, I32)]), BF16)
                          for kk in range(K)]

                    @pl.loop(0, W // L)
                    def _(q):
                        sl = pl.ds(q * L, L)
                        terms = [ws[kk] * plsc.bitcast(buf[b][kk, r, sl], BF16) for kk in range(K)]
                        while len(terms) > 1:
                            terms = [terms[i] + terms[i + 1] for i in range(0, len(terms), 2)]
                        outv[b][r, sl] = plsc.bitcast(terms[0], I32)

                store(j, b).start()

        store(per_w - 2, 0).wait()
        store(per_w - 1, 1).wait()

    return k(ys, dest, ww)


def _expert_kernel(b0_ref, nb_ref, cnt_ref, xs_hbm, w1_ref, w3_ref, w2_ref, ys_hbm,
                   w13_s, w2_s, xbuf, ybuf, sem_in, sem_out):
    e = pl.program_id(0)
    n_exp = pl.num_programs(0)
    F = w1_ref.shape[2]
    bm = xbuf.shape[1]
    half = xbuf.shape[2]
    nb = nb_ref[e]
    b0 = b0_ref[e]
    total = b0_ref[n_exp - 1] + nb_ref[n_exp - 1]

    nbuf = xbuf.shape[0]

    def in_copy(g):
        slot = g % nbuf
        return pltpu.make_async_copy(xs_hbm.at[pl.ds(g * bm, bm)], xbuf.at[slot], sem_in.at[slot])

    def out_copy(g):
        slot = g % nbuf
        return pltpu.make_async_copy(ybuf.at[slot], ys_hbm.at[pl.ds(g * bm, bm)], sem_out.at[slot])

    @pl.when(e == 0)
    def _():
        for g0 in range(nbuf - 1):
            @pl.when(g0 < total)
            def _():
                in_copy(g0).start()

    @pl.when(nb > 0)
    def _():
        w13_s[:, :F] = w1_ref[0].astype(BF16)
        w13_s[:, F:] = w3_ref[0].astype(BF16)
        w2_s[...] = w2_ref[0].astype(BF16)

    def block(j, carry):
        g = b0 + j
        slot = g % nbuf
        in_copy(g).wait()

        @pl.when(g + nbuf - 1 < total)
        def _():
            in_copy(g + nbuf - 1).start()

        @pl.when(g >= nbuf)
        def _():
            out_copy(g - nbuf).wait()

        def swiglu(rows):
            lo, hi = _unpack_bf16_pair(xbuf[slot, :rows, :])
            ab = _dot(lo.astype(BF16), w13_s[:half, :]) + _dot(hi.astype(BF16), w13_s[half:, :])
            hid = (_silu(ab[:, :F]) * ab[:, F:]).astype(BF16)
            ybuf[slot, :rows, :] = _pack_bf16_pair(_dot(hid, w2_s[...]))

        rows_left = cnt_ref[e] - j * bm

        @pl.when(rows_left > bm // 2)
        def _():
            swiglu(bm)

        @pl.when(rows_left <= bm // 2)
        def _():
            swiglu(bm // 2)
            ybuf[slot, bm // 2:, :] = jnp.zeros((bm - bm // 2, half), I32)

        out_copy(g).start()
        return carry

    lax.fori_loop(0, nb, block, 0)

    @pl.when(e == n_exp - 1)
    def _():
        for back in range(nbuf, 0, -1):
            @pl.when(total >= back)
            def _():
                out_copy(total - back).wait()


def _experts(xs, blk0, nblk, counts, w_e1, w_e3, w_e2):
    n_slots, W = xs.shape
    E, D, F = w_e1.shape
    bm = EXPERT_BLOCK
    grid_spec = pltpu.PrefetchScalarGridSpec(
        num_scalar_prefetch=3,
        grid=(E,),
        in_specs=[
            pl.BlockSpec(memory_space=pl.ANY),
            pl.BlockSpec((1, D, F), lambda e, b0, nb, cnt: (e, 0, 0)),
            pl.BlockSpec((1, D, F), lambda e, b0, nb, cnt: (e, 0, 0)),
            pl.BlockSpec((1, F, D), lambda e, b0, nb, cnt: (e, 0, 0)),
        ],
        out_specs=pl.BlockSpec(memory_space=pl.ANY),
        scratch_shapes=[pltpu.VMEM((D, 2 * F), BF16), pltpu.VMEM((F, D), BF16),
                        pltpu.VMEM((EXPERT_RING, bm, W), I32), pltpu.VMEM((EXPERT_RING, bm, W), I32),
                        pltpu.SemaphoreType.DMA((EXPERT_RING,)), pltpu.SemaphoreType.DMA((EXPERT_RING,))],
    )
    return pl.pallas_call(
        _expert_kernel,
        out_shape=jax.ShapeDtypeStruct((n_slots, W), I32),
        grid_spec=grid_spec,
        compiler_params=_cparams("arbitrary"),
        name="experts",
    )(blk0, nblk, counts, xs, w_e1, w_e3, w_e2)


def _final_kernel(alpha, ms_ref, sh_ref, x1_ref, mod_ref, g_ref, b_ref, *rest):
    o_ref = rest[-1]
    gt2 = mod_ref[0, 5:6, :]
    lo, hi = _unpack_bf16_pair(ms_ref[...])
    ffn = jnp.concatenate([lo, hi], axis=1) + sh_ref[...].astype(F32)
    o_ref[...] = _layer_norm(alpha * x1_ref[...] + (1.0 + gt2) * ffn, g_ref[...], b_ref[...])


def _final_part(alpha, ms, sh, x1, mod6, ln_g, ln_b, S, tok0, row0, n_total, prev):
    D = x1.shape[1]
    N = n_total
    n_part, W = ms.shape
    tm = min(S, 512, n_part)
    assert n_part % tm == 0 and S % tm == 0 and tok0 % tm == 0 and row0 % tm == 0
    per_b = S // tm
    steps = n_part // tm
    off = tok0 // tm
    goff = row0 // tm
    tok = lambda n: pl.BlockSpec((tm, n), lambda i: (off + i, 0))
    full = lambda a: pl.BlockSpec(a.shape, lambda i: (0,) * a.ndim)
    in_specs = [pl.BlockSpec((tm, W), lambda i: (i, 0)), tok(D), tok(D),
                pl.BlockSpec((1, 6, D), lambda i: ((goff + i) // per_b, 0, 0)), full(ln_g), full(ln_b)]
    args = [ms, sh, x1, mod6, ln_g, ln_b]
    aliases = {}
    if prev is not None:
        in_specs.append(pl.BlockSpec(memory_space=pl.ANY))
        args.append(prev)
        aliases = {len(args) - 1: 0}
    return pl.pallas_call(
        functools.partial(_final_kernel, alpha),
        out_shape=jax.ShapeDtypeStruct((N, D), F32),
        grid=(steps,),
        in_specs=in_specs,
        out_specs=pl.BlockSpec((tm, D), lambda i: (goff + i, 0)),
        input_output_aliases=aliases,
        compiler_params=_cparams("arbitrary"),
        name="combine_ln2",
    )(*args)


def _moe(alpha, h2p, sh, idx, ww, rank, cnt, x1, mod6, w_e1, w_e3, w_e2, ln_g, ln_b, S, row0, n_total, out):
    N = x1.shape[0]
    E = w_e1.shape[0]
    bm = EXPERT_BLOCK
    counts = cnt[:, 0]
    padded = (counts + bm - 1) // bm * bm
    pad_end = jnp.cumsum(padded)
    pad_start = pad_end - padded
    n_slots = N * TOP_K + E * bm
    dest = _dest_slots(idx, rank, pad_start.astype(I32))
    xs = _dispatch_rows(h2p, dest, n_slots)
    ys = _experts(xs, (pad_start // bm).astype(I32), (padded // bm).astype(I32), counts.astype(I32),
                  w_e1, w_e3, w_e2)
    n_part = min(N, COMBINE_TOKENS) if row0 + N == n_total else N
    assert N % n_part == 0
    for p in range(N // n_part):
        tok0 = p * n_part
        ms = _combine_sum(ys, dest, ww, tok0, n_part)
        out = _final_part(alpha, ms, sh, x1, mod6, ln_g, ln_b, S, tok0, row0 + tok0, n_total, out)
    return out


def kernel(x, c, positions, w_ada, b_ada, w_in, w_gla_a2, b_gla_a, g_gla_norm, w_gla_o, g_cq, w_uq, g_ckv, w_ukv, w_mla_o, w_out, ln1_g, ln1_b, w_router, b_router, w_e1, w_e3, w_e2, w_s1, w_s3, w_s2, ln2_g, ln2_b):
    B, S, D = x.shape
    N = B * S
    depth = w_ada.shape[0]
    alpha = (2.0 * depth) ** 0.25
    row = lambda a: a.reshape(1, -1)
    cc, ss = _rope_tables(positions)
    x2 = x.reshape(N, D)
    for l in range(depth):
        mod6 = _modulation(c, w_ada[l], b_ada[l]).reshape(B, 6, D)
        qk, gv, gr, cq, ckv, gate_a, gate_b, tail = _in_projection(x2, mod6, _prep_w_in(w_in[l], D), S)
        wa2_p = jnp.concatenate(
            [w_gla_a2[l], jnp.zeros((LANES - GLA_GATE_RANK, w_gla_a2.shape[2]), F32)], axis=0).astype(BF16)
        y_a = _gla(qk, gv, gr, tail, wa2_p, row(b_gla_a[l]), row(g_gla_norm[l]), w_gla_o[l].astype(BF16), B, S)
        qc, kc, vv = _mla_prep(cq, ckv, tail, cc, ss, row(g_cq[l]), _prep_w_uq(w_uq[l]),
                               row(g_ckv[l]), _prep_w_ukv(w_ukv[l]))
        o_b = _mla_attention(qc, kc, vv, B, S)
        ws13 = jnp.concatenate([w_s1[l], w_s3[l]], axis=1).astype(BF16)
        ws2 = w_s2[l].astype(BF16)
        b_first = max(1, (B * MOE_FIRST_EIGHTHS) // 8) if B > 1 else B
        ranges = [(0, b_first * S)] + ([(b_first * S, (B - b_first) * S)] if B > b_first else [])
        routed =[_merge_route(alpha, x2, y_a, o_b, gate_a, gate_b, mod6, w_mla_o[l].astype(BF16),
                               w_out[l].astype(BF16), row(ln1_g[l]), row(ln1_b[l]), w_router[l].T.astype(BF16),
                               b_router[l], ws13, ws2, S, t0, n) for t0, n in ranges]
        out = None
        for (t0, n), (x1, h2p, sh, idx, ww, rank, cnt) in zip(ranges, routed):
            out = _moe(alpha, h2p, sh, idx, ww, rank, cnt, x1, mod6, w_e1[l], w_e3[l], w_e2[l],
                       row(ln2_g[l]), row(ln2_b[l]), S, t0, N, out)
        x2 = out
    return x2.reshape(B, S, D)
```
